```python
import math
import jax, jax.numpy as jnp
from jax import lax
import numpy as np

D_MODEL = 2048
BATCH = 8
SEQ = 8192
DEPTH = 4

CHUNK = 64
N_MIXERS = 3
EPS = 1e-6
MASK_VALUE = -1e30

ATT_HEADS = 16
ATT_HEAD_DIM = D_MODEL // ATT_HEADS
LEFT_CHUNKS = 8
BAND_LEFT = LEFT_CHUNKS * CHUNK
BAND = (LEFT_CHUNKS + 1) * CHUNK
MAX_REL = 256
NUM_REL = (CHUNK - 1) + MAX_REL + 1

POOL_WINDOWS = (2, 4, 8, 16)
POOL_GROUPS = len(POOL_WINDOWS)
POOL_DIM = D_MODEL // POOL_GROUPS

GDN_K_HEADS = 16
GDN_V_HEADS = 32
GDN_K_DIM = D_MODEL // GDN_K_HEADS
GDN_V_DIM = D_MODEL // GDN_K_HEADS
GDN_KEY = GDN_K_HEADS * GDN_K_DIM
GDN_VAL = GDN_V_HEADS * GDN_V_DIM
GDN_CONV = 4
GDN_CONV_CH = 2 * GDN_KEY + GDN_VAL
GDN_IN = GDN_CONV_CH + GDN_VAL + 2 * GDN_V_HEADS

D_FF = 128 * ((8 * D_MODEL // 3 + 127) // 128)
FFN_CONV = 3

N_ATT_LAYERS = (DEPTH + 2) // 3
N_POOL_LAYERS = (DEPTH + 1) // 3
N_GDN_LAYERS = DEPTH // 3

kernel_name = "hybrid_chunk_causal_encoder"


def rms_norm(x, gain):
    xf = x.astype(jnp.float32)
    y = xf * lax.rsqrt(jnp.mean(xf * xf, axis=-1, keepdims=True) + EPS)
    return (y * gain.astype(jnp.float32)).astype(x.dtype)


def l2_norm(x):
    xf = x.astype(jnp.float32)
    return xf * lax.rsqrt(jnp.sum(xf * xf, axis=-1, keepdims=True) + EPS)


def causal_depthwise_conv(x, w):
    k = w.shape[0]
    s = x.shape[1]
    xp = jnp.pad(x, ((0, 0), (k - 1, 0), (0, 0)))
    out = xp[:, 0:s] * w[0]
    for j in range(1, k):
        out = out + xp[:, j:j + s] * w[j]
    return out


def chunk_band_attention(h, w_qkv, q_gain, k_gain, rel_bias, w_o):
    b, s, d = h.shape
    nc = s // CHUNK
    qkv = h @ w_qkv
    q = qkv[..., :d].reshape(b, s, ATT_HEADS, ATT_HEAD_DIM)
    k = qkv[..., d:2 * d].reshape(b, s, ATT_HEADS, ATT_HEAD_DIM)
    v = qkv[..., 2 * d:].reshape(b, s, ATT_HEADS, ATT_HEAD_DIM)
    q = rms_norm(q, q_gain)
    k = rms_norm(k, k_gain)
    k_pad = jnp.pad(k, ((0, 0), (BAND_LEFT, 0), (0, 0), (0, 0)))
    v_pad = jnp.pad(v, ((0, 0), (BAND_LEFT, 0), (0, 0), (0, 0)))
    rel = BAND_LEFT + jnp.arange(CHUNK)[:, None] - jnp.arange(BAND)[None, :]
    rel_idx = jnp.clip(rel, -(CHUNK - 1), MAX_REL) + (CHUNK - 1)
    bias = rel_bias.astype(jnp.float32)[:, rel_idx]
    scale = ATT_HEAD_DIM ** -0.5
    q_chunks = q.reshape(b, nc, CHUNK, ATT_HEADS, ATT_HEAD_DIM).transpose(1, 0, 2, 3, 4)

    def per_chunk(args):
        c, qc = args
        kb = lax.dynamic_slice_in_dim(k_pad, c * CHUNK, BAND, axis=1)
        vb = lax.dynamic_slice_in_dim(v_pad, c * CHUNK, BAND, axis=1)
        sc = jnp.einsum('bqhd,bkhd->bhqk', qc, kb,
                        preferred_element_type=jnp.float32) * scale + bias
        valid = (c * CHUNK - BAND_LEFT + jnp.arange(BAND)) >= 0
        sc = jnp.where(valid[None, None, None, :], sc, MASK_VALUE)
        p = jax.nn.softmax(sc, axis=-1).astype(vb.dtype)
        return jnp.einsum('bhqk,bkhd->bqhd', p, vb)

    o = lax.map(per_chunk, (jnp.arange(nc, dtype=jnp.int32), q_chunks))
    o = o.transpose(1, 0, 2, 3, 4).reshape(b, s, d)
    return o @ w_o


def multiscale_pool_mixer(h, pool_w, pool_scale):
    b, s, d = h.shape
    hf = h.astype(jnp.float32)
    cs = jnp.cumsum(jnp.pad(hf, ((0, 0), (1, 0), (0, 0))), axis=1)
    pos = jnp.arange(s)
    groups = []
    for g, w in enumerate(POOL_WINDOWS):
        csg = cs[..., g * POOL_DIM:(g + 1) * POOL_DIM]
        upper = csg[:, 1:]
        lower = jnp.pad(csg[:, :s + 1 - w], ((0, 0), (w - 1, 0), (0, 0)))
        count = jnp.minimum(pos + 1, w).astype(jnp.float32)[None, :, None]
        groups.append((upper - lower) / count - hf[..., g * POOL_DIM:(g + 1) * POOL_DIM])
    pooled = jnp.stack(groups, axis=2)
    y = jnp.einsum('bsgc,gce->bsge', pooled, pool_w.astype(jnp.float32)).reshape(b, s, d)
    return (y * pool_scale.astype(jnp.float32)).astype(h.dtype)


def gated_delta_rule(q, k, v, g, beta):
    b, s, h, dk = q.shape
    dv = v.shape[-1]
    nc = s // CHUNK

    def to_chunks(t):
        return t.reshape(b, nc, CHUNK, h, -1).transpose(0, 3, 1, 2, 4)

    q, k, v = to_chunks(q), to_chunks(k), to_chunks(v)
    g = g.reshape(b, nc, CHUNK, h).transpose(0, 3, 1, 2)
    beta = beta.reshape(b, nc, CHUNK, h).transpose(0, 3, 1, 2)
    gc = jnp.cumsum(g, axis=-1)
    idx = jnp.arange(CHUNK)
    causal = idx[:, None] >= idx[None, :]
    strict = idx[:, None] > idx[None, :]
    diff = gc[..., :, None] - gc[..., None, :]
    decay = jnp.where(causal, jnp.exp(jnp.where(causal, diff, 0.0)), 0.0)
    kb = k * beta[..., None]
    vb = v * beta[..., None]
    a_strict = jnp.where(strict, jnp.einsum('bhncd,bhnjd->bhncj', kb, k) * decay, 0.0)
    eye = jnp.eye(CHUNK, dtype=jnp.float32)
    rhs = jnp.concatenate([vb, kb * jnp.exp(gc)[..., None]], axis=-1)
    sol = lax.linalg.triangular_solve(a_strict + eye, rhs, left_side=True, lower=True,
                                      unit_diagonal=True)
    u = sol[..., :dv]
    w = sol[..., dv:]
    attn = jnp.einsum('bhncd,bhnjd->bhncj', q, k) * decay
    qg = q * jnp.exp(gc)[..., None]
    k_state = k * jnp.exp(gc[..., -1:] - gc)[..., None]
    chunk_decay = jnp.exp(gc[..., -1])
    xs = tuple(jnp.moveaxis(t, 2, 0) for t in (qg, attn, u, w, k_state, chunk_decay))

    def step(state, inp):
        qg_c, attn_c, u_c, w_c, ks_c, dec_c = inp
        v_new = u_c - jnp.einsum('bhcd,bhde->bhce', w_c, state)
        o_c = (jnp.einsum('bhcd,bhde->bhce', qg_c, state)
               + jnp.einsum('bhcj,bhje->bhce', attn_c, v_new))
        state = state * dec_c[..., None, None] + jnp.einsum('bhcd,bhce->bhde', ks_c, v_new)
        return state, o_c

    state0 = jnp.zeros((b, h, dk, dv), jnp.float32)
    _, o = lax.scan(step, state0, xs)
    return o.transpose(1, 0, 3, 2, 4).reshape(b, s, h, dv)


def gated_deltanet_mixer(h, w_in, conv_w, a_log, dt_bias, o_gain, w_o):
    b, s, _ = h.shape
    proj = h @ w_in
    qkv = jax.nn.silu(causal_depthwise_conv(proj[..., :GDN_CONV_CH], conv_w))
    gate = proj[..., GDN_CONV_CH:GDN_CONV_CH + GDN_VAL]
    a = proj[..., GDN_CONV_CH + GDN_VAL:GDN_CONV_CH + GDN_VAL + GDN_V_HEADS]
    bt = proj[..., GDN_CONV_CH + GDN_VAL + GDN_V_HEADS:]
    q = qkv[..., :GDN_KEY].reshape(b, s, GDN_K_HEADS, GDN_K_DIM)
    k = qkv[..., GDN_KEY:2 * GDN_KEY].reshape(b, s, GDN_K_HEADS, GDN_K_DIM)
    v = qkv[..., 2 * GDN_KEY:].reshape(b, s, GDN_V_HEADS, GDN_V_DIM).astype(jnp.float32)
    q = l2_norm(q) * (GDN_K_DIM ** -0.5)
    k = l2_norm(k)
    rep = GDN_V_HEADS // GDN_K_HEADS
    q = jnp.repeat(q, rep, axis=2)
    k = jnp.repeat(k, rep, axis=2)
    beta = jax.nn.sigmoid(bt.astype(jnp.float32))
    g = -jnp.exp(a_log.astype(jnp.float32)) * jax.nn.softplus(
        a.astype(jnp.float32) + dt_bias.astype(jnp.float32))
    o = gated_delta_rule(q, k, v, g, beta)
    o = rms_norm(o, o_gain) * jax.nn.silu(
        gate.astype(jnp.float32).reshape(b, s, GDN_V_HEADS, GDN_V_DIM))
    return o.reshape(b, s, GDN_VAL).astype(h.dtype) @ w_o


def conv_ffn(h, w_up, conv_w, w_down):
    up = h @ w_up
    u = causal_depthwise_conv(up[..., :D_FF], conv_w)
    return (jax.nn.silu(u) * up[..., D_FF:]) @ w_down


def _normal(k, shape, scale):
    return scale * jax.random.normal(k, shape, jnp.float32)


def _fwd_setup_inputs(seed: int = 0) -> dict:
    key = jax.random.key(seed)
    ks = jax.random.split(key, 24)
    d = D_MODEL
    dt = jnp.exp(jax.random.uniform(ks[14], (N_GDN_LAYERS, GDN_V_HEADS), jnp.float32,
                                    minval=math.log(1e-3), maxval=math.log(1e-1)))
    return {
        "x": _normal(ks[0], (BATCH, SEQ, d), 1.0),
        "mix_norm": 1.0 + _normal(ks[1], (DEPTH, d), 0.02),
        "ffn_norm": 1.0 + _normal(ks[2], (DEPTH, d), 0.02),
        "att_w_qkv": _normal(ks[3], (N_ATT_LAYERS, d, 3 * d), d ** -0.5),
        "att_q_gain": 1.0 + _normal(ks[4], (N_ATT_LAYERS, ATT_HEAD_DIM), 0.02),
        "att_k_gain": 1.0 + _normal(ks[5], (N_ATT_LAYERS, ATT_HEAD_DIM), 0.02),
        "att_rel_bias": _normal(ks[6], (N_ATT_LAYERS, ATT_HEADS, NUM_REL), 0.2),
        "att_w_o": _normal(ks[7], (N_ATT_LAYERS, d, d), d ** -0.5),
        "pool_w": _normal(ks[8], (N_POOL_LAYERS, POOL_GROUPS, POOL_DIM, POOL_DIM),
                           POOL_DIM ** -0.5),
        "pool_scale": 1.0 + _normal(ks[9], (N_POOL_LAYERS, d), 0.1),
        "gdn_w_in": _normal(ks[10], (N_GDN_LAYERS, d, GDN_IN), d ** -0.5),
        "gdn_conv": _normal(ks[11], (N_GDN_LAYERS, GDN_CONV, GDN_CONV_CH), GDN_CONV ** -0.5),
        "gdn_a_log": jnp.log(jax.random.uniform(ks[12], (N_GDN_LAYERS, GDN_V_HEADS),
                                                 jnp.float32, minval=1.0, maxval=16.0)),
        "gdn_dt_bias": dt + jnp.log(-jnp.expm1(-dt)),
        "gdn_o_gain": 1.0 + _normal(ks[13], (N_GDN_LAYERS, GDN_V_DIM), 0.02),
        "gdn_w_o": _normal(ks[15], (N_GDN_LAYERS, GDN_VAL, d), GDN_VAL ** -0.5),
        "ffn_w_up": _normal(ks[16], (DEPTH, d, 2 * D_FF), d ** -0.5),
        "ffn_conv": _normal(ks[17], (DEPTH, FFN_CONV, D_FF), FFN_CONV ** -0.5),
        "ffn_w_down": _normal(ks[18], (DEPTH, D_FF, d), D_FF ** -0.5),
    }


def _fwd_reference(x, mix_norm, ffn_norm, att_w_qkv, att_q_gain, att_k_gain, att_rel_bias,
              att_w_o, pool_w, pool_scale, gdn_w_in, gdn_conv, gdn_a_log, gdn_dt_bias,
              gdn_o_gain, gdn_w_o, ffn_w_up, ffn_conv, ffn_w_down):
    for i in range(DEPTH):
        kind = i % N_MIXERS
        j = i // N_MIXERS
        h = rms_norm(x, mix_norm[i])
        if kind == 0:
            y = chunk_band_attention(h, att_w_qkv[j], att_q_gain[j], att_k_gain[j],
                                     att_rel_bias[j], att_w_o[j])
        elif kind == 1:
            y = multiscale_pool_mixer(h, pool_w[j], pool_scale[j])
        else:
            y = gated_deltanet_mixer(h, gdn_w_in[j], gdn_conv[j], gdn_a_log[j],
                                     gdn_dt_bias[j], gdn_o_gain[j], gdn_w_o[j])
        x = x + y
        h = rms_norm(x, ffn_norm[i])
        x = x + conv_ffn(h, ffn_w_up[i], ffn_conv[i], ffn_w_down[i])
    return x


import jax as _jax
import jax.numpy as _jnp

TWIN_FORMAT = 'train_step'
FWD_PARAMS = ['x', 'mix_norm', 'ffn_norm', 'att_w_qkv', 'att_q_gain', 'att_k_gain', 'att_rel_bias', 'att_w_o', 'pool_w', 'pool_scale', 'gdn_w_in', 'gdn_conv', 'gdn_a_log', 'gdn_dt_bias', 'gdn_o_gain', 'gdn_w_o', 'ffn_w_up', 'ffn_conv', 'ffn_w_down']
TWIN_WEIGHTS = ['mix_norm', 'ffn_norm', 'att_w_qkv', 'att_q_gain', 'att_k_gain', 'att_rel_bias', 'att_w_o', 'pool_w', 'pool_scale', 'gdn_w_in', 'gdn_conv', 'gdn_a_log', 'gdn_dt_bias', 'gdn_o_gain', 'gdn_w_o', 'ffn_w_up', 'ffn_conv', 'ffn_w_down']
TWIN_DIFF_INPUT = 'x'
TWIN_INPUTS = ['x', 'mix_norm', 'ffn_norm', 'att_w_qkv', 'att_q_gain', 'att_k_gain', 'att_rel_bias', 'att_w_o', 'pool_w', 'pool_scale', 'gdn_w_in', 'gdn_conv', 'gdn_a_log', 'gdn_dt_bias', 'gdn_o_gain', 'gdn_w_o', 'ffn_w_up', 'ffn_conv', 'ffn_w_down', 'loss_target', 'm_mix_norm', 'm_ffn_norm', 'm_att_w_qkv', 'm_att_q_gain', 'm_att_k_gain', 'm_att_rel_bias', 'm_att_w_o', 'm_pool_w', 'm_pool_scale', 'm_gdn_w_in', 'm_gdn_conv', 'm_gdn_a_log', 'm_gdn_dt_bias', 'm_gdn_o_gain', 'm_gdn_w_o', 'm_ffn_w_up', 'm_ffn_conv', 'm_ffn_w_down', 'v_mix_norm', 'v_ffn_norm', 'v_att_w_qkv', 'v_att_q_gain', 'v_att_k_gain', 'v_att_rel_bias', 'v_att_w_o', 'v_pool_w', 'v_pool_scale', 'v_gdn_w_in', 'v_gdn_conv', 'v_gdn_a_log', 'v_gdn_dt_bias', 'v_gdn_o_gain', 'v_gdn_w_o', 'v_ffn_w_up', 'v_ffn_conv', 'v_ffn_w_down']
TWIN_OUTPUTS = ['loss', 'grad_x', 'grad_mix_norm', 'grad_ffn_norm', 'grad_att_w_qkv', 'grad_att_q_gain', 'grad_att_k_gain', 'grad_att_rel_bias', 'grad_att_w_o', 'grad_pool_w', 'grad_pool_scale', 'grad_gdn_w_in', 'grad_gdn_conv', 'grad_gdn_a_log', 'grad_gdn_dt_bias', 'grad_gdn_o_gain', 'grad_gdn_w_o', 'grad_ffn_w_up', 'grad_ffn_conv', 'grad_ffn_w_down', 'delta_mix_norm', 'delta_ffn_norm', 'delta_att_w_qkv', 'delta_att_q_gain', 'delta_att_k_gain', 'delta_att_rel_bias', 'delta_att_w_o', 'delta_pool_w', 'delta_pool_scale', 'delta_gdn_w_in', 'delta_gdn_conv', 'delta_gdn_a_log', 'delta_gdn_dt_bias', 'delta_gdn_o_gain', 'delta_gdn_w_o', 'delta_ffn_w_up', 'delta_ffn_conv', 'delta_ffn_w_down', 'new_m_mix_norm', 'new_m_ffn_norm', 'new_m_att_w_qkv', 'new_m_att_q_gain', 'new_m_att_k_gain', 'new_m_att_rel_bias', 'new_m_att_w_o', 'new_m_pool_w', 'new_m_pool_scale', 'new_m_gdn_w_in', 'new_m_gdn_conv', 'new_m_gdn_a_log', 'new_m_gdn_dt_bias', 'new_m_gdn_o_gain', 'new_m_gdn_w_o', 'new_m_ffn_w_up', 'new_m_ffn_conv', 'new_m_ffn_w_down', 'new_v_mix_norm', 'new_v_ffn_norm', 'new_v_att_w_qkv', 'new_v_att_q_gain', 'new_v_att_k_gain', 'new_v_att_rel_bias', 'new_v_att_w_o', 'new_v_pool_w', 'new_v_pool_scale', 'new_v_gdn_w_in', 'new_v_gdn_conv', 'new_v_gdn_a_log', 'new_v_gdn_dt_bias', 'new_v_gdn_o_gain', 'new_v_gdn_w_o', 'new_v_ffn_w_up', 'new_v_ffn_conv', 'new_v_ffn_w_down']
TWIN_LEAF_KINDS = {'loss': 'loss', 'grad_x': 'grad_x', 'grad_mix_norm': 'grad_w', 'grad_ffn_norm': 'grad_w', 'grad_att_w_qkv': 'grad_w', 'grad_att_q_gain': 'grad_w', 'grad_att_k_gain': 'grad_w', 'grad_att_rel_bias': 'grad_w', 'grad_att_w_o': 'grad_w', 'grad_pool_w': 'grad_w', 'grad_pool_scale': 'grad_w', 'grad_gdn_w_in': 'grad_w', 'grad_gdn_conv': 'grad_w', 'grad_gdn_a_log': 'grad_w', 'grad_gdn_dt_bias': 'grad_w', 'grad_gdn_o_gain': 'grad_w', 'grad_gdn_w_o': 'grad_w', 'grad_ffn_w_up': 'grad_w', 'grad_ffn_conv': 'grad_w', 'grad_ffn_w_down': 'grad_w', 'delta_mix_norm': 'delta_w', 'delta_ffn_norm': 'delta_w', 'delta_att_w_qkv': 'delta_w', 'delta_att_q_gain': 'delta_w', 'delta_att_k_gain': 'delta_w', 'delta_att_rel_bias': 'delta_w', 'delta_att_w_o': 'delta_w', 'delta_pool_w': 'delta_w', 'delta_pool_scale': 'delta_w', 'delta_gdn_w_in': 'delta_w', 'delta_gdn_conv': 'delta_w', 'delta_gdn_a_log': 'delta_w', 'delta_gdn_dt_bias': 'delta_w', 'delta_gdn_o_gain': 'delta_w', 'delta_gdn_w_o': 'delta_w', 'delta_ffn_w_up': 'delta_w', 'delta_ffn_conv': 'delta_w', 'delta_ffn_w_down': 'delta_w', 'new_m_mix_norm': 'new_m', 'new_m_ffn_norm': 'new_m', 'new_m_att_w_qkv': 'new_m', 'new_m_att_q_gain': 'new_m', 'new_m_att_k_gain': 'new_m', 'new_m_att_rel_bias': 'new_m', 'new_m_att_w_o': 'new_m', 'new_m_pool_w': 'new_m', 'new_m_pool_scale': 'new_m', 'new_m_gdn_w_in': 'new_m', 'new_m_gdn_conv': 'new_m', 'new_m_gdn_a_log': 'new_m', 'new_m_gdn_dt_bias': 'new_m', 'new_m_gdn_o_gain': 'new_m', 'new_m_gdn_w_o': 'new_m', 'new_m_ffn_w_up': 'new_m', 'new_m_ffn_conv': 'new_m', 'new_m_ffn_w_down': 'new_m', 'new_v_mix_norm': 'new_v', 'new_v_ffn_norm': 'new_v', 'new_v_att_w_qkv': 'new_v', 'new_v_att_q_gain': 'new_v', 'new_v_att_k_gain': 'new_v', 'new_v_att_rel_bias': 'new_v', 'new_v_att_w_o': 'new_v', 'new_v_pool_w': 'new_v', 'new_v_pool_scale': 'new_v', 'new_v_gdn_w_in': 'new_v', 'new_v_gdn_conv': 'new_v', 'new_v_gdn_a_log': 'new_v', 'new_v_gdn_dt_bias': 'new_v', 'new_v_gdn_o_gain': 'new_v', 'new_v_gdn_w_o': 'new_v', 'new_v_ffn_w_up': 'new_v', 'new_v_ffn_conv': 'new_v', 'new_v_ffn_w_down': 'new_v'}


def _forward(args):
    return _fwd_reference(*[args[k] for k in FWD_PARAMS])


def _output_shape():
    def fwd():
        inp = _fwd_setup_inputs(0)
        return _fwd_reference(*[inp[k] for k in FWD_PARAMS])
    out = _jax.eval_shape(fwd)
    return out.shape, out.dtype

N_MICROBATCH = 1
ADAM_LR = 0.001
ADAM_B1 = 0.9
ADAM_B2 = 0.999
ADAM_EPS = 1e-08
ADAM_WD = 0.01
ADAM_STEP = 10
PER_EXAMPLE_BATCH_AXIS = {'x': 0, 'loss_target': 0}
SHARED_INPUTS = []
_WEIGHT_DTYPES = {'mix_norm': _jnp.float32, 'ffn_norm': _jnp.float32, 'att_w_qkv': _jnp.float32, 'att_q_gain': _jnp.float32, 'att_k_gain': _jnp.float32, 'att_rel_bias': _jnp.float32, 'att_w_o': _jnp.float32, 'pool_w': _jnp.float32, 'pool_scale': _jnp.float32, 'gdn_w_in': _jnp.float32, 'gdn_conv': _jnp.float32, 'gdn_a_log': _jnp.float32, 'gdn_dt_bias': _jnp.float32, 'gdn_o_gain': _jnp.float32, 'gdn_w_o': _jnp.float32, 'ffn_w_up': _jnp.float32, 'ffn_conv': _jnp.float32, 'ffn_w_down': _jnp.float32}
MOMENT_SCALE = {'mix_norm': 1.437766e+01, 'ffn_norm': 2.560217e+01, 'att_w_qkv': 1.099953e-01, 'att_q_gain': 1.656820e+00, 'att_k_gain': 1.653760e+00, 'att_rel_bias': 6.010707e-02, 'att_w_o': 1.856753e-01, 'pool_w': 1.330671e+00, 'pool_scale': 2.499533e+01, 'gdn_w_in': 2.132409e-01, 'gdn_conv': 3.981643e-01, 'gdn_a_log': 2.006074e+01, 'gdn_dt_bias': 1.910844e+01, 'gdn_o_gain': 1.780696e+02, 'gdn_w_o': 1.077660e+00, 'ffn_w_up': 2.495937e-01, 'ffn_conv': 2.854962e+00, 'ffn_w_down': 3.723133e-01}


def _to_microbatches(a, axis):
    t = _jnp.moveaxis(a, axis, 0)
    t = t.reshape((N_MICROBATCH, t.shape[0] // N_MICROBATCH) + t.shape[1:])
    return _jnp.moveaxis(t, 1, axis + 1)


def setup_inputs(seed: int = 0) -> dict:
    inp = _fwd_setup_inputs(seed)
    key = _jax.random.fold_in(_jax.random.key(seed), 7919)
    shape, _ = _output_shape()
    out = dict(inp)
    out["loss_target"] = _jax.random.normal(_jax.random.fold_in(key, 0), shape, _jnp.float32)
    for i, name in enumerate(TWIN_WEIGHTS):
        w = inp[name].astype(_jnp.float32)
        if MOMENT_SCALE is None:
            s = _jnp.sqrt(_jnp.mean(_jnp.square(w)) + 1e-30)
        else:
            s = MOMENT_SCALE[name]
        km, kv = _jax.random.split(_jax.random.fold_in(key, i + 1))
        out[name] = w
        out["m_" + name] = s * _jax.random.normal(km, w.shape, _jnp.float32)
        out["v_" + name] = (s * s) * _jax.random.uniform(kv, w.shape, _jnp.float32, 0.5, 1.5)
    if N_MICROBATCH > 1:
        for name, axis in PER_EXAMPLE_BATCH_AXIS.items():
            out[name] = _to_microbatches(out[name], axis)
    return {'x': out['x'], 'mix_norm': out['mix_norm'], 'ffn_norm': out['ffn_norm'], 'att_w_qkv': out['att_w_qkv'], 'att_q_gain': out['att_q_gain'], 'att_k_gain': out['att_k_gain'], 'att_rel_bias': out['att_rel_bias'], 'att_w_o': out['att_w_o'], 'pool_w': out['pool_w'], 'pool_scale': out['pool_scale'], 'gdn_w_in': out['gdn_w_in'], 'gdn_conv': out['gdn_conv'], 'gdn_a_log': out['gdn_a_log'], 'gdn_dt_bias': out['gdn_dt_bias'], 'gdn_o_gain': out['gdn_o_gain'], 'gdn_w_o': out['gdn_w_o'], 'ffn_w_up': out['ffn_w_up'], 'ffn_conv': out['ffn_conv'], 'ffn_w_down': out['ffn_w_down'], 'loss_target': out['loss_target'], 'm_mix_norm': out['m_mix_norm'], 'm_ffn_norm': out['m_ffn_norm'], 'm_att_w_qkv': out['m_att_w_qkv'], 'm_att_q_gain': out['m_att_q_gain'], 'm_att_k_gain': out['m_att_k_gain'], 'm_att_rel_bias': out['m_att_rel_bias'], 'm_att_w_o': out['m_att_w_o'], 'm_pool_w': out['m_pool_w'], 'm_pool_scale': out['m_pool_scale'], 'm_gdn_w_in': out['m_gdn_w_in'], 'm_gdn_conv': out['m_gdn_conv'], 'm_gdn_a_log': out['m_gdn_a_log'], 'm_gdn_dt_bias': out['m_gdn_dt_bias'], 'm_gdn_o_gain': out['m_gdn_o_gain'], 'm_gdn_w_o': out['m_gdn_w_o'], 'm_ffn_w_up': out['m_ffn_w_up'], 'm_ffn_conv': out['m_ffn_conv'], 'm_ffn_w_down': out['m_ffn_w_down'], 'v_mix_norm': out['v_mix_norm'], 'v_ffn_norm': out['v_ffn_norm'], 'v_att_w_qkv': out['v_att_w_qkv'], 'v_att_q_gain': out['v_att_q_gain'], 'v_att_k_gain': out['v_att_k_gain'], 'v_att_rel_bias': out['v_att_rel_bias'], 'v_att_w_o': out['v_att_w_o'], 'v_pool_w': out['v_pool_w'], 'v_pool_scale': out['v_pool_scale'], 'v_gdn_w_in': out['v_gdn_w_in'], 'v_gdn_conv': out['v_gdn_conv'], 'v_gdn_a_log': out['v_gdn_a_log'], 'v_gdn_dt_bias': out['v_gdn_dt_bias'], 'v_gdn_o_gain': out['v_gdn_o_gain'], 'v_gdn_w_o': out['v_gdn_w_o'], 'v_ffn_w_up': out['v_ffn_w_up'], 'v_ffn_conv': out['v_ffn_conv'], 'v_ffn_w_down': out['v_ffn_w_down']}


def _loss(weights, diff, rest, loss_target):
    with _jax.named_scope("forward"):
        args = {**rest, TWIN_DIFF_INPUT: diff, **{k: w.astype(_WEIGHT_DTYPES[k]) for k, w in weights.items()}}
        y = _forward(args)
    with _jax.named_scope("loss_head"):
        err = _jnp.square(y.astype(_jnp.float32) - loss_target)
        return 0.5 * _jnp.sum(_jnp.mean(err, axis=-1)) if err.ndim else 0.5 * err


def _adamw(w, g, m, v):
    m = ADAM_B1 * m + (1.0 - ADAM_B1) * g
    v = ADAM_B2 * v + (1.0 - ADAM_B2) * _jnp.square(g)
    m_hat = m / (1.0 - ADAM_B1 ** ADAM_STEP)
    v_hat = v / (1.0 - ADAM_B2 ** ADAM_STEP)
    delta = -ADAM_LR * (m_hat / (_jnp.sqrt(v_hat) + ADAM_EPS) + ADAM_WD * w)
    return delta, m, v


def reference(x, mix_norm, ffn_norm, att_w_qkv, att_q_gain, att_k_gain, att_rel_bias, att_w_o, pool_w, pool_scale, gdn_w_in, gdn_conv, gdn_a_log, gdn_dt_bias, gdn_o_gain, gdn_w_o, ffn_w_up, ffn_conv, ffn_w_down, loss_target, m_mix_norm, m_ffn_norm, m_att_w_qkv, m_att_q_gain, m_att_k_gain, m_att_rel_bias, m_att_w_o, m_pool_w, m_pool_scale, m_gdn_w_in, m_gdn_conv, m_gdn_a_log, m_gdn_dt_bias, m_gdn_o_gain, m_gdn_w_o, m_ffn_w_up, m_ffn_conv, m_ffn_w_down, v_mix_norm, v_ffn_norm, v_att_w_qkv, v_att_q_gain, v_att_k_gain, v_att_rel_bias, v_att_w_o, v_pool_w, v_pool_scale, v_gdn_w_in, v_gdn_conv, v_gdn_a_log, v_gdn_dt_bias, v_gdn_o_gain, v_gdn_w_o, v_ffn_w_up, v_ffn_conv, v_ffn_w_down):
    given = dict(x=x, mix_norm=mix_norm, ffn_norm=ffn_norm, att_w_qkv=att_w_qkv, att_q_gain=att_q_gain, att_k_gain=att_k_gain, att_rel_bias=att_rel_bias, att_w_o=att_w_o, pool_w=pool_w, pool_scale=pool_scale, gdn_w_in=gdn_w_in, gdn_conv=gdn_conv, gdn_a_log=gdn_a_log, gdn_dt_bias=gdn_dt_bias, gdn_o_gain=gdn_o_gain, gdn_w_o=gdn_w_o, ffn_w_up=ffn_w_up, ffn_conv=ffn_conv, ffn_w_down=ffn_w_down, loss_target=loss_target, m_mix_norm=m_mix_norm, m_ffn_norm=m_ffn_norm, m_att_w_qkv=m_att_w_qkv, m_att_q_gain=m_att_q_gain, m_att_k_gain=m_att_k_gain, m_att_rel_bias=m_att_rel_bias, m_att_w_o=m_att_w_o, m_pool_w=m_pool_w, m_pool_scale=m_pool_scale, m_gdn_w_in=m_gdn_w_in, m_gdn_conv=m_gdn_conv, m_gdn_a_log=m_gdn_a_log, m_gdn_dt_bias=m_gdn_dt_bias, m_gdn_o_gain=m_gdn_o_gain, m_gdn_w_o=m_gdn_w_o, m_ffn_w_up=m_ffn_w_up, m_ffn_conv=m_ffn_conv, m_ffn_w_down=m_ffn_w_down, v_mix_norm=v_mix_norm, v_ffn_norm=v_ffn_norm, v_att_w_qkv=v_att_w_qkv, v_att_q_gain=v_att_q_gain, v_att_k_gain=v_att_k_gain, v_att_rel_bias=v_att_rel_bias, v_att_w_o=v_att_w_o, v_pool_w=v_pool_w, v_pool_scale=v_pool_scale, v_gdn_w_in=v_gdn_w_in, v_gdn_conv=v_gdn_conv, v_gdn_a_log=v_gdn_a_log, v_gdn_dt_bias=v_gdn_dt_bias, v_gdn_o_gain=v_gdn_o_gain, v_gdn_w_o=v_gdn_w_o, v_ffn_w_up=v_ffn_w_up, v_ffn_conv=v_ffn_conv, v_ffn_w_down=v_ffn_w_down)
    weights = {n: given[n] for n in TWIN_WEIGHTS}
    shared = {n: given[n] for n in SHARED_INPUTS}
    per_example = {n: given[n] for n in ['x']}
    grad_fn = _jax.value_and_grad(_loss, argnums=(0, 1))

    def one_microbatch(ex, loss_target):
        ex = dict(ex)
        diff = ex.pop(TWIN_DIFF_INPUT)
        return grad_fn(weights, diff, {**shared, **ex}, loss_target)

    if N_MICROBATCH == 1:
        loss, (grad_w, grad_x) = one_microbatch(per_example, given["loss_target"])
    else:
        def body(carry, xs):
            loss_sum, grad_sum = carry
            l_k, (gw_k, gx_k) = one_microbatch(xs[0], xs[1])
            with _jax.named_scope("update"):
                return (loss_sum + l_k, _jax.tree.map(_jnp.add, grad_sum, gw_k)), gx_k

        init = (_jnp.zeros((), _jnp.float32), _jax.tree.map(_jnp.zeros_like, weights))
        (loss, grad_w), grad_x = _jax.lax.scan(body, init, (per_example, given["loss_target"]))
    with _jax.named_scope("update"):
        delta_w, new_m, new_v = {}, {}, {}
        for n in TWIN_WEIGHTS:
            delta_w[n], new_m[n], new_v[n] = _adamw(weights[n], grad_w[n], given["m_" + n], given["v_" + n])
    return (loss, grad_x, *[grad_w[n] for n in TWIN_WEIGHTS], *[delta_w[n] for n in TWIN_WEIGHTS],
            *[new_m[n] for n in TWIN_WEIGHTS], *[new_v[n] for n in TWIN_WEIGHTS])
```

```python
import functools

import jax
import jax.numpy as jnp
from jax import lax
from jax.experimental import pallas as pl
from jax.experimental.pallas import tpu as pltpu

F32 = jnp.float32
BF16 = jnp.bfloat16
MXU_DTYPE = BF16
HI = lax.Precision.HIGHEST
S = jax.ShapeDtypeStruct
BS = pl.BlockSpec

EPS = 1e-6
MASK_VALUE = -1e30
CHUNK = 64
HEAD = 128
LEFT_CHUNKS = 8
BAND_LEFT = LEFT_CHUNKS * CHUNK
BAND = BAND_LEFT + CHUNK
MAX_REL = 256
POOL_WINDOWS = (2, 4, 8, 16)
POOL_HALO = 16
GDN_CONV = 4
FFN_CONV = 3
SUBLANES = 8
LANES = 128
FF_ALIGN = 512
N_CHIPS = 4
ADAM_LR, ADAM_B1, ADAM_B2, ADAM_EPS, ADAM_WD, ADAM_STEP = 0.001, 0.9, 0.999, 1e-08, 0.01, 10
MIB = 1024 * 1024
MESH = pl.DeviceIdType.MESH


def _cp(sems, vmem_mib=40):
    return pltpu.CompilerParams(dimension_semantics=sems, vmem_limit_bytes=vmem_mib * MIB)


def _pick(n, cands):
    for c in cands:
        if n % c == 0:
            return c
    return n


def _mx(x):
    return x.astype(MXU_DTYPE)


def _dot(a, b, dims, hi=False):
    if hi:
        return lax.dot_general(a.astype(F32), b.astype(F32), (dims, ((), ())), precision=HI,
                               preferred_element_type=F32)
    return lax.dot_general(_mx(a), _mx(b), (dims, ((), ())), preferred_element_type=F32)


def dot_nn(a, b, hi=False):
    return _dot(a, b, ((1,), (0,)), hi)


def dot_nt(a, b, hi=False):
    return _dot(a, b, ((1,), (1,)), hi)


def dot_tn(a, b, hi=False):
    return _dot(a, b, ((0,), (0,)), hi)


def _sigmoid(x):
    return 1.0 / (1.0 + jnp.exp(-x))


def matmul(a, b, mode, *, out_dtype=F32, res=None, name):
    if mode == "nn":
        (M, K), N = a.shape, b.shape[1]
    elif mode == "nt":
        (M, K), N = a.shape, b.shape[0]
    else:
        (K, M), N = a.shape, b.shape[1]
    bm = _pick(M, (1024, 512, 256, 128))
    bn = _pick(N, (1024, 512, 256, 128))
    bk = _pick(K, (2048, 1408, 1024, 512, 256, 128))
    nk = K // bk
    if mode == "nn":
        a_spec = BS((bm, bk), lambda i, j, k: (i, k))
        b_spec = BS((bk, bn), lambda i, j, k: (k, j))
        dot = dot_nn
    elif mode == "nt":
        a_spec = BS((bm, bk), lambda i, j, k: (i, k))
        b_spec = BS((bn, bk), lambda i, j, k: (j, k))
        dot = dot_nt
    else:
        a_spec = BS((bk, bm), lambda i, j, k: (k, i))
        b_spec = BS((bk, bn), lambda i, j, k: (k, j))
        dot = dot_tn
    o_spec = BS((bm, bn), lambda i, j, k: (i, j))
    has_res = res is not None

    def body(*refs):
        if has_res:
            a_ref, b_ref, r_ref, o_ref, acc = refs
        else:
            a_ref, b_ref, o_ref, acc = refs
            r_ref = None
        k = pl.program_id(2)
        p = dot(a_ref[...], b_ref[...])

        def finish(total):
            if has_res:
                total = r_ref[...] + total
            o_ref[...] = total.astype(o_ref.dtype)

        if nk == 1:
            finish(p)
        else:
            @pl.when(k == 0)
            def _():
                acc[...] = p

            @pl.when(jnp.logical_and(k > 0, k < nk - 1))
            def _():
                acc[...] += p

            @pl.when(k == nk - 1)
            def _():
                finish(acc[...] + p)

    in_specs = [a_spec, b_spec] + ([o_spec] if has_res else [])
    args = (a, b) + ((res,) if has_res else ())
    return pl.pallas_call(
        body, grid=(M // bm, N // bn, nk), in_specs=in_specs, out_specs=o_spec,
        out_shape=S((M, N), out_dtype), scratch_shapes=[pltpu.VMEM((bm, bn), F32)],
        compiler_params=_cp(("parallel", "parallel", "arbitrary"), 48), name=name)(*args)


def norm_fwd(x, gain, out_dtypes, name):
    T, D = x.shape
    bt = _pick(T, (256, 128, 64))

    def body(x_ref, g_ref, *o_refs):
        xv = x_ref[...]
        r = lax.rsqrt(jnp.mean(xv * xv, axis=-1, keepdims=True) + EPS)
        y = (xv * r) * g_ref[...]
        for o in o_refs:
            o[...] = y.astype(o.dtype)

    row = BS((bt, D), lambda i: (i, 0))
    return pl.pallas_call(
        body, grid=(T // bt,), in_specs=[row, BS((1, D), lambda i: (0, 0))],
        out_specs=[row] * len(out_dtypes), out_shape=[S((T, D), dt) for dt in out_dtypes],
        compiler_params=_cp(("parallel",)), name=name)(x, gain)


def norm_bwd(x, gain, dy, dres, name):
    T, D = x.shape
    bt = _pick(T, (256, 128, 64))

    def body(x_ref, g_ref, dy_ref, dres_ref, dx_ref, dg_ref):
        i = pl.program_id(0)
        xv = x_ref[...]
        dyv = dy_ref[...].astype(F32)
        r = lax.rsqrt(jnp.mean(xv * xv, axis=-1, keepdims=True) + EPS)
        xhat = xv * r
        dxhat = dyv * g_ref[...]
        dx = r * (dxhat - xhat * jnp.mean(dxhat * xhat, axis=-1, keepdims=True))
        dx_ref[...] = dres_ref[...] + dx

        @pl.when(i == 0)
        def _():
            dg_ref[...] = jnp.zeros_like(dg_ref)

        dg_ref[...] += jnp.sum(dyv * xhat, axis=0, keepdims=True)

    row = BS((bt, D), lambda i: (i, 0))
    vec = BS((1, D), lambda i: (0, 0))
    return pl.pallas_call(
        body, grid=(T // bt,), in_specs=[row, vec, row, row], out_specs=[row, vec],
        out_shape=[S((T, D), F32), S((1, D), F32)],
        compiler_params=_cp(("arbitrary",)), name=name)(x, gain, dy, dres)


def loss_and_grad(y, target, name):
    T, D = y.shape
    bt = _pick(T, (256, 128, 64))
    nt = T // bt

    def body(y_ref, t_ref, l_ref, dy_ref, acc):
        i = pl.program_id(0)
        e = y_ref[...] - t_ref[...]
        dy_ref[...] = e * (1.0 / D)

        @pl.when(i == 0)
        def _():
            acc[...] = jnp.zeros_like(acc)

        acc[...] += jnp.sum(e * e, axis=0, keepdims=True)

        @pl.when(i == nt - 1)
        def _():
            l_ref[...] = jnp.sum(acc[...], axis=1, keepdims=True) * (0.5 / D)

    row = BS((bt, D), lambda i: (i, 0))
    return pl.pallas_call(
        body, grid=(nt,), in_specs=[row, row], out_specs=[BS((1, 1), lambda i: (0, 0)), row],
        out_shape=[S((1, 1), F32), S((T, D), F32)], scratch_shapes=[pltpu.VMEM((1, D), F32)],
        compiler_params=_cp(("arbitrary",)), name=name)(y, target)


def _prev_halo(bt):
    return lambda i: (jnp.maximum(i * (bt // SUBLANES) - 1, 0), 0)


def _ffn_u(i, a_ref, halo_ref, w_ref, ext, bt):
    ext[pl.ds(0, SUBLANES), :] = jnp.where(i > 0, halo_ref[...], 0.0)
    ext[pl.ds(SUBLANES, bt), :] = a_ref[...]
    u = w_ref[2:3, :] * ext[pl.ds(SUBLANES, bt), :]
    u += w_ref[1:2, :] * ext[pl.ds(SUBLANES - 1, bt), :]
    u += w_ref[0:1, :] * ext[pl.ds(SUBLANES - 2, bt), :]
    return u


def ffn_act_fwd(up, conv_w, name):
    T, F2 = up.shape
    Fp = F2 // 2
    bt = _pick(T, (128, 64))

    def body(a_ref, b_ref, halo_ref, w_ref, o_ref, ext):
        u = _ffn_u(pl.program_id(0), a_ref, halo_ref, w_ref, ext, bt)
        o_ref[...] = (u * _sigmoid(u) * b_ref[...]).astype(o_ref.dtype)

    return pl.pallas_call(
        body, grid=(T // bt,),
        in_specs=[BS((bt, Fp), lambda i: (i, 0)), BS((bt, Fp), lambda i: (i, 1)),
                  BS((SUBLANES, Fp), _prev_halo(bt)), BS((SUBLANES, Fp), lambda i: (0, 0))],
        out_specs=BS((bt, Fp), lambda i: (i, 0)), out_shape=S((T, Fp), MXU_DTYPE),
        scratch_shapes=[pltpu.VMEM((bt + SUBLANES, Fp), F32)],
        compiler_params=_cp(("arbitrary",)), name=name)(up, up, up, conv_w)


def ffn_act_bwd_a(up, conv_w, g_act, name):
    T, F2 = up.shape
    Fp = F2 // 2
    bt = _pick(T, (128, 64))

    def body(a_ref, b_ref, halo_ref, w_ref, g_ref, du_ref, db_ref, dw_ref, ext):
        i = pl.program_id(0)
        u = _ffn_u(i, a_ref, halo_ref, w_ref, ext, bt)
        sg = _sigmoid(u)
        g = g_ref[...]
        db_ref[...] = (g * (u * sg)).astype(db_ref.dtype)
        du = g * b_ref[...] * (sg * (1.0 + u * (1.0 - sg)))
        du_ref[...] = du

        @pl.when(i == 0)
        def _():
            dw_ref[...] = jnp.zeros_like(dw_ref)

        for j in range(FFN_CONV):
            shifted = ext[pl.ds(SUBLANES - (FFN_CONV - 1) + j, bt), :]
            dw_ref[j:j + 1, :] += jnp.sum(du * shifted, axis=0, keepdims=True)

    blk = BS((bt, Fp), lambda i: (i, 0))
    full = BS((SUBLANES, Fp), lambda i: (0, 0))
    return pl.pallas_call(
        body, grid=(T // bt,),
        in_specs=[blk, BS((bt, Fp), lambda i: (i, 1)), BS((SUBLANES, Fp), _prev_halo(bt)), full, blk],
        out_specs=[blk, blk, full],
        out_shape=[S((T, Fp), F32), S((T, Fp), MXU_DTYPE), S((SUBLANES, Fp), F32)],
        scratch_shapes=[pltpu.VMEM((bt + SUBLANES, Fp), F32)],
        compiler_params=_cp(("arbitrary",)), name=name)(up, up, up, conv_w, g_act)


def ffn_act_bwd_b(du, db, conv_w, name):
    T, Fp = du.shape
    bt = _pick(T, (128, 64))
    nt = T // bt

    def body(du_ref, halo_ref, db_ref, w_ref, o_ref, ext):
        i = pl.program_id(0)
        ext[pl.ds(0, bt), :] = du_ref[...]
        ext[pl.ds(bt, SUBLANES), :] = jnp.where(i < nt - 1, halo_ref[...], 0.0)
        da = w_ref[2:3, :] * ext[pl.ds(0, bt), :]
        da += w_ref[1:2, :] * ext[pl.ds(1, bt), :]
        da += w_ref[0:1, :] * ext[pl.ds(2, bt), :]
        o_ref[:, pl.ds(0, Fp)] = da.astype(o_ref.dtype)
        o_ref[:, pl.ds(Fp, Fp)] = db_ref[...]

    blk = BS((bt, Fp), lambda i: (i, 0))
    nxt = BS((SUBLANES, Fp), lambda i: (jnp.minimum((i + 1) * (bt // SUBLANES), T // SUBLANES - 1), 0))
    return pl.pallas_call(
        body, grid=(nt,), in_specs=[blk, nxt, blk, BS((SUBLANES, Fp), lambda i: (0, 0))],
        out_specs=BS((bt, 2 * Fp), lambda i: (i, 0)), out_shape=S((T, 2 * Fp), MXU_DTYPE),
        scratch_shapes=[pltpu.VMEM((bt + SUBLANES, Fp), F32)],
        compiler_params=_cp(("arbitrary",)), name=name)(du, du, db, conv_w)


def _attn_fill(k_ref, v_ref, gk, kn_scr, vb_scr, T):
    kn_scr[pl.ds(0, BAND_LEFT), :] = jnp.zeros((BAND_LEFT, HEAD), kn_scr.dtype)
    vb_scr[pl.ds(0, BAND_LEFT), :] = jnp.zeros((BAND_LEFT, HEAD), vb_scr.dtype)
    rb = 512

    def fill(r, carry):
        rows = pl.ds(pl.multiple_of(r * rb, rb), rb)
        dst = pl.ds(pl.multiple_of(BAND_LEFT + r * rb, rb), rb)
        k = k_ref[rows, :]
        rk = lax.rsqrt(jnp.mean(k * k, axis=-1, keepdims=True) + EPS)
        kn_scr[dst, :] = ((k * rk) * gk).astype(kn_scr.dtype)
        vb_scr[dst, :] = v_ref[rows, :].astype(vb_scr.dtype)
        return carry

    lax.fori_loop(0, T // rb, fill, 0)


def _attn_probs(c, q_ref, gq, bias_ref, kn_scr):
    q = q_ref[...]
    rq = lax.rsqrt(jnp.mean(q * q, axis=-1, keepdims=True) + EPS)
    qn = (q * rq) * gq
    band = pl.ds(pl.multiple_of(c * CHUNK, CHUNK), BAND)
    kb = kn_scr[band, :]
    s = dot_nt(qn, kb) * (HEAD ** -0.5) + bias_ref[0]
    pos = c * CHUNK - BAND_LEFT + lax.broadcasted_iota(jnp.int32, (CHUNK, BAND), 1)
    s = jnp.where(pos >= 0, s, MASK_VALUE)
    m = jnp.max(s, axis=-1, keepdims=True)
    e = jnp.exp(s - m)
    p = e / jnp.sum(e, axis=-1, keepdims=True)
    return q, rq, qn, kb, p


def attn_fwd(qkv, gq, gk, bias, name):
    T, D3 = qkv.shape
    D = D3 // 3
    H = D // HEAD
    NC = T // CHUNK

    def body(q_ref, k_ref, v_ref, gq_ref, gk_ref, bias_ref, o_ref, kn_scr, vb_scr):
        c = pl.program_id(1)

        @pl.when(c == 0)
        def _():
            _attn_fill(k_ref, v_ref, gk_ref[...], kn_scr, vb_scr, T)

        _, _, _, _, p = _attn_probs(c, q_ref, gq_ref[...], bias_ref, kn_scr)
        band = pl.ds(pl.multiple_of(c * CHUNK, CHUNK), BAND)
        o_ref[...] = dot_nn(p, vb_scr[band, :]).astype(o_ref.dtype)

    vec = BS((1, HEAD), lambda h, c: (0, 0))
    return pl.pallas_call(
        body, grid=(H, NC),
        in_specs=[BS((CHUNK, HEAD), lambda h, c: (c, h)), BS((T, HEAD), lambda h, c: (0, H + h)),
                  BS((T, HEAD), lambda h, c: (0, 2 * H + h)), vec, vec,
                  BS((1, CHUNK, BAND), lambda h, c: (h, 0, 0))],
        out_specs=BS((CHUNK, HEAD), lambda h, c: (c, h)), out_shape=S((T, D), MXU_DTYPE),
        scratch_shapes=[pltpu.VMEM((T + BAND_LEFT, HEAD), MXU_DTYPE)] * 2,
        compiler_params=_cp(("arbitrary", "arbitrary"), 48), name=name)(qkv, qkv, qkv, gq, gk, bias)


def attn_bwd(qkv, do, gq, gk, bias, name):
    T, D3 = qkv.shape
    D = D3 // 3
    H = D // HEAD
    NC = T // CHUNK
    scale = HEAD ** -0.5

    def body(q_ref, k_ref, v_ref, do_ref, gq_ref, gk_ref, bias_ref,
             dq_ref, dk_ref, dv_ref, dgq_ref, dgk_ref, dbias_ref, kn_scr, vb_scr, dkn_acc, dv_acc):
        h = pl.program_id(0)
        c = pl.program_id(1)
        gq = gq_ref[...]
        gk = gk_ref[...]

        @pl.when(c == 0)
        def _():
            _attn_fill(k_ref, v_ref, gk, kn_scr, vb_scr, T)
            dkn_acc[...] = jnp.zeros_like(dkn_acc)
            dv_acc[...] = jnp.zeros_like(dv_acc)
            dbias_ref[...] = jnp.zeros_like(dbias_ref)

        @pl.when(jnp.logical_and(c == 0, h == 0))
        def _():
            dgq_ref[...] = jnp.zeros_like(dgq_ref)
            dgk_ref[...] = jnp.zeros_like(dgk_ref)

        q, rq, qn, kb, p = _attn_probs(c, q_ref, gq, bias_ref, kn_scr)
        band = pl.ds(pl.multiple_of(c * CHUNK, CHUNK), BAND)
        dov = do_ref[...]
        dv_acc[band, :] += dot_tn(p, dov)
        dp = dot_nt(dov, vb_scr[band, :])
        ds = p * (dp - jnp.sum(dp * p, axis=-1, keepdims=True))
        dbias_ref[0] += ds
        dss = ds * scale
        dqn = dot_nn(dss, kb)
        dkn_acc[band, :] += dot_tn(dss, qn)
        xhat = q * rq
        dgq_ref[...] += jnp.sum(dqn * xhat, axis=0, keepdims=True)
        dxhat = dqn * gq
        dq = rq * (dxhat - xhat * jnp.mean(dxhat * xhat, axis=-1, keepdims=True))
        dq_ref[...] = dq.astype(dq_ref.dtype)

        @pl.when(c == NC - 1)
        def _():
            rb = 512

            def fin(r, carry):
                rows = pl.ds(pl.multiple_of(r * rb, rb), rb)
                src = pl.ds(pl.multiple_of(BAND_LEFT + r * rb, rb), rb)
                k = k_ref[rows, :]
                rk = lax.rsqrt(jnp.mean(k * k, axis=-1, keepdims=True) + EPS)
                khat = k * rk
                dkn = dkn_acc[src, :]
                dgk_ref[...] += jnp.sum(dkn * khat, axis=0, keepdims=True)
                dkh = dkn * gk
                dk = rk * (dkh - khat * jnp.mean(dkh * khat, axis=-1, keepdims=True))
                dk_ref[rows, :] = dk.astype(dk_ref.dtype)
                dv_ref[rows, :] = dv_acc[src, :].astype(dv_ref.dtype)
                return carry

            lax.fori_loop(0, T // rb, fin, 0)

    vec = BS((1, HEAD), lambda h, c: (0, 0))
    qblk = BS((CHUNK, HEAD), lambda h, c: (c, h))
    col = BS((T, HEAD), lambda h, c: (0, h))
    return pl.pallas_call(
        body, grid=(H, NC),
        in_specs=[qblk, BS((T, HEAD), lambda h, c: (0, H + h)), BS((T, HEAD), lambda h, c: (0, 2 * H + h)),
                  qblk, vec, vec, BS((1, CHUNK, BAND), lambda h, c: (h, 0, 0))],
        out_specs=[qblk, col, col, vec, vec, BS((1, CHUNK, BAND), lambda h, c: (h, 0, 0))],
        out_shape=[S((T, D), MXU_DTYPE)] * 3 + [S((1, HEAD), F32)] * 2 + [S((H, CHUNK, BAND), F32)],
        scratch_shapes=[pltpu.VMEM((T + BAND_LEFT, HEAD), MXU_DTYPE)] * 2
        + [pltpu.VMEM((T + BAND_LEFT, HEAD), F32)] * 2,
        compiler_params=_cp(("arbitrary", "arbitrary"), 56), name=name)(qkv, qkv, qkv, do, gq, gk, bias)


def rel_bias_reduce(dbias_t, num_rel, name):
    H = dbias_t.shape[1]

    def body(d_ref, o_ref):
        kk = lax.broadcasted_iota(jnp.int32, (BAND, num_rel), 0)
        rr = lax.broadcasted_iota(jnp.int32, (BAND, num_rel), 1)
        acc = jnp.zeros((H, num_rel), F32)
        for qi in range(CHUNK):
            idx = jnp.clip(BAND_LEFT + qi - kk, -(CHUNK - 1), MAX_REL) + (CHUNK - 1)
            onehot = (idx == rr).astype(F32)
            acc += dot_nn(d_ref[qi], onehot, hi=True)
        o_ref[...] = acc

    return pl.pallas_call(body, out_shape=S((H, num_rel), F32), name=name,
                          compiler_params=pltpu.CompilerParams(vmem_limit_bytes=40 * MIB))(dbias_t)


def pool_fwd(h, x, w, scale, name):
    T, D = h.shape
    G = len(POOL_WINDOWS)
    Dg = D // G
    bt = _pick(T, (256, 128, 64))

    def body(h_ref, halo_ref, x_ref, w_ref, s_ref, o_ref, p_ref, ext):
        i = pl.program_id(0)
        ext[pl.ds(0, POOL_HALO), :] = jnp.where(i > 0, halo_ref[...], 0.0)
        ext[pl.ds(POOL_HALO, bt), :] = h_ref[...]
        t = i * bt + lax.broadcasted_iota(jnp.int32, (bt, 1), 0)
        for g, win in enumerate(POOL_WINDOWS):
            cols = pl.ds(g * Dg, Dg)
            acc = ext[pl.ds(POOL_HALO, bt), cols]
            for j in range(1, win):
                acc += ext[pl.ds(POOL_HALO - j, bt), cols]
            count = jnp.minimum(t + 1, win).astype(F32)
            pooled = acc / count - h_ref[:, cols]
            p_ref[:, cols] = pooled.astype(p_ref.dtype)
            y = dot_nn(pooled, w_ref[g]) * s_ref[:, cols]
            o_ref[:, cols] = x_ref[:, cols] + y

    row = BS((bt, D), lambda i: (i, 0))
    return pl.pallas_call(
        body, grid=(T // bt,),
        in_specs=[row, BS((POOL_HALO, D), lambda i: (jnp.maximum(i * (bt // POOL_HALO) - 1, 0), 0)), row,
                  BS((G, Dg, Dg), lambda i: (0, 0, 0)), BS((1, D), lambda i: (0, 0))],
        out_specs=[row, row], out_shape=[S((T, D), F32), S((T, D), MXU_DTYPE)],
        scratch_shapes=[pltpu.VMEM((bt + POOL_HALO, D), F32)],
        compiler_params=_cp(("arbitrary",)), name=name)(h, h, x, w, scale)


def pool_bwd_a(dy, pooled, w, scale, name):
    T, D = dy.shape
    G = len(POOL_WINDOWS)
    Dg = D // G
    bt = _pick(T, (256, 128, 64))

    def body(dy_ref, p_ref, w_ref, s_ref, dp_ref, dw_ref, ds_ref):
        i = pl.program_id(0)

        @pl.when(i == 0)
        def _():
            dw_ref[...] = jnp.zeros_like(dw_ref)
            ds_ref[...] = jnp.zeros_like(ds_ref)

        for g in range(G):
            cols = pl.ds(g * Dg, Dg)
            pg = p_ref[:, cols]
            dyg = dy_ref[:, cols]
            ypre = dot_nn(pg, w_ref[g])
            ds_ref[:, cols] += jnp.sum(dyg * ypre, axis=0, keepdims=True)
            dys = dyg * s_ref[:, cols]
            dp_ref[:, cols] = dot_nt(dys, w_ref[g])
            dw_ref[g] += dot_tn(pg, dys)

    row = BS((bt, D), lambda i: (i, 0))
    wspec = BS((G, Dg, Dg), lambda i: (0, 0, 0))
    vec = BS((1, D), lambda i: (0, 0))
    return pl.pallas_call(
        body, grid=(T // bt,), in_specs=[row, row, wspec, vec], out_specs=[row, wspec, vec],
        out_shape=[S((T, D), F32), S((G, Dg, Dg), F32), S((1, D), F32)],
        compiler_params=_cp(("arbitrary",)), name=name)(dy, pooled, w, scale)


def pool_bwd_b(dpooled, name):
    T, D = dpooled.shape
    G = len(POOL_WINDOWS)
    Dg = D // G
    bt = _pick(T, (256, 128, 64))
    nt = T // bt

    def body(d_ref, halo_ref, o_ref, ext):
        i = pl.program_id(0)
        t = i * bt + lax.broadcasted_iota(jnp.int32, (bt, 1), 0)
        for g, win in enumerate(POOL_WINDOWS):
            cols = pl.ds(g * Dg, Dg)
            count = jnp.minimum(t + 1, win).astype(F32)
            ext[pl.ds(0, bt), cols] = d_ref[:, cols] / count
            ext[pl.ds(bt, POOL_HALO), cols] = jnp.where(i < nt - 1, halo_ref[:, cols] * (1.0 / win), 0.0)
            acc = ext[pl.ds(0, bt), cols]
            for j in range(1, win):
                acc += ext[pl.ds(j, bt), cols]
            o_ref[:, cols] = acc - d_ref[:, cols]

    row = BS((bt, D), lambda i: (i, 0))
    nxt = BS((POOL_HALO, D), lambda i: (jnp.minimum((i + 1) * (bt // POOL_HALO), T // POOL_HALO - 1), 0))
    return pl.pallas_call(
        body, grid=(nt,), in_specs=[row, nxt], out_specs=row, out_shape=S((T, D), F32),
        scratch_shapes=[pltpu.VMEM((bt + POOL_HALO, D), F32)],
        compiler_params=_cp(("arbitrary",)), name=name)(dpooled, dpooled)


def _gdn_u(i, x_ref, halo_ref, w_ref, ext, bt):
    ext[pl.ds(0, SUBLANES), :] = jnp.where(i > 0, halo_ref[...], 0.0)
    ext[pl.ds(SUBLANES, bt), :] = x_ref[...]
    u = w_ref[3:4, :] * ext[pl.ds(SUBLANES, bt), :]
    for j in range(GDN_CONV - 1):
        u += w_ref[j:j + 1, :] * ext[pl.ds(SUBLANES - (GDN_CONV - 1) + j, bt), :]
    return u


def gdn_pre_fwd(proj, conv_w, key_dim, name):
    T = proj.shape[0]
    C = conv_w.shape[1]
    cb = min(1024, key_dim)
    nq, nqk, J = key_dim // cb, 2 * key_dim // cb, C // cb
    bt = _pick(T, (256, 128, 64))

    def body(x_ref, halo_ref, w_ref, o_ref, ext):
        i, j = pl.program_id(0), pl.program_id(1)
        u = _gdn_u(i, x_ref, halo_ref, w_ref, ext, bt)
        s = u * _sigmoid(u)

        @pl.when(j < nqk)
        def _():
            sc = jnp.where(j < nq, HEAD ** -0.5, 1.0)
            for hh in range(cb // HEAD):
                cols = pl.ds(hh * HEAD, HEAD)
                blk = s[:, hh * HEAD:(hh + 1) * HEAD]
                r = lax.rsqrt(jnp.sum(blk * blk, axis=-1, keepdims=True) + EPS)
                o_ref[:, cols] = (blk * r) * sc

        @pl.when(j >= nqk)
        def _():
            o_ref[...] = s

    return pl.pallas_call(
        body, grid=(T // bt, J),
        in_specs=[BS((bt, cb), lambda i, j: (i, j)),
                  BS((SUBLANES, cb), lambda i, j: (jnp.maximum(i * (bt // SUBLANES) - 1, 0), j)),
                  BS((SUBLANES, cb), lambda i, j: (0, j))],
        out_specs=BS((bt, cb), lambda i, j: (i, j)), out_shape=S((T, C), F32),
        scratch_shapes=[pltpu.VMEM((bt + SUBLANES, cb), F32)],
        compiler_params=_cp(("arbitrary", "arbitrary")), name=name)(proj, proj, conv_w)


def gdn_pre_bwd_a(proj, conv_w, dq_v, dk_v, dv, key_dim, name):
    T = proj.shape[0]
    C = conv_w.shape[1]
    cb = min(1024, key_dim)
    nq, nqk, J = key_dim // cb, 2 * key_dim // cb, C // cb
    nv = J - nqk
    bt = _pick(T, (256, 128, 64))

    def body(x_ref, halo_ref, w_ref, dq_ref, dk_ref, dv_ref, du_ref, dw_ref, ext, ds_scr):
        j, i = pl.program_id(0), pl.program_id(1)
        u = _gdn_u(i, x_ref, halo_ref, w_ref, ext, bt)
        sg = _sigmoid(u)
        s = u * sg

        @pl.when(j < nqk)
        def _():
            sc = jnp.where(j < nq, HEAD ** -0.5, 1.0)
            for hh in range(cb // HEAD):
                lo = 2 * hh * HEAD
                dq2 = dq_ref[:, lo:lo + HEAD] + dq_ref[:, lo + HEAD:lo + 2 * HEAD]
                dk2 = dk_ref[:, lo:lo + HEAD] + dk_ref[:, lo + HEAD:lo + 2 * HEAD]
                dn = jnp.where(j < nq, dq2, dk2)
                blk = s[:, hh * HEAD:(hh + 1) * HEAD]
                r = lax.rsqrt(jnp.sum(blk * blk, axis=-1, keepdims=True) + EPS)
                shat = blk * r
                ds_scr[:, pl.ds(hh * HEAD, HEAD)] = (sc * r) * (dn - shat * jnp.sum(dn * shat, axis=-1, keepdims=True))

        @pl.when(j >= nqk)
        def _():
            ds_scr[...] = dv_ref[...]

        du = ds_scr[...] * (sg * (1.0 + u * (1.0 - sg)))
        du_ref[...] = du

        @pl.when(i == 0)
        def _():
            dw_ref[...] = jnp.zeros_like(dw_ref)

        for k in range(GDN_CONV):
            shifted = ext[pl.ds(SUBLANES - (GDN_CONV - 1) + k, bt), :]
            dw_ref[k:k + 1, :] += jnp.sum(du * shifted, axis=0, keepdims=True)

    blk = BS((bt, cb), lambda j, i: (i, j))
    return pl.pallas_call(
        body, grid=(J, T // bt),
        in_specs=[blk, BS((SUBLANES, cb), lambda j, i: (jnp.maximum(i * (bt // SUBLANES) - 1, 0), j)),
                  BS((SUBLANES, cb), lambda j, i: (0, j)),
                  BS((bt, 2 * cb), lambda j, i: (i, jnp.minimum(j, nq - 1))),
                  BS((bt, 2 * cb), lambda j, i: (i, jnp.clip(j - nq, 0, nq - 1))),
                  BS((bt, cb), lambda j, i: (i, jnp.clip(j - nqk, 0, nv - 1)))],
        out_specs=[blk, BS((SUBLANES, cb), lambda j, i: (0, j))],
        out_shape=[S((T, C), F32), S((SUBLANES, C), F32)],
        scratch_shapes=[pltpu.VMEM((bt + SUBLANES, cb), F32), pltpu.VMEM((bt, cb), F32)],
        compiler_params=_cp(("arbitrary", "arbitrary")), name=name)(proj, proj, conv_w, dq_v, dk_v, dv)


def gdn_pre_bwd_b(du, dgate, conv_w, key_dim, name):
    T, C = du.shape
    V = dgate.shape[1]
    cb = min(1024, key_dim)
    J = C // cb
    J2 = (C + V) // cb
    bt = _pick(T, (256, 128, 64))
    nt = T // bt

    def body(du_ref, halo_ref, w_ref, dg_ref, o_ref, ext):
        i, j = pl.program_id(0), pl.program_id(1)

        @pl.when(j < J)
        def _():
            ext[pl.ds(0, bt), :] = du_ref[...]
            ext[pl.ds(bt, SUBLANES), :] = jnp.where(i < nt - 1, halo_ref[...], 0.0)
            da = w_ref[3:4, :] * ext[pl.ds(0, bt), :]
            for k in range(GDN_CONV - 1):
                da += w_ref[k:k + 1, :] * ext[pl.ds(GDN_CONV - 1 - k, bt), :]
            o_ref[...] = da.astype(o_ref.dtype)

        @pl.when(j >= J)
        def _():
            o_ref[...] = dg_ref[...]

    jc = lambda j: jnp.minimum(j, J - 1)
    return pl.pallas_call(
        body, grid=(nt, J2),
        in_specs=[BS((bt, cb), lambda i, j: (i, jc(j))),
                  BS((SUBLANES, cb), lambda i, j: (jnp.minimum((i + 1) * (bt // SUBLANES), T // SUBLANES - 1), jc(j))),
                  BS((SUBLANES, cb), lambda i, j: (0, jc(j))),
                  BS((bt, cb), lambda i, j: (i, jnp.maximum(j - J, 0)))],
        out_specs=BS((bt, cb), lambda i, j: (i, j)), out_shape=S((T, C + V), MXU_DTYPE),
        scratch_shapes=[pltpu.VMEM((bt + SUBLANES, cb), F32)],
        compiler_params=_cp(("arbitrary", "arbitrary")), name=name)(du, du, conv_w, dgate)


def _softplus(x):
    return jnp.maximum(x, 0.0) + jnp.log1p(jnp.exp(-jnp.abs(x)))


def gdn_gate_fwd(a, b, a_log, dt_bias, name):
    T, HV = a.shape
    bt = _pick(T, (1024, 512, 256, 128, 64))

    def body(a_ref, b_ref, al_ref, dt_ref, g_ref, be_ref):
        g_ref[...] = -jnp.exp(al_ref[...]) * _softplus(a_ref[...] + dt_ref[...])
        be_ref[...] = _sigmoid(b_ref[...])

    row = BS((bt, HV), lambda i: (i, 0))
    vec = BS((1, HV), lambda i: (0, 0))
    return pl.pallas_call(body, grid=(T // bt,), in_specs=[row, row, vec, vec], out_specs=[row, row],
                          out_shape=[S((T, HV), F32)] * 2, compiler_params=_cp(("parallel",)), name=name)(
                              a, b, a_log, dt_bias)


def gdn_gate_bwd(a, b, a_log, dt_bias, dg, dbeta, name):
    T, HV = a.shape
    bt = _pick(T, (1024, 512, 256, 128, 64))

    def body(a_ref, b_ref, al_ref, dt_ref, dg_ref, dbe_ref, da_ref, db_ref, dal_ref, ddt_ref):
        i = pl.program_id(0)
        x = a_ref[...] + dt_ref[...]
        ea = jnp.exp(al_ref[...])
        dgv = dg_ref[...]
        da = dgv * (-ea * _sigmoid(x))
        da_ref[...] = da
        be = _sigmoid(b_ref[...])
        db_ref[...] = dbe_ref[...] * be * (1.0 - be)

        @pl.when(i == 0)
        def _():
            dal_ref[...] = jnp.zeros_like(dal_ref)
            ddt_ref[...] = jnp.zeros_like(ddt_ref)

        dal_ref[...] += jnp.sum(dgv * (-ea * _softplus(x)), axis=0, keepdims=True)
        ddt_ref[...] += jnp.sum(da, axis=0, keepdims=True)

    row = BS((bt, HV), lambda i: (i, 0))
    vec = BS((1, HV), lambda i: (0, 0))
    return pl.pallas_call(body, grid=(T // bt,), in_specs=[row, row, vec, vec, row, row],
                          out_specs=[row, row, vec, vec],
                          out_shape=[S((T, HV), F32)] * 2 + [S((1, HV), F32)] * 2,
                          compiler_params=_cp(("arbitrary",)), name=name)(a, b, a_log, dt_bias, dg, dbeta)


def _col(row_vec, eye):
    return jnp.sum(jnp.where(eye, row_vec, 0.0), axis=1, keepdims=True)


def _row(col_vec, eye):
    return jnp.sum(jnp.where(eye, col_vec, 0.0), axis=0, keepdims=True)


def _delta_chunk(q, k, v, g_row, b_row):
    C = CHUNK
    ii = lax.broadcasted_iota(jnp.int32, (C, C), 0)
    jj = lax.broadcasted_iota(jnp.int32, (C, C), 1)
    eye, causal, strict = ii == jj, ii >= jj, ii > jj
    g_col = _col(g_row, eye)
    gc_row = jnp.sum(jnp.where(ii <= jj, g_col, 0.0), axis=0, keepdims=True)
    gc_col = _col(gc_row, eye)
    gl = jnp.sum(jnp.where(jj[0:1, :] == C - 1, gc_row, 0.0), axis=1, keepdims=True)
    decay = jnp.where(causal, jnp.exp(jnp.where(causal, gc_col - gc_row, 0.0)), 0.0)
    b_col = _col(b_row, eye)
    kb = k * b_col
    vb = v * b_col
    m = dot_nt(kb, k)
    a = jnp.where(strict, m * decay, 0.0)
    tinv = jnp.where(eye, 1.0, 0.0) - a
    pw = dot_nn(a, a, hi=True)
    for step in range(5):
        tinv = tinv + dot_nn(tinv, pw, hi=True)
        if step < 4:
            pw = dot_nn(pw, pw, hi=True)
    egc = jnp.exp(gc_col)
    kbg = kb * egc
    u = dot_nn(tinv, vb, hi=True)
    w = dot_nn(tinv, kbg, hi=True)
    n = dot_nt(q, k)
    attn = jnp.where(causal, n * decay, 0.0)
    qg = q * egc
    ekl = jnp.exp(gl - gc_col)
    ks = k * ekl
    dec = jnp.exp(gl)
    return dict(eye=eye, causal=causal, strict=strict, ii=ii, jj=jj, gc_col=gc_col, gl=gl, decay=decay,
                b_col=b_col, kb=kb, vb=vb, m=m, tinv=tinv, egc=egc, kbg=kbg, u=u, w=w, n=n, attn=attn,
                qg=qg, ekl=ekl, ks=ks, dec=dec)


def delta_fwd(qkvn, g_rows, b_rows, key_dim, name):
    T = qkvn.shape[0]
    NK = key_dim // HEAD
    HV = g_rows.shape[0]
    rep = HV // NK
    NC = T // CHUNK

    def body(q_ref, k_ref, v_ref, g_ref, b_ref, o_ref, st_ref, state):
        n = pl.program_id(1)

        @pl.when(n == 0)
        def _():
            state[...] = jnp.zeros_like(state)

        c = _delta_chunk(q_ref[...], k_ref[...], v_ref[...], g_ref[0, 0], b_ref[0, 0])
        s0 = state[...]
        st_ref[0, 0] = s0
        vn = c["u"] - dot_nn(c["w"], s0)
        o_ref[...] = dot_nn(c["qg"], s0) + dot_nn(c["attn"], vn)
        state[...] = s0 * c["dec"] + dot_tn(c["ks"], vn)

    vrow = BS((1, 1, 1, CHUNK), lambda h, n: (h, n, 0, 0))
    return pl.pallas_call(
        body, grid=(HV, NC),
        in_specs=[BS((CHUNK, HEAD), lambda h, n: (n, lax.div(h, rep))),
                  BS((CHUNK, HEAD), lambda h, n: (n, NK + lax.div(h, rep))),
                  BS((CHUNK, HEAD), lambda h, n: (n, 2 * NK + h)), vrow, vrow],
        out_specs=[BS((CHUNK, HEAD), lambda h, n: (n, h)), BS((1, 1, HEAD, HEAD), lambda h, n: (h, n, 0, 0))],
        out_shape=[S((T, HV * HEAD), F32), S((HV, NC, HEAD, HEAD), F32)],
        scratch_shapes=[pltpu.VMEM((HEAD, HEAD), F32)],
        compiler_params=_cp(("arbitrary", "arbitrary")), name=name)(qkvn, qkvn, qkvn, g_rows, b_rows)


def delta_bwd(qkvn, g_rows, b_rows, states, do, key_dim, name):
    T = qkvn.shape[0]
    NK = key_dim // HEAD
    HV = g_rows.shape[0]
    rep = HV // NK
    NC = T // CHUNK

    def body(q_ref, k_ref, v_ref, g_ref, b_ref, st_ref, do_ref, dq_ref, dk_ref, dv_ref, dg_ref, dbe_ref, dstate):
        step = pl.program_id(1)

        @pl.when(step == 0)
        def _():
            dstate[...] = jnp.zeros_like(dstate)

        q, k, v = q_ref[...], k_ref[...], v_ref[...]
        c = _delta_chunk(q, k, v, g_ref[0, 0], b_ref[0, 0])
        eye, causal, strict = c["eye"], c["causal"], c["strict"]
        s0 = st_ref[0, 0]
        dsn = dstate[...]
        dov = do_ref[...]
        vn = c["u"] - dot_nn(c["w"], s0)
        dvn = dot_tn(c["attn"], dov) + dot_nn(c["ks"], dsn)
        dattn = dot_nt(dov, vn)
        dqg = dot_nt(dov, s0)
        dks = dot_nt(vn, dsn)
        ddec = jnp.sum(jnp.sum(s0 * dsn, axis=1, keepdims=True), axis=0, keepdims=True)
        dw = -dot_nt(dvn, s0)
        dstate[...] = dot_tn(c["qg"], dov) + dsn * c["dec"] - dot_tn(c["w"], dvn)
        tinv = c["tinv"]
        dvb = dot_tn(tinv, dvn, hi=True)
        dkbg = dot_tn(tinv, dw, hi=True)
        dt = dot_nt(dvn, c["vb"], hi=True) + dot_nt(dw, c["kbg"], hi=True)
        da = -dot_tn(tinv, dot_nt(dt, tinv, hi=True), hi=True)
        da = jnp.where(strict, da, 0.0)
        dm = da * c["decay"]
        dattn = jnp.where(causal, dattn, 0.0)
        dn = dattn * c["decay"]
        e = (da * c["m"] + dattn * c["n"]) * c["decay"]
        dkb = dot_nn(dm, k) + dkbg * c["egc"]
        dk = dot_tn(dm, c["kb"]) + dot_tn(dn, q) + dks * c["ekl"] + dkb * c["b_col"]
        dq = dot_nn(dn, k) + dqg * c["egc"]
        dks_ks = jnp.sum(dks * c["ks"], axis=1, keepdims=True)
        dgc_col = (jnp.sum(e, axis=1, keepdims=True) + jnp.sum(dkbg * c["kbg"], axis=1, keepdims=True)
                   + jnp.sum(dqg * c["qg"], axis=1, keepdims=True) - dks_ks)
        dgc_col = dgc_col - _col(jnp.sum(e, axis=0, keepdims=True), eye)
        dgl = jnp.sum(dks_ks, axis=0, keepdims=True) + ddec * c["dec"]
        dgc_col = dgc_col + jnp.where(c["ii"][:, 0:1] == CHUNK - 1, dgl, 0.0)
        dq_ref[...] = dq
        dk_ref[...] = dk
        dv_ref[...] = dvb * c["b_col"]
        dbe_col = jnp.sum(dvb * v, axis=1, keepdims=True) + jnp.sum(dkb * k, axis=1, keepdims=True)
        dbe_ref[0, 0] = _row(dbe_col, eye)
        dg_ref[0, 0] = jnp.sum(jnp.where(causal, dgc_col, 0.0), axis=0, keepdims=True)

    rev = lambda n: NC - 1 - n
    vrow = BS((1, 1, 1, CHUNK), lambda h, n: (h, rev(n), 0, 0))
    vblk = BS((CHUNK, HEAD), lambda h, n: (rev(n), h))
    return pl.pallas_call(
        body, grid=(HV, NC),
        in_specs=[BS((CHUNK, HEAD), lambda h, n: (rev(n), lax.div(h, rep))),
                  BS((CHUNK, HEAD), lambda h, n: (rev(n), NK + lax.div(h, rep))),
                  BS((CHUNK, HEAD), lambda h, n: (rev(n), 2 * NK + h)), vrow, vrow,
                  BS((1, 1, HEAD, HEAD), lambda h, n: (h, rev(n), 0, 0)), vblk],
        out_specs=[vblk, vblk, vblk, vrow, vrow],
        out_shape=[S((T, HV * HEAD), F32)] * 3 + [S((HV, NC, 1, CHUNK), F32)] * 2,
        scratch_shapes=[pltpu.VMEM((HEAD, HEAD), F32)],
        compiler_params=_cp(("arbitrary", "arbitrary")), name=name)(qkvn, qkvn, qkvn, g_rows, b_rows, states, do)


def gdn_post_fwd(o, proj, gate_col0, o_gain, name):
    T, V = o.shape
    cb = min(1024, V)
    j0 = gate_col0 // cb
    bt = _pick(T, (256, 128, 64))

    def body(o_ref, g_ref, gain_ref, y_ref):
        for hh in range(cb // HEAD):
            cols = pl.ds(hh * HEAD, HEAD)
            ov = o_ref[:, cols]
            gt = g_ref[:, cols]
            r = lax.rsqrt(jnp.mean(ov * ov, axis=-1, keepdims=True) + EPS)
            y_ref[:, cols] = (((ov * r) * gain_ref[...]) * (gt * _sigmoid(gt))).astype(y_ref.dtype)

    return pl.pallas_call(
        body, grid=(T // bt, V // cb),
        in_specs=[BS((bt, cb), lambda i, j: (i, j)), BS((bt, cb), lambda i, j: (i, j0 + j)),
                  BS((1, HEAD), lambda i, j: (0, 0))],
        out_specs=BS((bt, cb), lambda i, j: (i, j)), out_shape=S((T, V), MXU_DTYPE),
        compiler_params=_cp(("parallel", "parallel")), name=name)(o, proj, o_gain)


def gdn_post_bwd(o, proj, gate_col0, o_gain, dy, name):
    T, V = o.shape
    cb = min(1024, V)
    j0 = gate_col0 // cb
    bt = _pick(T, (256, 128, 64))

    def body(o_ref, g_ref, gain_ref, dy_ref, do_ref, dgt_ref, dgain_ref):
        i, j = pl.program_id(0), pl.program_id(1)

        @pl.when(jnp.logical_and(i == 0, j == 0))
        def _():
            dgain_ref[...] = jnp.zeros_like(dgain_ref)

        gain = gain_ref[...]
        for hh in range(cb // HEAD):
            cols = pl.ds(hh * HEAD, HEAD)
            ov = o_ref[:, cols]
            gt = g_ref[:, cols]
            dyv = dy_ref[:, cols]
            r = lax.rsqrt(jnp.mean(ov * ov, axis=-1, keepdims=True) + EPS)
            ohat = ov * r
            sg = _sigmoid(gt)
            dyn = dyv * (gt * sg)
            dgt_ref[:, cols] = (dyv * (ohat * gain) * (sg * (1.0 + gt * (1.0 - sg)))).astype(dgt_ref.dtype)
            dgain_ref[...] += jnp.sum(dyn * ohat, axis=0, keepdims=True)
            dh = dyn * gain
            do_ref[:, cols] = r * (dh - ohat * jnp.mean(dh * ohat, axis=-1, keepdims=True))

    blk = BS((bt, cb), lambda i, j: (i, j))
    vec = BS((1, HEAD), lambda i, j: (0, 0))
    return pl.pallas_call(
        body, grid=(T // bt, V // cb),
        in_specs=[blk, BS((bt, cb), lambda i, j: (i, j0 + j)), vec, blk],
        out_specs=[blk, blk, vec], out_shape=[S((T, V), F32), S((T, V), MXU_DTYPE), S((1, HEAD), F32)],
        compiler_params=_cp(("arbitrary", "arbitrary")), name=name)(o, proj, o_gain, dy)


def adamw(w, g, m, v, name):
    shape = w.shape
    n = 1
    for d in shape:
        n *= d
    cols = shape[-1] if len(shape) > 1 else n
    rows = n // cols
    br = rows
    for cand in (512, 256, 128, 64, 32, 16, 8):
        if rows % cand == 0 and cand * cols * 4 * 14 <= 36 * MIB:
            br = cand
            break
    c1 = 1.0 - ADAM_B1 ** ADAM_STEP
    c2 = 1.0 - ADAM_B2 ** ADAM_STEP

    def body(w_ref, g_ref, m_ref, v_ref, d_ref, nm_ref, nv_ref):
        gv = g_ref[...]
        nm = ADAM_B1 * m_ref[...] + (1.0 - ADAM_B1) * gv
        nv = ADAM_B2 * v_ref[...] + (1.0 - ADAM_B2) * (gv * gv)
        nm_ref[...] = nm
        nv_ref[...] = nv
        d_ref[...] = -ADAM_LR * ((nm / c1) / (jnp.sqrt(nv / c2) + ADAM_EPS) + ADAM_WD * w_ref[...])

    blk = BS((br, cols), lambda i: (i, 0))
    outs = pl.pallas_call(
        body, grid=(rows // br,), in_specs=[blk] * 4, out_specs=[blk] * 3,
        out_shape=[S((rows, cols), F32)] * 3, compiler_params=_cp(("parallel",), 48), name=name)(
            *[t.reshape(rows, cols) for t in (w, g, m, v)])
    return [t.reshape(shape) for t in outs]


def _rel_index():
    import numpy as np
    rel = BAND_LEFT + np.arange(CHUNK)[:, None] - np.arange(BAND)[None, :]
    return np.clip(rel, -(CHUNK - 1), MAX_REL) + (CHUNK - 1)


def _ffn_fwd(x, gain, p, tag):
    (h2,) = norm_fwd(x, gain, [MXU_DTYPE], f"{tag}_norm")
    up = matmul(h2, p["w_up"], "nn", name=f"{tag}_up")
    act = ffn_act_fwd(up, p["conv"], f"{tag}_act")
    out = matmul(act, p["w_down"], "nn", res=x, name=f"{tag}_down")
    return out, (x, h2, up, act)


def _ffn_bwd(dx, saved, gain, p, tag):
    x, h2, up, act = saved
    g_act = matmul(dx, p["w_down"], "nt", name=f"{tag}_bdown")
    d_down = matmul(act, dx, "tn", name=f"{tag}_wdown")
    du, db, d_conv = ffn_act_bwd_a(up, p["conv"], g_act, f"{tag}_bact_a")
    dup = ffn_act_bwd_b(du, db, p["conv"], f"{tag}_bact_b")
    dh2 = matmul(dup, p["w_up"], "nt", name=f"{tag}_bup")
    d_up = matmul(h2, dup, "tn", name=f"{tag}_wup")
    dx, d_gain = norm_bwd(x, gain, dh2, dx, f"{tag}_bnorm")
    return dx, d_gain, dict(w_up=d_up, conv=d_conv, w_down=d_down)


def _att_fwd(x, gain, p, tag):
    (h,) = norm_fwd(x, gain, [MXU_DTYPE], f"{tag}_norm")
    qkv = matmul(h, p["w_qkv"], "nn", name=f"{tag}_qkv")
    bias = p["rel_bias"][:, _rel_index()]
    o = attn_fwd(qkv, p["q_gain"], p["k_gain"], bias, f"{tag}_core")
    out = matmul(o, p["w_o"], "nn", res=x, name=f"{tag}_out")
    return out, (x, h, qkv, o, bias)


def _att_bwd(dx, saved, gain, p, tag):
    x, h, qkv, o, bias = saved
    do = matmul(dx, p["w_o"], "nt", name=f"{tag}_bout")
    d_wo = matmul(o, dx, "tn", name=f"{tag}_wout")
    dq, dk, dv, d_gq, d_gk, dbias = attn_bwd(qkv, do, p["q_gain"], p["k_gain"], bias, f"{tag}_bcore")
    dqkv = jnp.concatenate([dq, dk, dv], axis=1)
    dh = matmul(dqkv, p["w_qkv"], "nt", name=f"{tag}_bqkv")
    d_wqkv = matmul(h, dqkv, "tn", name=f"{tag}_wqkv")
    d_rb = rel_bias_reduce(dbias.transpose(1, 0, 2), p["rel_bias"].shape[1], f"{tag}_brel")
    dx, d_gain = norm_bwd(x, gain, dh, dx, f"{tag}_bnorm")
    return dx, d_gain, dict(w_qkv=d_wqkv, w_o=d_wo, q_gain=d_gq, k_gain=d_gk, rel_bias=d_rb)


def _pool_fwd(x, gain, p, tag):
    (hf,) = norm_fwd(x, gain, [F32], f"{tag}_norm")
    out, pooled = pool_fwd(hf, x, p["w"], p["scale"], f"{tag}_core")
    return out, (x, pooled)


def _pool_bwd(dx, saved, gain, p, tag):
    x, pooled = saved
    dpooled, d_w, d_scale = pool_bwd_a(dx, pooled, p["w"], p["scale"], f"{tag}_bcore_a")
    dh = pool_bwd_b(dpooled, f"{tag}_bcore_b")
    dx, d_gain = norm_bwd(x, gain, dh, dx, f"{tag}_bnorm")
    return dx, d_gain, dict(w=d_w, scale=d_scale)


def _rows_layout(t, hv):
    return t.T.reshape(hv, t.shape[0] // CHUNK, 1, CHUNK)


def _gdn_fwd(x, gain, p, tag):
    T = x.shape[0]
    hv = p["a_log"].shape[1]
    key_dim = p["key_dim"]
    C = p["conv"].shape[1]
    (h,) = norm_fwd(x, gain, [MXU_DTYPE], f"{tag}_norm")
    proj = matmul(h, p["w_main"], "nn", name=f"{tag}_in")
    ab = matmul(h, p["w_ab"], "nn", name=f"{tag}_in_ab")
    a, b = ab[:, :hv], ab[:, hv:2 * hv]
    qkvn = gdn_pre_fwd(proj, p["conv"], key_dim, f"{tag}_pre")
    g, beta = gdn_gate_fwd(a, b, p["a_log"], p["dt_bias"], f"{tag}_gate")
    g_rows, b_rows = _rows_layout(g, hv), _rows_layout(beta, hv)
    o, states = delta_fwd(qkvn, g_rows, b_rows, key_dim, f"{tag}_delta")
    y = gdn_post_fwd(o, proj, C, p["o_gain"], f"{tag}_post")
    out = matmul(y, p["w_o"], "nn", res=x, name=f"{tag}_out")
    return out, (x, h, proj, a, b, qkvn, g_rows, b_rows, o, states, y)


def _gdn_bwd(dx, saved, gain, p, tag):
    x, h, proj, a, b, qkvn, g_rows, b_rows, o, states, y = saved
    T = x.shape[0]
    hv = p["a_log"].shape[1]
    key_dim = p["key_dim"]
    C = p["conv"].shape[1]
    dy = matmul(dx, p["w_o"], "nt", name=f"{tag}_bout")
    d_wo = matmul(y, dx, "tn", name=f"{tag}_wout")
    do, dgate, d_ogain = gdn_post_bwd(o, proj, C, p["o_gain"], dy, f"{tag}_bpost")
    dq_v, dk_v, dv, dg_rows, dbe_rows = delta_bwd(qkvn, g_rows, b_rows, states, do, key_dim, f"{tag}_bdelta")
    dg = dg_rows.reshape(hv, T).T
    dbeta = dbe_rows.reshape(hv, T).T
    da, db, d_alog, d_dtb = gdn_gate_bwd(a, b, p["a_log"], p["dt_bias"], dg, dbeta, f"{tag}_bgate")
    du, d_conv = gdn_pre_bwd_a(proj, p["conv"], dq_v, dk_v, dv, key_dim, f"{tag}_bpre_a")
    dproj = gdn_pre_bwd_b(du, dgate, p["conv"], key_dim, f"{tag}_bpre_b")
    dab = jnp.concatenate([da, db, jnp.zeros((T, LANES - 2 * hv), F32)], axis=1)
    dh = matmul(dproj, p["w_main"], "nt", name=f"{tag}_bin")
    dh = matmul(dab, p["w_ab"], "nt", res=dh, name=f"{tag}_bin_ab")
    d_main = matmul(h, dproj, "tn", name=f"{tag}_win")
    d_ab = matmul(h, dab, "tn", name=f"{tag}_win_ab")
    dx, d_gain = norm_bwd(x, gain, dh, dx, f"{tag}_bnorm")
    return dx, d_gain, dict(w_main=d_main, w_ab=d_ab, conv=d_conv, a_log=d_alog, dt_bias=d_dtb,
                            o_gain=d_ogain, w_o=d_wo)


_MIXERS = ((_att_fwd, _att_bwd), (_pool_fwd, _pool_bwd), (_gdn_fwd, _gdn_bwd))


def local_step(x, target, W):
    depth = len(W["ffn"])
    saved = []
    for i in range(depth):
        kind, j = i % 3, i // 3
        mp = (W["att"], W["pool"], W["gdn"])[kind][j]
        x, s_mix = _MIXERS[kind][0](x, W["mix_norm"][i:i + 1], mp, f"l{i}_mix")
        x, s_ffn = _ffn_fwd(x, W["ffn_norm"][i:i + 1], W["ffn"][i], f"l{i}_ffn")
        saved.append((s_mix, s_ffn))
    loss, dx = loss_and_grad(x, target, "loss")
    G = dict(mix_norm=[None] * depth, ffn_norm=[None] * depth, ffn=[None] * depth,
             att=[None] * len(W["att"]), pool=[None] * len(W["pool"]), gdn=[None] * len(W["gdn"]))
    for i in reversed(range(depth)):
        kind, j = i % 3, i // 3
        s_mix, s_ffn = saved[i]
        dx, G["ffn_norm"][i], G["ffn"][i] = _ffn_bwd(dx, s_ffn, W["ffn_norm"][i:i + 1], W["ffn"][i], f"l{i}_ffn")
        mp = (W["att"], W["pool"], W["gdn"])[kind][j]
        dx, G["mix_norm"][i], gm = _MIXERS[kind][1](dx, s_mix, W["mix_norm"][i:i + 1], mp, f"l{i}_mix")
        G[("att", "pool", "gdn")[kind]][j] = gm
    return loss, dx, G


SHARD_AXIS = dict(att_w_qkv=2, att_w_o=1, pool_w=2, gdn_w_in=2, gdn_w_o=1, ffn_w_up=2, ffn_w_down=1,
                  att_rel_bias=2, gdn_conv=2, ffn_conv=2)
BIG = ("att_w_qkv", "att_w_o", "pool_w", "gdn_w_in", "gdn_w_o", "ffn_w_up", "ffn_w_down")
SMALL_SHARDED = ("att_rel_bias", "gdn_conv", "ffn_conv")
REPLICATED = ("mix_norm", "ffn_norm", "att_q_gain", "att_k_gain", "pool_scale", "gdn_a_log", "gdn_dt_bias",
              "gdn_o_gain")
WEIGHTS = ("mix_norm", "ffn_norm", "att_w_qkv", "att_q_gain", "att_k_gain", "att_rel_bias", "att_w_o", "pool_w",
           "pool_scale", "gdn_w_in", "gdn_conv", "gdn_a_log", "gdn_dt_bias", "gdn_o_gain", "gdn_w_o", "ffn_w_up",
           "ffn_conv", "ffn_w_down")


def _merge(stacked, axis):
    t = jnp.moveaxis(stacked, 0, axis)
    return t.reshape(t.shape[:axis] + (t.shape[axis] * t.shape[axis + 1],) + t.shape[axis + 2:])


def _split(full, axis):
    n = full.shape[axis] // N_CHIPS
    t = full.reshape(full.shape[:axis] + (N_CHIPS, n) + full.shape[axis + 1:])
    return jnp.moveaxis(t, axis, 0)


def _pad_to(t, axis, size):
    pad = [(0, 0)] * t.ndim
    pad[axis] = (0, size - t.shape[axis])
    return jnp.pad(t, pad)


def _round_up(n, m):
    return (n + m - 1) // m * m


def build_weights(full):
    depth = full["ffn_w_up"].shape[0]
    F = full["ffn_conv"].shape[2]
    Fp = _round_up(F, FF_ALIGN)
    W = dict(mix_norm=full["mix_norm"], ffn_norm=full["ffn_norm"], att=[], pool=[], gdn=[], ffn=[])
    for i in range(depth):
        wu = full["ffn_w_up"][i]
        W["ffn"].append(dict(
            w_up=jnp.concatenate([_pad_to(wu[:, :F], 1, Fp), _pad_to(wu[:, F:], 1, Fp)], axis=1),
            conv=_pad_to(_pad_to(full["ffn_conv"][i], 1, Fp), 0, SUBLANES),
            w_down=_pad_to(full["ffn_w_down"][i], 0, Fp)))
    for j in range(full["att_w_qkv"].shape[0]):
        W["att"].append(dict(w_qkv=full["att_w_qkv"][j], w_o=full["att_w_o"][j], q_gain=full["att_q_gain"][j:j + 1],
                             k_gain=full["att_k_gain"][j:j + 1], rel_bias=full["att_rel_bias"][j]))
    for j in range(full["pool_w"].shape[0]):
        W["pool"].append(dict(w=full["pool_w"][j], scale=full["pool_scale"][j:j + 1]))
    for j in range(full["gdn_w_in"].shape[0]):
        C = full["gdn_conv"].shape[2]
        V = full["gdn_w_o"].shape[1]
        w_in = full["gdn_w_in"][j]
        W["gdn"].append(dict(
            w_main=w_in[:, :C + V], w_ab=_pad_to(w_in[:, C + V:], 1, LANES),
            conv=_pad_to(full["gdn_conv"][j], 0, SUBLANES), a_log=full["gdn_a_log"][j:j + 1],
            dt_bias=full["gdn_dt_bias"][j:j + 1], o_gain=full["gdn_o_gain"][j:j + 1], w_o=full["gdn_w_o"][j],
            key_dim=(C - V) // 2))
    return W


def full_grads(G, full):
    F = full["ffn_conv"].shape[2]
    Fp = _round_up(F, FF_ALIGN)
    hv = full["gdn_a_log"].shape[1]
    out = dict(
        mix_norm=jnp.concatenate(G["mix_norm"], axis=0), ffn_norm=jnp.concatenate(G["ffn_norm"], axis=0),
        ffn_w_up=jnp.stack([jnp.concatenate([g["w_up"][:, :F], g["w_up"][:, Fp:Fp + F]], axis=1) for g in G["ffn"]]),
        ffn_conv=jnp.stack([g["conv"][:FFN_CONV, :F] for g in G["ffn"]]),
        ffn_w_down=jnp.stack([g["w_down"][:F] for g in G["ffn"]]),
        att_w_qkv=jnp.stack([g["w_qkv"] for g in G["att"]]), att_w_o=jnp.stack([g["w_o"] for g in G["att"]]),
        att_q_gain=jnp.concatenate([g["q_gain"] for g in G["att"]], axis=0),
        att_k_gain=jnp.concatenate([g["k_gain"] for g in G["att"]], axis=0),
        att_rel_bias=jnp.stack([g["rel_bias"] for g in G["att"]]),
        pool_w=jnp.stack([g["w"] for g in G["pool"]]),
        pool_scale=jnp.concatenate([g["scale"] for g in G["pool"]], axis=0),
        gdn_w_in=jnp.stack([jnp.concatenate([g["w_main"], g["w_ab"][:, :2 * hv]], axis=1) for g in G["gdn"]]),
        gdn_conv=jnp.stack([g["conv"][:GDN_CONV] for g in G["gdn"]]),
        gdn_a_log=jnp.concatenate([g["a_log"] for g in G["gdn"]], axis=0),
        gdn_dt_bias=jnp.concatenate([g["dt_bias"] for g in G["gdn"]], axis=0),
        gdn_o_gain=jnp.concatenate([g["o_gain"] for g in G["gdn"]], axis=0),
        gdn_w_o=jnp.stack([g["w_o"] for g in G["gdn"]]))
    return out


ANY = BS(memory_space=pl.ANY)
PACK_COLS = 1024
PACK_ROWS = 32


def _place():
    x, y, c = lax.axis_index("x"), lax.axis_index("y"), lax.axis_index("c")
    chips = [(1 - x, y), (x, 1 - y), (1 - x, 1 - y)]
    return x, y, c, chips


def _remote(src, dst, send_sem, recv_sem, to):
    return pltpu.make_async_remote_copy(src_ref=src, dst_ref=dst, send_sem=send_sem, recv_sem=recv_sem,
                                        device_id=to, device_id_type=MESH)


def gather_chips(shard, name):
    R, C = shard.shape
    half = R // 2

    def body(x_ref, o_ref, send_sems, recv_sems, local_sem):
        x, y, c, chips = _place()
        mine_rows = pl.ds(c * half, half)
        other_rows = pl.ds((1 - c) * half, half)
        own = pltpu.make_async_copy(x_ref, o_ref.at[2 * x + y], local_sem)
        own.start()
        first = [_remote(x_ref.at[mine_rows], o_ref.at[2 * x + y, mine_rows], send_sems.at[j], recv_sems.at[j],
                         (cx, cy, c)) for j, (cx, cy) in enumerate(chips)]
        for cp in first:
            cp.start()
        passed = []
        for j, (cx, cy) in enumerate(chips):
            landed = o_ref.at[2 * cx + cy, mine_rows]
            _remote(landed, landed, send_sems.at[j], recv_sems.at[j], (cx, cy, c)).wait_recv()
            cp = _remote(landed, landed, send_sems.at[3 + j], recv_sems.at[3 + j], (x, y, 1 - c))
            cp.start()
            passed.append(cp)
        for j, (cx, cy) in enumerate(chips):
            landed = o_ref.at[2 * cx + cy, other_rows]
            _remote(landed, landed, send_sems.at[3 + j], recv_sems.at[3 + j], (x, y, 1 - c)).wait_recv()
        for cp in first + passed:
            cp.wait_send()
        own.wait()

    return pl.pallas_call(
        body, out_shape=S((N_CHIPS, R, C), shard.dtype), in_specs=[ANY], out_specs=ANY,
        scratch_shapes=[pltpu.SemaphoreType.DMA((6,)), pltpu.SemaphoreType.DMA((6,)), pltpu.SemaphoreType.DMA],
        name=name)(shard)


def swap_halves(g, name):
    n, R, C = g.shape
    half = R // 2

    def body(g_ref, o_ref, send_sem, recv_sem):
        x, y, c, _ = _place()
        cp = _remote(g_ref.at[:, pl.ds((1 - c) * half, half)], o_ref, send_sem, recv_sem, (x, y, 1 - c))
        cp.start()
        cp.wait()

    return pl.pallas_call(
        body, out_shape=S((n, half, C), g.dtype), in_specs=[ANY], out_specs=ANY,
        scratch_shapes=[pltpu.SemaphoreType.DMA, pltpu.SemaphoreType.DMA], name=name)(g)


def add_halves(g, other, c, name):
    n, R, C = g.shape
    half = R // 2
    br = _pick(half, (512, 256, 128, 64, 32, 16))
    nb = half // br

    def body(c_ref, g_ref, o_ref, out_ref):
        out_ref[...] = (g_ref[...] + o_ref[...]).astype(out_ref.dtype)

    return pl.pallas_call(
        body, grid_spec=pltpu.PrefetchScalarGridSpec(
            num_scalar_prefetch=1, grid=(n, nb),
            in_specs=[BS((1, br, C), lambda s, i, c_ref: (s, c_ref[0] * nb + i, 0)),
                      BS((1, br, C), lambda s, i, c_ref: (s, i, 0))],
            out_specs=BS((1, br, C), lambda s, i, c_ref: (s, i, 0))),
        out_shape=S((n, half, C), BF16), compiler_params=_cp(("parallel", "parallel")), name=name)(
            c.reshape(1), g, other)


def scatter_chips(p, name):
    n, h, C = p.shape

    def body(p_ref, o_ref, send_sems, recv_sems):
        x, y, c, chips = _place()
        sent = [_remote(p_ref.at[2 * cx + cy], o_ref.at[j], send_sems.at[j], recv_sems.at[j], (cx, cy, c))
                for j, (cx, cy) in enumerate(chips)]
        for cp in sent:
            cp.start()
        for cp in sent:
            cp.wait()

    return pl.pallas_call(
        body, out_shape=S((n - 1, h, C), p.dtype), in_specs=[ANY], out_specs=ANY,
        scratch_shapes=[pltpu.SemaphoreType.DMA((3,)), pltpu.SemaphoreType.DMA((3,))], name=name)(p)


def add_chips(p, got, s_me, name):
    n, h, C = p.shape
    br = _pick(h, (512, 256, 128, 64, 32, 16))

    def body(s_ref, p_ref, g_ref, out_ref):
        acc = p_ref[0].astype(F32)
        for j in range(n - 1):
            acc += g_ref[j].astype(F32)
        out_ref[...] = acc

    return pl.pallas_call(
        body, grid_spec=pltpu.PrefetchScalarGridSpec(
            num_scalar_prefetch=1, grid=(h // br,),
            in_specs=[BS((1, br, C), lambda i, s_ref: (s_ref[0], i, 0)),
                      BS((n - 1, br, C), lambda i, s_ref: (0, i, 0))],
            out_specs=BS((br, C), lambda i, s_ref: (i, 0))),
        out_shape=S((h, C), F32), compiler_params=_cp(("parallel",)), name=name)(s_me.reshape(1), p, got)


def join_halves(r, name):
    h, C = r.shape

    def body(r_ref, o_ref, send_sem, recv_sem, local_sem):
        x, y, c, _ = _place()
        mine = o_ref.at[pl.ds(c * h, h)]
        own = pltpu.make_async_copy(r_ref, mine, local_sem)
        own.start()
        cp = _remote(r_ref, mine, send_sem, recv_sem, (x, y, 1 - c))
        cp.start()
        cp.wait_send()
        theirs = o_ref.at[pl.ds((1 - c) * h, h)]
        _remote(r_ref, theirs, send_sem, recv_sem, (x, y, 1 - c)).wait_recv()
        own.wait()

    return pl.pallas_call(
        body, out_shape=S((2 * h, C), r.dtype), in_specs=[ANY], out_specs=ANY,
        scratch_shapes=[pltpu.SemaphoreType.DMA, pltpu.SemaphoreType.DMA, pltpu.SemaphoreType.DMA],
        name=name)(r)


def sum_devices(v, name):
    R, C = v.shape

    def body(v_ref, o_ref, slots, send_sems, recv_sems):
        x, y, c, _ = _place()
        me = 4 * x + 2 * y + c
        slots[me] = v_ref[...]
        sent = []
        for r in range(1, 8):
            peer = (x ^ (r >> 2), y ^ ((r >> 1) & 1), c ^ (r & 1))
            cp = _remote(v_ref, slots.at[me], send_sems.at[r - 1], recv_sems.at[r - 1], peer)
            cp.start()
            sent.append(cp)
        for r in range(1, 8):
            peer = (x ^ (r >> 2), y ^ ((r >> 1) & 1), c ^ (r & 1))
            theirs = slots.at[4 * peer[0] + 2 * peer[1] + peer[2]]
            _remote(v_ref, theirs, send_sems.at[r - 1], recv_sems.at[r - 1], peer).wait_recv()
        for cp in sent:
            cp.wait_send()
        acc = slots[0]
        for k in range(1, 8):
            acc += slots[k]
        o_ref[...] = acc

    vm = BS(memory_space=pltpu.VMEM)
    return pl.pallas_call(
        body, out_shape=S((R, C), F32), in_specs=[vm], out_specs=vm,
        scratch_shapes=[pltpu.VMEM((8, R, C), F32), pltpu.SemaphoreType.DMA((7,)), pltpu.SemaphoreType.DMA((7,))],
        compiler_params=pltpu.CompilerParams(vmem_limit_bytes=32 * MIB), name=name)(v)


def _pack(arrays, dtype, cols, row_mult):
    flat = jnp.concatenate([a.astype(dtype).reshape(-1) for a in arrays])
    n = flat.shape[0]
    total = _round_up(n, cols * row_mult)
    return jnp.pad(flat, (0, total - n)).reshape(total // cols, cols)


def _unpack(flat, shapes):
    out, off = [], 0
    for shp in shapes:
        n = 1
        for d in shp:
            n *= d
        out.append(flat[..., off:off + n].reshape(flat.shape[:-1] + tuple(shp)))
        off += n
    return out


def _layer_groups(depth):
    groups = []
    for i in range(depth):
        kind, j = i % 3, i // 3
        mix = ((("att_w_qkv", j), ("att_w_o", j)), (("pool_w", j),), (("gdn_w_in", j), ("gdn_w_o", j)))[kind]
        groups.append(mix + (("ffn_w_up", i), ("ffn_w_down", i)))
    return groups


def kernel(x, mix_norm, ffn_norm, att_w_qkv, att_q_gain, att_k_gain, att_rel_bias, att_w_o, pool_w, pool_scale, gdn_w_in, gdn_conv, gdn_a_log, gdn_dt_bias, gdn_o_gain, gdn_w_o, ffn_w_up, ffn_conv, ffn_w_down, loss_target, m_mix_norm, m_ffn_norm, m_att_w_qkv, m_att_q_gain, m_att_k_gain, m_att_rel_bias, m_att_w_o, m_pool_w, m_pool_scale, m_gdn_w_in, m_gdn_conv, m_gdn_a_log, m_gdn_dt_bias, m_gdn_o_gain, m_gdn_w_o, m_ffn_w_up, m_ffn_conv, m_ffn_w_down, v_mix_norm, v_ffn_norm, v_att_w_qkv, v_att_q_gain, v_att_k_gain, v_att_rel_bias, v_att_w_o, v_pool_w, v_pool_scale, v_gdn_w_in, v_gdn_conv, v_gdn_a_log, v_gdn_dt_bias, v_gdn_o_gain, v_gdn_w_o, v_ffn_w_up, v_ffn_conv, v_ffn_w_down):
    w = dict(mix_norm=mix_norm, ffn_norm=ffn_norm, att_w_qkv=att_w_qkv, att_q_gain=att_q_gain, att_k_gain=att_k_gain, att_rel_bias=att_rel_bias, att_w_o=att_w_o, pool_w=pool_w, pool_scale=pool_scale, gdn_w_in=gdn_w_in, gdn_conv=gdn_conv, gdn_a_log=gdn_a_log, gdn_dt_bias=gdn_dt_bias, gdn_o_gain=gdn_o_gain, gdn_w_o=gdn_w_o, ffn_w_up=ffn_w_up, ffn_conv=ffn_conv, ffn_w_down=ffn_w_down)
    m = dict(mix_norm=m_mix_norm, ffn_norm=m_ffn_norm, att_w_qkv=m_att_w_qkv, att_q_gain=m_att_q_gain, att_k_gain=m_att_k_gain, att_rel_bias=m_att_rel_bias, att_w_o=m_att_w_o, pool_w=m_pool_w, pool_scale=m_pool_scale, gdn_w_in=m_gdn_w_in, gdn_conv=m_gdn_conv, gdn_a_log=m_gdn_a_log, gdn_dt_bias=m_gdn_dt_bias, gdn_o_gain=m_gdn_o_gain, gdn_w_o=m_gdn_w_o, ffn_w_up=m_ffn_w_up, ffn_conv=m_ffn_conv, ffn_w_down=m_ffn_w_down)
    v = dict(mix_norm=v_mix_norm, ffn_norm=v_ffn_norm, att_w_qkv=v_att_w_qkv, att_q_gain=v_att_q_gain, att_k_gain=v_att_k_gain, att_rel_bias=v_att_rel_bias, att_w_o=v_att_w_o, pool_w=v_pool_w, pool_scale=v_pool_scale, gdn_w_in=v_gdn_w_in, gdn_conv=v_gdn_conv, gdn_a_log=v_gdn_a_log, gdn_dt_bias=v_gdn_dt_bias, gdn_o_gain=v_gdn_o_gain, gdn_w_o=v_gdn_w_o, ffn_w_up=v_ffn_w_up, ffn_conv=v_ffn_conv, ffn_w_down=v_ffn_w_down)
    depth = ffn_w_up.shape[0]
    my_c = lax.axis_index("c").astype(jnp.int32)
    my_chip = (2 * lax.axis_index("x") + lax.axis_index("y")).astype(jnp.int32)
    groups = _layer_groups(depth)

    full = {n: w[n] for n in REPLICATED}
    parts = {n: [None] * w[n].shape[0] for n in BIG}
    for i, group in enumerate(groups):
        shards = [w[n][j] for n, j in group]
        got = gather_chips(_pack(shards, MXU_DTYPE, PACK_COLS, PACK_ROWS), f"gather_l{i}")
        got = got.reshape(N_CHIPS, -1)
        for (n, j), t in zip(group, _unpack(got, [s.shape for s in shards])):
            parts[n][j] = _merge(t, SHARD_AXIS[n] - 1)
    for n in BIG:
        full[n] = jnp.stack(parts[n])
    small = [w[n] for n in SMALL_SHARDED]
    got = gather_chips(_pack(small, F32, LANES, PACK_ROWS), "gather_small").reshape(N_CHIPS, -1)
    for n, t in zip(SMALL_SHARDED, _unpack(got, [s.shape for s in small])):
        full[n] = _merge(t, SHARD_AXIS[n])

    W = build_weights(full)
    loss, grad_x, G = local_step(x[0], loss_target[0], W)
    loss = lax.psum(loss[0, 0], ("x", "y", "c"))
    gfull = full_grads(G, full)

    grads = {}
    gparts = {n: [None] * w[n].shape[0] for n in BIG}
    for i, group in enumerate(groups):
        stacked = [_split(gfull[n][j], SHARD_AXIS[n] - 1).reshape(N_CHIPS, -1) for n, j in group]
        flat = jnp.concatenate(stacked, axis=1)
        n_el = flat.shape[1]
        total = _round_up(n_el, PACK_COLS * PACK_ROWS)
        g4 = jnp.pad(flat, ((0, 0), (0, total - n_el))).reshape(N_CHIPS, total // PACK_COLS, PACK_COLS)
        theirs = swap_halves(g4, f"rs_swap_l{i}")
        chip_sum = add_halves(g4, theirs, my_c, f"rs_add2_l{i}")
        got = scatter_chips(chip_sum, f"rs_scatter_l{i}")
        mine = add_chips(chip_sum, got, my_chip, f"rs_add4_l{i}")
        summed = join_halves(mine, f"rs_join_l{i}").reshape(-1)
        for (n, j), t in zip(group, _unpack(summed, [w[n][j].shape for n, j in group])):
            gparts[n][j] = t
    for n in BIG:
        grads[n] = jnp.stack(gparts[n])

    small_names = REPLICATED + SMALL_SHARDED
    packed = _pack([gfull[n] for n in small_names], F32, LANES, SUBLANES)
    summed = sum_devices(packed, "sum_small").reshape(-1)
    for n, t in zip(small_names, _unpack(summed, [gfull[n].shape for n in small_names])):
        if n in SHARD_AXIS:
            size = w[n].shape[SHARD_AXIS[n]]
            t = lax.dynamic_slice_in_dim(t, my_chip * size, size, axis=SHARD_AXIS[n])
        grads[n] = t

    delta, new_m, new_v = {}, {}, {}
    for n in WEIGHTS:
        delta[n], new_m[n], new_v[n] = adamw(w[n], grads[n], m[n], v[n], f"adamw_{n}")
    return (loss, grad_x[None], *[grads[n] for n in WEIGHTS], *[delta[n] for n in WEIGHTS],
            *[new_m[n] for n in WEIGHTS], *[new_v[n] for n in WEIGHTS])
```

```python
import functools

import jax
import jax.numpy as jnp
from jax import lax
from jax.experimental import pallas as pl
from jax.experimental.pallas import tpu as pltpu

F32 = jnp.float32
BF16 = jnp.bfloat16
MXU_DTYPE = BF16
HI = lax.Precision.HIGHEST
S = jax.ShapeDtypeStruct
BS = pl.BlockSpec

EPS = 1e-6
MASK_VALUE = -1e30
CHUNK = 64
HEAD = 128
LEFT_CHUNKS = 8
BAND_LEFT = LEFT_CHUNKS * CHUNK
BAND = BAND_LEFT + CHUNK
MAX_REL = 256
POOL_WINDOWS = (2, 4, 8, 16)
POOL_HALO = 16
GDN_CONV = 4
FFN_CONV = 3
SUBLANES = 8
LANES = 128
FF_ALIGN = 512
N_CHIPS = 4
ADAM_LR, ADAM_B1, ADAM_B2, ADAM_EPS, ADAM_WD, ADAM_STEP = 0.001, 0.9, 0.999, 1e-08, 0.01, 10
MIB = 1024 * 1024
MESH = pl.DeviceIdType.MESH


def _cp(sems, vmem_mib=40):
    return pltpu.CompilerParams(dimension_semantics=sems, vmem_limit_bytes=vmem_mib * MIB)


def _pick(n, cands):
    for c in cands:
        if n % c == 0:
            return c
    return n


def _mx(x):
    return x.astype(MXU_DTYPE)


def _dot(a, b, dims, hi=False):
    if hi:
        return lax.dot_general(a.astype(F32), b.astype(F32), (dims, ((), ())), precision=HI,
                               preferred_element_type=F32)
    return lax.dot_general(_mx(a), _mx(b), (dims, ((), ())), preferred_element_type=F32)


def dot_nn(a, b, hi=False):
    return _dot(a, b, ((1,), (0,)), hi)


def dot_nt(a, b, hi=False):
    return _dot(a, b, ((1,), (1,)), hi)


def dot_tn(a, b, hi=False):
    return _dot(a, b, ((0,), (0,)), hi)


def _sigmoid(x):
    return 1.0 / (1.0 + jnp.exp(-x))


def matmul(a, b, mode, *, out_dtype=F32, res=None, name):
    if mode == "nn":
        (M, K), N = a.shape, b.shape[1]
    elif mode == "nt":
        (M, K), N = a.shape, b.shape[0]
    else:
        (K, M), N = a.shape, b.shape[1]
    bm = _pick(M, (1024, 512, 256, 128))
    bn = _pick(N, (1024, 512, 256, 128))
    bk = _pick(K, (2048, 1408, 1024, 512, 256, 128))
    nk = K // bk
    if mode == "nn":
        a_spec = BS((bm, bk), lambda i, j, k: (i, k))
        b_spec = BS((bk, bn), lambda i, j, k: (k, j))
        dot = dot_nn
    elif mode == "nt":
        a_spec = BS((bm, bk), lambda i, j, k: (i, k))
        b_spec = BS((bn, bk), lambda i, j, k: (j, k))
        dot = dot_nt
    else:
        a_spec = BS((bk, bm), lambda i, j, k: (k, i))
        b_spec = BS((bk, bn), lambda i, j, k: (k, j))
        dot = dot_tn
    o_spec = BS((bm, bn), lambda i, j, k: (i, j))
    has_res = res is not None

    def body(*refs):
        if has_res:
            a_ref, b_ref, r_ref, o_ref, acc = refs
        else:
            a_ref, b_ref, o_ref, acc = refs
            r_ref = None
        k = pl.program_id(2)
        p = dot(a_ref[...], b_ref[...])

        def finish(total):
            if has_res:
                total = r_ref[...] + total
            o_ref[...] = total.astype(o_ref.dtype)

        if nk == 1:
            finish(p)
        else:
            @pl.when(k == 0)
            def _():
                acc[...] = p

            @pl.when(jnp.logical_and(k > 0, k < nk - 1))
            def _():
                acc[...] += p

            @pl.when(k == nk - 1)
            def _():
                finish(acc[...] + p)

    in_specs = [a_spec, b_spec] + ([o_spec] if has_res else [])
    args = (a, b) + ((res,) if has_res else ())
    return pl.pallas_call(
        body, grid=(M // bm, N // bn, nk), in_specs=in_specs, out_specs=o_spec,
        out_shape=S((M, N), out_dtype), scratch_shapes=[pltpu.VMEM((bm, bn), F32)],
        compiler_params=_cp(("parallel", "parallel", "arbitrary"), 48), name=name)(*args)


def norm_fwd(x, gain, out_dtypes, name):
    T, D = x.shape
    bt = _pick(T, (256, 128, 64))

    def body(x_ref, g_ref, *o_refs):
        xv = x_ref[...]
        r = lax.rsqrt(jnp.mean(xv * xv, axis=-1, keepdims=True) + EPS)
        y = (xv * r) * g_ref[...]
        for o in o_refs:
            o[...] = y.astype(o.dtype)

    row = BS((bt, D), lambda i: (i, 0))
    return pl.pallas_call(
        body, grid=(T // bt,), in_specs=[row, BS((1, D), lambda i: (0, 0))],
        out_specs=[row] * len(out_dtypes), out_shape=[S((T, D), dt) for dt in out_dtypes],
        compiler_params=_cp(("parallel",)), name=name)(x, gain)


def norm_bwd(x, gain, dy, dres, name):
    T, D = x.shape
    bt = _pick(T, (256, 128, 64))

    def body(x_ref, g_ref, dy_ref, dres_ref, dx_ref, dg_ref):
        i = pl.program_id(0)
        xv = x_ref[...]
        dyv = dy_ref[...].astype(F32)
        r = lax.rsqrt(jnp.mean(xv * xv, axis=-1, keepdims=True) + EPS)
        xhat = xv * r
        dxhat = dyv * g_ref[...]
        dx = r * (dxhat - xhat * jnp.mean(dxhat * xhat, axis=-1, keepdims=True))
        dx_ref[...] = dres_ref[...] + dx

        @pl.when(i == 0)
        def _():
            dg_ref[...] = jnp.zeros_like(dg_ref)

        dg_ref[...] += jnp.sum(dyv * xhat, axis=0, keepdims=True)

    row = BS((bt, D), lambda i: (i, 0))
    vec = BS((1, D), lambda i: (0, 0))
    return pl.pallas_call(
        body, grid=(T // bt,), in_specs=[row, vec, row, row], out_specs=[row, vec],
        out_shape=[S((T, D), F32), S((1, D), F32)],
        compiler_params=_cp(("arbitrary",)), name=name)(x, gain, dy, dres)


def loss_and_grad(y, target, name):
    T, D = y.shape
    bt = _pick(T, (256, 128, 64))
    nt = T // bt

    def body(y_ref, t_ref, l_ref, dy_ref, acc):
        i = pl.program_id(0)
        e = y_ref[...] - t_ref[...]
        dy_ref[...] = e * (1.0 / D)

        @pl.when(i == 0)
        def _():
            acc[...] = jnp.zeros_like(acc)

        acc[...] += jnp.sum(e * e, axis=0, keepdims=True)

        @pl.when(i == nt - 1)
        def _():
            l_ref[...] = jnp.sum(acc[...], axis=1, keepdims=True) * (0.5 / D)

    row = BS((bt, D), lambda i: (i, 0))
    return pl.pallas_call(
        body, grid=(nt,), in_specs=[row, row], out_specs=[BS((1, 1), lambda i: (0, 0)), row],
        out_shape=[S((1, 1), F32), S((T, D), F32)], scratch_shapes=[pltpu.VMEM((1, D), F32)],
        compiler_params=_cp(("arbitrary",)), name=name)(y, target)


def _prev_halo(bt):
    return lambda i: (jnp.maximum(i * (bt // SUBLANES) - 1, 0), 0)


def _ffn_u(i, a_ref, halo_ref, w_ref, ext, bt):
    ext[pl.ds(0, SUBLANES), :] = jnp.where(i > 0, halo_ref[...], 0.0)
    ext[pl.ds(SUBLANES, bt), :] = a_ref[...]
    u = w_ref[2:3, :] * ext[pl.ds(SUBLANES, bt), :]
    u += w_ref[1:2, :] * ext[pl.ds(SUBLANES - 1, bt), :]
    u += w_ref[0:1, :] * ext[pl.ds(SUBLANES - 2, bt), :]
    return u


def ffn_act_fwd(up, conv_w, name):
    T, F2 = up.shape
    Fp = F2 // 2
    bt = _pick(T, (128, 64))

    def body(a_ref, b_ref, halo_ref, w_ref, o_ref, ext):
        u = _ffn_u(pl.program_id(0), a_ref, halo_ref, w_ref, ext, bt)
        o_ref[...] = (u * _sigmoid(u) * b_ref[...]).astype(o_ref.dtype)

    return pl.pallas_call(
        body, grid=(T // bt,),
        in_specs=[BS((bt, Fp), lambda i: (i, 0)), BS((bt, Fp), lambda i: (i, 1)),
                  BS((SUBLANES, Fp), _prev_halo(bt)), BS((SUBLANES, Fp), lambda i: (0, 0))],
        out_specs=BS((bt, Fp), lambda i: (i, 0)), out_shape=S((T, Fp), MXU_DTYPE),
        scratch_shapes=[pltpu.VMEM((bt + SUBLANES, Fp), F32)],
        compiler_params=_cp(("arbitrary",)), name=name)(up, up, up, conv_w)


def ffn_act_bwd_a(up, conv_w, g_act, name):
    T, F2 = up.shape
    Fp = F2 // 2
    bt = _pick(T, (128, 64))

    def body(a_ref, b_ref, halo_ref, w_ref, g_ref, du_ref, db_ref, dw_ref, ext):
        i = pl.program_id(0)
        u = _ffn_u(i, a_ref, halo_ref, w_ref, ext, bt)
        sg = _sigmoid(u)
        g = g_ref[...]
        db_ref[...] = (g * (u * sg)).astype(db_ref.dtype)
        du = g * b_ref[...] * (sg * (1.0 + u * (1.0 - sg)))
        du_ref[...] = du

        @pl.when(i == 0)
        def _():
            dw_ref[...] = jnp.zeros_like(dw_ref)

        for j in range(FFN_CONV):
            shifted = ext[pl.ds(SUBLANES - (FFN_CONV - 1) + j, bt), :]
            dw_ref[j:j + 1, :] += jnp.sum(du * shifted, axis=0, keepdims=True)

    blk = BS((bt, Fp), lambda i: (i, 0))
    full = BS((SUBLANES, Fp), lambda i: (0, 0))
    return pl.pallas_call(
        body, grid=(T // bt,),
        in_specs=[blk, BS((bt, Fp), lambda i: (i, 1)), BS((SUBLANES, Fp), _prev_halo(bt)), full, blk],
        out_specs=[blk, blk, full],
        out_shape=[S((T, Fp), F32), S((T, Fp), MXU_DTYPE), S((SUBLANES, Fp), F32)],
        scratch_shapes=[pltpu.VMEM((bt + SUBLANES, Fp), F32)],
        compiler_params=_cp(("arbitrary",)), name=name)(up, up, up, conv_w, g_act)


def ffn_act_bwd_b(du, db, conv_w, name):
    T, Fp = du.shape
    bt = _pick(T, (128, 64))
    nt = T // bt

    def body(du_ref, halo_ref, db_ref, w_ref, o_ref, ext):
        i = pl.program_id(0)
        ext[pl.ds(0, bt), :] = du_ref[...]
        ext[pl.ds(bt, SUBLANES), :] = jnp.where(i < nt - 1, halo_ref[...], 0.0)
        da = w_ref[2:3, :] * ext[pl.ds(0, bt), :]
        da += w_ref[1:2, :] * ext[pl.ds(1, bt), :]
        da += w_ref[0:1, :] * ext[pl.ds(2, bt), :]
        o_ref[:, pl.ds(0, Fp)] = da.astype(o_ref.dtype)
        o_ref[:, pl.ds(Fp, Fp)] = db_ref[...]

    blk = BS((bt, Fp), lambda i: (i, 0))
    nxt = BS((SUBLANES, Fp), lambda i: (jnp.minimum((i + 1) * (bt // SUBLANES), T // SUBLANES - 1), 0))
    return pl.pallas_call(
        body, grid=(nt,), in_specs=[blk, nxt, blk, BS((SUBLANES, Fp), lambda i: (0, 0))],
        out_specs=BS((bt, 2 * Fp), lambda i: (i, 0)), out_shape=S((T, 2 * Fp), MXU_DTYPE),
        scratch_shapes=[pltpu.VMEM((bt + SUBLANES, Fp), F32)],
        compiler_params=_cp(("arbitrary",)), name=name)(du, du, db, conv_w)


def _attn_fill(k_ref, v_ref, gk, kn_scr, vb_scr, T):
    kn_scr[pl.ds(0, BAND_LEFT), :] = jnp.zeros((BAND_LEFT, HEAD), kn_scr.dtype)
    vb_scr[pl.ds(0, BAND_LEFT), :] = jnp.zeros((BAND_LEFT, HEAD), vb_scr.dtype)
    rb = 512

    def fill(r, carry):
        rows = pl.ds(pl.multiple_of(r * rb, rb), rb)
        dst = pl.ds(pl.multiple_of(BAND_LEFT + r * rb, rb), rb)
        k = k_ref[rows, :]
        rk = lax.rsqrt(jnp.mean(k * k, axis=-1, keepdims=True) + EPS)
        kn_scr[dst, :] = ((k * rk) * gk).astype(kn_scr.dtype)
        vb_scr[dst, :] = v_ref[rows, :].astype(vb_scr.dtype)
        return carry

    lax.fori_loop(0, T // rb, fill, 0)


def _attn_probs(c, q_ref, gq, bias_ref, kn_scr):
    q = q_ref[...]
    rq = lax.rsqrt(jnp.mean(q * q, axis=-1, keepdims=True) + EPS)
    qn = (q * rq) * gq
    band = pl.ds(pl.multiple_of(c * CHUNK, CHUNK), BAND)
    kb = kn_scr[band, :]
    s = dot_nt(qn, kb) * (HEAD ** -0.5) + bias_ref[0]
    pos = c * CHUNK - BAND_LEFT + lax.broadcasted_iota(jnp.int32, (CHUNK, BAND), 1)
    s = jnp.where(pos >= 0, s, MASK_VALUE)
    m = jnp.max(s, axis=-1, keepdims=True)
    e = jnp.exp(s - m)
    p = e / jnp.sum(e, axis=-1, keepdims=True)
    return q, rq, qn, kb, p


def attn_fwd(qkv, gq, gk, bias, name):
    T, D3 = qkv.shape
    D = D3 // 3
    H = D // HEAD
    NC = T // CHUNK

    def body(q_ref, k_ref, v_ref, gq_ref, gk_ref, bias_ref, o_ref, kn_scr, vb_scr):
        c = pl.program_id(1)

        @pl.when(c == 0)
        def _():
            _attn_fill(k_ref, v_ref, gk_ref[...], kn_scr, vb_scr, T)

        _, _, _, _, p = _attn_probs(c, q_ref, gq_ref[...], bias_ref, kn_scr)
        band = pl.ds(pl.multiple_of(c * CHUNK, CHUNK), BAND)
        o_ref[...] = dot_nn(p, vb_scr[band, :]).astype(o_ref.dtype)

    vec = BS((1, HEAD), lambda h, c: (0, 0))
    return pl.pallas_call(
        body, grid=(H, NC),
        in_specs=[BS((CHUNK, HEAD), lambda h, c: (c, h)), BS((T, HEAD), lambda h, c: (0, H + h)),
                  BS((T, HEAD), lambda h, c: (0, 2 * H + h)), vec, vec,
                  BS((1, CHUNK, BAND), lambda h, c: (h, 0, 0))],
        out_specs=BS((CHUNK, HEAD), lambda h, c: (c, h)), out_shape=S((T, D), MXU_DTYPE),
        scratch_shapes=[pltpu.VMEM((T + BAND_LEFT, HEAD), MXU_DTYPE)] * 2,
        compiler_params=_cp(("arbitrary", "arbitrary"), 48), name=name)(qkv, qkv, qkv, gq, gk, bias)


def attn_bwd(qkv, do, gq, gk, bias, name):
    T, D3 = qkv.shape
    D = D3 // 3
    H = D // HEAD
    NC = T // CHUNK
    scale = HEAD ** -0.5

    def body(q_ref, k_ref, v_ref, do_ref, gq_ref, gk_ref, bias_ref,
             dq_ref, dk_ref, dv_ref, dgq_ref, dgk_ref, dbias_ref, kn_scr, vb_scr, dkn_acc, dv_acc):
        h = pl.program_id(0)
        c = pl.program_id(1)
        gq = gq_ref[...]
        gk = gk_ref[...]

        @pl.when(c == 0)
        def _():
            _attn_fill(k_ref, v_ref, gk, kn_scr, vb_scr, T)
            dkn_acc[...] = jnp.zeros_like(dkn_acc)
            dv_acc[...] = jnp.zeros_like(dv_acc)
            dbias_ref[...] = jnp.zeros_like(dbias_ref)

        @pl.when(jnp.logical_and(c == 0, h == 0))
        def _():
            dgq_ref[...] = jnp.zeros_like(dgq_ref)
            dgk_ref[...] = jnp.zeros_like(dgk_ref)

        q, rq, qn, kb, p = _attn_probs(c, q_ref, gq, bias_ref, kn_scr)
        band = pl.ds(pl.multiple_of(c * CHUNK, CHUNK), BAND)
        dov = do_ref[...]
        dv_acc[band, :] += dot_tn(p, dov)
        dp = dot_nt(dov, vb_scr[band, :])
        ds = p * (dp - jnp.sum(dp * p, axis=-1, keepdims=True))
        dbias_ref[0] += ds
        dss = ds * scale
        dqn = dot_nn(dss, kb)
        dkn_acc[band, :] += dot_tn(dss, qn)
        xhat = q * rq
        dgq_ref[...] += jnp.sum(dqn * xhat, axis=0, keepdims=True)
        dxhat = dqn * gq
        dq = rq * (dxhat - xhat * jnp.mean(dxhat * xhat, axis=-1, keepdims=True))
        dq_ref[...] = dq.astype(dq_ref.dtype)

        @pl.when(c == NC - 1)
        def _():
            rb = 512

            def fin(r, carry):
                rows = pl.ds(pl.multiple_of(r * rb, rb), rb)
                src = pl.ds(pl.multiple_of(BAND_LEFT + r * rb, rb), rb)
                k = k_ref[rows, :]
                rk = lax.rsqrt(jnp.mean(k * k, axis=-1, keepdims=True) + EPS)
                khat = k * rk
                dkn = dkn_acc[src, :]
                dgk_ref[...] += jnp.sum(dkn * khat, axis=0, keepdims=True)
                dkh = dkn * gk
                dk = rk * (dkh - khat * jnp.mean(dkh * khat, axis=-1, keepdims=True))
                dk_ref[rows, :] = dk.astype(dk_ref.dtype)
                dv_ref[rows, :] = dv_acc[src, :].astype(dv_ref.dtype)
                return carry

            lax.fori_loop(0, T // rb, fin, 0)

    vec = BS((1, HEAD), lambda h, c: (0, 0))
    qblk = BS((CHUNK, HEAD), lambda h, c: (c, h))
    col = BS((T, HEAD), lambda h, c: (0, h))
    return pl.pallas_call(
        body, grid=(H, NC),
        in_specs=[qblk, BS((T, HEAD), lambda h, c: (0, H + h)), BS((T, HEAD), lambda h, c: (0, 2 * H + h)),
                  qblk, vec, vec, BS((1, CHUNK, BAND), lambda h, c: (h, 0, 0))],
        out_specs=[qblk, col, col, vec, vec, BS((1, CHUNK, BAND), lambda h, c: (h, 0, 0))],
        out_shape=[S((T, D), MXU_DTYPE)] * 3 + [S((1, HEAD), F32)] * 2 + [S((H, CHUNK, BAND), F32)],
        scratch_shapes=[pltpu.VMEM((T + BAND_LEFT, HEAD), MXU_DTYPE)] * 2
        + [pltpu.VMEM((T + BAND_LEFT, HEAD), F32)] * 2,
        compiler_params=_cp(("arbitrary", "arbitrary"), 56), name=name)(qkv, qkv, qkv, do, gq, gk, bias)


def _rel_onehot(qi, num_rel):
    kk = lax.broadcasted_iota(jnp.int32, (BAND, num_rel), 0)
    rr = lax.broadcasted_iota(jnp.int32, (BAND, num_rel), 1)
    idx = jnp.clip(BAND_LEFT + qi - kk, -(CHUNK - 1), MAX_REL) + (CHUNK - 1)
    return (idx == rr).astype(F32)


def rel_bias_expand(table, name):
    H, num_rel = table.shape

    def body(t_ref, o_ref):
        for qi in range(CHUNK):
            o_ref[qi] = dot_nt(t_ref[...], _rel_onehot(qi, num_rel), hi=True)

    return pl.pallas_call(body, out_shape=S((CHUNK, H, BAND), F32), name=name,
                          compiler_params=pltpu.CompilerParams(vmem_limit_bytes=40 * MIB))(table)


def rel_bias_reduce(dbias_t, num_rel, name):
    H = dbias_t.shape[1]

    def body(d_ref, o_ref):
        acc = jnp.zeros((H, num_rel), F32)
        for qi in range(CHUNK):
            acc += dot_nn(d_ref[qi], _rel_onehot(qi, num_rel), hi=True)
        o_ref[...] = acc

    return pl.pallas_call(body, out_shape=S((H, num_rel), F32), name=name,
                          compiler_params=pltpu.CompilerParams(vmem_limit_bytes=40 * MIB))(dbias_t)


def pool_fwd(h, x, w, scale, name):
    T, D = h.shape
    G = len(POOL_WINDOWS)
    Dg = D // G
    bt = _pick(T, (256, 128, 64))

    def body(h_ref, halo_ref, x_ref, w_ref, s_ref, o_ref, p_ref, ext):
        i = pl.program_id(0)
        ext[pl.ds(0, POOL_HALO), :] = jnp.where(i > 0, halo_ref[...], 0.0)
        ext[pl.ds(POOL_HALO, bt), :] = h_ref[...]
        t = i * bt + lax.broadcasted_iota(jnp.int32, (bt, 1), 0)
        for g, win in enumerate(POOL_WINDOWS):
            cols = pl.ds(g * Dg, Dg)
            acc = ext[pl.ds(POOL_HALO, bt), cols]
            for j in range(1, win):
                acc += ext[pl.ds(POOL_HALO - j, bt), cols]
            count = jnp.minimum(t + 1, win).astype(F32)
            pooled = acc / count - h_ref[:, cols]
            p_ref[:, cols] = pooled.astype(p_ref.dtype)
            y = dot_nn(pooled, w_ref[g]) * s_ref[:, cols]
            o_ref[:, cols] = x_ref[:, cols] + y

    row = BS((bt, D), lambda i: (i, 0))
    return pl.pallas_call(
        body, grid=(T // bt,),
        in_specs=[row, BS((POOL_HALO, D), lambda i: (jnp.maximum(i * (bt // POOL_HALO) - 1, 0), 0)), row,
                  BS((G, Dg, Dg), lambda i: (0, 0, 0)), BS((1, D), lambda i: (0, 0))],
        out_specs=[row, row], out_shape=[S((T, D), F32), S((T, D), MXU_DTYPE)],
        scratch_shapes=[pltpu.VMEM((bt + POOL_HALO, D), F32)],
        compiler_params=_cp(("arbitrary",)), name=name)(h, h, x, w, scale)


def pool_bwd_a(dy, pooled, w, scale, name):
    T, D = dy.shape
    G = len(POOL_WINDOWS)
    Dg = D // G
    bt = _pick(T, (256, 128, 64))

    def body(dy_ref, p_ref, w_ref, s_ref, dp_ref, dw_ref, ds_ref):
        i = pl.program_id(0)

        @pl.when(i == 0)
        def _():
            dw_ref[...] = jnp.zeros_like(dw_ref)
            ds_ref[...] = jnp.zeros_like(ds_ref)

        for g in range(G):
            cols = pl.ds(g * Dg, Dg)
            pg = p_ref[:, cols]
            dyg = dy_ref[:, cols]
            ypre = dot_nn(pg, w_ref[g])
            ds_ref[:, cols] += jnp.sum(dyg * ypre, axis=0, keepdims=True)
            dys = dyg * s_ref[:, cols]
            dp_ref[:, cols] = dot_nt(dys, w_ref[g])
            dw_ref[g] += dot_tn(pg, dys)

    row = BS((bt, D), lambda i: (i, 0))
    wspec = BS((G, Dg, Dg), lambda i: (0, 0, 0))
    vec = BS((1, D), lambda i: (0, 0))
    return pl.pallas_call(
        body, grid=(T // bt,), in_specs=[row, row, wspec, vec], out_specs=[row, wspec, vec],
        out_shape=[S((T, D), F32), S((G, Dg, Dg), F32), S((1, D), F32)],
        compiler_params=_cp(("arbitrary",)), name=name)(dy, pooled, w, scale)


def pool_bwd_b(dpooled, name):
    T, D = dpooled.shape
    G = len(POOL_WINDOWS)
    Dg = D // G
    bt = _pick(T, (256, 128, 64))
    nt = T // bt

    def body(d_ref, halo_ref, o_ref, ext):
        i = pl.program_id(0)
        t = i * bt + lax.broadcasted_iota(jnp.int32, (bt, 1), 0)
        for g, win in enumerate(POOL_WINDOWS):
            cols = pl.ds(g * Dg, Dg)
            count = jnp.minimum(t + 1, win).astype(F32)
            ext[pl.ds(0, bt), cols] = d_ref[:, cols] / count
            ext[pl.ds(bt, POOL_HALO), cols] = jnp.where(i < nt - 1, halo_ref[:, cols] * (1.0 / win), 0.0)
            acc = ext[pl.ds(0, bt), cols]
            for j in range(1, win):
                acc += ext[pl.ds(j, bt), cols]
            o_ref[:, cols] = acc - d_ref[:, cols]

    row = BS((bt, D), lambda i: (i, 0))
    nxt = BS((POOL_HALO, D), lambda i: (jnp.minimum((i + 1) * (bt // POOL_HALO), T // POOL_HALO - 1), 0))
    return pl.pallas_call(
        body, grid=(nt,), in_specs=[row, nxt], out_specs=row, out_shape=S((T, D), F32),
        scratch_shapes=[pltpu.VMEM((bt + POOL_HALO, D), F32)],
        compiler_params=_cp(("arbitrary",)), name=name)(dpooled, dpooled)


def _gdn_u(i, x_ref, halo_ref, w_ref, ext, bt):
    ext[pl.ds(0, SUBLANES), :] = jnp.where(i > 0, halo_ref[...], 0.0)
    ext[pl.ds(SUBLANES, bt), :] = x_ref[...]
    u = w_ref[3:4, :] * ext[pl.ds(SUBLANES, bt), :]
    for j in range(GDN_CONV - 1):
        u += w_ref[j:j + 1, :] * ext[pl.ds(SUBLANES - (GDN_CONV - 1) + j, bt), :]
    return u


def gdn_pre_fwd(proj, conv_w, key_dim, name):
    T = proj.shape[0]
    C = conv_w.shape[1]
    cb = min(1024, key_dim)
    nq, nqk, J = key_dim // cb, 2 * key_dim // cb, C // cb
    bt = _pick(T, (256, 128, 64))

    def body(x_ref, halo_ref, w_ref, o_ref, ext):
        i, j = pl.program_id(0), pl.program_id(1)
        u = _gdn_u(i, x_ref, halo_ref, w_ref, ext, bt)
        s = u * _sigmoid(u)

        @pl.when(j < nqk)
        def _():
            sc = jnp.where(j < nq, HEAD ** -0.5, 1.0)
            for hh in range(cb // HEAD):
                cols = pl.ds(hh * HEAD, HEAD)
                blk = s[:, hh * HEAD:(hh + 1) * HEAD]
                r = lax.rsqrt(jnp.sum(blk * blk, axis=-1, keepdims=True) + EPS)
                o_ref[:, cols] = (blk * r) * sc

        @pl.when(j >= nqk)
        def _():
            o_ref[...] = s

    return pl.pallas_call(
        body, grid=(T // bt, J),
        in_specs=[BS((bt, cb), lambda i, j: (i, j)),
                  BS((SUBLANES, cb), lambda i, j: (jnp.maximum(i * (bt // SUBLANES) - 1, 0), j)),
                  BS((SUBLANES, cb), lambda i, j: (0, j))],
        out_specs=BS((bt, cb), lambda i, j: (i, j)), out_shape=S((T, C), F32),
        scratch_shapes=[pltpu.VMEM((bt + SUBLANES, cb), F32)],
        compiler_params=_cp(("arbitrary", "arbitrary")), name=name)(proj, proj, conv_w)


def gdn_pre_bwd_a(proj, conv_w, dq_v, dk_v, dv, key_dim, name):
    T = proj.shape[0]
    C = conv_w.shape[1]
    cb = min(1024, key_dim)
    nq, nqk, J = key_dim // cb, 2 * key_dim // cb, C // cb
    nv = J - nqk
    bt = _pick(T, (256, 128, 64))

    def body(x_ref, halo_ref, w_ref, dq_ref, dk_ref, dv_ref, du_ref, dw_ref, ext, ds_scr):
        j, i = pl.program_id(0), pl.program_id(1)
        u = _gdn_u(i, x_ref, halo_ref, w_ref, ext, bt)
        sg = _sigmoid(u)
        s = u * sg

        @pl.when(j < nqk)
        def _():
            sc = jnp.where(j < nq, HEAD ** -0.5, 1.0)
            for hh in range(cb // HEAD):
                lo = 2 * hh * HEAD
                dq2 = dq_ref[:, lo:lo + HEAD] + dq_ref[:, lo + HEAD:lo + 2 * HEAD]
                dk2 = dk_ref[:, lo:lo + HEAD] + dk_ref[:, lo + HEAD:lo + 2 * HEAD]
                dn = jnp.where(j < nq, dq2, dk2)
                blk = s[:, hh * HEAD:(hh + 1) * HEAD]
                r = lax.rsqrt(jnp.sum(blk * blk, axis=-1, keepdims=True) + EPS)
                shat = blk * r
                ds_scr[:, pl.ds(hh * HEAD, HEAD)] = (sc * r) * (dn - shat * jnp.sum(dn * shat, axis=-1, keepdims=True))

        @pl.when(j >= nqk)
        def _():
            ds_scr[...] = dv_ref[...]

        du = ds_scr[...] * (sg * (1.0 + u * (1.0 - sg)))
        du_ref[...] = du

        @pl.when(i == 0)
        def _():
            dw_ref[...] = jnp.zeros_like(dw_ref)

        for k in range(GDN_CONV):
            shifted = ext[pl.ds(SUBLANES - (GDN_CONV - 1) + k, bt), :]
            dw_ref[k:k + 1, :] += jnp.sum(du * shifted, axis=0, keepdims=True)

    blk = BS((bt, cb), lambda j, i: (i, j))
    return pl.pallas_call(
        body, grid=(J, T // bt),
        in_specs=[blk, BS((SUBLANES, cb), lambda j, i: (jnp.maximum(i * (bt // SUBLANES) - 1, 0), j)),
                  BS((SUBLANES, cb), lambda j, i: (0, j)),
                  BS((bt, 2 * cb), lambda j, i: (i, jnp.minimum(j, nq - 1))),
                  BS((bt, 2 * cb), lambda j, i: (i, jnp.clip(j - nq, 0, nq - 1))),
                  BS((bt, cb), lambda j, i: (i, jnp.clip(j - nqk, 0, nv - 1)))],
        out_specs=[blk, BS((SUBLANES, cb), lambda j, i: (0, j))],
        out_shape=[S((T, C), F32), S((SUBLANES, C), F32)],
        scratch_shapes=[pltpu.VMEM((bt + SUBLANES, cb), F32), pltpu.VMEM((bt, cb), F32)],
        compiler_params=_cp(("arbitrary", "arbitrary")), name=name)(proj, proj, conv_w, dq_v, dk_v, dv)


def gdn_pre_bwd_b(du, dgate, conv_w, key_dim, name):
    T, C = du.shape
    V = dgate.shape[1]
    cb = min(1024, key_dim)
    J = C // cb
    J2 = (C + V) // cb
    bt = _pick(T, (256, 128, 64))
    nt = T // bt

    def body(du_ref, halo_ref, w_ref, dg_ref, o_ref, ext):
        i, j = pl.program_id(0), pl.program_id(1)

        @pl.when(j < J)
        def _():
            ext[pl.ds(0, bt), :] = du_ref[...]
            ext[pl.ds(bt, SUBLANES), :] = jnp.where(i < nt - 1, halo_ref[...], 0.0)
            da = w_ref[3:4, :] * ext[pl.ds(0, bt), :]
            for k in range(GDN_CONV - 1):
                da += w_ref[k:k + 1, :] * ext[pl.ds(GDN_CONV - 1 - k, bt), :]
            o_ref[...] = da.astype(o_ref.dtype)

        @pl.when(j >= J)
        def _():
            o_ref[...] = dg_ref[...]

    jc = lambda j: jnp.minimum(j, J - 1)
    return pl.pallas_call(
        body, grid=(nt, J2),
        in_specs=[BS((bt, cb), lambda i, j: (i, jc(j))),
                  BS((SUBLANES, cb), lambda i, j: (jnp.minimum((i + 1) * (bt // SUBLANES), T // SUBLANES - 1), jc(j))),
                  BS((SUBLANES, cb), lambda i, j: (0, jc(j))),
                  BS((bt, cb), lambda i, j: (i, jnp.maximum(j - J, 0)))],
        out_specs=BS((bt, cb), lambda i, j: (i, j)), out_shape=S((T, C + V), MXU_DTYPE),
        scratch_shapes=[pltpu.VMEM((bt + SUBLANES, cb), F32)],
        compiler_params=_cp(("arbitrary", "arbitrary")), name=name)(du, du, conv_w, dgate)


def _softplus(x):
    return jnp.maximum(x, 0.0) + jnp.log1p(jnp.exp(-jnp.abs(x)))


def gdn_gate_fwd(a, b, a_log, dt_bias, name):
    T, HV = a.shape
    bt = _pick(T, (1024, 512, 256, 128, 64))

    def body(a_ref, b_ref, al_ref, dt_ref, g_ref, be_ref):
        g_ref[...] = -jnp.exp(al_ref[...]) * _softplus(a_ref[...] + dt_ref[...])
        be_ref[...] = _sigmoid(b_ref[...])

    row = BS((bt, HV), lambda i: (i, 0))
    vec = BS((1, HV), lambda i: (0, 0))
    return pl.pallas_call(body, grid=(T // bt,), in_specs=[row, row, vec, vec], out_specs=[row, row],
                          out_shape=[S((T, HV), F32)] * 2, compiler_params=_cp(("parallel",)), name=name)(
                              a, b, a_log, dt_bias)


def gdn_gate_bwd(a, b, a_log, dt_bias, dg, dbeta, name):
    T, HV = a.shape
    bt = _pick(T, (1024, 512, 256, 128, 64))

    def body(a_ref, b_ref, al_ref, dt_ref, dg_ref, dbe_ref, da_ref, db_ref, dal_ref, ddt_ref):
        i = pl.program_id(0)
        x = a_ref[...] + dt_ref[...]
        ea = jnp.exp(al_ref[...])
        dgv = dg_ref[...]
        da = dgv * (-ea * _sigmoid(x))
        da_ref[...] = da
        be = _sigmoid(b_ref[...])
        db_ref[...] = dbe_ref[...] * be * (1.0 - be)

        @pl.when(i == 0)
        def _():
            dal_ref[...] = jnp.zeros_like(dal_ref)
            ddt_ref[...] = jnp.zeros_like(ddt_ref)

        dal_ref[...] += jnp.sum(dgv * (-ea * _softplus(x)), axis=0, keepdims=True)
        ddt_ref[...] += jnp.sum(da, axis=0, keepdims=True)

    row = BS((bt, HV), lambda i: (i, 0))
    vec = BS((1, HV), lambda i: (0, 0))
    return pl.pallas_call(body, grid=(T // bt,), in_specs=[row, row, vec, vec, row, row],
                          out_specs=[row, row, vec, vec],
                          out_shape=[S((T, HV), F32)] * 2 + [S((1, HV), F32)] * 2,
                          compiler_params=_cp(("arbitrary",)), name=name)(a, b, a_log, dt_bias, dg, dbeta)


def _col(row_vec, eye):
    return jnp.sum(jnp.where(eye, row_vec, 0.0), axis=1, keepdims=True)


def _row(col_vec, eye):
    return jnp.sum(jnp.where(eye, col_vec, 0.0), axis=0, keepdims=True)


def _delta_chunk(q, k, v, g_row, b_row):
    C = CHUNK
    ii = lax.broadcasted_iota(jnp.int32, (C, C), 0)
    jj = lax.broadcasted_iota(jnp.int32, (C, C), 1)
    eye, causal, strict = ii == jj, ii >= jj, ii > jj
    g_col = _col(g_row, eye)
    gc_row = jnp.sum(jnp.where(ii <= jj, g_col, 0.0), axis=0, keepdims=True)
    gc_col = _col(gc_row, eye)
    gl = jnp.sum(jnp.where(jj[0:1, :] == C - 1, gc_row, 0.0), axis=1, keepdims=True)
    decay = jnp.where(causal, jnp.exp(jnp.where(causal, gc_col - gc_row, 0.0)), 0.0)
    b_col = _col(b_row, eye)
    kb = k * b_col
    vb = v * b_col
    m = dot_nt(kb, k)
    a = jnp.where(strict, m * decay, 0.0)
    tinv = jnp.where(eye, 1.0, 0.0) - a
    pw = dot_nn(a, a, hi=True)
    for step in range(5):
        tinv = tinv + dot_nn(tinv, pw, hi=True)
        if step < 4:
            pw = dot_nn(pw, pw, hi=True)
    egc = jnp.exp(gc_col)
    kbg = kb * egc
    u = dot_nn(tinv, vb, hi=True)
    w = dot_nn(tinv, kbg, hi=True)
    n = dot_nt(q, k)
    attn = jnp.where(causal, n * decay, 0.0)
    qg = q * egc
    ekl = jnp.exp(gl - gc_col)
    ks = k * ekl
    dec = jnp.exp(gl)
    return dict(eye=eye, causal=causal, strict=strict, ii=ii, jj=jj, gc_col=gc_col, gl=gl, decay=decay,
                b_col=b_col, kb=kb, vb=vb, m=m, tinv=tinv, egc=egc, kbg=kbg, u=u, w=w, n=n, attn=attn,
                qg=qg, ekl=ekl, ks=ks, dec=dec)


def delta_fwd(qkvn, g_rows, b_rows, key_dim, name):
    T = qkvn.shape[0]
    NK = key_dim // HEAD
    HV = g_rows.shape[0]
    rep = HV // NK
    NC = T // CHUNK

    def body(q_ref, k_ref, v_ref, g_ref, b_ref, o_ref, st_ref, state):
        n = pl.program_id(1)

        @pl.when(n == 0)
        def _():
            state[...] = jnp.zeros_like(state)

        c = _delta_chunk(q_ref[...], k_ref[...], v_ref[...], g_ref[0, 0], b_ref[0, 0])
        s0 = state[...]
        st_ref[0, 0] = s0
        vn = c["u"] - dot_nn(c["w"], s0)
        o_ref[...] = dot_nn(c["qg"], s0) + dot_nn(c["attn"], vn)
        state[...] = s0 * c["dec"] + dot_tn(c["ks"], vn)

    vrow = BS((1, 1, 1, CHUNK), lambda h, n: (h, n, 0, 0))
    return pl.pallas_call(
        body, grid=(HV, NC),
        in_specs=[BS((CHUNK, HEAD), lambda h, n: (n, lax.div(h, rep))),
                  BS((CHUNK, HEAD), lambda h, n: (n, NK + lax.div(h, rep))),
                  BS((CHUNK, HEAD), lambda h, n: (n, 2 * NK + h)), vrow, vrow],
        out_specs=[BS((CHUNK, HEAD), lambda h, n: (n, h)), BS((1, 1, HEAD, HEAD), lambda h, n: (h, n, 0, 0))],
        out_shape=[S((T, HV * HEAD), F32), S((HV, NC, HEAD, HEAD), F32)],
        scratch_shapes=[pltpu.VMEM((HEAD, HEAD), F32)],
        compiler_params=_cp(("arbitrary", "arbitrary")), name=name)(qkvn, qkvn, qkvn, g_rows, b_rows)


def delta_bwd(qkvn, g_rows, b_rows, states, do, key_dim, name):
    T = qkvn.shape[0]
    NK = key_dim // HEAD
    HV = g_rows.shape[0]
    rep = HV // NK
    NC = T // CHUNK

    def body(q_ref, k_ref, v_ref, g_ref, b_ref, st_ref, do_ref, dq_ref, dk_ref, dv_ref, dg_ref, dbe_ref, dstate):
        step = pl.program_id(1)

        @pl.when(step == 0)
        def _():
            dstate[...] = jnp.zeros_like(dstate)

        q, k, v = q_ref[...], k_ref[...], v_ref[...]
        c = _delta_chunk(q, k, v, g_ref[0, 0], b_ref[0, 0])
        eye, causal, strict = c["eye"], c["causal"], c["strict"]
        s0 = st_ref[0, 0]
        dsn = dstate[...]
        dov = do_ref[...]
        vn = c["u"] - dot_nn(c["w"], s0)
        dvn = dot_tn(c["attn"], dov) + dot_nn(c["ks"], dsn)
        dattn = dot_nt(dov, vn)
        dqg = dot_nt(dov, s0)
        dks = dot_nt(vn, dsn)
        ddec = jnp.sum(jnp.sum(s0 * dsn, axis=1, keepdims=True), axis=0, keepdims=True)
        dw = -dot_nt(dvn, s0)
        dstate[...] = dot_tn(c["qg"], dov) + dsn * c["dec"] - dot_tn(c["w"], dvn)
        tinv = c["tinv"]
        dvb = dot_tn(tinv, dvn, hi=True)
        dkbg = dot_tn(tinv, dw, hi=True)
        dt = dot_nt(dvn, c["vb"], hi=True) + dot_nt(dw, c["kbg"], hi=True)
        da = -dot_tn(tinv, dot_nt(dt, tinv, hi=True), hi=True)
        da = jnp.where(strict, da, 0.0)
        dm = da * c["decay"]
        dattn = jnp.where(causal, dattn, 0.0)
        dn = dattn * c["decay"]
        e = (da * c["m"] + dattn * c["n"]) * c["decay"]
        dkb = dot_nn(dm, k) + dkbg * c["egc"]
        dk = dot_tn(dm, c["kb"]) + dot_tn(dn, q) + dks * c["ekl"] + dkb * c["b_col"]
        dq = dot_nn(dn, k) + dqg * c["egc"]
        dks_ks = jnp.sum(dks * c["ks"], axis=1, keepdims=True)
        dgc_col = (jnp.sum(e, axis=1, keepdims=True) + jnp.sum(dkbg * c["kbg"], axis=1, keepdims=True)
                   + jnp.sum(dqg * c["qg"], axis=1, keepdims=True) - dks_ks)
        dgc_col = dgc_col - _col(jnp.sum(e, axis=0, keepdims=True), eye)
        dgl = jnp.sum(dks_ks, axis=0, keepdims=True) + ddec * c["dec"]
        dgc_col = dgc_col + jnp.where(c["ii"][:, 0:1] == CHUNK - 1, dgl, 0.0)
        dq_ref[...] = dq
        dk_ref[...] = dk
        dv_ref[...] = dvb * c["b_col"]
        dbe_col = jnp.sum(dvb * v, axis=1, keepdims=True) + jnp.sum(dkb * k, axis=1, keepdims=True)
        dbe_ref[0, 0] = _row(dbe_col, eye)
        dg_ref[0, 0] = jnp.sum(jnp.where(causal, dgc_col, 0.0), axis=0, keepdims=True)

    rev = lambda n: NC - 1 - n
    vrow = BS((1, 1, 1, CHUNK), lambda h, n: (h, rev(n), 0, 0))
    vblk = BS((CHUNK, HEAD), lambda h, n: (rev(n), h))
    return pl.pallas_call(
        body, grid=(HV, NC),
        in_specs=[BS((CHUNK, HEAD), lambda h, n: (rev(n), lax.div(h, rep))),
                  BS((CHUNK, HEAD), lambda h, n: (rev(n), NK + lax.div(h, rep))),
                  BS((CHUNK, HEAD), lambda h, n: (rev(n), 2 * NK + h)), vrow, vrow,
                  BS((1, 1, HEAD, HEAD), lambda h, n: (h, rev(n), 0, 0)), vblk],
        out_specs=[vblk, vblk, vblk, vrow, vrow],
        out_shape=[S((T, HV * HEAD), F32)] * 3 + [S((HV, NC, 1, CHUNK), F32)] * 2,
        scratch_shapes=[pltpu.VMEM((HEAD, HEAD), F32)],
        compiler_params=_cp(("arbitrary", "arbitrary")), name=name)(qkvn, qkvn, qkvn, g_rows, b_rows, states, do)


def gdn_post_fwd(o, proj, gate_col0, o_gain, name):
    T, V = o.shape
    cb = min(1024, V)
    j0 = gate_col0 // cb
    bt = _pick(T, (256, 128, 64))

    def body(o_ref, g_ref, gain_ref, y_ref):
        for hh in range(cb // HEAD):
            cols = pl.ds(hh * HEAD, HEAD)
            ov = o_ref[:, cols]
            gt = g_ref[:, cols]
            r = lax.rsqrt(jnp.mean(ov * ov, axis=-1, keepdims=True) + EPS)
            y_ref[:, cols] = (((ov * r) * gain_ref[...]) * (gt * _sigmoid(gt))).astype(y_ref.dtype)

    return pl.pallas_call(
        body, grid=(T // bt, V // cb),
        in_specs=[BS((bt, cb), lambda i, j: (i, j)), BS((bt, cb), lambda i, j: (i, j0 + j)),
                  BS((1, HEAD), lambda i, j: (0, 0))],
        out_specs=BS((bt, cb), lambda i, j: (i, j)), out_shape=S((T, V), MXU_DTYPE),
        compiler_params=_cp(("parallel", "parallel")), name=name)(o, proj, o_gain)


def gdn_post_bwd(o, proj, gate_col0, o_gain, dy, name):
    T, V = o.shape
    cb = min(1024, V)
    j0 = gate_col0 // cb
    bt = _pick(T, (256, 128, 64))

    def body(o_ref, g_ref, gain_ref, dy_ref, do_ref, dgt_ref, dgain_ref):
        i, j = pl.program_id(0), pl.program_id(1)

        @pl.when(jnp.logical_and(i == 0, j == 0))
        def _():
            dgain_ref[...] = jnp.zeros_like(dgain_ref)

        gain = gain_ref[...]
        for hh in range(cb // HEAD):
            cols = pl.ds(hh * HEAD, HEAD)
            ov = o_ref[:, cols]
            gt = g_ref[:, cols]
            dyv = dy_ref[:, cols]
            r = lax.rsqrt(jnp.mean(ov * ov, axis=-1, keepdims=True) + EPS)
            ohat = ov * r
            sg = _sigmoid(gt)
            dyn = dyv * (gt * sg)
            dgt_ref[:, cols] = (dyv * (ohat * gain) * (sg * (1.0 + gt * (1.0 - sg)))).astype(dgt_ref.dtype)
            dgain_ref[...] += jnp.sum(dyn * ohat, axis=0, keepdims=True)
            dh = dyn * gain
            do_ref[:, cols] = r * (dh - ohat * jnp.mean(dh * ohat, axis=-1, keepdims=True))

    blk = BS((bt, cb), lambda i, j: (i, j))
    vec = BS((1, HEAD), lambda i, j: (0, 0))
    return pl.pallas_call(
        body, grid=(T // bt, V // cb),
        in_specs=[blk, BS((bt, cb), lambda i, j: (i, j0 + j)), vec, blk],
        out_specs=[blk, blk, vec], out_shape=[S((T, V), F32), S((T, V), MXU_DTYPE), S((1, HEAD), F32)],
        compiler_params=_cp(("arbitrary", "arbitrary")), name=name)(o, proj, o_gain, dy)


def adamw(w, g, m, v, name):
    shape = w.shape
    n = 1
    for d in shape:
        n *= d
    cols = shape[-1] if len(shape) > 1 else n
    rows = n // cols
    br = rows
    for cand in (512, 256, 128, 64, 32, 16, 8):
        if rows % cand == 0 and cand * cols * 4 * 14 <= 36 * MIB:
            br = cand
            break
    c1 = 1.0 - ADAM_B1 ** ADAM_STEP
    c2 = 1.0 - ADAM_B2 ** ADAM_STEP

    def body(w_ref, g_ref, m_ref, v_ref, d_ref, nm_ref, nv_ref):
        gv = g_ref[...]
        nm = ADAM_B1 * m_ref[...] + (1.0 - ADAM_B1) * gv
        nv = ADAM_B2 * v_ref[...] + (1.0 - ADAM_B2) * (gv * gv)
        nm_ref[...] = nm
        nv_ref[...] = nv
        d_ref[...] = -ADAM_LR * ((nm / c1) / (jnp.sqrt(nv / c2) + ADAM_EPS) + ADAM_WD * w_ref[...])

    blk = BS((br, cols), lambda i: (i, 0))
    outs = pl.pallas_call(
        body, grid=(rows // br,), in_specs=[blk] * 4, out_specs=[blk] * 3,
        out_shape=[S((rows, cols), F32)] * 3, compiler_params=_cp(("parallel",), 48), name=name)(
            *[t.reshape(rows, cols) for t in (w, g, m, v)])
    return [t.reshape(shape) for t in outs]


def _ffn_fwd(x, gain, p, tag):
    (h2,) = norm_fwd(x, gain, [MXU_DTYPE], f"{tag}_norm")
    up = matmul(h2, p["w_up_t"], "nt", name=f"{tag}_up")
    act = ffn_act_fwd(up, p["conv"], f"{tag}_act")
    out = matmul(act, p["w_down"], "nn", res=x, name=f"{tag}_down")
    return out, (x, h2, up, act)


def _ffn_bwd(dx, saved, gain, p, tag):
    x, h2, up, act = saved
    g_act = matmul(dx, p["w_down"], "nt", name=f"{tag}_bdown")
    d_down = matmul(act, dx, "tn", name=f"{tag}_wdown")
    du, db, d_conv = ffn_act_bwd_a(up, p["conv"], g_act, f"{tag}_bact_a")
    dup = ffn_act_bwd_b(du, db, p["conv"], f"{tag}_bact_b")
    dh2 = matmul(dup, p["w_up_t"], "nn", name=f"{tag}_bup")
    d_up_t = matmul(dup, h2, "tn", name=f"{tag}_wup")
    dx, d_gain = norm_bwd(x, gain, dh2, dx, f"{tag}_bnorm")
    return dx, d_gain, dict(w_up_t=d_up_t, conv=d_conv, w_down=d_down)


def _att_fwd(x, gain, p, tag):
    (h,) = norm_fwd(x, gain, [MXU_DTYPE], f"{tag}_norm")
    qkv = matmul(h, p["w_qkv_t"], "nt", name=f"{tag}_qkv")
    bias = rel_bias_expand(p["rel_bias"], f"{tag}_rel").transpose(1, 0, 2)
    o = attn_fwd(qkv, p["q_gain"], p["k_gain"], bias, f"{tag}_core")
    out = matmul(o, p["w_o"], "nn", res=x, name=f"{tag}_out")
    return out, (x, h, qkv, o, bias)


def _att_bwd(dx, saved, gain, p, tag):
    x, h, qkv, o, bias = saved
    do = matmul(dx, p["w_o"], "nt", name=f"{tag}_bout")
    d_wo = matmul(o, dx, "tn", name=f"{tag}_wout")
    dq, dk, dv, d_gq, d_gk, dbias = attn_bwd(qkv, do, p["q_gain"], p["k_gain"], bias, f"{tag}_bcore")
    dqkv = jnp.concatenate([dq, dk, dv], axis=1)
    dh = matmul(dqkv, p["w_qkv_t"], "nn", name=f"{tag}_bqkv")
    d_wqkv_t = matmul(dqkv, h, "tn", name=f"{tag}_wqkv")
    d_rb = rel_bias_reduce(dbias.transpose(1, 0, 2), p["rel_bias"].shape[1], f"{tag}_brel")
    dx, d_gain = norm_bwd(x, gain, dh, dx, f"{tag}_bnorm")
    return dx, d_gain, dict(w_qkv_t=d_wqkv_t, w_o=d_wo, q_gain=d_gq, k_gain=d_gk, rel_bias=d_rb)


def _pool_fwd(x, gain, p, tag):
    (hf,) = norm_fwd(x, gain, [F32], f"{tag}_norm")
    out, pooled = pool_fwd(hf, x, p["w"], p["scale"], f"{tag}_core")
    return out, (x, pooled)


def _pool_bwd(dx, saved, gain, p, tag):
    x, pooled = saved
    dpooled, d_w, d_scale = pool_bwd_a(dx, pooled, p["w"], p["scale"], f"{tag}_bcore_a")
    dh = pool_bwd_b(dpooled, f"{tag}_bcore_b")
    dx, d_gain = norm_bwd(x, gain, dh, dx, f"{tag}_bnorm")
    return dx, d_gain, dict(w=d_w, scale=d_scale)


def _rows_layout(t, hv):
    return t.T.reshape(hv, t.shape[0] // CHUNK, 1, CHUNK)


def _gdn_fwd(x, gain, p, tag):
    T = x.shape[0]
    hv = p["a_log"].shape[1]
    key_dim = p["key_dim"]
    C = p["conv"].shape[1]
    (h,) = norm_fwd(x, gain, [MXU_DTYPE], f"{tag}_norm")
    proj = matmul(h, p["w_main_t"], "nt", name=f"{tag}_in")
    ab = matmul(h, p["w_ab_t"], "nt", name=f"{tag}_in_ab")
    a, b = ab[:, :hv], ab[:, hv:2 * hv]
    qkvn = gdn_pre_fwd(proj, p["conv"], key_dim, f"{tag}_pre")
    g, beta = gdn_gate_fwd(a, b, p["a_log"], p["dt_bias"], f"{tag}_gate")
    g_rows, b_rows = _rows_layout(g, hv), _rows_layout(beta, hv)
    o, states = delta_fwd(qkvn, g_rows, b_rows, key_dim, f"{tag}_delta")
    y = gdn_post_fwd(o, proj, C, p["o_gain"], f"{tag}_post")
    out = matmul(y, p["w_o"], "nn", res=x, name=f"{tag}_out")
    return out, (x, h, proj, a, b, qkvn, g_rows, b_rows, o, states, y)


def _gdn_bwd(dx, saved, gain, p, tag):
    x, h, proj, a, b, qkvn, g_rows, b_rows, o, states, y = saved
    T = x.shape[0]
    hv = p["a_log"].shape[1]
    key_dim = p["key_dim"]
    C = p["conv"].shape[1]
    dy = matmul(dx, p["w_o"], "nt", name=f"{tag}_bout")
    d_wo = matmul(y, dx, "tn", name=f"{tag}_wout")
    do, dgate, d_ogain = gdn_post_bwd(o, proj, C, p["o_gain"], dy, f"{tag}_bpost")
    dq_v, dk_v, dv, dg_rows, dbe_rows = delta_bwd(qkvn, g_rows, b_rows, states, do, key_dim, f"{tag}_bdelta")
    dg = dg_rows.reshape(hv, T).T
    dbeta = dbe_rows.reshape(hv, T).T
    da, db, d_alog, d_dtb = gdn_gate_bwd(a, b, p["a_log"], p["dt_bias"], dg, dbeta, f"{tag}_bgate")
    du, d_conv = gdn_pre_bwd_a(proj, p["conv"], dq_v, dk_v, dv, key_dim, f"{tag}_bpre_a")
    dproj = gdn_pre_bwd_b(du, dgate, p["conv"], key_dim, f"{tag}_bpre_b")
    dab = jnp.concatenate([da, db, jnp.zeros((T, LANES - 2 * hv), F32)], axis=1)
    dh = matmul(dproj, p["w_main_t"], "nn", name=f"{tag}_bin")
    dh = matmul(dab, p["w_ab_t"], "nn", res=dh, name=f"{tag}_bin_ab")
    d_main_t = matmul(dproj, h, "tn", name=f"{tag}_win")
    d_ab_t = matmul(dab, h, "tn", name=f"{tag}_win_ab")
    dx, d_gain = norm_bwd(x, gain, dh, dx, f"{tag}_bnorm")
    return dx, d_gain, dict(w_main_t=d_main_t, w_ab_t=d_ab_t, conv=d_conv, a_log=d_alog, dt_bias=d_dtb,
                            o_gain=d_ogain, w_o=d_wo)


_MIXERS = ((_att_fwd, _att_bwd), (_pool_fwd, _pool_bwd), (_gdn_fwd, _gdn_bwd))


def local_step(x, target, W):
    depth = len(W["ffn"])
    saved = []
    for i in range(depth):
        kind, j = i % 3, i // 3
        mp = (W["att"], W["pool"], W["gdn"])[kind][j]
        x, s_mix = _MIXERS[kind][0](x, W["mix_norm"][i:i + 1], mp, f"l{i}_mix")
        x, s_ffn = _ffn_fwd(x, W["ffn_norm"][i:i + 1], W["ffn"][i], f"l{i}_ffn")
        saved.append((s_mix, s_ffn))
    loss, dx = loss_and_grad(x, target, "loss")
    G = dict(mix_norm=[None] * depth, ffn_norm=[None] * depth, ffn=[None] * depth,
             att=[None] * len(W["att"]), pool=[None] * len(W["pool"]), gdn=[None] * len(W["gdn"]))
    for i in reversed(range(depth)):
        kind, j = i % 3, i // 3
        s_mix, s_ffn = saved[i]
        dx, G["ffn_norm"][i], G["ffn"][i] = _ffn_bwd(dx, s_ffn, W["ffn_norm"][i:i + 1], W["ffn"][i], f"l{i}_ffn")
        mp = (W["att"], W["pool"], W["gdn"])[kind][j]
        dx, G["mix_norm"][i], gm = _MIXERS[kind][1](dx, s_mix, W["mix_norm"][i:i + 1], mp, f"l{i}_mix")
        G[("att", "pool", "gdn")[kind]][j] = gm
    return loss, dx, G


SHARD_AXIS = dict(att_w_qkv=2, att_w_o=1, pool_w=2, gdn_w_in=2, gdn_w_o=1, ffn_w_up=2, ffn_w_down=1,
                  att_rel_bias=2, gdn_conv=2, ffn_conv=2)
BIG = ("att_w_qkv", "att_w_o", "pool_w", "gdn_w_in", "gdn_w_o", "ffn_w_up", "ffn_w_down")
SMALL_SHARDED = ("att_rel_bias", "gdn_conv", "ffn_conv")
REPLICATED = ("mix_norm", "ffn_norm", "att_q_gain", "att_k_gain", "pool_scale", "gdn_a_log", "gdn_dt_bias",
              "gdn_o_gain")
WEIGHTS = ("mix_norm", "ffn_norm", "att_w_qkv", "att_q_gain", "att_k_gain", "att_rel_bias", "att_w_o", "pool_w",
           "pool_scale", "gdn_w_in", "gdn_conv", "gdn_a_log", "gdn_dt_bias", "gdn_o_gain", "gdn_w_o", "ffn_w_up",
           "ffn_conv", "ffn_w_down")


def _merge(stacked, axis):
    t = jnp.moveaxis(stacked, 0, axis)
    return t.reshape(t.shape[:axis] + (t.shape[axis] * t.shape[axis + 1],) + t.shape[axis + 2:])


def _split(full, axis):
    n = full.shape[axis] // N_CHIPS
    t = full.reshape(full.shape[:axis] + (N_CHIPS, n) + full.shape[axis + 1:])
    return jnp.moveaxis(t, axis, 0)


def _pad_to(t, axis, size):
    pad = [(0, 0)] * t.ndim
    pad[axis] = (0, size - t.shape[axis])
    return jnp.pad(t, pad)


def _round_up(n, m):
    return (n + m - 1) // m * m


def to_comm(name, t):
    if name in ("att_w_qkv", "gdn_w_in"):
        return t.T
    if name == "ffn_w_up":
        d, n = t.shape
        return _pad_to(t.T.reshape(2, n // 2, d), 1, _round_up(n // 2, LANES)).reshape(-1, d)
    if name == "ffn_w_down":
        return _pad_to(t, 0, _round_up(t.shape[0], LANES))
    if name == "pool_w":
        return t.reshape(-1, t.shape[-1])
    return t


def from_comm(name, r, shape):
    if name in ("att_w_qkv", "gdn_w_in"):
        return r.T
    if name == "ffn_w_up":
        d, n = shape
        return r.reshape(2, -1, d)[:, :n // 2].reshape(n, d).T
    if name == "ffn_w_down":
        return r[:shape[0]]
    return r.reshape(shape)


def _rows(t):
    return t.reshape(-1, t.shape[-1])


def build_weights(big, full):
    depth = full["ffn_conv"].shape[0]
    F4 = full["ffn_conv"].shape[2] // N_CHIPS
    F4p = _round_up(F4, LANES)
    W = dict(mix_norm=full["mix_norm"], ffn_norm=full["ffn_norm"], att=[], pool=[], gdn=[], ffn=[])
    for i in range(depth):
        conv = _pad_to(full["ffn_conv"][i].reshape(FFN_CONV, N_CHIPS, F4), 2, F4p).reshape(FFN_CONV, -1)
        W["ffn"].append(dict(w_up_t=_rows(big["ffn_w_up", i]), conv=_pad_to(conv, 0, SUBLANES),
                             w_down=_rows(big["ffn_w_down", i])))
    for j in range(full["att_q_gain"].shape[0]):
        W["att"].append(dict(w_qkv_t=_rows(big["att_w_qkv", j]), w_o=_rows(big["att_w_o", j]),
                             q_gain=full["att_q_gain"][j:j + 1], k_gain=full["att_k_gain"][j:j + 1],
                             rel_bias=full["att_rel_bias"][j]))
    for j in range(full["pool_scale"].shape[0]):
        t = big["pool_w", j]
        G = len(POOL_WINDOWS)
        dg = t.shape[-1]
        w = jnp.moveaxis(t.reshape(N_CHIPS, G, dg // N_CHIPS, dg), 0, 1).reshape(G, dg, dg)
        W["pool"].append(dict(w=w, scale=full["pool_scale"][j:j + 1]))
    for j in range(full["gdn_a_log"].shape[0]):
        C = full["gdn_conv"].shape[2]
        wt = _rows(big["gdn_w_in", j])
        V = _rows(big["gdn_w_o", j]).shape[0]
        W["gdn"].append(dict(
            w_main_t=wt[:C + V], w_ab_t=_pad_to(wt[C + V:], 0, LANES),
            conv=_pad_to(full["gdn_conv"][j], 0, SUBLANES), a_log=full["gdn_a_log"][j:j + 1],
            dt_bias=full["gdn_dt_bias"][j:j + 1], o_gain=full["gdn_o_gain"][j:j + 1], w_o=_rows(big["gdn_w_o", j]),
            key_dim=(C - V) // 2))
    return W


def big_grads(G, hv):
    out = {}

    def slots(t):
        return t.reshape(N_CHIPS, t.shape[0] // N_CHIPS, t.shape[1])

    for i, g in enumerate(G["ffn"]):
        out["ffn_w_up", i] = slots(g["w_up_t"])
        out["ffn_w_down", i] = slots(g["w_down"])
    for j, g in enumerate(G["att"]):
        out["att_w_qkv", j] = slots(g["w_qkv_t"])
        out["att_w_o", j] = slots(g["w_o"])
    for j, g in enumerate(G["pool"]):
        n, dg, _ = g["w"].shape
        out["pool_w", j] = jnp.moveaxis(g["w"].reshape(n, N_CHIPS, dg // N_CHIPS, dg), 1, 0).reshape(N_CHIPS, -1, dg)
    for j, g in enumerate(G["gdn"]):
        out["gdn_w_in", j] = slots(jnp.concatenate([g["w_main_t"], g["w_ab_t"][:2 * hv]], axis=0))
        out["gdn_w_o", j] = slots(g["w_o"])
    return out


def small_grads(G, full):
    F = full["ffn_conv"].shape[2]
    F4 = F // N_CHIPS

    def conv(g):
        return g["conv"][:FFN_CONV].reshape(FFN_CONV, N_CHIPS, -1)[:, :, :F4].reshape(FFN_CONV, F)

    return dict(
        mix_norm=jnp.concatenate(G["mix_norm"], axis=0), ffn_norm=jnp.concatenate(G["ffn_norm"], axis=0),
        ffn_conv=jnp.stack([conv(g) for g in G["ffn"]]),
        att_q_gain=jnp.concatenate([g["q_gain"] for g in G["att"]], axis=0),
        att_k_gain=jnp.concatenate([g["k_gain"] for g in G["att"]], axis=0),
        att_rel_bias=jnp.stack([g["rel_bias"] for g in G["att"]]),
        pool_scale=jnp.concatenate([g["scale"] for g in G["pool"]], axis=0),
        gdn_conv=jnp.stack([g["conv"][:GDN_CONV] for g in G["gdn"]]),
        gdn_a_log=jnp.concatenate([g["a_log"] for g in G["gdn"]], axis=0),
        gdn_dt_bias=jnp.concatenate([g["dt_bias"] for g in G["gdn"]], axis=0),
        gdn_o_gain=jnp.concatenate([g["o_gain"] for g in G["gdn"]], axis=0))


ANY = BS(memory_space=pl.ANY)
PACK_COLS = 1024
PACK_ROWS = 32


def _place():
    x, y, c = lax.axis_index("x"), lax.axis_index("y"), lax.axis_index("c")
    chips = [(1 - x, y), (x, 1 - y), (1 - x, 1 - y)]
    return x, y, c, chips


def _remote(src, dst, send_sem, recv_sem, to):
    return pltpu.make_async_remote_copy(src_ref=src, dst_ref=dst, send_sem=send_sem, recv_sem=recv_sem,
                                        device_id=to, device_id_type=MESH)


def gather_chips(shard, name):
    R, C = shard.shape
    half = R // 2

    def body(x_ref, o_ref, send_sems, recv_sems, local_sem):
        x, y, c, chips = _place()
        mine_rows = pl.ds(c * half, half)
        other_rows = pl.ds((1 - c) * half, half)
        own = pltpu.make_async_copy(x_ref, o_ref.at[2 * x + y], local_sem)
        own.start()
        first = [_remote(x_ref.at[mine_rows], o_ref.at[2 * x + y, mine_rows], send_sems.at[j], recv_sems.at[j],
                         (cx, cy, c)) for j, (cx, cy) in enumerate(chips)]
        for cp in first:
            cp.start()
        passed = []
        for j, (cx, cy) in enumerate(chips):
            landed = o_ref.at[2 * cx + cy, mine_rows]
            _remote(landed, landed, send_sems.at[j], recv_sems.at[j], (cx, cy, c)).wait_recv()
            cp = _remote(landed, landed, send_sems.at[3 + j], recv_sems.at[3 + j], (x, y, 1 - c))
            cp.start()
            passed.append(cp)
        for j, (cx, cy) in enumerate(chips):
            landed = o_ref.at[2 * cx + cy, other_rows]
            _remote(landed, landed, send_sems.at[3 + j], recv_sems.at[3 + j], (x, y, 1 - c)).wait_recv()
        for cp in first + passed:
            cp.wait_send()
        own.wait()

    return pl.pallas_call(
        body, out_shape=S((N_CHIPS, R, C), shard.dtype), in_specs=[ANY], out_specs=ANY,
        scratch_shapes=[pltpu.SemaphoreType.DMA((6,)), pltpu.SemaphoreType.DMA((6,)), pltpu.SemaphoreType.DMA],
        name=name)(shard)


def _tile(R, hc):
    if R % 256 == 0:
        return _pick(R, (512, 256)), hc
    return R, _pick(hc, (256, 128))


def _half(c, hc):
    return pl.ds(pl.multiple_of(c * hc, hc), hc)


def prep_slot(t, s_me, name):
    R, C = t.shape
    br, bc = _tile(R, C // 2)

    def body(s_ref, t_ref, o_ref):
        o_ref[0] = t_ref[...].astype(o_ref.dtype)

    return pl.pallas_call(
        body, grid_spec=pltpu.PrefetchScalarGridSpec(
            num_scalar_prefetch=1, grid=(R // br, C // bc),
            in_specs=[BS((br, bc), lambda i, j, s: (i, j))],
            out_specs=BS((1, br, bc), lambda i, j, s: (s[0], i, j))),
        out_shape=S((N_CHIPS, R, C), MXU_DTYPE), compiler_params=_cp(("parallel", "parallel")), name=name)(
            s_me.reshape(1), t)


def gather_slots(arrs, name):
    nt = len(arrs)

    def body(*refs):
        outs = refs[nt:2 * nt]
        send_sems, recv_sems = refs[2 * nt:]
        x, y, c, chips = _place()
        me = 2 * x + y
        first, passed = [], []
        for t, o in enumerate(outs):
            mine = _half(c, o.shape[2] // 2)
            for j, (cx, cy) in enumerate(chips):
                cp = _remote(o.at[me, :, mine], o.at[me, :, mine], send_sems.at[t, j], recv_sems.at[t, j], (cx, cy, c))
                cp.start()
                first.append(cp)
        for t, o in enumerate(outs):
            mine = _half(c, o.shape[2] // 2)
            for j, (cx, cy) in enumerate(chips):
                landed = o.at[2 * cx + cy, :, mine]
                _remote(landed, landed, send_sems.at[t, j], recv_sems.at[t, j], (cx, cy, c)).wait_recv()
                cp = _remote(landed, landed, send_sems.at[t, 3 + j], recv_sems.at[t, 3 + j], (x, y, 1 - c))
                cp.start()
                passed.append(cp)
        for t, o in enumerate(outs):
            other = _half(1 - c, o.shape[2] // 2)
            for j, (cx, cy) in enumerate(chips):
                landed = o.at[2 * cx + cy, :, other]
                _remote(landed, landed, send_sems.at[t, 3 + j], recv_sems.at[t, 3 + j], (x, y, 1 - c)).wait_recv()
        for cp in first + passed:
            cp.wait_send()

    return pl.pallas_call(
        body, out_shape=[S(a.shape, a.dtype) for a in arrs], in_specs=[ANY] * nt, out_specs=[ANY] * nt,
        input_output_aliases={t: t for t in range(nt)},
        scratch_shapes=[pltpu.SemaphoreType.DMA((nt, 6)), pltpu.SemaphoreType.DMA((nt, 6))], name=name)(*arrs)


def swap_cols(gs, name):
    nt = len(gs)

    def body(*refs):
        ins, outs = refs[:nt], refs[nt:2 * nt]
        send_sems, recv_sems = refs[2 * nt:]
        x, y, c, _ = _place()
        sent = []
        for t, (g, o) in enumerate(zip(ins, outs)):
            cp = _remote(g.at[:, :, _half(1 - c, o.shape[2])], o, send_sems.at[t], recv_sems.at[t], (x, y, 1 - c))
            cp.start()
            sent.append(cp)
        for cp in sent:
            cp.wait()

    return pl.pallas_call(
        body, out_shape=[S(g.shape[:2] + (g.shape[2] // 2,), g.dtype) for g in gs], in_specs=[ANY] * nt,
        out_specs=[ANY] * nt, scratch_shapes=[pltpu.SemaphoreType.DMA((nt,)), pltpu.SemaphoreType.DMA((nt,))],
        name=name)(*gs)


def add_cols2(g, other, c, name):
    n, R, C = g.shape
    hc = C // 2
    br, bc = _tile(R, hc)
    nj = hc // bc

    def body(c_ref, g_ref, o_ref, out_ref):
        out_ref[...] = (g_ref[...] + o_ref[...]).astype(out_ref.dtype)

    blk = BS((1, br, bc), lambda s, i, j, c_ref: (s, i, j))
    return pl.pallas_call(
        body, grid_spec=pltpu.PrefetchScalarGridSpec(
            num_scalar_prefetch=1, grid=(n, R // br, nj),
            in_specs=[BS((1, br, bc), lambda s, i, j, c_ref: (s, i, c_ref[0] * nj + j)), blk], out_specs=blk),
        out_shape=S((n, R, hc), BF16), compiler_params=_cp(("parallel", "parallel", "parallel")), name=name)(
            c, g, other)


def scatter_cols(ps, name):
    nt = len(ps)

    def body(*refs):
        ins, outs = refs[:nt], refs[nt:2 * nt]
        send_sems, recv_sems = refs[2 * nt:]
        x, y, c, chips = _place()
        sent = []
        for t, (p, o) in enumerate(zip(ins, outs)):
            for j, (cx, cy) in enumerate(chips):
                cp = _remote(p.at[2 * cx + cy], o.at[j], send_sems.at[t, j], recv_sems.at[t, j], (cx, cy, c))
                cp.start()
                sent.append(cp)
        for cp in sent:
            cp.wait()

    return pl.pallas_call(
        body, out_shape=[S((N_CHIPS - 1,) + p.shape[1:], p.dtype) for p in ps], in_specs=[ANY] * nt,
        out_specs=[ANY] * nt,
        scratch_shapes=[pltpu.SemaphoreType.DMA((nt, 3)), pltpu.SemaphoreType.DMA((nt, 3))], name=name)(*ps)


def add_cols4(p, got, place, name):
    n, R, hc = p.shape
    br, bc = _tile(R, hc)
    nj = hc // bc

    def body(pl_ref, p_ref, g_ref, out_ref):
        acc = p_ref[0].astype(F32)
        for j in range(n - 1):
            acc += g_ref[j].astype(F32)
        out_ref[...] = acc

    return pl.pallas_call(
        body, grid_spec=pltpu.PrefetchScalarGridSpec(
            num_scalar_prefetch=1, grid=(R // br, nj),
            in_specs=[BS((1, br, bc), lambda i, j, pl_ref: (pl_ref[0], i, j)),
                      BS((n - 1, br, bc), lambda i, j, pl_ref: (0, i, j))],
            out_specs=BS((br, bc), lambda i, j, pl_ref: (i, pl_ref[1] * nj + j))),
        out_shape=S((R, 2 * hc), F32), compiler_params=_cp(("parallel", "parallel")), name=name)(place, p, got)


def join_cols(rs, name):
    nt = len(rs)

    def body(*refs):
        outs = refs[nt:2 * nt]
        send_sems, recv_sems = refs[2 * nt:]
        x, y, c, _ = _place()
        sent = []
        for t, o in enumerate(outs):
            mine = o.at[:, _half(c, o.shape[1] // 2)]
            cp = _remote(mine, mine, send_sems.at[t], recv_sems.at[t], (x, y, 1 - c))
            cp.start()
            sent.append(cp)
        for t, o in enumerate(outs):
            theirs = o.at[:, _half(1 - c, o.shape[1] // 2)]
            _remote(theirs, theirs, send_sems.at[t], recv_sems.at[t], (x, y, 1 - c)).wait_recv()
        for cp in sent:
            cp.wait_send()

    return pl.pallas_call(
        body, out_shape=[S(r.shape, r.dtype) for r in rs], in_specs=[ANY] * nt, out_specs=[ANY] * nt,
        input_output_aliases={t: t for t in range(nt)},
        scratch_shapes=[pltpu.SemaphoreType.DMA((nt,)), pltpu.SemaphoreType.DMA((nt,))], name=name)(*rs)


def sum_devices(v, name):
    R, C = v.shape

    def body(v_ref, o_ref, slots, send_sems, recv_sems):
        x, y, c, _ = _place()
        me = 4 * x + 2 * y + c
        slots[me] = v_ref[...]
        sent = []
        for r in range(1, 8):
            peer = (x ^ (r >> 2), y ^ ((r >> 1) & 1), c ^ (r & 1))
            cp = _remote(v_ref, slots.at[me], send_sems.at[r - 1], recv_sems.at[r - 1], peer)
            cp.start()
            sent.append(cp)
        for r in range(1, 8):
            peer = (x ^ (r >> 2), y ^ ((r >> 1) & 1), c ^ (r & 1))
            theirs = slots.at[4 * peer[0] + 2 * peer[1] + peer[2]]
            _remote(v_ref, theirs, send_sems.at[r - 1], recv_sems.at[r - 1], peer).wait_recv()
        for cp in sent:
            cp.wait_send()
        acc = slots[0]
        for k in range(1, 8):
            acc += slots[k]
        o_ref[...] = acc

    vm = BS(memory_space=pltpu.VMEM)
    return pl.pallas_call(
        body, out_shape=S((R, C), F32), in_specs=[vm], out_specs=vm,
        scratch_shapes=[pltpu.VMEM((8, R, C), F32), pltpu.SemaphoreType.DMA((7,)), pltpu.SemaphoreType.DMA((7,))],
        compiler_params=pltpu.CompilerParams(vmem_limit_bytes=32 * MIB), name=name)(v)


def _pack(arrays, dtype, cols, row_mult):
    flat = jnp.concatenate([a.astype(dtype).reshape(-1) for a in arrays])
    n = flat.shape[0]
    total = _round_up(n, cols * row_mult)
    return jnp.pad(flat, (0, total - n)).reshape(total // cols, cols)


def _unpack(flat, shapes):
    out, off = [], 0
    for shp in shapes:
        n = 1
        for d in shp:
            n *= d
        out.append(flat[..., off:off + n].reshape(flat.shape[:-1] + tuple(shp)))
        off += n
    return out


def _layer_groups(depth):
    groups = []
    for i in range(depth):
        kind, j = i % 3, i // 3
        mix = ((("att_w_qkv", j), ("att_w_o", j)), (("pool_w", j),), (("gdn_w_in", j), ("gdn_w_o", j)))[kind]
        groups.append(mix + (("ffn_w_up", i), ("ffn_w_down", i)))
    return groups


def kernel(x, mix_norm, ffn_norm, att_w_qkv, att_q_gain, att_k_gain, att_rel_bias, att_w_o, pool_w, pool_scale, gdn_w_in, gdn_conv, gdn_a_log, gdn_dt_bias, gdn_o_gain, gdn_w_o, ffn_w_up, ffn_conv, ffn_w_down, loss_target, m_mix_norm, m_ffn_norm, m_att_w_qkv, m_att_q_gain, m_att_k_gain, m_att_rel_bias, m_att_w_o, m_pool_w, m_pool_scale, m_gdn_w_in, m_gdn_conv, m_gdn_a_log, m_gdn_dt_bias, m_gdn_o_gain, m_gdn_w_o, m_ffn_w_up, m_ffn_conv, m_ffn_w_down, v_mix_norm, v_ffn_norm, v_att_w_qkv, v_att_q_gain, v_att_k_gain, v_att_rel_bias, v_att_w_o, v_pool_w, v_pool_scale, v_gdn_w_in, v_gdn_conv, v_gdn_a_log, v_gdn_dt_bias, v_gdn_o_gain, v_gdn_w_o, v_ffn_w_up, v_ffn_conv, v_ffn_w_down):
    w = dict(mix_norm=mix_norm, ffn_norm=ffn_norm, att_w_qkv=att_w_qkv, att_q_gain=att_q_gain, att_k_gain=att_k_gain, att_rel_bias=att_rel_bias, att_w_o=att_w_o, pool_w=pool_w, pool_scale=pool_scale, gdn_w_in=gdn_w_in, gdn_conv=gdn_conv, gdn_a_log=gdn_a_log, gdn_dt_bias=gdn_dt_bias, gdn_o_gain=gdn_o_gain, gdn_w_o=gdn_w_o, ffn_w_up=ffn_w_up, ffn_conv=ffn_conv, ffn_w_down=ffn_w_down)
    m = dict(mix_norm=m_mix_norm, ffn_norm=m_ffn_norm, att_w_qkv=m_att_w_qkv, att_q_gain=m_att_q_gain, att_k_gain=m_att_k_gain, att_rel_bias=m_att_rel_bias, att_w_o=m_att_w_o, pool_w=m_pool_w, pool_scale=m_pool_scale, gdn_w_in=m_gdn_w_in, gdn_conv=m_gdn_conv, gdn_a_log=m_gdn_a_log, gdn_dt_bias=m_gdn_dt_bias, gdn_o_gain=m_gdn_o_gain, gdn_w_o=m_gdn_w_o, ffn_w_up=m_ffn_w_up, ffn_conv=m_ffn_conv, ffn_w_down=m_ffn_w_down)
    v = dict(mix_norm=v_mix_norm, ffn_norm=v_ffn_norm, att_w_qkv=v_att_w_qkv, att_q_gain=v_att_q_gain, att_k_gain=v_att_k_gain, att_rel_bias=v_att_rel_bias, att_w_o=v_att_w_o, pool_w=v_pool_w, pool_scale=v_pool_scale, gdn_w_in=v_gdn_w_in, gdn_conv=v_gdn_conv, gdn_a_log=v_gdn_a_log, gdn_dt_bias=v_gdn_dt_bias, gdn_o_gain=v_gdn_o_gain, gdn_w_o=v_gdn_w_o, ffn_w_up=v_ffn_w_up, ffn_conv=v_ffn_conv, ffn_w_down=v_ffn_w_down)
    depth = ffn_w_up.shape[0]
    my_c = lax.axis_index("c").astype(jnp.int32)
    my_chip = (2 * lax.axis_index("x") + lax.axis_index("y")).astype(jnp.int32)
    groups = _layer_groups(depth)

    core = my_c.reshape(1)
    place = jnp.stack([my_chip, my_c])

    full = {n: w[n] for n in REPLICATED}
    big = {}
    for i, group in enumerate(groups):
        slots = [prep_slot(to_comm(n, w[n][j]), my_chip, f"prep_{n}_{j}") for n, j in group]
        for (n, j), t in zip(group, gather_slots(slots, f"gather_l{i}")):
            big[n, j] = t
    small = [w[n] for n in SMALL_SHARDED]
    got = gather_chips(_pack(small, F32, LANES, PACK_ROWS), "gather_small").reshape(N_CHIPS, -1)
    for n, t in zip(SMALL_SHARDED, _unpack(got, [s.shape for s in small])):
        full[n] = _merge(t, SHARD_AXIS[n])

    W = build_weights(big, full)
    loss, grad_x, G = local_step(x[0], loss_target[0], W)
    loss = lax.psum(loss[0, 0], ("x", "y", "c"))
    gfull = small_grads(G, full)
    gbig = big_grads(G, gdn_a_log.shape[1])

    grads = {}
    gparts = {n: [None] * w[n].shape[0] for n in BIG}
    for i, group in enumerate(groups):
        gs = [gbig[n, j] for n, j in group]
        theirs = swap_cols(gs, f"rs_swap_l{i}")
        chip_sums = [add_cols2(g, a, core, f"rs_add2_{n}_{j}") for (n, j), g, a in zip(group, gs, theirs)]
        got = scatter_cols(chip_sums, f"rs_scatter_l{i}")
        mine = [add_cols4(p, b, place, f"rs_add4_{n}_{j}") for (n, j), p, b in zip(group, chip_sums, got)]
        for (n, j), r in zip(group, join_cols(mine, f"rs_join_l{i}")):
            gparts[n][j] = from_comm(n, r, w[n][j].shape)
    for n in BIG:
        grads[n] = jnp.stack(gparts[n])

    small_names = REPLICATED + SMALL_SHARDED
    packed = _pack([gfull[n] for n in small_names], F32, LANES, SUBLANES)
    summed = sum_devices(packed, "sum_small").reshape(-1)
    for n, t in zip(small_names, _unpack(summed, [gfull[n].shape for n in small_names])):
        if n in SHARD_AXIS:
            size = w[n].shape[SHARD_AXIS[n]]
            t = lax.dynamic_slice_in_dim(t, my_chip * size, size, axis=SHARD_AXIS[n])
        grads[n] = t

    delta, new_m, new_v = {}, {}, {}
    for n in WEIGHTS:
        delta[n], new_m[n], new_v[n] = adamw(w[n], grads[n], m[n], v[n], f"adamw_{n}")
    return (loss, grad_x[None], *[grads[n] for n in WEIGHTS], *[delta[n] for n in WEIGHTS],
            *[new_m[n] for n in WEIGHTS], *[new_v[n] for n in WEIGHTS])
```

```python
import functools

import jax
import jax.numpy as jnp
from jax import lax
from jax.experimental import pallas as pl
from jax.experimental.pallas import tpu as pltpu

F32 = jnp.float32
BF16 = jnp.bfloat16
MXU_DTYPE = BF16
HI = lax.Precision.HIGHEST
S = jax.ShapeDtypeStruct
BS = pl.BlockSpec

EPS = 1e-6
MASK_VALUE = -1e30
CHUNK = 64
HEAD = 128
LEFT_CHUNKS = 8
BAND_LEFT = LEFT_CHUNKS * CHUNK
BAND = BAND_LEFT + CHUNK
MAX_REL = 256
ATT_CHUNKS = 4
ATT_QB = ATT_CHUNKS * CHUNK
ATT_BAND = BAND_LEFT + ATT_QB
POOL_WINDOWS = (2, 4, 8, 16)
POOL_HALO = 16
GDN_CONV = 4
DELTA_HEADS = 8
FFN_CONV = 3
SUBLANES = 8
LANES = 128
FF_ALIGN = 512
N_CHIPS = 4
ADAM_LR, ADAM_B1, ADAM_B2, ADAM_EPS, ADAM_WD, ADAM_STEP = 0.001, 0.9, 0.999, 1e-08, 0.01, 10
MIB = 1024 * 1024
MESH = pl.DeviceIdType.MESH


def _cp(sems, vmem_mib=40):
    return pltpu.CompilerParams(dimension_semantics=sems, vmem_limit_bytes=vmem_mib * MIB)


def _pick(n, cands):
    for c in cands:
        if n % c == 0:
            return c
    return n


def _mx(x):
    return x.astype(MXU_DTYPE)


def _dot(a, b, dims, hi=False):
    if hi:
        return lax.dot_general(a.astype(F32), b.astype(F32), (dims, ((), ())), precision=HI,
                               preferred_element_type=F32)
    return lax.dot_general(_mx(a), _mx(b), (dims, ((), ())), preferred_element_type=F32)


def dot_nn(a, b, hi=False):
    return _dot(a, b, ((1,), (0,)), hi)


def dot_nt(a, b, hi=False):
    return _dot(a, b, ((1,), (1,)), hi)


def dot_tn(a, b, hi=False):
    return _dot(a, b, ((0,), (0,)), hi)


def _sigmoid(x):
    return 1.0 / (1.0 + jnp.exp(-x))


def matmul(a, b, mode, *, out_dtype=F32, res=None, name):
    if mode == "nn":
        (M, K), N = a.shape, b.shape[1]
    elif mode == "nt":
        (M, K), N = a.shape, b.shape[0]
    else:
        (K, M), N = a.shape, b.shape[1]
    bm = _pick(M, (1024, 512, 256, 128))
    bn = _pick(N, (1024, 512, 256, 128))
    bk = _pick(K, (2048, 1408, 1024, 512, 256, 128))
    nk = K // bk
    if mode == "nn":
        a_spec = BS((bm, bk), lambda i, j, k: (i, k))
        b_spec = BS((bk, bn), lambda i, j, k: (k, j))
        dot = dot_nn
    elif mode == "nt":
        a_spec = BS((bm, bk), lambda i, j, k: (i, k))
        b_spec = BS((bn, bk), lambda i, j, k: (j, k))
        dot = dot_nt
    else:
        a_spec = BS((bk, bm), lambda i, j, k: (k, i))
        b_spec = BS((bk, bn), lambda i, j, k: (k, j))
        dot = dot_tn
    o_spec = BS((bm, bn), lambda i, j, k: (i, j))
    has_res = res is not None

    def body(*refs):
        if has_res:
            a_ref, b_ref, r_ref, o_ref, acc = refs
        else:
            a_ref, b_ref, o_ref, acc = refs
            r_ref = None
        k = pl.program_id(2)
        p = dot(a_ref[...], b_ref[...])

        def finish(total):
            if has_res:
                total = r_ref[...] + total
            o_ref[...] = total.astype(o_ref.dtype)

        if nk == 1:
            finish(p)
        else:
            @pl.when(k == 0)
            def _():
                acc[...] = p

            @pl.when(jnp.logical_and(k > 0, k < nk - 1))
            def _():
                acc[...] += p

            @pl.when(k == nk - 1)
            def _():
                finish(acc[...] + p)

    in_specs = [a_spec, b_spec] + ([o_spec] if has_res else [])
    args = (a, b) + ((res,) if has_res else ())
    return pl.pallas_call(
        body, grid=(M // bm, N // bn, nk), in_specs=in_specs, out_specs=o_spec,
        out_shape=S((M, N), out_dtype), scratch_shapes=[pltpu.VMEM((bm, bn), F32)],
        compiler_params=_cp(("parallel", "parallel", "arbitrary"), 48), name=name)(*args)


def norm_fwd(x, gain, out_dtypes, name):
    T, D = x.shape
    bt = _pick(T, (256, 128, 64))

    def body(x_ref, g_ref, *o_refs):
        xv = x_ref[...]
        r = lax.rsqrt(jnp.mean(xv * xv, axis=-1, keepdims=True) + EPS)
        y = (xv * r) * g_ref[...]
        for o in o_refs:
            o[...] = y.astype(o.dtype)

    row = BS((bt, D), lambda i: (i, 0))
    return pl.pallas_call(
        body, grid=(T // bt,), in_specs=[row, BS((1, D), lambda i: (0, 0))],
        out_specs=[row] * len(out_dtypes), out_shape=[S((T, D), dt) for dt in out_dtypes],
        compiler_params=_cp(("parallel",)), name=name)(x, gain)


def norm_bwd(x, gain, dy, dres, name):
    T, D = x.shape
    bt = _pick(T, (256, 128, 64))

    def body(x_ref, g_ref, dy_ref, dres_ref, dx_ref, dg_ref):
        i = pl.program_id(0)
        xv = x_ref[...]
        dyv = dy_ref[...].astype(F32)
        r = lax.rsqrt(jnp.mean(xv * xv, axis=-1, keepdims=True) + EPS)
        xhat = xv * r
        dxhat = dyv * g_ref[...]
        dx = r * (dxhat - xhat * jnp.mean(dxhat * xhat, axis=-1, keepdims=True))
        dx_ref[...] = dres_ref[...] + dx

        @pl.when(i == 0)
        def _():
            dg_ref[...] = jnp.zeros_like(dg_ref)

        dg_ref[...] += jnp.sum(dyv * xhat, axis=0, keepdims=True)

    row = BS((bt, D), lambda i: (i, 0))
    vec = BS((1, D), lambda i: (0, 0))
    return pl.pallas_call(
        body, grid=(T // bt,), in_specs=[row, vec, row, row], out_specs=[row, vec],
        out_shape=[S((T, D), F32), S((1, D), F32)],
        compiler_params=_cp(("arbitrary",)), name=name)(x, gain, dy, dres)


def loss_and_grad(y, target, name):
    T, D = y.shape
    bt = _pick(T, (256, 128, 64))
    nt = T // bt

    def body(y_ref, t_ref, l_ref, dy_ref, acc):
        i = pl.program_id(0)
        e = y_ref[...] - t_ref[...]
        dy_ref[...] = e * (1.0 / D)

        @pl.when(i == 0)
        def _():
            acc[...] = jnp.zeros_like(acc)

        acc[...] += jnp.sum(e * e, axis=0, keepdims=True)

        @pl.when(i == nt - 1)
        def _():
            l_ref[...] = jnp.sum(acc[...], axis=1, keepdims=True) * (0.5 / D)

    row = BS((bt, D), lambda i: (i, 0))
    return pl.pallas_call(
        body, grid=(nt,), in_specs=[row, row], out_specs=[BS((1, 1), lambda i: (0, 0)), row],
        out_shape=[S((1, 1), F32), S((T, D), F32)], scratch_shapes=[pltpu.VMEM((1, D), F32)],
        compiler_params=_cp(("arbitrary",)), name=name)(y, target)


def _prev_halo(bt):
    return lambda i: (jnp.maximum(i * (bt // SUBLANES) - 1, 0), 0)


def _ffn_u(i, a_ref, halo_ref, w_ref, ext, bt):
    ext[pl.ds(0, SUBLANES), :] = jnp.where(i > 0, halo_ref[...], 0.0)
    ext[pl.ds(SUBLANES, bt), :] = a_ref[...]
    u = w_ref[2:3, :] * ext[pl.ds(SUBLANES, bt), :]
    u += w_ref[1:2, :] * ext[pl.ds(SUBLANES - 1, bt), :]
    u += w_ref[0:1, :] * ext[pl.ds(SUBLANES - 2, bt), :]
    return u


def ffn_act_fwd(up, conv_w, name):
    T, F2 = up.shape
    Fp = F2 // 2
    bt = _pick(T, (128, 64))

    def body(a_ref, b_ref, halo_ref, w_ref, o_ref, ext):
        u = _ffn_u(pl.program_id(0), a_ref, halo_ref, w_ref, ext, bt)
        o_ref[...] = (u * _sigmoid(u) * b_ref[...]).astype(o_ref.dtype)

    return pl.pallas_call(
        body, grid=(T // bt,),
        in_specs=[BS((bt, Fp), lambda i: (i, 0)), BS((bt, Fp), lambda i: (i, 1)),
                  BS((SUBLANES, Fp), _prev_halo(bt)), BS((SUBLANES, Fp), lambda i: (0, 0))],
        out_specs=BS((bt, Fp), lambda i: (i, 0)), out_shape=S((T, Fp), MXU_DTYPE),
        scratch_shapes=[pltpu.VMEM((bt + SUBLANES, Fp), F32)],
        compiler_params=_cp(("arbitrary",)), name=name)(up, up, up, conv_w)


def ffn_act_bwd_a(up, conv_w, g_act, name):
    T, F2 = up.shape
    Fp = F2 // 2
    bt = _pick(T, (128, 64))

    def body(a_ref, b_ref, halo_ref, w_ref, g_ref, du_ref, db_ref, dw_ref, ext):
        i = pl.program_id(0)
        u = _ffn_u(i, a_ref, halo_ref, w_ref, ext, bt)
        sg = _sigmoid(u)
        g = g_ref[...]
        db_ref[...] = (g * (u * sg)).astype(db_ref.dtype)
        du = g * b_ref[...] * (sg * (1.0 + u * (1.0 - sg)))
        du_ref[...] = du

        @pl.when(i == 0)
        def _():
            dw_ref[...] = jnp.zeros_like(dw_ref)

        for j in range(FFN_CONV):
            shifted = ext[pl.ds(SUBLANES - (FFN_CONV - 1) + j, bt), :]
            dw_ref[j:j + 1, :] += jnp.sum(du * shifted, axis=0, keepdims=True)

    blk = BS((bt, Fp), lambda i: (i, 0))
    full = BS((SUBLANES, Fp), lambda i: (0, 0))
    return pl.pallas_call(
        body, grid=(T // bt,),
        in_specs=[blk, BS((bt, Fp), lambda i: (i, 1)), BS((SUBLANES, Fp), _prev_halo(bt)), full, blk],
        out_specs=[blk, blk, full],
        out_shape=[S((T, Fp), F32), S((T, Fp), MXU_DTYPE), S((SUBLANES, Fp), F32)],
        scratch_shapes=[pltpu.VMEM((bt + SUBLANES, Fp), F32)],
        compiler_params=_cp(("arbitrary",)), name=name)(up, up, up, conv_w, g_act)


def ffn_act_bwd_b(du, db, conv_w, name):
    T, Fp = du.shape
    bt = _pick(T, (128, 64))
    nt = T // bt

    def body(du_ref, halo_ref, db_ref, w_ref, o_ref, ext):
        i = pl.program_id(0)
        ext[pl.ds(0, bt), :] = du_ref[...]
        ext[pl.ds(bt, SUBLANES), :] = jnp.where(i < nt - 1, halo_ref[...], 0.0)
        da = w_ref[2:3, :] * ext[pl.ds(0, bt), :]
        da += w_ref[1:2, :] * ext[pl.ds(1, bt), :]
        da += w_ref[0:1, :] * ext[pl.ds(2, bt), :]
        o_ref[:, pl.ds(0, Fp)] = da.astype(o_ref.dtype)
        o_ref[:, pl.ds(Fp, Fp)] = db_ref[...]

    blk = BS((bt, Fp), lambda i: (i, 0))
    nxt = BS((SUBLANES, Fp), lambda i: (jnp.minimum((i + 1) * (bt // SUBLANES), T // SUBLANES - 1), 0))
    return pl.pallas_call(
        body, grid=(nt,), in_specs=[blk, nxt, blk, BS((SUBLANES, Fp), lambda i: (0, 0))],
        out_specs=BS((bt, 2 * Fp), lambda i: (i, 0)), out_shape=S((T, 2 * Fp), MXU_DTYPE),
        scratch_shapes=[pltpu.VMEM((bt + SUBLANES, Fp), F32)],
        compiler_params=_cp(("arbitrary",)), name=name)(du, du, db, conv_w)


def _attn_fill(k_ref, v_ref, gk, kn_scr, vb_scr, T):
    kn_scr[pl.ds(0, BAND_LEFT), :] = jnp.zeros((BAND_LEFT, HEAD), kn_scr.dtype)
    vb_scr[pl.ds(0, BAND_LEFT), :] = jnp.zeros((BAND_LEFT, HEAD), vb_scr.dtype)
    rb = 512

    def fill(r, carry):
        rows = pl.ds(pl.multiple_of(r * rb, rb), rb)
        dst = pl.ds(pl.multiple_of(BAND_LEFT + r * rb, rb), rb)
        k = k_ref[rows, :]
        rk = lax.rsqrt(jnp.mean(k * k, axis=-1, keepdims=True) + EPS)
        kn_scr[dst, :] = ((k * rk) * gk).astype(kn_scr.dtype)
        vb_scr[dst, :] = v_ref[rows, :].astype(vb_scr.dtype)
        return carry

    lax.fori_loop(0, T // rb, fill, 0)


def _attn_probs(c, q_ref, gq, bias_ref, kn_scr):
    q = q_ref[...]
    rq = lax.rsqrt(jnp.mean(q * q, axis=-1, keepdims=True) + EPS)
    qn = (q * rq) * gq
    band = pl.ds(pl.multiple_of(c * ATT_QB, ATT_QB), ATT_BAND)
    kb = kn_scr[band, :]
    s = dot_nt(qn, kb) * (HEAD ** -0.5) + bias_ref[0]
    pos = c * ATT_QB - BAND_LEFT + lax.broadcasted_iota(jnp.int32, (ATT_QB, ATT_BAND), 1)
    s = jnp.where(pos >= 0, s, MASK_VALUE)
    m = jnp.max(s, axis=-1, keepdims=True)
    e = jnp.exp(s - m)
    p = e / jnp.sum(e, axis=-1, keepdims=True)
    return q, rq, qn, kb, p


def attn_fwd(qkv, gq, gk, bias, name):
    T, D3 = qkv.shape
    D = D3 // 3
    H = D // HEAD
    NC = T // ATT_QB

    def body(q_ref, k_ref, v_ref, gq_ref, gk_ref, bias_ref, o_ref, kn_scr, vb_scr):
        c = pl.program_id(1)

        @pl.when(c == 0)
        def _():
            _attn_fill(k_ref, v_ref, gk_ref[...], kn_scr, vb_scr, T)

        _, _, _, _, p = _attn_probs(c, q_ref, gq_ref[...], bias_ref, kn_scr)
        band = pl.ds(pl.multiple_of(c * ATT_QB, ATT_QB), ATT_BAND)
        o_ref[...] = dot_nn(p, vb_scr[band, :]).astype(o_ref.dtype)

    vec = BS((1, HEAD), lambda h, c: (0, 0))
    return pl.pallas_call(
        body, grid=(H, NC),
        in_specs=[BS((ATT_QB, HEAD), lambda h, c: (c, h)), BS((T, HEAD), lambda h, c: (0, H + h)),
                  BS((T, HEAD), lambda h, c: (0, 2 * H + h)), vec, vec,
                  BS((1, ATT_QB, ATT_BAND), lambda h, c: (h, 0, 0))],
        out_specs=BS((ATT_QB, HEAD), lambda h, c: (c, h)), out_shape=S((T, D), MXU_DTYPE),
        scratch_shapes=[pltpu.VMEM((T + BAND_LEFT, HEAD), MXU_DTYPE)] * 2,
        compiler_params=_cp(("arbitrary", "arbitrary"), 48), name=name)(qkv, qkv, qkv, gq, gk, bias)


def attn_bwd(qkv, do, gq, gk, bias, name):
    T, D3 = qkv.shape
    D = D3 // 3
    H = D // HEAD
    NC = T // ATT_QB
    scale = HEAD ** -0.5

    def body(q_ref, k_ref, v_ref, do_ref, gq_ref, gk_ref, bias_ref,
             dq_ref, dk_ref, dv_ref, dgq_ref, dgk_ref, dbias_ref, kn_scr, vb_scr, dkn_acc, dv_acc):
        h = pl.program_id(0)
        c = pl.program_id(1)
        gq = gq_ref[...]
        gk = gk_ref[...]

        @pl.when(c == 0)
        def _():
            _attn_fill(k_ref, v_ref, gk, kn_scr, vb_scr, T)
            dkn_acc[...] = jnp.zeros_like(dkn_acc)
            dv_acc[...] = jnp.zeros_like(dv_acc)
            dbias_ref[...] = jnp.zeros_like(dbias_ref)

        @pl.when(jnp.logical_and(c == 0, h == 0))
        def _():
            dgq_ref[...] = jnp.zeros_like(dgq_ref)
            dgk_ref[...] = jnp.zeros_like(dgk_ref)

        q, rq, qn, kb, p = _attn_probs(c, q_ref, gq, bias_ref, kn_scr)
        band = pl.ds(pl.multiple_of(c * ATT_QB, ATT_QB), ATT_BAND)
        dov = do_ref[...]
        dv_acc[band, :] += dot_tn(p, dov)
        dp = dot_nt(dov, vb_scr[band, :])
        ds = p * (dp - jnp.sum(dp * p, axis=-1, keepdims=True))
        dbias_ref[0] += ds
        dss = ds * scale
        dqn = dot_nn(dss, kb)
        dkn_acc[band, :] += dot_tn(dss, qn)
        xhat = q * rq
        dgq_ref[...] += jnp.sum(dqn * xhat, axis=0, keepdims=True)
        dxhat = dqn * gq
        dq = rq * (dxhat - xhat * jnp.mean(dxhat * xhat, axis=-1, keepdims=True))
        dq_ref[...] = dq.astype(dq_ref.dtype)

        @pl.when(c == NC - 1)
        def _():
            rb = 512

            def fin(r, carry):
                rows = pl.ds(pl.multiple_of(r * rb, rb), rb)
                src = pl.ds(pl.multiple_of(BAND_LEFT + r * rb, rb), rb)
                k = k_ref[rows, :]
                rk = lax.rsqrt(jnp.mean(k * k, axis=-1, keepdims=True) + EPS)
                khat = k * rk
                dkn = dkn_acc[src, :]
                dgk_ref[...] += jnp.sum(dkn * khat, axis=0, keepdims=True)
                dkh = dkn * gk
                dk = rk * (dkh - khat * jnp.mean(dkh * khat, axis=-1, keepdims=True))
                dk_ref[rows, :] = dk.astype(dk_ref.dtype)
                dv_ref[rows, :] = dv_acc[src, :].astype(dv_ref.dtype)
                return carry

            lax.fori_loop(0, T // rb, fin, 0)

    vec = BS((1, HEAD), lambda h, c: (0, 0))
    qblk = BS((ATT_QB, HEAD), lambda h, c: (c, h))
    col = BS((T, HEAD), lambda h, c: (0, h))
    bblk = BS((1, ATT_QB, ATT_BAND), lambda h, c: (h, 0, 0))
    return pl.pallas_call(
        body, grid=(H, NC),
        in_specs=[qblk, BS((T, HEAD), lambda h, c: (0, H + h)), BS((T, HEAD), lambda h, c: (0, 2 * H + h)),
                  qblk, vec, vec, bblk],
        out_specs=[qblk, col, col, vec, vec, bblk],
        out_shape=[S((T, D), MXU_DTYPE)] * 3 + [S((1, HEAD), F32)] * 2 + [S((H, ATT_QB, ATT_BAND), F32)],
        scratch_shapes=[pltpu.VMEM((T + BAND_LEFT, HEAD), MXU_DTYPE)] * 2
        + [pltpu.VMEM((T + BAND_LEFT, HEAD), F32)] * 2,
        compiler_params=_cp(("arbitrary", "arbitrary"), 56), name=name)(qkv, qkv, qkv, do, gq, gk, bias)


def _rel_onehot(qi, num_rel):
    kk = lax.broadcasted_iota(jnp.int32, (BAND, num_rel), 0)
    rr = lax.broadcasted_iota(jnp.int32, (BAND, num_rel), 1)
    idx = jnp.clip(BAND_LEFT + qi - kk, -(CHUNK - 1), MAX_REL) + (CHUNK - 1)
    return (idx == rr).astype(F32)


def rel_bias_expand(table, name):
    H, num_rel = table.shape

    def body(t_ref, o_ref):
        for qi in range(CHUNK):
            o_ref[qi] = dot_nt(t_ref[...], _rel_onehot(qi, num_rel), hi=True)

    return pl.pallas_call(body, out_shape=S((CHUNK, H, BAND), F32), name=name,
                          compiler_params=pltpu.CompilerParams(vmem_limit_bytes=40 * MIB))(table)


def rel_bias_reduce(dbias_t, num_rel, name):
    H = dbias_t.shape[1]

    def body(d_ref, o_ref):
        acc = jnp.zeros((H, num_rel), F32)
        for qi in range(CHUNK):
            acc += dot_nn(d_ref[qi], _rel_onehot(qi, num_rel), hi=True)
        o_ref[...] = acc

    return pl.pallas_call(body, out_shape=S((H, num_rel), F32), name=name,
                          compiler_params=pltpu.CompilerParams(vmem_limit_bytes=40 * MIB))(dbias_t)


def pool_fwd(h, x, w, scale, name):
    T, D = h.shape
    G = len(POOL_WINDOWS)
    Dg = D // G
    bt = _pick(T, (256, 128, 64))

    def body(h_ref, halo_ref, x_ref, w_ref, s_ref, o_ref, p_ref, ext):
        i = pl.program_id(0)
        ext[pl.ds(0, POOL_HALO), :] = jnp.where(i > 0, halo_ref[...], 0.0)
        ext[pl.ds(POOL_HALO, bt), :] = h_ref[...]
        t = i * bt + lax.broadcasted_iota(jnp.int32, (bt, 1), 0)
        for g, win in enumerate(POOL_WINDOWS):
            cols = pl.ds(g * Dg, Dg)
            acc = ext[pl.ds(POOL_HALO, bt), cols]
            for j in range(1, win):
                acc += ext[pl.ds(POOL_HALO - j, bt), cols]
            count = jnp.minimum(t + 1, win).astype(F32)
            pooled = acc / count - h_ref[:, cols]
            p_ref[:, cols] = pooled.astype(p_ref.dtype)
            y = dot_nn(pooled, w_ref[g]) * s_ref[:, cols]
            o_ref[:, cols] = x_ref[:, cols] + y

    row = BS((bt, D), lambda i: (i, 0))
    return pl.pallas_call(
        body, grid=(T // bt,),
        in_specs=[row, BS((POOL_HALO, D), lambda i: (jnp.maximum(i * (bt // POOL_HALO) - 1, 0), 0)), row,
                  BS((G, Dg, Dg), lambda i: (0, 0, 0)), BS((1, D), lambda i: (0, 0))],
        out_specs=[row, row], out_shape=[S((T, D), F32), S((T, D), MXU_DTYPE)],
        scratch_shapes=[pltpu.VMEM((bt + POOL_HALO, D), F32)],
        compiler_params=_cp(("arbitrary",)), name=name)(h, h, x, w, scale)


def pool_bwd_a(dy, pooled, w, scale, name):
    T, D = dy.shape
    G = len(POOL_WINDOWS)
    Dg = D // G
    bt = _pick(T, (256, 128, 64))

    def body(dy_ref, p_ref, w_ref, s_ref, dp_ref, dw_ref, ds_ref):
        i = pl.program_id(0)

        @pl.when(i == 0)
        def _():
            dw_ref[...] = jnp.zeros_like(dw_ref)
            ds_ref[...] = jnp.zeros_like(ds_ref)

        for g in range(G):
            cols = pl.ds(g * Dg, Dg)
            pg = p_ref[:, cols]
            dyg = dy_ref[:, cols]
            ypre = dot_nn(pg, w_ref[g])
            ds_ref[:, cols] += jnp.sum(dyg * ypre, axis=0, keepdims=True)
            dys = dyg * s_ref[:, cols]
            dp_ref[:, cols] = dot_nt(dys, w_ref[g])
            dw_ref[g] += dot_tn(pg, dys)

    row = BS((bt, D), lambda i: (i, 0))
    wspec = BS((G, Dg, Dg), lambda i: (0, 0, 0))
    vec = BS((1, D), lambda i: (0, 0))
    return pl.pallas_call(
        body, grid=(T // bt,), in_specs=[row, row, wspec, vec], out_specs=[row, wspec, vec],
        out_shape=[S((T, D), F32), S((G, Dg, Dg), F32), S((1, D), F32)],
        compiler_params=_cp(("arbitrary",)), name=name)(dy, pooled, w, scale)


def pool_bwd_b(dpooled, name):
    T, D = dpooled.shape
    G = len(POOL_WINDOWS)
    Dg = D // G
    bt = _pick(T, (256, 128, 64))
    nt = T // bt

    def body(d_ref, halo_ref, o_ref, ext):
        i = pl.program_id(0)
        t = i * bt + lax.broadcasted_iota(jnp.int32, (bt, 1), 0)
        for g, win in enumerate(POOL_WINDOWS):
            cols = pl.ds(g * Dg, Dg)
            count = jnp.minimum(t + 1, win).astype(F32)
            ext[pl.ds(0, bt), cols] = d_ref[:, cols] / count
            ext[pl.ds(bt, POOL_HALO), cols] = jnp.where(i < nt - 1, halo_ref[:, cols] * (1.0 / win), 0.0)
            acc = ext[pl.ds(0, bt), cols]
            for j in range(1, win):
                acc += ext[pl.ds(j, bt), cols]
            o_ref[:, cols] = acc - d_ref[:, cols]

    row = BS((bt, D), lambda i: (i, 0))
    nxt = BS((POOL_HALO, D), lambda i: (jnp.minimum((i + 1) * (bt // POOL_HALO), T // POOL_HALO - 1), 0))
    return pl.pallas_call(
        body, grid=(nt,), in_specs=[row, nxt], out_specs=row, out_shape=S((T, D), F32),
        scratch_shapes=[pltpu.VMEM((bt + POOL_HALO, D), F32)],
        compiler_params=_cp(("arbitrary",)), name=name)(dpooled, dpooled)


def _gdn_u(i, x_ref, halo_ref, w_ref, ext, bt):
    ext[pl.ds(0, SUBLANES), :] = jnp.where(i > 0, halo_ref[...], 0.0)
    ext[pl.ds(SUBLANES, bt), :] = x_ref[...]
    u = w_ref[3:4, :] * ext[pl.ds(SUBLANES, bt), :]
    for j in range(GDN_CONV - 1):
        u += w_ref[j:j + 1, :] * ext[pl.ds(SUBLANES - (GDN_CONV - 1) + j, bt), :]
    return u


def gdn_pre_fwd(proj, conv_w, key_dim, name):
    T = proj.shape[0]
    C = conv_w.shape[1]
    cb = min(1024, key_dim)
    nq, nqk, J = key_dim // cb, 2 * key_dim // cb, C // cb
    bt = _pick(T, (256, 128, 64))

    def body(x_ref, halo_ref, w_ref, o_ref, ext):
        i, j = pl.program_id(0), pl.program_id(1)
        u = _gdn_u(i, x_ref, halo_ref, w_ref, ext, bt)
        s = u * _sigmoid(u)

        @pl.when(j < nqk)
        def _():
            sc = jnp.where(j < nq, HEAD ** -0.5, 1.0)
            for hh in range(cb // HEAD):
                cols = pl.ds(hh * HEAD, HEAD)
                blk = s[:, hh * HEAD:(hh + 1) * HEAD]
                r = lax.rsqrt(jnp.sum(blk * blk, axis=-1, keepdims=True) + EPS)
                o_ref[:, cols] = (blk * r) * sc

        @pl.when(j >= nqk)
        def _():
            o_ref[...] = s

    return pl.pallas_call(
        body, grid=(T // bt, J),
        in_specs=[BS((bt, cb), lambda i, j: (i, j)),
                  BS((SUBLANES, cb), lambda i, j: (jnp.maximum(i * (bt // SUBLANES) - 1, 0), j)),
                  BS((SUBLANES, cb), lambda i, j: (0, j))],
        out_specs=BS((bt, cb), lambda i, j: (i, j)), out_shape=S((T, C), F32),
        scratch_shapes=[pltpu.VMEM((bt + SUBLANES, cb), F32)],
        compiler_params=_cp(("arbitrary", "arbitrary")), name=name)(proj, proj, conv_w)


def gdn_pre_bwd_a(proj, conv_w, dq_v, dk_v, dv, key_dim, name):
    T = proj.shape[0]
    C = conv_w.shape[1]
    cb = min(1024, key_dim)
    nq, nqk, J = key_dim // cb, 2 * key_dim // cb, C // cb
    nv = J - nqk
    bt = _pick(T, (256, 128, 64))

    def body(x_ref, halo_ref, w_ref, dq_ref, dk_ref, dv_ref, du_ref, dw_ref, ext, ds_scr):
        j, i = pl.program_id(0), pl.program_id(1)
        u = _gdn_u(i, x_ref, halo_ref, w_ref, ext, bt)
        sg = _sigmoid(u)
        s = u * sg

        @pl.when(j < nqk)
        def _():
            sc = jnp.where(j < nq, HEAD ** -0.5, 1.0)
            for hh in range(cb // HEAD):
                lo = 2 * hh * HEAD
                dq2 = dq_ref[:, lo:lo + HEAD] + dq_ref[:, lo + HEAD:lo + 2 * HEAD]
                dk2 = dk_ref[:, lo:lo + HEAD] + dk_ref[:, lo + HEAD:lo + 2 * HEAD]
                dn = jnp.where(j < nq, dq2, dk2)
                blk = s[:, hh * HEAD:(hh + 1) * HEAD]
                r = lax.rsqrt(jnp.sum(blk * blk, axis=-1, keepdims=True) + EPS)
                shat = blk * r
                ds_scr[:, pl.ds(hh * HEAD, HEAD)] = (sc * r) * (dn - shat * jnp.sum(dn * shat, axis=-1, keepdims=True))

        @pl.when(j >= nqk)
        def _():
            ds_scr[...] = dv_ref[...]

        du = ds_scr[...] * (sg * (1.0 + u * (1.0 - sg)))
        du_ref[...] = du

        @pl.when(i == 0)
        def _():
            dw_ref[...] = jnp.zeros_like(dw_ref)

        for k in range(GDN_CONV):
            shifted = ext[pl.ds(SUBLANES - (GDN_CONV - 1) + k, bt), :]
            dw_ref[k:k + 1, :] += jnp.sum(du * shifted, axis=0, keepdims=True)

    blk = BS((bt, cb), lambda j, i: (i, j))
    return pl.pallas_call(
        body, grid=(J, T // bt),
        in_specs=[blk, BS((SUBLANES, cb), lambda j, i: (jnp.maximum(i * (bt // SUBLANES) - 1, 0), j)),
                  BS((SUBLANES, cb), lambda j, i: (0, j)),
                  BS((bt, 2 * cb), lambda j, i: (i, jnp.minimum(j, nq - 1))),
                  BS((bt, 2 * cb), lambda j, i: (i, jnp.clip(j - nq, 0, nq - 1))),
                  BS((bt, cb), lambda j, i: (i, jnp.clip(j - nqk, 0, nv - 1)))],
        out_specs=[blk, BS((SUBLANES, cb), lambda j, i: (0, j))],
        out_shape=[S((T, C), F32), S((SUBLANES, C), F32)],
        scratch_shapes=[pltpu.VMEM((bt + SUBLANES, cb), F32), pltpu.VMEM((bt, cb), F32)],
        compiler_params=_cp(("arbitrary", "arbitrary")), name=name)(proj, proj, conv_w, dq_v, dk_v, dv)


def gdn_pre_bwd_b(du, dgate, conv_w, key_dim, name):
    T, C = du.shape
    V = dgate.shape[1]
    cb = min(1024, key_dim)
    J = C // cb
    J2 = (C + V) // cb
    bt = _pick(T, (256, 128, 64))
    nt = T // bt

    def body(du_ref, halo_ref, w_ref, dg_ref, o_ref, ext):
        i, j = pl.program_id(0), pl.program_id(1)

        @pl.when(j < J)
        def _():
            ext[pl.ds(0, bt), :] = du_ref[...]
            ext[pl.ds(bt, SUBLANES), :] = jnp.where(i < nt - 1, halo_ref[...], 0.0)
            da = w_ref[3:4, :] * ext[pl.ds(0, bt), :]
            for k in range(GDN_CONV - 1):
                da += w_ref[k:k + 1, :] * ext[pl.ds(GDN_CONV - 1 - k, bt), :]
            o_ref[...] = da.astype(o_ref.dtype)

        @pl.when(j >= J)
        def _():
            o_ref[...] = dg_ref[...]

    jc = lambda j: jnp.minimum(j, J - 1)
    return pl.pallas_call(
        body, grid=(nt, J2),
        in_specs=[BS((bt, cb), lambda i, j: (i, jc(j))),
                  BS((SUBLANES, cb), lambda i, j: (jnp.minimum((i + 1) * (bt // SUBLANES), T // SUBLANES - 1), jc(j))),
                  BS((SUBLANES, cb), lambda i, j: (0, jc(j))),
                  BS((bt, cb), lambda i, j: (i, jnp.maximum(j - J, 0)))],
        out_specs=BS((bt, cb), lambda i, j: (i, j)), out_shape=S((T, C + V), MXU_DTYPE),
        scratch_shapes=[pltpu.VMEM((bt + SUBLANES, cb), F32)],
        compiler_params=_cp(("arbitrary", "arbitrary")), name=name)(du, du, conv_w, dgate)


def _softplus(x):
    return jnp.maximum(x, 0.0) + jnp.log1p(jnp.exp(-jnp.abs(x)))


def gdn_gate_fwd(a, b, a_log, dt_bias, name):
    T, HV = a.shape
    bt = _pick(T, (1024, 512, 256, 128, 64))

    def body(a_ref, b_ref, al_ref, dt_ref, g_ref, be_ref):
        g_ref[...] = -jnp.exp(al_ref[...]) * _softplus(a_ref[...] + dt_ref[...])
        be_ref[...] = _sigmoid(b_ref[...])

    row = BS((bt, HV), lambda i: (i, 0))
    vec = BS((1, HV), lambda i: (0, 0))
    return pl.pallas_call(body, grid=(T // bt,), in_specs=[row, row, vec, vec], out_specs=[row, row],
                          out_shape=[S((T, HV), F32)] * 2, compiler_params=_cp(("parallel",)), name=name)(
                              a, b, a_log, dt_bias)


def gdn_gate_bwd(a, b, a_log, dt_bias, dg, dbeta, name):
    T, HV = a.shape
    bt = _pick(T, (1024, 512, 256, 128, 64))

    def body(a_ref, b_ref, al_ref, dt_ref, dg_ref, dbe_ref, da_ref, db_ref, dal_ref, ddt_ref):
        i = pl.program_id(0)
        x = a_ref[...] + dt_ref[...]
        ea = jnp.exp(al_ref[...])
        dgv = dg_ref[...]
        da = dgv * (-ea * _sigmoid(x))
        da_ref[...] = da
        be = _sigmoid(b_ref[...])
        db_ref[...] = dbe_ref[...] * be * (1.0 - be)

        @pl.when(i == 0)
        def _():
            dal_ref[...] = jnp.zeros_like(dal_ref)
            ddt_ref[...] = jnp.zeros_like(ddt_ref)

        dal_ref[...] += jnp.sum(dgv * (-ea * _softplus(x)), axis=0, keepdims=True)
        ddt_ref[...] += jnp.sum(da, axis=0, keepdims=True)

    row = BS((bt, HV), lambda i: (i, 0))
    vec = BS((1, HV), lambda i: (0, 0))
    return pl.pallas_call(body, grid=(T // bt,), in_specs=[row, row, vec, vec, row, row],
                          out_specs=[row, row, vec, vec],
                          out_shape=[S((T, HV), F32)] * 2 + [S((1, HV), F32)] * 2,
                          compiler_params=_cp(("arbitrary",)), name=name)(a, b, a_log, dt_bias, dg, dbeta)


def _col(row_vec, eye):
    return jnp.sum(jnp.where(eye, row_vec, 0.0), axis=1, keepdims=True)


def _row(col_vec, eye):
    return jnp.sum(jnp.where(eye, col_vec, 0.0), axis=0, keepdims=True)


def _each(f, *lists):
    return [f(*args) for args in zip(*lists)]


def _mul(a, b):
    return a * b


def _hdot_nn(a, b):
    return dot_nn(a, b, hi=True)


def _delta_chunk(q, k, v, g_row, b_row, rep):
    C = CHUNK
    ii = lax.broadcasted_iota(jnp.int32, (C, C), 0)
    jj = lax.broadcasted_iota(jnp.int32, (C, C), 1)
    eye, causal, strict = ii == jj, ii >= jj, ii > jj
    q_v = [q[p // rep] for p in range(len(v))]
    k_v = [k[p // rep] for p in range(len(v))]
    g_col = _each(lambda g: _col(g, eye), g_row)
    gc_row = _each(lambda g: jnp.sum(jnp.where(ii <= jj, g, 0.0), axis=0, keepdims=True), g_col)
    gc_col = _each(lambda g: _col(g, eye), gc_row)
    gl = _each(lambda g: jnp.sum(jnp.where(jj[0:1, :] == C - 1, g, 0.0), axis=1, keepdims=True), gc_row)
    decay = _each(lambda gc, gr: jnp.where(causal, jnp.exp(jnp.where(causal, gc - gr, 0.0)), 0.0), gc_col, gc_row)
    b_col = _each(lambda b: _col(b, eye), b_row)
    kb = _each(_mul, k_v, b_col)
    vb = _each(_mul, v, b_col)
    m = _each(dot_nt, kb, k_v)
    a = _each(lambda m_, d_: jnp.where(strict, m_ * d_, 0.0), m, decay)
    ident = jnp.where(eye, 1.0, 0.0)
    tinv = [ident - a_ for a_ in a]
    pw = _each(_hdot_nn, a, a)
    for step in range(5):
        tinv = _each(lambda t_, p_: t_ + _hdot_nn(t_, p_), tinv, pw)
        if step < 4:
            pw = _each(_hdot_nn, pw, pw)
    egc = _each(jnp.exp, gc_col)
    kbg = _each(_mul, kb, egc)
    u = _each(_hdot_nn, tinv, vb)
    w = _each(_hdot_nn, tinv, kbg)
    n_k = _each(dot_nt, q, k)
    n = [n_k[p // rep] for p in range(len(v))]
    attn = _each(lambda n_, d_: jnp.where(causal, n_ * d_, 0.0), n, decay)
    qg = _each(_mul, q_v, egc)
    ekl = _each(lambda l_, c_: jnp.exp(l_ - c_), gl, gc_col)
    ks = _each(_mul, k_v, ekl)
    dec = _each(jnp.exp, gl)
    return dict(eye=eye, causal=causal, strict=strict, ii=ii, jj=jj, q=q_v, k=k_v, gc_col=gc_col, gl=gl, decay=decay,
                b_col=b_col, kb=kb, vb=vb, m=m, tinv=tinv, egc=egc, kbg=kbg, u=u, w=w, n=n, attn=attn,
                qg=qg, ekl=ekl, ks=ks, dec=dec)


def delta_fwd(qkvn, g_rows, b_rows, key_dim, name):
    T = qkvn.shape[0]
    NK = key_dim // HEAD
    HV = g_rows.shape[0]
    rep = HV // NK
    NC = T // CHUNK

    P = DELTA_HEADS
    kw, vw = HEAD * P // rep, HEAD * P

    def body(q_ref, k_ref, v_ref, g_ref, b_ref, o_ref, st_ref, state):
        n = pl.program_id(1)

        @pl.when(n == 0)
        def _():
            state[...] = jnp.zeros_like(state)

        heads = range(P)
        q = [q_ref[:, pl.ds(kh * HEAD, HEAD)] for kh in range(P // rep)]
        k = [k_ref[:, pl.ds(kh * HEAD, HEAD)] for kh in range(P // rep)]
        v = [v_ref[:, pl.ds(p * HEAD, HEAD)] for p in heads]
        s0 = [state[p] for p in heads]
        c = _delta_chunk(q, k, v, [g_ref[p, 0] for p in heads], [b_ref[p, 0] for p in heads], rep)
        vn = _each(lambda u_, w_, s_: u_ - dot_nn(w_, s_), c["u"], c["w"], s0)
        o = _each(lambda qg_, s_, at_, vn_: dot_nn(qg_, s_) + dot_nn(at_, vn_), c["qg"], s0, c["attn"], vn)
        s1 = _each(lambda s_, d_, ks_, vn_: s_ * d_ + dot_tn(ks_, vn_), s0, c["dec"], c["ks"], vn)
        for p in heads:
            st_ref[p, 0] = s0[p]
            o_ref[:, pl.ds(p * HEAD, HEAD)] = o[p]
            state[p] = s1[p]

    vrow = BS((P, 1, 1, CHUNK), lambda h, n: (h, n, 0, 0))
    return pl.pallas_call(
        body, grid=(HV // P, NC),
        in_specs=[BS((CHUNK, kw), lambda h, n: (n, h)),
                  BS((CHUNK, kw), lambda h, n: (n, key_dim // kw + h)),
                  BS((CHUNK, vw), lambda h, n: (n, 2 * key_dim // vw + h)), vrow, vrow],
        out_specs=[BS((CHUNK, vw), lambda h, n: (n, h)), BS((P, 1, HEAD, HEAD), lambda h, n: (h, n, 0, 0))],
        out_shape=[S((T, HV * HEAD), F32), S((HV, NC, HEAD, HEAD), F32)],
        scratch_shapes=[pltpu.VMEM((P, HEAD, HEAD), F32)],
        compiler_params=_cp(("arbitrary", "arbitrary")), name=name)(qkvn, qkvn, qkvn, g_rows, b_rows)


def delta_bwd(qkvn, g_rows, b_rows, states, do, key_dim, name):
    T = qkvn.shape[0]
    NK = key_dim // HEAD
    HV = g_rows.shape[0]
    rep = HV // NK
    NC = T // CHUNK
    P = DELTA_HEADS
    kw, vw = HEAD * P // rep, HEAD * P

    def body(q_ref, k_ref, v_ref, g_ref, b_ref, st_ref, do_ref, dq_ref, dk_ref, dv_ref, dg_ref, dbe_ref, dstate):
        step = pl.program_id(1)

        @pl.when(step == 0)
        def _():
            dstate[...] = jnp.zeros_like(dstate)

        heads = range(P)
        q_k = [q_ref[:, pl.ds(kh * HEAD, HEAD)] for kh in range(P // rep)]
        k_k = [k_ref[:, pl.ds(kh * HEAD, HEAD)] for kh in range(P // rep)]
        v = [v_ref[:, pl.ds(p * HEAD, HEAD)] for p in heads]
        s0 = [st_ref[p, 0] for p in heads]
        dsn = [dstate[p] for p in heads]
        dov = [do_ref[:, pl.ds(p * HEAD, HEAD)] for p in heads]
        c = _delta_chunk(q_k, k_k, v, [g_ref[p, 0] for p in heads], [b_ref[p, 0] for p in heads], rep)
        eye, causal, strict = c["eye"], c["causal"], c["strict"]
        q, k, tinv, decay = c["q"], c["k"], c["tinv"], c["decay"]

        def rsum(a, b):
            return jnp.sum(a * b, axis=1, keepdims=True)

        vn = _each(lambda u_, w_, s_: u_ - dot_nn(w_, s_), c["u"], c["w"], s0)
        dvn = _each(lambda at_, do_, ks_, ds_: dot_tn(at_, do_) + dot_nn(ks_, ds_), c["attn"], dov, c["ks"], dsn)
        dattn = _each(lambda do_, vn_: jnp.where(causal, dot_nt(do_, vn_), 0.0), dov, vn)
        dqg = _each(dot_nt, dov, s0)
        dks = _each(dot_nt, vn, dsn)
        ddec = _each(lambda s_, ds_: jnp.sum(rsum(s_, ds_), axis=0, keepdims=True), s0, dsn)
        dw = _each(lambda dvn_, s_: -dot_nt(dvn_, s_), dvn, s0)
        ds0 = _each(lambda qg_, do_, ds_, d_, w_, dvn_: dot_tn(qg_, do_) + ds_ * d_ - dot_tn(w_, dvn_),
                    c["qg"], dov, dsn, c["dec"], c["w"], dvn)
        dvb = _each(lambda t_, x_: dot_tn(t_, x_, hi=True), tinv, dvn)
        dkbg = _each(lambda t_, x_: dot_tn(t_, x_, hi=True), tinv, dw)
        dt = _each(lambda dvn_, vb_, dw_, kbg_: dot_nt(dvn_, vb_, hi=True) + dot_nt(dw_, kbg_, hi=True),
                   dvn, c["vb"], dw, c["kbg"])
        dtt = _each(lambda dt_, t_: dot_nt(dt_, t_, hi=True), dt, tinv)
        da = _each(lambda t_, x_: jnp.where(strict, -dot_tn(t_, x_, hi=True), 0.0), tinv, dtt)
        dm = _each(_mul, da, decay)
        dn = _each(_mul, dattn, decay)
        e = _each(lambda da_, m_, dat_, n_, d_: (da_ * m_ + dat_ * n_) * d_, da, c["m"], dattn, c["n"], decay)
        dkb = _each(lambda dm_, k_, dkbg_, egc_: dot_nn(dm_, k_) + dkbg_ * egc_, dm, k, dkbg, c["egc"])
        dk = _each(lambda dm_, kb_, dn_, q_, dks_, ekl_, dkb_, b_: dot_tn(dm_, kb_) + dot_tn(dn_, q_) + dks_ * ekl_
                   + dkb_ * b_, dm, c["kb"], dn, q, dks, c["ekl"], dkb, c["b_col"])
        dq = _each(lambda dn_, k_, dqg_, egc_: dot_nn(dn_, k_) + dqg_ * egc_, dn, k, dqg, c["egc"])
        dks_ks = _each(rsum, dks, c["ks"])
        dgc_col = _each(lambda e_, dkbg_, kbg_, dqg_, qg_, x_: jnp.sum(e_, axis=1, keepdims=True) + rsum(dkbg_, kbg_)
                        + rsum(dqg_, qg_) - x_ - _col(jnp.sum(e_, axis=0, keepdims=True), eye),
                        e, dkbg, c["kbg"], dqg, c["qg"], dks_ks)
        dgl = _each(lambda x_, dd_, d_: jnp.sum(x_, axis=0, keepdims=True) + dd_ * d_, dks_ks, ddec, c["dec"])
        last = c["ii"][:, 0:1] == CHUNK - 1
        dgc_col = _each(lambda g_, l_: g_ + jnp.where(last, l_, 0.0), dgc_col, dgl)
        dbe_col = _each(lambda dvb_, v_, dkb_, k_: rsum(dvb_, v_) + rsum(dkb_, k_), dvb, v, dkb, k)
        for p in heads:
            vc = pl.ds(p * HEAD, HEAD)
            dstate[p] = ds0[p]
            dq_ref[:, vc] = dq[p]
            dk_ref[:, vc] = dk[p]
            dv_ref[:, vc] = dvb[p] * c["b_col"][p]
            dbe_ref[p, 0] = _row(dbe_col[p], eye)
            dg_ref[p, 0] = jnp.sum(jnp.where(causal, dgc_col[p], 0.0), axis=0, keepdims=True)

    rev = lambda n: NC - 1 - n
    vrow = BS((P, 1, 1, CHUNK), lambda h, n: (h, rev(n), 0, 0))
    vblk = BS((CHUNK, vw), lambda h, n: (rev(n), h))
    return pl.pallas_call(
        body, grid=(HV // P, NC),
        in_specs=[BS((CHUNK, kw), lambda h, n: (rev(n), h)),
                  BS((CHUNK, kw), lambda h, n: (rev(n), key_dim // kw + h)),
                  BS((CHUNK, vw), lambda h, n: (rev(n), 2 * key_dim // vw + h)), vrow, vrow,
                  BS((P, 1, HEAD, HEAD), lambda h, n: (h, rev(n), 0, 0)), vblk],
        out_specs=[vblk, vblk, vblk, vrow, vrow],
        out_shape=[S((T, HV * HEAD), F32)] * 3 + [S((HV, NC, 1, CHUNK), F32)] * 2,
        scratch_shapes=[pltpu.VMEM((P, HEAD, HEAD), F32)],
        compiler_params=_cp(("arbitrary", "arbitrary")), name=name)(qkvn, qkvn, qkvn, g_rows, b_rows, states, do)


def gdn_post_fwd(o, proj, gate_col0, o_gain, name):
    T, V = o.shape
    cb = min(1024, V)
    j0 = gate_col0 // cb
    bt = _pick(T, (256, 128, 64))

    def body(o_ref, g_ref, gain_ref, y_ref):
        for hh in range(cb // HEAD):
            cols = pl.ds(hh * HEAD, HEAD)
            ov = o_ref[:, cols]
            gt = g_ref[:, cols]
            r = lax.rsqrt(jnp.mean(ov * ov, axis=-1, keepdims=True) + EPS)
            y_ref[:, cols] = (((ov * r) * gain_ref[...]) * (gt * _sigmoid(gt))).astype(y_ref.dtype)

    return pl.pallas_call(
        body, grid=(T // bt, V // cb),
        in_specs=[BS((bt, cb), lambda i, j: (i, j)), BS((bt, cb), lambda i, j: (i, j0 + j)),
                  BS((1, HEAD), lambda i, j: (0, 0))],
        out_specs=BS((bt, cb), lambda i, j: (i, j)), out_shape=S((T, V), MXU_DTYPE),
        compiler_params=_cp(("parallel", "parallel")), name=name)(o, proj, o_gain)


def gdn_post_bwd(o, proj, gate_col0, o_gain, dy, name):
    T, V = o.shape
    cb = min(1024, V)
    j0 = gate_col0 // cb
    bt = _pick(T, (256, 128, 64))

    def body(o_ref, g_ref, gain_ref, dy_ref, do_ref, dgt_ref, dgain_ref):
        i, j = pl.program_id(0), pl.program_id(1)

        @pl.when(jnp.logical_and(i == 0, j == 0))
        def _():
            dgain_ref[...] = jnp.zeros_like(dgain_ref)

        gain = gain_ref[...]
        for hh in range(cb // HEAD):
            cols = pl.ds(hh * HEAD, HEAD)
            ov = o_ref[:, cols]
            gt = g_ref[:, cols]
            dyv = dy_ref[:, cols]
            r = lax.rsqrt(jnp.mean(ov * ov, axis=-1, keepdims=True) + EPS)
            ohat = ov * r
            sg = _sigmoid(gt)
            dyn = dyv * (gt * sg)
            dgt_ref[:, cols] = (dyv * (ohat * gain) * (sg * (1.0 + gt * (1.0 - sg)))).astype(dgt_ref.dtype)
            dgain_ref[...] += jnp.sum(dyn * ohat, axis=0, keepdims=True)
            dh = dyn * gain
            do_ref[:, cols] = r * (dh - ohat * jnp.mean(dh * ohat, axis=-1, keepdims=True))

    blk = BS((bt, cb), lambda i, j: (i, j))
    vec = BS((1, HEAD), lambda i, j: (0, 0))
    return pl.pallas_call(
        body, grid=(T // bt, V // cb),
        in_specs=[blk, BS((bt, cb), lambda i, j: (i, j0 + j)), vec, blk],
        out_specs=[blk, blk, vec], out_shape=[S((T, V), F32), S((T, V), MXU_DTYPE), S((1, HEAD), F32)],
        compiler_params=_cp(("arbitrary", "arbitrary")), name=name)(o, proj, o_gain, dy)


def adamw(w, g, m, v, name):
    shape = w.shape
    n = 1
    for d in shape:
        n *= d
    cols = shape[-1] if len(shape) > 1 else n
    rows = n // cols
    br = rows
    for cand in (512, 256, 128, 64, 32, 16, 8):
        if rows % cand == 0 and cand * cols * 4 * 14 <= 36 * MIB:
            br = cand
            break
    c1 = 1.0 - ADAM_B1 ** ADAM_STEP
    c2 = 1.0 - ADAM_B2 ** ADAM_STEP

    def body(w_ref, g_ref, m_ref, v_ref, d_ref, nm_ref, nv_ref):
        gv = g_ref[...]
        nm = ADAM_B1 * m_ref[...] + (1.0 - ADAM_B1) * gv
        nv = ADAM_B2 * v_ref[...] + (1.0 - ADAM_B2) * (gv * gv)
        nm_ref[...] = nm
        nv_ref[...] = nv
        d_ref[...] = -ADAM_LR * ((nm / c1) / (jnp.sqrt(nv / c2) + ADAM_EPS) + ADAM_WD * w_ref[...])

    blk = BS((br, cols), lambda i: (i, 0))
    outs = pl.pallas_call(
        body, grid=(rows // br,), in_specs=[blk] * 4, out_specs=[blk] * 3,
        out_shape=[S((rows, cols), F32)] * 3, compiler_params=_cp(("parallel",), 48), name=name)(
            *[t.reshape(rows, cols) for t in (w, g, m, v)])
    return [t.reshape(shape) for t in outs]


def _ffn_fwd(x, gain, p, tag):
    (h2,) = norm_fwd(x, gain, [MXU_DTYPE], f"{tag}_norm")
    up = matmul(h2, p["w_up_t"], "nt", name=f"{tag}_up")
    act = ffn_act_fwd(up, p["conv"], f"{tag}_act")
    out = matmul(act, p["w_down"], "nn", res=x, name=f"{tag}_down")
    return out, (x, h2, up, act)


def _ffn_bwd(dx, saved, gain, p, tag):
    x, h2, up, act = saved
    g_act = matmul(dx, p["w_down"], "nt", name=f"{tag}_bdown")
    d_down = matmul(act, dx, "tn", name=f"{tag}_wdown")
    du, db, d_conv = ffn_act_bwd_a(up, p["conv"], g_act, f"{tag}_bact_a")
    dup = ffn_act_bwd_b(du, db, p["conv"], f"{tag}_bact_b")
    dh2 = matmul(dup, p["w_up_t"], "nn", name=f"{tag}_bup")
    d_up_t = matmul(dup, h2, "tn", name=f"{tag}_wup")
    dx, d_gain = norm_bwd(x, gain, dh2, dx, f"{tag}_bnorm")
    return dx, d_gain, dict(w_up_t=d_up_t, conv=d_conv, w_down=d_down)


def _joint_bias(bias):
    rows = [jnp.pad(bias, ((0, 0), (0, 0), (qc * CHUNK, (ATT_CHUNKS - 1 - qc) * CHUNK)), constant_values=MASK_VALUE)
            for qc in range(ATT_CHUNKS)]
    return jnp.concatenate(rows, axis=1)


def _joint_bias_grad(dbias):
    return sum(dbias[:, qc * CHUNK:(qc + 1) * CHUNK, qc * CHUNK:qc * CHUNK + BAND] for qc in range(ATT_CHUNKS))


def _att_fwd(x, gain, p, tag):
    (h,) = norm_fwd(x, gain, [MXU_DTYPE], f"{tag}_norm")
    qkv = matmul(h, p["w_qkv_t"], "nt", name=f"{tag}_qkv")
    bias = _joint_bias(rel_bias_expand(p["rel_bias"], f"{tag}_rel").transpose(1, 0, 2))
    o = attn_fwd(qkv, p["q_gain"], p["k_gain"], bias, f"{tag}_core")
    out = matmul(o, p["w_o"], "nn", res=x, name=f"{tag}_out")
    return out, (x, h, qkv, o, bias)


def _att_bwd(dx, saved, gain, p, tag):
    x, h, qkv, o, bias = saved
    do = matmul(dx, p["w_o"], "nt", name=f"{tag}_bout")
    d_wo = matmul(o, dx, "tn", name=f"{tag}_wout")
    dq, dk, dv, d_gq, d_gk, dbias = attn_bwd(qkv, do, p["q_gain"], p["k_gain"], bias, f"{tag}_bcore")
    dqkv = jnp.concatenate([dq, dk, dv], axis=1)
    dh = matmul(dqkv, p["w_qkv_t"], "nn", name=f"{tag}_bqkv")
    d_wqkv_t = matmul(dqkv, h, "tn", name=f"{tag}_wqkv")
    d_rb = rel_bias_reduce(_joint_bias_grad(dbias).transpose(1, 0, 2), p["rel_bias"].shape[1], f"{tag}_brel")
    dx, d_gain = norm_bwd(x, gain, dh, dx, f"{tag}_bnorm")
    return dx, d_gain, dict(w_qkv_t=d_wqkv_t, w_o=d_wo, q_gain=d_gq, k_gain=d_gk, rel_bias=d_rb)


def _pool_fwd(x, gain, p, tag):
    (hf,) = norm_fwd(x, gain, [F32], f"{tag}_norm")
    out, pooled = pool_fwd(hf, x, p["w"], p["scale"], f"{tag}_core")
    return out, (x, pooled)


def _pool_bwd(dx, saved, gain, p, tag):
    x, pooled = saved
    dpooled, d_w, d_scale = pool_bwd_a(dx, pooled, p["w"], p["scale"], f"{tag}_bcore_a")
    dh = pool_bwd_b(dpooled, f"{tag}_bcore_b")
    dx, d_gain = norm_bwd(x, gain, dh, dx, f"{tag}_bnorm")
    return dx, d_gain, dict(w=d_w, scale=d_scale)


def _rows_layout(t, hv):
    return t.T.reshape(hv, t.shape[0] // CHUNK, 1, CHUNK)


def _gdn_fwd(x, gain, p, tag):
    T = x.shape[0]
    hv = p["a_log"].shape[1]
    key_dim = p["key_dim"]
    C = p["conv"].shape[1]
    (h,) = norm_fwd(x, gain, [MXU_DTYPE], f"{tag}_norm")
    proj = matmul(h, p["w_main_t"], "nt", name=f"{tag}_in")
    ab = matmul(h, p["w_ab_t"], "nt", name=f"{tag}_in_ab")
    a, b = ab[:, :hv], ab[:, hv:2 * hv]
    qkvn = gdn_pre_fwd(proj, p["conv"], key_dim, f"{tag}_pre")
    g, beta = gdn_gate_fwd(a, b, p["a_log"], p["dt_bias"], f"{tag}_gate")
    g_rows, b_rows = _rows_layout(g, hv), _rows_layout(beta, hv)
    o, states = delta_fwd(qkvn, g_rows, b_rows, key_dim, f"{tag}_delta")
    y = gdn_post_fwd(o, proj, C, p["o_gain"], f"{tag}_post")
    out = matmul(y, p["w_o"], "nn", res=x, name=f"{tag}_out")
    return out, (x, h, proj, a, b, qkvn, g_rows, b_rows, o, states, y)


def _gdn_bwd(dx, saved, gain, p, tag):
    x, h, proj, a, b, qkvn, g_rows, b_rows, o, states, y = saved
    T = x.shape[0]
    hv = p["a_log"].shape[1]
    key_dim = p["key_dim"]
    C = p["conv"].shape[1]
    dy = matmul(dx, p["w_o"], "nt", name=f"{tag}_bout")
    d_wo = matmul(y, dx, "tn", name=f"{tag}_wout")
    do, dgate, d_ogain = gdn_post_bwd(o, proj, C, p["o_gain"], dy, f"{tag}_bpost")
    dq_v, dk_v, dv, dg_rows, dbe_rows = delta_bwd(qkvn, g_rows, b_rows, states, do, key_dim, f"{tag}_bdelta")
    dg = dg_rows.reshape(hv, T).T
    dbeta = dbe_rows.reshape(hv, T).T
    da, db, d_alog, d_dtb = gdn_gate_bwd(a, b, p["a_log"], p["dt_bias"], dg, dbeta, f"{tag}_bgate")
    du, d_conv = gdn_pre_bwd_a(proj, p["conv"], dq_v, dk_v, dv, key_dim, f"{tag}_bpre_a")
    dproj = gdn_pre_bwd_b(du, dgate, p["conv"], key_dim, f"{tag}_bpre_b")
    dab = jnp.concatenate([da, db, jnp.zeros((T, LANES - 2 * hv), F32)], axis=1)
    dh = matmul(dproj, p["w_main_t"], "nn", name=f"{tag}_bin")
    dh = matmul(dab, p["w_ab_t"], "nn", res=dh, name=f"{tag}_bin_ab")
    d_main_t = matmul(dproj, h, "tn", name=f"{tag}_win")
    d_ab_t = matmul(dab, h, "tn", name=f"{tag}_win_ab")
    dx, d_gain = norm_bwd(x, gain, dh, dx, f"{tag}_bnorm")
    return dx, d_gain, dict(w_main_t=d_main_t, w_ab_t=d_ab_t, conv=d_conv, a_log=d_alog, dt_bias=d_dtb,
                            o_gain=d_ogain, w_o=d_wo)


_MIXERS = ((_att_fwd, _att_bwd), (_pool_fwd, _pool_bwd), (_gdn_fwd, _gdn_bwd))


def local_step(x, target, W):
    depth = len(W["ffn"])
    saved = []
    for i in range(depth):
        kind, j = i % 3, i // 3
        mp = (W["att"], W["pool"], W["gdn"])[kind][j]
        x, s_mix = _MIXERS[kind][0](x, W["mix_norm"][i:i + 1], mp, f"l{i}_mix")
        x, s_ffn = _ffn_fwd(x, W["ffn_norm"][i:i + 1], W["ffn"][i], f"l{i}_ffn")
        saved.append((s_mix, s_ffn))
    loss, dx = loss_and_grad(x, target, "loss")
    G = dict(mix_norm=[None] * depth, ffn_norm=[None] * depth, ffn=[None] * depth,
             att=[None] * len(W["att"]), pool=[None] * len(W["pool"]), gdn=[None] * len(W["gdn"]))
    for i in reversed(range(depth)):
        kind, j = i % 3, i // 3
        s_mix, s_ffn = saved[i]
        dx, G["ffn_norm"][i], G["ffn"][i] = _ffn_bwd(dx, s_ffn, W["ffn_norm"][i:i + 1], W["ffn"][i], f"l{i}_ffn")
        mp = (W["att"], W["pool"], W["gdn"])[kind][j]
        dx, G["mix_norm"][i], gm = _MIXERS[kind][1](dx, s_mix, W["mix_norm"][i:i + 1], mp, f"l{i}_mix")
        G[("att", "pool", "gdn")[kind]][j] = gm
    return loss, dx, G


SHARD_AXIS = dict(att_w_qkv=2, att_w_o=1, pool_w=2, gdn_w_in=2, gdn_w_o=1, ffn_w_up=2, ffn_w_down=1,
                  att_rel_bias=2, gdn_conv=2, ffn_conv=2)
BIG = ("att_w_qkv", "att_w_o", "pool_w", "gdn_w_in", "gdn_w_o", "ffn_w_up", "ffn_w_down")
SMALL_SHARDED = ("att_rel_bias", "gdn_conv", "ffn_conv")
REPLICATED = ("mix_norm", "ffn_norm", "att_q_gain", "att_k_gain", "pool_scale", "gdn_a_log", "gdn_dt_bias",
              "gdn_o_gain")
WEIGHTS = ("mix_norm", "ffn_norm", "att_w_qkv", "att_q_gain", "att_k_gain", "att_rel_bias", "att_w_o", "pool_w",
           "pool_scale", "gdn_w_in", "gdn_conv", "gdn_a_log", "gdn_dt_bias", "gdn_o_gain", "gdn_w_o", "ffn_w_up",
           "ffn_conv", "ffn_w_down")


def _merge(stacked, axis):
    t = jnp.moveaxis(stacked, 0, axis)
    return t.reshape(t.shape[:axis] + (t.shape[axis] * t.shape[axis + 1],) + t.shape[axis + 2:])


def _split(full, axis):
    n = full.shape[axis] // N_CHIPS
    t = full.reshape(full.shape[:axis] + (N_CHIPS, n) + full.shape[axis + 1:])
    return jnp.moveaxis(t, axis, 0)


def _pad_to(t, axis, size):
    pad = [(0, 0)] * t.ndim
    pad[axis] = (0, size - t.shape[axis])
    return jnp.pad(t, pad)


def _round_up(n, m):
    return (n + m - 1) // m * m


def to_comm(name, t):
    if name in ("att_w_qkv", "gdn_w_in"):
        return t.T
    if name == "ffn_w_up":
        d, n = t.shape
        return _pad_to(t.T.reshape(2, n // 2, d), 1, _round_up(n // 2, LANES)).reshape(-1, d)
    if name == "ffn_w_down":
        return _pad_to(t, 0, _round_up(t.shape[0], LANES))
    if name == "pool_w":
        return t.reshape(-1, t.shape[-1])
    return t


def from_comm(name, r, shape):
    if name in ("att_w_qkv", "gdn_w_in"):
        return r.T
    if name == "ffn_w_up":
        d, n = shape
        return r.reshape(2, -1, d)[:, :n // 2].reshape(n, d).T
    if name == "ffn_w_down":
        return r[:shape[0]]
    return r.reshape(shape)


def _rows(t):
    return t.reshape(-1, t.shape[-1])


def build_weights(big, full):
    depth = full["ffn_conv"].shape[0]
    F4 = full["ffn_conv"].shape[2] // N_CHIPS
    F4p = _round_up(F4, LANES)
    W = dict(mix_norm=full["mix_norm"], ffn_norm=full["ffn_norm"], att=[], pool=[], gdn=[], ffn=[])
    for i in range(depth):
        conv = _pad_to(full["ffn_conv"][i].reshape(FFN_CONV, N_CHIPS, F4), 2, F4p).reshape(FFN_CONV, -1)
        W["ffn"].append(dict(w_up_t=_rows(big["ffn_w_up", i]), conv=_pad_to(conv, 0, SUBLANES),
                             w_down=_rows(big["ffn_w_down", i])))
    for j in range(full["att_q_gain"].shape[0]):
        W["att"].append(dict(w_qkv_t=_rows(big["att_w_qkv", j]), w_o=_rows(big["att_w_o", j]),
                             q_gain=full["att_q_gain"][j:j + 1], k_gain=full["att_k_gain"][j:j + 1],
                             rel_bias=full["att_rel_bias"][j]))
    for j in range(full["pool_scale"].shape[0]):
        t = big["pool_w", j]
        G = len(POOL_WINDOWS)
        dg = t.shape[-1]
        w = jnp.moveaxis(t.reshape(N_CHIPS, G, dg // N_CHIPS, dg), 0, 1).reshape(G, dg, dg)
        W["pool"].append(dict(w=w, scale=full["pool_scale"][j:j + 1]))
    for j in range(full["gdn_a_log"].shape[0]):
        C = full["gdn_conv"].shape[2]
        wt = _rows(big["gdn_w_in", j])
        V = _rows(big["gdn_w_o", j]).shape[0]
        W["gdn"].append(dict(
            w_main_t=wt[:C + V], w_ab_t=_pad_to(wt[C + V:], 0, LANES),
            conv=_pad_to(full["gdn_conv"][j], 0, SUBLANES), a_log=full["gdn_a_log"][j:j + 1],
            dt_bias=full["gdn_dt_bias"][j:j + 1], o_gain=full["gdn_o_gain"][j:j + 1], w_o=_rows(big["gdn_w_o", j]),
            key_dim=(C - V) // 2))
    return W


def big_grads(G, hv):
    out = {}

    def slots(t):
        return t.reshape(N_CHIPS, t.shape[0] // N_CHIPS, t.shape[1])

    for i, g in enumerate(G["ffn"]):
        out["ffn_w_up", i] = slots(g["w_up_t"])
        out["ffn_w_down", i] = slots(g["w_down"])
    for j, g in enumerate(G["att"]):
        out["att_w_qkv", j] = slots(g["w_qkv_t"])
        out["att_w_o", j] = slots(g["w_o"])
    for j, g in enumerate(G["pool"]):
        n, dg, _ = g["w"].shape
        out["pool_w", j] = jnp.moveaxis(g["w"].reshape(n, N_CHIPS, dg // N_CHIPS, dg), 1, 0).reshape(N_CHIPS, -1, dg)
    for j, g in enumerate(G["gdn"]):
        out["gdn_w_in", j] = slots(jnp.concatenate([g["w_main_t"], g["w_ab_t"][:2 * hv]], axis=0))
        out["gdn_w_o", j] = slots(g["w_o"])
    return out


def small_grads(G, full):
    F = full["ffn_conv"].shape[2]
    F4 = F // N_CHIPS

    def conv(g):
        return g["conv"][:FFN_CONV].reshape(FFN_CONV, N_CHIPS, -1)[:, :, :F4].reshape(FFN_CONV, F)

    return dict(
        mix_norm=jnp.concatenate(G["mix_norm"], axis=0), ffn_norm=jnp.concatenate(G["ffn_norm"], axis=0),
        ffn_conv=jnp.stack([conv(g) for g in G["ffn"]]),
        att_q_gain=jnp.concatenate([g["q_gain"] for g in G["att"]], axis=0),
        att_k_gain=jnp.concatenate([g["k_gain"] for g in G["att"]], axis=0),
        att_rel_bias=jnp.stack([g["rel_bias"] for g in G["att"]]),
        pool_scale=jnp.concatenate([g["scale"] for g in G["pool"]], axis=0),
        gdn_conv=jnp.stack([g["conv"][:GDN_CONV] for g in G["gdn"]]),
        gdn_a_log=jnp.concatenate([g["a_log"] for g in G["gdn"]], axis=0),
        gdn_dt_bias=jnp.concatenate([g["dt_bias"] for g in G["gdn"]], axis=0),
        gdn_o_gain=jnp.concatenate([g["o_gain"] for g in G["gdn"]], axis=0))


ANY = BS(memory_space=pl.ANY)
PACK_COLS = 1024
PACK_ROWS = 32


def _place():
    x, y, c = lax.axis_index("x"), lax.axis_index("y"), lax.axis_index("c")
    chips = [(1 - x, y), (x, 1 - y), (1 - x, 1 - y)]
    return x, y, c, chips


def _remote(src, dst, send_sem, recv_sem, to):
    return pltpu.make_async_remote_copy(src_ref=src, dst_ref=dst, send_sem=send_sem, recv_sem=recv_sem,
                                        device_id=to, device_id_type=MESH)


def gather_chips(shard, name):
    R, C = shard.shape
    half = R // 2

    def body(x_ref, o_ref, send_sems, recv_sems, local_sem):
        x, y, c, chips = _place()
        mine_rows = pl.ds(c * half, half)
        other_rows = pl.ds((1 - c) * half, half)
        own = pltpu.make_async_copy(x_ref, o_ref.at[2 * x + y], local_sem)
        own.start()
        first = [_remote(x_ref.at[mine_rows], o_ref.at[2 * x + y, mine_rows], send_sems.at[j], recv_sems.at[j],
                         (cx, cy, c)) for j, (cx, cy) in enumerate(chips)]
        for cp in first:
            cp.start()
        passed = []
        for j, (cx, cy) in enumerate(chips):
            landed = o_ref.at[2 * cx + cy, mine_rows]
            _remote(landed, landed, send_sems.at[j], recv_sems.at[j], (cx, cy, c)).wait_recv()
            cp = _remote(landed, landed, send_sems.at[3 + j], recv_sems.at[3 + j], (x, y, 1 - c))
            cp.start()
            passed.append(cp)
        for j, (cx, cy) in enumerate(chips):
            landed = o_ref.at[2 * cx + cy, other_rows]
            _remote(landed, landed, send_sems.at[3 + j], recv_sems.at[3 + j], (x, y, 1 - c)).wait_recv()
        for cp in first + passed:
            cp.wait_send()
        own.wait()

    return pl.pallas_call(
        body, out_shape=S((N_CHIPS, R, C), shard.dtype), in_specs=[ANY], out_specs=ANY,
        scratch_shapes=[pltpu.SemaphoreType.DMA((6,)), pltpu.SemaphoreType.DMA((6,)), pltpu.SemaphoreType.DMA],
        name=name)(shard)


def _tile(R, hc):
    if R % 256 == 0:
        return _pick(R, (512, 256)), hc
    return R, _pick(hc, (256, 128))


def _half(c, hc):
    return pl.ds(pl.multiple_of(c * hc, hc), hc)


def prep_slot(t, s_me, name):
    R, C = t.shape
    br, bc = _tile(R, C // 2)

    def body(s_ref, t_ref, o_ref):
        o_ref[0] = t_ref[...].astype(o_ref.dtype)

    return pl.pallas_call(
        body, grid_spec=pltpu.PrefetchScalarGridSpec(
            num_scalar_prefetch=1, grid=(R // br, C // bc),
            in_specs=[BS((br, bc), lambda i, j, s: (i, j))],
            out_specs=BS((1, br, bc), lambda i, j, s: (s[0], i, j))),
        out_shape=S((N_CHIPS, R, C), MXU_DTYPE), compiler_params=_cp(("parallel", "parallel")), name=name)(
            s_me.reshape(1), t)


def gather_slots(arrs, name):
    nt = len(arrs)

    def body(*refs):
        outs = refs[nt:2 * nt]
        send_sems, recv_sems = refs[2 * nt:]
        x, y, c, chips = _place()
        me = 2 * x + y
        first, passed = [], []
        for t, o in enumerate(outs):
            mine = _half(c, o.shape[2] // 2)
            for j, (cx, cy) in enumerate(chips):
                cp = _remote(o.at[me, :, mine], o.at[me, :, mine], send_sems.at[t, j], recv_sems.at[t, j], (cx, cy, c))
                cp.start()
                first.append(cp)
        for t, o in enumerate(outs):
            mine = _half(c, o.shape[2] // 2)
            for j, (cx, cy) in enumerate(chips):
                landed = o.at[2 * cx + cy, :, mine]
                _remote(landed, landed, send_sems.at[t, j], recv_sems.at[t, j], (cx, cy, c)).wait_recv()
                cp = _remote(landed, landed, send_sems.at[t, 3 + j], recv_sems.at[t, 3 + j], (x, y, 1 - c))
                cp.start()
                passed.append(cp)
        for t, o in enumerate(outs):
            other = _half(1 - c, o.shape[2] // 2)
            for j, (cx, cy) in enumerate(chips):
                landed = o.at[2 * cx + cy, :, other]
                _remote(landed, landed, send_sems.at[t, 3 + j], recv_sems.at[t, 3 + j], (x, y, 1 - c)).wait_recv()
        for cp in first + passed:
            cp.wait_send()

    return pl.pallas_call(
        body, out_shape=[S(a.shape, a.dtype) for a in arrs], in_specs=[ANY] * nt, out_specs=[ANY] * nt,
        input_output_aliases={t: t for t in range(nt)},
        scratch_shapes=[pltpu.SemaphoreType.DMA((nt, 6)), pltpu.SemaphoreType.DMA((nt, 6))], name=name)(*arrs)


def swap_cols(gs, name):
    nt = len(gs)

    def body(*refs):
        ins, outs = refs[:nt], refs[nt:2 * nt]
        send_sems, recv_sems = refs[2 * nt:]
        x, y, c, _ = _place()
        sent = []
        for t, (g, o) in enumerate(zip(ins, outs)):
            cp = _remote(g.at[:, :, _half(1 - c, o.shape[2])], o, send_sems.at[t], recv_sems.at[t], (x, y, 1 - c))
            cp.start()
            sent.append(cp)
        for cp in sent:
            cp.wait()

    return pl.pallas_call(
        body, out_shape=[S(g.shape[:2] + (g.shape[2] // 2,), g.dtype) for g in gs], in_specs=[ANY] * nt,
        out_specs=[ANY] * nt, scratch_shapes=[pltpu.SemaphoreType.DMA((nt,)), pltpu.SemaphoreType.DMA((nt,))],
        name=name)(*gs)


def add_cols2(g, other, c, name):
    n, R, C = g.shape
    hc = C // 2
    br, bc = _tile(R, hc)
    nj = hc // bc

    def body(c_ref, g_ref, o_ref, out_ref):
        out_ref[...] = (g_ref[...] + o_ref[...]).astype(out_ref.dtype)

    blk = BS((1, br, bc), lambda s, i, j, c_ref: (s, i, j))
    return pl.pallas_call(
        body, grid_spec=pltpu.PrefetchScalarGridSpec(
            num_scalar_prefetch=1, grid=(n, R // br, nj),
            in_specs=[BS((1, br, bc), lambda s, i, j, c_ref: (s, i, c_ref[0] * nj + j)), blk], out_specs=blk),
        out_shape=S((n, R, hc), BF16), compiler_params=_cp(("parallel", "parallel", "parallel")), name=name)(
            c, g, other)


def scatter_cols(ps, name):
    nt = len(ps)

    def body(*refs):
        ins, outs = refs[:nt], refs[nt:2 * nt]
        send_sems, recv_sems = refs[2 * nt:]
        x, y, c, chips = _place()
        sent = []
        for t, (p, o) in enumerate(zip(ins, outs)):
            for j, (cx, cy) in enumerate(chips):
                cp = _remote(p.at[2 * cx + cy], o.at[j], send_sems.at[t, j], recv_sems.at[t, j], (cx, cy, c))
                cp.start()
                sent.append(cp)
        for cp in sent:
            cp.wait()

    return pl.pallas_call(
        body, out_shape=[S((N_CHIPS - 1,) + p.shape[1:], p.dtype) for p in ps], in_specs=[ANY] * nt,
        out_specs=[ANY] * nt,
        scratch_shapes=[pltpu.SemaphoreType.DMA((nt, 3)), pltpu.SemaphoreType.DMA((nt, 3))], name=name)(*ps)


def add_cols4(p, got, place, name):
    n, R, hc = p.shape
    br, bc = _tile(R, hc)
    nj = hc // bc

    def body(pl_ref, p_ref, g_ref, out_ref):
        acc = p_ref[0].astype(F32)
        for j in range(n - 1):
            acc += g_ref[j].astype(F32)
        out_ref[...] = acc

    return pl.pallas_call(
        body, grid_spec=pltpu.PrefetchScalarGridSpec(
            num_scalar_prefetch=1, grid=(R // br, nj),
            in_specs=[BS((1, br, bc), lambda i, j, pl_ref: (pl_ref[0], i, j)),
                      BS((n - 1, br, bc), lambda i, j, pl_ref: (0, i, j))],
            out_specs=BS((br, bc), lambda i, j, pl_ref: (i, pl_ref[1] * nj + j))),
        out_shape=S((R, 2 * hc), F32), compiler_params=_cp(("parallel", "parallel")), name=name)(place, p, got)


def join_cols(rs, name):
    nt = len(rs)

    def body(*refs):
        outs = refs[nt:2 * nt]
        send_sems, recv_sems = refs[2 * nt:]
        x, y, c, _ = _place()
        sent = []
        for t, o in enumerate(outs):
            mine = o.at[:, _half(c, o.shape[1] // 2)]
            cp = _remote(mine, mine, send_sems.at[t], recv_sems.at[t], (x, y, 1 - c))
            cp.start()
            sent.append(cp)
        for t, o in enumerate(outs):
            theirs = o.at[:, _half(1 - c, o.shape[1] // 2)]
            _remote(theirs, theirs, send_sems.at[t], recv_sems.at[t], (x, y, 1 - c)).wait_recv()
        for cp in sent:
            cp.wait_send()

    return pl.pallas_call(
        body, out_shape=[S(r.shape, r.dtype) for r in rs], in_specs=[ANY] * nt, out_specs=[ANY] * nt,
        input_output_aliases={t: t for t in range(nt)},
        scratch_shapes=[pltpu.SemaphoreType.DMA((nt,)), pltpu.SemaphoreType.DMA((nt,))], name=name)(*rs)


def sum_devices(v, name):
    R, C = v.shape

    def body(v_ref, o_ref, slots, send_sems, recv_sems):
        x, y, c, _ = _place()
        me = 4 * x + 2 * y + c
        slots[me] = v_ref[...]
        sent = []
        for r in range(1, 8):
            peer = (x ^ (r >> 2), y ^ ((r >> 1) & 1), c ^ (r & 1))
            cp = _remote(v_ref, slots.at[me], send_sems.at[r - 1], recv_sems.at[r - 1], peer)
            cp.start()
            sent.append(cp)
        for r in range(1, 8):
            peer = (x ^ (r >> 2), y ^ ((r >> 1) & 1), c ^ (r & 1))
            theirs = slots.at[4 * peer[0] + 2 * peer[1] + peer[2]]
            _remote(v_ref, theirs, send_sems.at[r - 1], recv_sems.at[r - 1], peer).wait_recv()
        for cp in sent:
            cp.wait_send()
        acc = slots[0]
        for k in range(1, 8):
            acc += slots[k]
        o_ref[...] = acc

    vm = BS(memory_space=pltpu.VMEM)
    return pl.pallas_call(
        body, out_shape=S((R, C), F32), in_specs=[vm], out_specs=vm,
        scratch_shapes=[pltpu.VMEM((8, R, C), F32), pltpu.SemaphoreType.DMA((7,)), pltpu.SemaphoreType.DMA((7,))],
        compiler_params=pltpu.CompilerParams(vmem_limit_bytes=32 * MIB), name=name)(v)


def _pack(arrays, dtype, cols, row_mult):
    flat = jnp.concatenate([a.astype(dtype).reshape(-1) for a in arrays])
    n = flat.shape[0]
    total = _round_up(n, cols * row_mult)
    return jnp.pad(flat, (0, total - n)).reshape(total // cols, cols)


def _unpack(flat, shapes):
    out, off = [], 0
    for shp in shapes:
        n = 1
        for d in shp:
            n *= d
        out.append(flat[..., off:off + n].reshape(flat.shape[:-1] + tuple(shp)))
        off += n
    return out


def _layer_groups(depth):
    groups = []
    for i in range(depth):
        kind, j = i % 3, i // 3
        mix = ((("att_w_qkv", j), ("att_w_o", j)), (("pool_w", j),), (("gdn_w_in", j), ("gdn_w_o", j)))[kind]
        groups.append(mix + (("ffn_w_up", i), ("ffn_w_down", i)))
    return groups


def kernel(x, mix_norm, ffn_norm, att_w_qkv, att_q_gain, att_k_gain, att_rel_bias, att_w_o, pool_w, pool_scale, gdn_w_in, gdn_conv, gdn_a_log, gdn_dt_bias, gdn_o_gain, gdn_w_o, ffn_w_up, ffn_conv, ffn_w_down, loss_target, m_mix_norm, m_ffn_norm, m_att_w_qkv, m_att_q_gain, m_att_k_gain, m_att_rel_bias, m_att_w_o, m_pool_w, m_pool_scale, m_gdn_w_in, m_gdn_conv, m_gdn_a_log, m_gdn_dt_bias, m_gdn_o_gain, m_gdn_w_o, m_ffn_w_up, m_ffn_conv, m_ffn_w_down, v_mix_norm, v_ffn_norm, v_att_w_qkv, v_att_q_gain, v_att_k_gain, v_att_rel_bias, v_att_w_o, v_pool_w, v_pool_scale, v_gdn_w_in, v_gdn_conv, v_gdn_a_log, v_gdn_dt_bias, v_gdn_o_gain, v_gdn_w_o, v_ffn_w_up, v_ffn_conv, v_ffn_w_down):
    w = dict(mix_norm=mix_norm, ffn_norm=ffn_norm, att_w_qkv=att_w_qkv, att_q_gain=att_q_gain, att_k_gain=att_k_gain, att_rel_bias=att_rel_bias, att_w_o=att_w_o, pool_w=pool_w, pool_scale=pool_scale, gdn_w_in=gdn_w_in, gdn_conv=gdn_conv, gdn_a_log=gdn_a_log, gdn_dt_bias=gdn_dt_bias, gdn_o_gain=gdn_o_gain, gdn_w_o=gdn_w_o, ffn_w_up=ffn_w_up, ffn_conv=ffn_conv, ffn_w_down=ffn_w_down)
    m = dict(mix_norm=m_mix_norm, ffn_norm=m_ffn_norm, att_w_qkv=m_att_w_qkv, att_q_gain=m_att_q_gain, att_k_gain=m_att_k_gain, att_rel_bias=m_att_rel_bias, att_w_o=m_att_w_o, pool_w=m_pool_w, pool_scale=m_pool_scale, gdn_w_in=m_gdn_w_in, gdn_conv=m_gdn_conv, gdn_a_log=m_gdn_a_log, gdn_dt_bias=m_gdn_dt_bias, gdn_o_gain=m_gdn_o_gain, gdn_w_o=m_gdn_w_o, ffn_w_up=m_ffn_w_up, ffn_conv=m_ffn_conv, ffn_w_down=m_ffn_w_down)
    v = dict(mix_norm=v_mix_norm, ffn_norm=v_ffn_norm, att_w_qkv=v_att_w_qkv, att_q_gain=v_att_q_gain, att_k_gain=v_att_k_gain, att_rel_bias=v_att_rel_bias, att_w_o=v_att_w_o, pool_w=v_pool_w, pool_scale=v_pool_scale, gdn_w_in=v_gdn_w_in, gdn_conv=v_gdn_conv, gdn_a_log=v_gdn_a_log, gdn_dt_bias=v_gdn_dt_bias, gdn_o_gain=v_gdn_o_gain, gdn_w_o=v_gdn_w_o, ffn_w_up=v_ffn_w_up, ffn_conv=v_ffn_conv, ffn_w_down=v_ffn_w_down)
    depth = ffn_w_up.shape[0]
    my_c = lax.axis_index("c").astype(jnp.int32)
    my_chip = (2 * lax.axis_index("x") + lax.axis_index("y")).astype(jnp.int32)
    groups = _layer_groups(depth)

    core = my_c.reshape(1)
    place = jnp.stack([my_chip, my_c])

    full = {n: w[n] for n in REPLICATED}
    big = {}
    for i, group in enumerate(groups):
        slots = [prep_slot(to_comm(n, w[n][j]), my_chip, f"prep_{n}_{j}") for n, j in group]
        for (n, j), t in zip(group, gather_slots(slots, f"gather_l{i}")):
            big[n, j] = t
    small = [w[n] for n in SMALL_SHARDED]
    got = gather_chips(_pack(small, F32, LANES, PACK_ROWS), "gather_small").reshape(N_CHIPS, -1)
    for n, t in zip(SMALL_SHARDED, _unpack(got, [s.shape for s in small])):
        full[n] = _merge(t, SHARD_AXIS[n])

    W = build_weights(big, full)
    loss, grad_x, G = local_step(x[0], loss_target[0], W)
    loss = lax.psum(loss[0, 0], ("x", "y", "c"))
    gfull = small_grads(G, full)
    gbig = big_grads(G, gdn_a_log.shape[1])

    grads = {}
    gparts = {n: [None] * w[n].shape[0] for n in BIG}
    for i, group in enumerate(groups):
        gs = [gbig[n, j] for n, j in group]
        theirs = swap_cols(gs, f"rs_swap_l{i}")
        chip_sums = [add_cols2(g, a, core, f"rs_add2_{n}_{j}") for (n, j), g, a in zip(group, gs, theirs)]
        got = scatter_cols(chip_sums, f"rs_scatter_l{i}")
        mine = [add_cols4(p, b, place, f"rs_add4_{n}_{j}") for (n, j), p, b in zip(group, chip_sums, got)]
        for (n, j), r in zip(group, join_cols(mine, f"rs_join_l{i}")):
            gparts[n][j] = from_comm(n, r, w[n][j].shape)
    for n in BIG:
        grads[n] = jnp.stack(gparts[n])

    small_names = REPLICATED + SMALL_SHARDED
    packed = _pack([gfull[n] for n in small_names], F32, LANES, SUBLANES)
    summed = sum_devices(packed, "sum_small").reshape(-1)
    for n, t in zip(small_names, _unpack(summed, [gfull[n].shape for n in small_names])):
        if n in SHARD_AXIS:
            size = w[n].shape[SHARD_AXIS[n]]
            t = lax.dynamic_slice_in_dim(t, my_chip * size, size, axis=SHARD_AXIS[n])
        grads[n] = t

    delta, new_m, new_v = {}, {}, {}
    for n in WEIGHTS:
        delta[n], new_m[n], new_v[n] = adamw(w[n], grads[n], m[n], v[n], f"adamw_{n}")
    return (loss, grad_x[None], *[grads[n] for n in WEIGHTS], *[delta[n] for n in WEIGHTS],
            *[new_m[n] for n in WEIGHTS], *[new_v[n] for n in WEIGHTS])
```

```python
import functools

import jax
import jax.numpy as jnp
from jax import lax
from jax.experimental import pallas as pl
from jax.experimental.pallas import tpu as pltpu

F32 = jnp.float32
BF16 = jnp.bfloat16
MXU_DTYPE = BF16
HI = lax.Precision.HIGHEST
S = jax.ShapeDtypeStruct
BS = pl.BlockSpec

EPS = 1e-6
MASK_VALUE = -1e30
CHUNK = 64
HEAD = 128
LEFT_CHUNKS = 8
BAND_LEFT = LEFT_CHUNKS * CHUNK
BAND = BAND_LEFT + CHUNK
MAX_REL = 256
ATT_CHUNKS = 4
ATT_QB = ATT_CHUNKS * CHUNK
ATT_BAND = BAND_LEFT + ATT_QB
POOL_WINDOWS = (2, 4, 8, 16)
POOL_HALO = 16
GDN_CONV = 4
DELTA_HEADS = 8
FFN_CONV = 3
SUBLANES = 8
LANES = 128
FF_ALIGN = 512
N_CHIPS = 4
ADAM_LR, ADAM_B1, ADAM_B2, ADAM_EPS, ADAM_WD, ADAM_STEP = 0.001, 0.9, 0.999, 1e-08, 0.01, 10
MIB = 1024 * 1024
MESH = pl.DeviceIdType.MESH


def _cp(sems, vmem_mib=40):
    return pltpu.CompilerParams(dimension_semantics=sems, vmem_limit_bytes=vmem_mib * MIB)


def _pick(n, cands):
    for c in cands:
        if n % c == 0:
            return c
    return n


def _mx(x):
    return x.astype(MXU_DTYPE)


def _hi_lo(x):
    hi = x.astype(MXU_DTYPE)
    return hi, (x - hi.astype(F32)).astype(MXU_DTYPE)


def _dot(a, b, dims, hi=False):
    dn = (dims, ((), ()))
    if hi is True or (hi == "split" and MXU_DTYPE == F32):
        return lax.dot_general(a.astype(F32), b.astype(F32), dn, precision=HI, preferred_element_type=F32)
    if hi == "split":
        ah, al = _hi_lo(a)
        bh, bl = _hi_lo(b)
        return (lax.dot_general(ah, bh, dn, preferred_element_type=F32)
                + (lax.dot_general(ah, bl, dn, preferred_element_type=F32)
                   + lax.dot_general(al, bh, dn, preferred_element_type=F32)))
    return lax.dot_general(_mx(a), _mx(b), dn, preferred_element_type=F32)


def dot_nn(a, b, hi=False):
    return _dot(a, b, ((1,), (0,)), hi)


def dot_nt(a, b, hi=False):
    return _dot(a, b, ((1,), (1,)), hi)


def dot_tn(a, b, hi=False):
    return _dot(a, b, ((0,), (0,)), hi)


def _sigmoid(x):
    return 1.0 / (1.0 + jnp.exp(-x))


def matmul(a, b, mode, *, out_dtype=F32, res=None, name):
    if mode == "nn":
        (M, K), N = a.shape, b.shape[1]
    elif mode == "nt":
        (M, K), N = a.shape, b.shape[0]
    else:
        (K, M), N = a.shape, b.shape[1]
    bm = _pick(M, (1024, 512, 256, 128))
    bn = _pick(N, (1024, 512, 256, 128))
    bk = _pick(K, (2816, 2048, 1408, 1024, 512, 256, 128))
    nk = K // bk
    if mode == "nn":
        a_spec = BS((bm, bk), lambda i, j, k: (i, k))
        b_spec = BS((bk, bn), lambda i, j, k: (k, j))
        dot = dot_nn
    elif mode == "nt":
        a_spec = BS((bm, bk), lambda i, j, k: (i, k))
        b_spec = BS((bn, bk), lambda i, j, k: (j, k))
        dot = dot_nt
    else:
        a_spec = BS((bk, bm), lambda i, j, k: (k, i))
        b_spec = BS((bk, bn), lambda i, j, k: (k, j))
        dot = dot_tn
    o_spec = BS((bm, bn), lambda i, j, k: (i, j))
    has_res = res is not None

    def body(*refs):
        if has_res:
            a_ref, b_ref, r_ref, o_ref, acc = refs
        else:
            a_ref, b_ref, o_ref, acc = refs
            r_ref = None
        k = pl.program_id(2)
        p = dot(a_ref[...], b_ref[...])

        def finish(total):
            if has_res:
                total = r_ref[...] + total
            o_ref[...] = total.astype(o_ref.dtype)

        if nk == 1:
            finish(p)
        else:
            @pl.when(k == 0)
            def _():
                acc[...] = p

            @pl.when(jnp.logical_and(k > 0, k < nk - 1))
            def _():
                acc[...] += p

            @pl.when(k == nk - 1)
            def _():
                finish(acc[...] + p)

    in_specs = [a_spec, b_spec] + ([o_spec] if has_res else [])
    args = (a, b) + ((res,) if has_res else ())
    return pl.pallas_call(
        body, grid=(M // bm, N // bn, nk), in_specs=in_specs, out_specs=o_spec,
        out_shape=S((M, N), out_dtype), scratch_shapes=[pltpu.VMEM((bm, bn), F32)],
        compiler_params=_cp(("parallel", "parallel", "arbitrary"), 48), name=name)(*args)


def norm_fwd(x, gain, out_dtypes, name):
    T, D = x.shape
    bt = _pick(T, (256, 128, 64))

    def body(x_ref, g_ref, *o_refs):
        xv = x_ref[...]
        r = lax.rsqrt(jnp.mean(xv * xv, axis=-1, keepdims=True) + EPS)
        y = (xv * r) * g_ref[...]
        for o in o_refs:
            o[...] = y.astype(o.dtype)

    row = BS((bt, D), lambda i: (i, 0))
    return pl.pallas_call(
        body, grid=(T // bt,), in_specs=[row, BS((1, D), lambda i: (0, 0))],
        out_specs=[row] * len(out_dtypes), out_shape=[S((T, D), dt) for dt in out_dtypes],
        compiler_params=_cp(("parallel",)), name=name)(x, gain)


def norm_bwd(x, gain, dy, dres, name):
    T, D = x.shape
    bt = _pick(T, (256, 128, 64))

    def body(x_ref, g_ref, dy_ref, dres_ref, dx_ref, dg_ref):
        i = pl.program_id(0)
        xv = x_ref[...]
        dyv = dy_ref[...].astype(F32)
        r = lax.rsqrt(jnp.mean(xv * xv, axis=-1, keepdims=True) + EPS)
        xhat = xv * r
        dxhat = dyv * g_ref[...]
        dx = r * (dxhat - xhat * jnp.mean(dxhat * xhat, axis=-1, keepdims=True))
        dx_ref[...] = dres_ref[...] + dx

        @pl.when(i == 0)
        def _():
            dg_ref[...] = jnp.zeros_like(dg_ref)

        dg_ref[...] += jnp.sum(dyv * xhat, axis=0, keepdims=True)

    row = BS((bt, D), lambda i: (i, 0))
    vec = BS((1, D), lambda i: (0, 0))
    return pl.pallas_call(
        body, grid=(T // bt,), in_specs=[row, vec, row, row], out_specs=[row, vec],
        out_shape=[S((T, D), F32), S((1, D), F32)],
        compiler_params=_cp(("arbitrary",)), name=name)(x, gain, dy, dres)


def loss_and_grad(y, target, name):
    T, D = y.shape
    bt = _pick(T, (256, 128, 64))
    nt = T // bt

    def body(y_ref, t_ref, l_ref, dy_ref, acc):
        i = pl.program_id(0)
        e = y_ref[...] - t_ref[...]
        dy_ref[...] = e * (1.0 / D)

        @pl.when(i == 0)
        def _():
            acc[...] = jnp.zeros_like(acc)

        acc[...] += jnp.sum(e * e, axis=0, keepdims=True)

        @pl.when(i == nt - 1)
        def _():
            l_ref[...] = jnp.sum(acc[...], axis=1, keepdims=True) * (0.5 / D)

    row = BS((bt, D), lambda i: (i, 0))
    return pl.pallas_call(
        body, grid=(nt,), in_specs=[row, row], out_specs=[BS((1, 1), lambda i: (0, 0)), row],
        out_shape=[S((1, 1), F32), S((T, D), F32)], scratch_shapes=[pltpu.VMEM((1, D), F32)],
        compiler_params=_cp(("arbitrary",)), name=name)(y, target)


def _prev_halo(bt):
    return lambda i: (jnp.maximum(i * (bt // SUBLANES) - 1, 0), 0)


def _ffn_u(i, a_ref, halo_ref, w_ref, ext, bt):
    ext[pl.ds(0, SUBLANES), :] = jnp.where(i > 0, halo_ref[...], 0.0)
    ext[pl.ds(SUBLANES, bt), :] = a_ref[...]
    u = w_ref[2:3, :] * ext[pl.ds(SUBLANES, bt), :]
    u += w_ref[1:2, :] * ext[pl.ds(SUBLANES - 1, bt), :]
    u += w_ref[0:1, :] * ext[pl.ds(SUBLANES - 2, bt), :]
    return u


def ffn_act_fwd(up, conv_w, name):
    T, F2 = up.shape
    Fp = F2 // 2
    bt = _pick(T, (128, 64))

    def body(a_ref, b_ref, halo_ref, w_ref, o_ref, ext):
        u = _ffn_u(pl.program_id(0), a_ref, halo_ref, w_ref, ext, bt)
        o_ref[...] = (u * _sigmoid(u) * b_ref[...]).astype(o_ref.dtype)

    return pl.pallas_call(
        body, grid=(T // bt,),
        in_specs=[BS((bt, Fp), lambda i: (i, 0)), BS((bt, Fp), lambda i: (i, 1)),
                  BS((SUBLANES, Fp), _prev_halo(bt)), BS((SUBLANES, Fp), lambda i: (0, 0))],
        out_specs=BS((bt, Fp), lambda i: (i, 0)), out_shape=S((T, Fp), MXU_DTYPE),
        scratch_shapes=[pltpu.VMEM((bt + SUBLANES, Fp), F32)],
        compiler_params=_cp(("arbitrary",)), name=name)(up, up, up, conv_w)


def ffn_act_bwd_a(up, conv_w, g_act, name):
    T, F2 = up.shape
    Fp = F2 // 2
    bt = _pick(T, (128, 64))

    def body(a_ref, b_ref, halo_ref, w_ref, g_ref, du_ref, db_ref, dw_ref, ext):
        i = pl.program_id(0)
        u = _ffn_u(i, a_ref, halo_ref, w_ref, ext, bt)
        sg = _sigmoid(u)
        g = g_ref[...]
        db_ref[...] = (g * (u * sg)).astype(db_ref.dtype)
        du = g * b_ref[...] * (sg * (1.0 + u * (1.0 - sg)))
        du_ref[...] = du

        @pl.when(i == 0)
        def _():
            dw_ref[...] = jnp.zeros_like(dw_ref)

        for j in range(FFN_CONV):
            shifted = ext[pl.ds(SUBLANES - (FFN_CONV - 1) + j, bt), :]
            dw_ref[j:j + 1, :] += jnp.sum(du * shifted, axis=0, keepdims=True)

    blk = BS((bt, Fp), lambda i: (i, 0))
    full = BS((SUBLANES, Fp), lambda i: (0, 0))
    return pl.pallas_call(
        body, grid=(T // bt,),
        in_specs=[blk, BS((bt, Fp), lambda i: (i, 1)), BS((SUBLANES, Fp), _prev_halo(bt)), full, blk],
        out_specs=[blk, blk, full],
        out_shape=[S((T, Fp), F32), S((T, Fp), MXU_DTYPE), S((SUBLANES, Fp), F32)],
        scratch_shapes=[pltpu.VMEM((bt + SUBLANES, Fp), F32)],
        compiler_params=_cp(("arbitrary",)), name=name)(up, up, up, conv_w, g_act)


def ffn_act_bwd_b(du, db, conv_w, name):
    T, Fp = du.shape
    bt = _pick(T, (128, 64))
    nt = T // bt

    def body(du_ref, halo_ref, db_ref, w_ref, o_ref, ext):
        i = pl.program_id(0)
        ext[pl.ds(0, bt), :] = du_ref[...]
        ext[pl.ds(bt, SUBLANES), :] = jnp.where(i < nt - 1, halo_ref[...], 0.0)
        da = w_ref[2:3, :] * ext[pl.ds(0, bt), :]
        da += w_ref[1:2, :] * ext[pl.ds(1, bt), :]
        da += w_ref[0:1, :] * ext[pl.ds(2, bt), :]
        o_ref[:, pl.ds(0, Fp)] = da.astype(o_ref.dtype)
        o_ref[:, pl.ds(Fp, Fp)] = db_ref[...]

    blk = BS((bt, Fp), lambda i: (i, 0))
    nxt = BS((SUBLANES, Fp), lambda i: (jnp.minimum((i + 1) * (bt // SUBLANES), T // SUBLANES - 1), 0))
    return pl.pallas_call(
        body, grid=(nt,), in_specs=[blk, nxt, blk, BS((SUBLANES, Fp), lambda i: (0, 0))],
        out_specs=BS((bt, 2 * Fp), lambda i: (i, 0)), out_shape=S((T, 2 * Fp), MXU_DTYPE),
        scratch_shapes=[pltpu.VMEM((bt + SUBLANES, Fp), F32)],
        compiler_params=_cp(("arbitrary",)), name=name)(du, du, db, conv_w)


def _attn_fill(k_ref, v_ref, gk, kn_scr, vb_scr, T):
    kn_scr[pl.ds(0, BAND_LEFT), :] = jnp.zeros((BAND_LEFT, HEAD), kn_scr.dtype)
    vb_scr[pl.ds(0, BAND_LEFT), :] = jnp.zeros((BAND_LEFT, HEAD), vb_scr.dtype)
    rb = 512

    def fill(r, carry):
        rows = pl.ds(pl.multiple_of(r * rb, rb), rb)
        dst = pl.ds(pl.multiple_of(BAND_LEFT + r * rb, rb), rb)
        k = k_ref[rows, :]
        rk = lax.rsqrt(jnp.mean(k * k, axis=-1, keepdims=True) + EPS)
        kn_scr[dst, :] = ((k * rk) * gk).astype(kn_scr.dtype)
        vb_scr[dst, :] = v_ref[rows, :].astype(vb_scr.dtype)
        return carry

    lax.fori_loop(0, T // rb, fill, 0)


def _attn_probs(c, q_ref, gq, bias_ref, kn_scr):
    q = q_ref[...]
    rq = lax.rsqrt(jnp.mean(q * q, axis=-1, keepdims=True) + EPS)
    qn = (q * rq) * gq
    band = pl.ds(pl.multiple_of(c * ATT_QB, ATT_QB), ATT_BAND)
    kb = kn_scr[band, :]
    s = dot_nt(qn, kb) * (HEAD ** -0.5) + bias_ref[0]
    pos = c * ATT_QB - BAND_LEFT + lax.broadcasted_iota(jnp.int32, (ATT_QB, ATT_BAND), 1)
    s = jnp.where(pos >= 0, s, MASK_VALUE)
    m = jnp.max(s, axis=-1, keepdims=True)
    e = jnp.exp(s - m)
    p = e / jnp.sum(e, axis=-1, keepdims=True)
    return q, rq, qn, kb, p


def attn_fwd(qkv, gq, gk, bias, name):
    T, D3 = qkv.shape
    D = D3 // 3
    H = D // HEAD
    NC = T // ATT_QB

    def body(q_ref, k_ref, v_ref, gq_ref, gk_ref, bias_ref, o_ref, kn_scr, vb_scr):
        c = pl.program_id(1)

        @pl.when(c == 0)
        def _():
            _attn_fill(k_ref, v_ref, gk_ref[...], kn_scr, vb_scr, T)

        _, _, _, _, p = _attn_probs(c, q_ref, gq_ref[...], bias_ref, kn_scr)
        band = pl.ds(pl.multiple_of(c * ATT_QB, ATT_QB), ATT_BAND)
        o_ref[...] = dot_nn(p, vb_scr[band, :]).astype(o_ref.dtype)

    vec = BS((1, HEAD), lambda h, c: (0, 0))
    return pl.pallas_call(
        body, grid=(H, NC),
        in_specs=[BS((ATT_QB, HEAD), lambda h, c: (c, h)), BS((T, HEAD), lambda h, c: (0, H + h)),
                  BS((T, HEAD), lambda h, c: (0, 2 * H + h)), vec, vec,
                  BS((1, ATT_QB, ATT_BAND), lambda h, c: (h, 0, 0))],
        out_specs=BS((ATT_QB, HEAD), lambda h, c: (c, h)), out_shape=S((T, D), MXU_DTYPE),
        scratch_shapes=[pltpu.VMEM((T + BAND_LEFT, HEAD), MXU_DTYPE)] * 2,
        compiler_params=_cp(("arbitrary", "arbitrary"), 48), name=name)(qkv, qkv, qkv, gq, gk, bias)


def attn_bwd(qkv, do, gq, gk, bias, name):
    T, D3 = qkv.shape
    D = D3 // 3
    H = D // HEAD
    NC = T // ATT_QB
    scale = HEAD ** -0.5

    def body(q_ref, k_ref, v_ref, do_ref, gq_ref, gk_ref, bias_ref,
             dq_ref, dk_ref, dv_ref, dgq_ref, dgk_ref, dbias_ref, kn_scr, vb_scr, dkn_acc, dv_acc):
        h = pl.program_id(0)
        c = pl.program_id(1)
        gq = gq_ref[...]
        gk = gk_ref[...]

        @pl.when(c == 0)
        def _():
            _attn_fill(k_ref, v_ref, gk, kn_scr, vb_scr, T)
            dkn_acc[...] = jnp.zeros_like(dkn_acc)
            dv_acc[...] = jnp.zeros_like(dv_acc)
            dbias_ref[...] = jnp.zeros_like(dbias_ref)

        @pl.when(jnp.logical_and(c == 0, h == 0))
        def _():
            dgq_ref[...] = jnp.zeros_like(dgq_ref)
            dgk_ref[...] = jnp.zeros_like(dgk_ref)

        q, rq, qn, kb, p = _attn_probs(c, q_ref, gq, bias_ref, kn_scr)
        band = pl.ds(pl.multiple_of(c * ATT_QB, ATT_QB), ATT_BAND)
        dov = do_ref[...]
        dv_acc[band, :] += dot_tn(p, dov)
        dp = dot_nt(dov, vb_scr[band, :])
        ds = p * (dp - jnp.sum(dp * p, axis=-1, keepdims=True))
        dbias_ref[0] += ds
        dss = ds * scale
        dqn = dot_nn(dss, kb)
        dkn_acc[band, :] += dot_tn(dss, qn)
        xhat = q * rq
        dgq_ref[...] += jnp.sum(dqn * xhat, axis=0, keepdims=True)
        dxhat = dqn * gq
        dq = rq * (dxhat - xhat * jnp.mean(dxhat * xhat, axis=-1, keepdims=True))
        dq_ref[...] = dq.astype(dq_ref.dtype)

        @pl.when(c == NC - 1)
        def _():
            rb = 512

            def fin(r, carry):
                rows = pl.ds(pl.multiple_of(r * rb, rb), rb)
                src = pl.ds(pl.multiple_of(BAND_LEFT + r * rb, rb), rb)
                k = k_ref[rows, :]
                rk = lax.rsqrt(jnp.mean(k * k, axis=-1, keepdims=True) + EPS)
                khat = k * rk
                dkn = dkn_acc[src, :]
                dgk_ref[...] += jnp.sum(dkn * khat, axis=0, keepdims=True)
                dkh = dkn * gk
                dk = rk * (dkh - khat * jnp.mean(dkh * khat, axis=-1, keepdims=True))
                dk_ref[rows, :] = dk.astype(dk_ref.dtype)
                dv_ref[rows, :] = dv_acc[src, :].astype(dv_ref.dtype)
                return carry

            lax.fori_loop(0, T // rb, fin, 0)

    vec = BS((1, HEAD), lambda h, c: (0, 0))
    qblk = BS((ATT_QB, HEAD), lambda h, c: (c, h))
    col = BS((T, HEAD), lambda h, c: (0, h))
    bblk = BS((1, ATT_QB, ATT_BAND), lambda h, c: (h, 0, 0))
    return pl.pallas_call(
        body, grid=(H, NC),
        in_specs=[qblk, BS((T, HEAD), lambda h, c: (0, H + h)), BS((T, HEAD), lambda h, c: (0, 2 * H + h)),
                  qblk, vec, vec, bblk],
        out_specs=[qblk, col, col, vec, vec, bblk],
        out_shape=[S((T, D), MXU_DTYPE)] * 3 + [S((1, HEAD), F32)] * 2 + [S((H, ATT_QB, ATT_BAND), F32)],
        scratch_shapes=[pltpu.VMEM((T + BAND_LEFT, HEAD), MXU_DTYPE)] * 2
        + [pltpu.VMEM((T + BAND_LEFT, HEAD), F32)] * 2,
        compiler_params=_cp(("arbitrary", "arbitrary"), 56), name=name)(qkv, qkv, qkv, do, gq, gk, bias)


def _rel_onehot(qi, num_rel):
    kk = lax.broadcasted_iota(jnp.int32, (BAND, num_rel), 0)
    rr = lax.broadcasted_iota(jnp.int32, (BAND, num_rel), 1)
    idx = jnp.clip(BAND_LEFT + qi - kk, -(CHUNK - 1), MAX_REL) + (CHUNK - 1)
    return (idx == rr).astype(F32)


def rel_bias_expand(table, name):
    H, num_rel = table.shape

    def body(t_ref, o_ref):
        for qi in range(CHUNK):
            o_ref[qi] = dot_nt(t_ref[...], _rel_onehot(qi, num_rel), hi=True)

    return pl.pallas_call(body, out_shape=S((CHUNK, H, BAND), F32), name=name,
                          compiler_params=pltpu.CompilerParams(vmem_limit_bytes=40 * MIB))(table)


def rel_bias_reduce(dbias_t, num_rel, name):
    H = dbias_t.shape[1]

    def body(d_ref, o_ref):
        acc = jnp.zeros((H, num_rel), F32)
        for qi in range(CHUNK):
            acc += dot_nn(d_ref[qi], _rel_onehot(qi, num_rel), hi=True)
        o_ref[...] = acc

    return pl.pallas_call(body, out_shape=S((H, num_rel), F32), name=name,
                          compiler_params=pltpu.CompilerParams(vmem_limit_bytes=40 * MIB))(dbias_t)


def pool_fwd(h, x, w, scale, name):
    T, D = h.shape
    G = len(POOL_WINDOWS)
    Dg = D // G
    bt = _pick(T, (256, 128, 64))

    def body(h_ref, halo_ref, x_ref, w_ref, s_ref, o_ref, p_ref, ext):
        i = pl.program_id(0)
        ext[pl.ds(0, POOL_HALO), :] = jnp.where(i > 0, halo_ref[...], 0.0)
        ext[pl.ds(POOL_HALO, bt), :] = h_ref[...]
        t = i * bt + lax.broadcasted_iota(jnp.int32, (bt, 1), 0)
        for g, win in enumerate(POOL_WINDOWS):
            cols = pl.ds(g * Dg, Dg)
            acc = ext[pl.ds(POOL_HALO, bt), cols]
            for j in range(1, win):
                acc += ext[pl.ds(POOL_HALO - j, bt), cols]
            count = jnp.minimum(t + 1, win).astype(F32)
            pooled = acc / count - h_ref[:, cols]
            p_ref[:, cols] = pooled.astype(p_ref.dtype)
            y = dot_nn(pooled, w_ref[g]) * s_ref[:, cols]
            o_ref[:, cols] = x_ref[:, cols] + y

    row = BS((bt, D), lambda i: (i, 0))
    return pl.pallas_call(
        body, grid=(T // bt,),
        in_specs=[row, BS((POOL_HALO, D), lambda i: (jnp.maximum(i * (bt // POOL_HALO) - 1, 0), 0)), row,
                  BS((G, Dg, Dg), lambda i: (0, 0, 0)), BS((1, D), lambda i: (0, 0))],
        out_specs=[row, row], out_shape=[S((T, D), F32), S((T, D), MXU_DTYPE)],
        scratch_shapes=[pltpu.VMEM((bt + POOL_HALO, D), F32)],
        compiler_params=_cp(("arbitrary",)), name=name)(h, h, x, w, scale)


def pool_bwd_a(dy, pooled, w, scale, name):
    T, D = dy.shape
    G = len(POOL_WINDOWS)
    Dg = D // G
    bt = _pick(T, (256, 128, 64))

    def body(dy_ref, p_ref, w_ref, s_ref, dp_ref, dw_ref, ds_ref):
        i = pl.program_id(0)

        @pl.when(i == 0)
        def _():
            dw_ref[...] = jnp.zeros_like(dw_ref)
            ds_ref[...] = jnp.zeros_like(ds_ref)

        for g in range(G):
            cols = pl.ds(g * Dg, Dg)
            pg = p_ref[:, cols]
            dyg = dy_ref[:, cols]
            ypre = dot_nn(pg, w_ref[g])
            ds_ref[:, cols] += jnp.sum(dyg * ypre, axis=0, keepdims=True)
            dys = dyg * s_ref[:, cols]
            dp_ref[:, cols] = dot_nt(dys, w_ref[g])
            dw_ref[g] += dot_tn(pg, dys)

    row = BS((bt, D), lambda i: (i, 0))
    wspec = BS((G, Dg, Dg), lambda i: (0, 0, 0))
    vec = BS((1, D), lambda i: (0, 0))
    return pl.pallas_call(
        body, grid=(T // bt,), in_specs=[row, row, wspec, vec], out_specs=[row, wspec, vec],
        out_shape=[S((T, D), F32), S((G, Dg, Dg), F32), S((1, D), F32)],
        compiler_params=_cp(("arbitrary",)), name=name)(dy, pooled, w, scale)


def pool_bwd_b(dpooled, name):
    T, D = dpooled.shape
    G = len(POOL_WINDOWS)
    Dg = D // G
    bt = _pick(T, (256, 128, 64))
    nt = T // bt

    def body(d_ref, halo_ref, o_ref, ext):
        i = pl.program_id(0)
        t = i * bt + lax.broadcasted_iota(jnp.int32, (bt, 1), 0)
        for g, win in enumerate(POOL_WINDOWS):
            cols = pl.ds(g * Dg, Dg)
            count = jnp.minimum(t + 1, win).astype(F32)
            ext[pl.ds(0, bt), cols] = d_ref[:, cols] / count
            ext[pl.ds(bt, POOL_HALO), cols] = jnp.where(i < nt - 1, halo_ref[:, cols] * (1.0 / win), 0.0)
            acc = ext[pl.ds(0, bt), cols]
            for j in range(1, win):
                acc += ext[pl.ds(j, bt), cols]
            o_ref[:, cols] = acc - d_ref[:, cols]

    row = BS((bt, D), lambda i: (i, 0))
    nxt = BS((POOL_HALO, D), lambda i: (jnp.minimum((i + 1) * (bt // POOL_HALO), T // POOL_HALO - 1), 0))
    return pl.pallas_call(
        body, grid=(nt,), in_specs=[row, nxt], out_specs=row, out_shape=S((T, D), F32),
        scratch_shapes=[pltpu.VMEM((bt + POOL_HALO, D), F32)],
        compiler_params=_cp(("arbitrary",)), name=name)(dpooled, dpooled)


def _gdn_u(i, x_ref, halo_ref, w_ref, ext, bt):
    ext[pl.ds(0, SUBLANES), :] = jnp.where(i > 0, halo_ref[...], 0.0)
    ext[pl.ds(SUBLANES, bt), :] = x_ref[...]
    u = w_ref[3:4, :] * ext[pl.ds(SUBLANES, bt), :]
    for j in range(GDN_CONV - 1):
        u += w_ref[j:j + 1, :] * ext[pl.ds(SUBLANES - (GDN_CONV - 1) + j, bt), :]
    return u


def gdn_pre_fwd(proj, conv_w, key_dim, name):
    T = proj.shape[0]
    C = conv_w.shape[1]
    cb = min(1024, key_dim)
    nq, nqk, J = key_dim // cb, 2 * key_dim // cb, C // cb
    bt = _pick(T, (256, 128, 64))

    def body(x_ref, halo_ref, w_ref, o_ref, ext):
        i, j = pl.program_id(0), pl.program_id(1)
        u = _gdn_u(i, x_ref, halo_ref, w_ref, ext, bt)
        s = u * _sigmoid(u)

        @pl.when(j < nqk)
        def _():
            sc = jnp.where(j < nq, HEAD ** -0.5, 1.0)
            for hh in range(cb // HEAD):
                cols = pl.ds(hh * HEAD, HEAD)
                blk = s[:, hh * HEAD:(hh + 1) * HEAD]
                r = lax.rsqrt(jnp.sum(blk * blk, axis=-1, keepdims=True) + EPS)
                o_ref[:, cols] = (blk * r) * sc

        @pl.when(j >= nqk)
        def _():
            o_ref[...] = s

    return pl.pallas_call(
        body, grid=(T // bt, J),
        in_specs=[BS((bt, cb), lambda i, j: (i, j)),
                  BS((SUBLANES, cb), lambda i, j: (jnp.maximum(i * (bt // SUBLANES) - 1, 0), j)),
                  BS((SUBLANES, cb), lambda i, j: (0, j))],
        out_specs=BS((bt, cb), lambda i, j: (i, j)), out_shape=S((T, C), F32),
        scratch_shapes=[pltpu.VMEM((bt + SUBLANES, cb), F32)],
        compiler_params=_cp(("arbitrary", "arbitrary")), name=name)(proj, proj, conv_w)


def gdn_pre_bwd_a(proj, conv_w, dq_v, dk_v, dv, key_dim, name):
    T = proj.shape[0]
    C = conv_w.shape[1]
    cb = min(1024, key_dim)
    nq, nqk, J = key_dim // cb, 2 * key_dim // cb, C // cb
    nv = J - nqk
    bt = _pick(T, (256, 128, 64))

    def body(x_ref, halo_ref, w_ref, dq_ref, dk_ref, dv_ref, du_ref, dw_ref, ext, ds_scr):
        j, i = pl.program_id(0), pl.program_id(1)
        u = _gdn_u(i, x_ref, halo_ref, w_ref, ext, bt)
        sg = _sigmoid(u)
        s = u * sg

        @pl.when(j < nqk)
        def _():
            sc = jnp.where(j < nq, HEAD ** -0.5, 1.0)
            for hh in range(cb // HEAD):
                lo = 2 * hh * HEAD
                dq2 = dq_ref[:, lo:lo + HEAD] + dq_ref[:, lo + HEAD:lo + 2 * HEAD]
                dk2 = dk_ref[:, lo:lo + HEAD] + dk_ref[:, lo + HEAD:lo + 2 * HEAD]
                dn = jnp.where(j < nq, dq2, dk2)
                blk = s[:, hh * HEAD:(hh + 1) * HEAD]
                r = lax.rsqrt(jnp.sum(blk * blk, axis=-1, keepdims=True) + EPS)
                shat = blk * r
                ds_scr[:, pl.ds(hh * HEAD, HEAD)] = (sc * r) * (dn - shat * jnp.sum(dn * shat, axis=-1, keepdims=True))

        @pl.when(j >= nqk)
        def _():
            ds_scr[...] = dv_ref[...]

        du = ds_scr[...] * (sg * (1.0 + u * (1.0 - sg)))
        du_ref[...] = du

        @pl.when(i == 0)
        def _():
            dw_ref[...] = jnp.zeros_like(dw_ref)

        for k in range(GDN_CONV):
            shifted = ext[pl.ds(SUBLANES - (GDN_CONV - 1) + k, bt), :]
            dw_ref[k:k + 1, :] += jnp.sum(du * shifted, axis=0, keepdims=True)

    blk = BS((bt, cb), lambda j, i: (i, j))
    return pl.pallas_call(
        body, grid=(J, T // bt),
        in_specs=[blk, BS((SUBLANES, cb), lambda j, i: (jnp.maximum(i * (bt // SUBLANES) - 1, 0), j)),
                  BS((SUBLANES, cb), lambda j, i: (0, j)),
                  BS((bt, 2 * cb), lambda j, i: (i, jnp.minimum(j, nq - 1))),
                  BS((bt, 2 * cb), lambda j, i: (i, jnp.clip(j - nq, 0, nq - 1))),
                  BS((bt, cb), lambda j, i: (i, jnp.clip(j - nqk, 0, nv - 1)))],
        out_specs=[blk, BS((SUBLANES, cb), lambda j, i: (0, j))],
        out_shape=[S((T, C), F32), S((SUBLANES, C), F32)],
        scratch_shapes=[pltpu.VMEM((bt + SUBLANES, cb), F32), pltpu.VMEM((bt, cb), F32)],
        compiler_params=_cp(("arbitrary", "arbitrary")), name=name)(proj, proj, conv_w, dq_v, dk_v, dv)


def gdn_pre_bwd_b(du, dgate, conv_w, key_dim, name):
    T, C = du.shape
    V = dgate.shape[1]
    cb = min(1024, key_dim)
    J = C // cb
    J2 = (C + V) // cb
    bt = _pick(T, (256, 128, 64))
    nt = T // bt

    def body(du_ref, halo_ref, w_ref, dg_ref, o_ref, ext):
        i, j = pl.program_id(0), pl.program_id(1)

        @pl.when(j < J)
        def _():
            ext[pl.ds(0, bt), :] = du_ref[...]
            ext[pl.ds(bt, SUBLANES), :] = jnp.where(i < nt - 1, halo_ref[...], 0.0)
            da = w_ref[3:4, :] * ext[pl.ds(0, bt), :]
            for k in range(GDN_CONV - 1):
                da += w_ref[k:k + 1, :] * ext[pl.ds(GDN_CONV - 1 - k, bt), :]
            o_ref[...] = da.astype(o_ref.dtype)

        @pl.when(j >= J)
        def _():
            o_ref[...] = dg_ref[...]

    jc = lambda j: jnp.minimum(j, J - 1)
    return pl.pallas_call(
        body, grid=(nt, J2),
        in_specs=[BS((bt, cb), lambda i, j: (i, jc(j))),
                  BS((SUBLANES, cb), lambda i, j: (jnp.minimum((i + 1) * (bt // SUBLANES), T // SUBLANES - 1), jc(j))),
                  BS((SUBLANES, cb), lambda i, j: (0, jc(j))),
                  BS((bt, cb), lambda i, j: (i, jnp.maximum(j - J, 0)))],
        out_specs=BS((bt, cb), lambda i, j: (i, j)), out_shape=S((T, C + V), MXU_DTYPE),
        scratch_shapes=[pltpu.VMEM((bt + SUBLANES, cb), F32)],
        compiler_params=_cp(("arbitrary", "arbitrary")), name=name)(du, du, conv_w, dgate)


def _softplus(x):
    return jnp.maximum(x, 0.0) + jnp.log1p(jnp.exp(-jnp.abs(x)))


def gdn_gate_fwd(a, b, a_log, dt_bias, name):
    T, HV = a.shape
    bt = _pick(T, (1024, 512, 256, 128, 64))

    def body(a_ref, b_ref, al_ref, dt_ref, g_ref, be_ref):
        g_ref[...] = -jnp.exp(al_ref[...]) * _softplus(a_ref[...] + dt_ref[...])
        be_ref[...] = _sigmoid(b_ref[...])

    row = BS((bt, HV), lambda i: (i, 0))
    vec = BS((1, HV), lambda i: (0, 0))
    return pl.pallas_call(body, grid=(T // bt,), in_specs=[row, row, vec, vec], out_specs=[row, row],
                          out_shape=[S((T, HV), F32)] * 2, compiler_params=_cp(("parallel",)), name=name)(
                              a, b, a_log, dt_bias)


def gdn_gate_bwd(a, b, a_log, dt_bias, dg, dbeta, name):
    T, HV = a.shape
    bt = _pick(T, (1024, 512, 256, 128, 64))

    def body(a_ref, b_ref, al_ref, dt_ref, dg_ref, dbe_ref, da_ref, db_ref, dal_ref, ddt_ref):
        i = pl.program_id(0)
        x = a_ref[...] + dt_ref[...]
        ea = jnp.exp(al_ref[...])
        dgv = dg_ref[...]
        da = dgv * (-ea * _sigmoid(x))
        da_ref[...] = da
        be = _sigmoid(b_ref[...])
        db_ref[...] = dbe_ref[...] * be * (1.0 - be)

        @pl.when(i == 0)
        def _():
            dal_ref[...] = jnp.zeros_like(dal_ref)
            ddt_ref[...] = jnp.zeros_like(ddt_ref)

        dal_ref[...] += jnp.sum(dgv * (-ea * _softplus(x)), axis=0, keepdims=True)
        ddt_ref[...] += jnp.sum(da, axis=0, keepdims=True)

    row = BS((bt, HV), lambda i: (i, 0))
    vec = BS((1, HV), lambda i: (0, 0))
    return pl.pallas_call(body, grid=(T // bt,), in_specs=[row, row, vec, vec, row, row],
                          out_specs=[row, row, vec, vec],
                          out_shape=[S((T, HV), F32)] * 2 + [S((1, HV), F32)] * 2,
                          compiler_params=_cp(("arbitrary",)), name=name)(a, b, a_log, dt_bias, dg, dbeta)


def _col(row_vec, eye):
    return jnp.sum(jnp.where(eye, row_vec, 0.0), axis=1, keepdims=True)


def _row(col_vec, eye):
    return jnp.sum(jnp.where(eye, col_vec, 0.0), axis=0, keepdims=True)


def _each(f, *lists):
    return [f(*args) for args in zip(*lists)]


def _mul(a, b):
    return a * b


def _hdot_nn(a, b):
    return dot_nn(a, b, hi="split")


def _delta_chunk(q, k, v, g_row, b_row, rep, tinv=None):
    C = CHUNK
    ii = lax.broadcasted_iota(jnp.int32, (C, C), 0)
    jj = lax.broadcasted_iota(jnp.int32, (C, C), 1)
    eye, causal, strict = ii == jj, ii >= jj, ii > jj
    q_v = [q[p // rep] for p in range(len(v))]
    k_v = [k[p // rep] for p in range(len(v))]
    g_col = _each(lambda g: _col(g, eye), g_row)
    gc_row = _each(lambda g: jnp.sum(jnp.where(ii <= jj, g, 0.0), axis=0, keepdims=True), g_col)
    gc_col = _each(lambda g: _col(g, eye), gc_row)
    gl = _each(lambda g: jnp.sum(jnp.where(jj[0:1, :] == C - 1, g, 0.0), axis=1, keepdims=True), gc_row)
    decay = _each(lambda gc, gr: jnp.where(causal, jnp.exp(jnp.where(causal, gc - gr, 0.0)), 0.0), gc_col, gc_row)
    b_col = _each(lambda b: _col(b, eye), b_row)
    kb = _each(_mul, k_v, b_col)
    vb = _each(_mul, v, b_col)
    m = _each(dot_nt, kb, k_v)
    a = _each(lambda m_, d_: jnp.where(strict, m_ * d_, 0.0), m, decay)
    if tinv is None:
        ident = jnp.where(eye, 1.0, 0.0)
        tinv = [ident - a_ for a_ in a]
        pw = _each(_hdot_nn, a, a)
        for step in range(5):
            tinv = _each(lambda t_, p_: t_ + _hdot_nn(t_, p_), tinv, pw)
            if step < 4:
                pw = _each(_hdot_nn, pw, pw)
    egc = _each(jnp.exp, gc_col)
    kbg = _each(_mul, kb, egc)
    u = _each(_hdot_nn, tinv, vb)
    w = _each(_hdot_nn, tinv, kbg)
    n_k = _each(dot_nt, q, k)
    n = [n_k[p // rep] for p in range(len(v))]
    attn = _each(lambda n_, d_: jnp.where(causal, n_ * d_, 0.0), n, decay)
    qg = _each(_mul, q_v, egc)
    ekl = _each(lambda l_, c_: jnp.exp(l_ - c_), gl, gc_col)
    ks = _each(_mul, k_v, ekl)
    dec = _each(jnp.exp, gl)
    return dict(eye=eye, causal=causal, strict=strict, ii=ii, jj=jj, q=q_v, k=k_v, gc_col=gc_col, gl=gl, decay=decay,
                b_col=b_col, kb=kb, vb=vb, m=m, tinv=tinv, egc=egc, kbg=kbg, u=u, w=w, n=n, attn=attn,
                qg=qg, ekl=ekl, ks=ks, dec=dec)


def delta_fwd(qkvn, g_rows, b_rows, key_dim, name):
    T = qkvn.shape[0]
    NK = key_dim // HEAD
    HV = g_rows.shape[0]
    rep = HV // NK
    NC = T // CHUNK

    P = DELTA_HEADS
    kw, vw = HEAD * P // rep, HEAD * P

    def body(q_ref, k_ref, v_ref, g_ref, b_ref, o_ref, st_ref, ti_ref, state):
        n = pl.program_id(1)

        @pl.when(n == 0)
        def _():
            state[...] = jnp.zeros_like(state)

        heads = range(P)
        q = [q_ref[:, pl.ds(kh * HEAD, HEAD)] for kh in range(P // rep)]
        k = [k_ref[:, pl.ds(kh * HEAD, HEAD)] for kh in range(P // rep)]
        v = [v_ref[:, pl.ds(p * HEAD, HEAD)] for p in heads]
        s0 = [state[p] for p in heads]
        c = _delta_chunk(q, k, v, [g_ref[p, 0] for p in heads], [b_ref[p, 0] for p in heads], rep)
        vn = _each(lambda u_, w_, s_: u_ - dot_nn(w_, s_), c["u"], c["w"], s0)
        o = _each(lambda qg_, s_, at_, vn_: dot_nn(qg_, s_) + dot_nn(at_, vn_), c["qg"], s0, c["attn"], vn)
        s1 = _each(lambda s_, d_, ks_, vn_: s_ * d_ + dot_tn(ks_, vn_), s0, c["dec"], c["ks"], vn)
        for p in heads:
            st_ref[p, 0] = s0[p]
            ti_ref[p, 0] = c["tinv"][p]
            o_ref[:, pl.ds(p * HEAD, HEAD)] = o[p]
            state[p] = s1[p]

    vrow = BS((P, 1, 1, CHUNK), lambda h, n: (h, n, 0, 0))
    return pl.pallas_call(
        body, grid=(HV // P, NC),
        in_specs=[BS((CHUNK, kw), lambda h, n: (n, h)),
                  BS((CHUNK, kw), lambda h, n: (n, key_dim // kw + h)),
                  BS((CHUNK, vw), lambda h, n: (n, 2 * key_dim // vw + h)), vrow, vrow],
        out_specs=[BS((CHUNK, vw), lambda h, n: (n, h)), BS((P, 1, HEAD, HEAD), lambda h, n: (h, n, 0, 0)),
                   BS((P, 1, CHUNK, CHUNK), lambda h, n: (h, n, 0, 0))],
        out_shape=[S((T, HV * HEAD), F32), S((HV, NC, HEAD, HEAD), F32), S((HV, NC, CHUNK, CHUNK), F32)],
        scratch_shapes=[pltpu.VMEM((P, HEAD, HEAD), F32)],
        compiler_params=_cp(("arbitrary", "arbitrary")), name=name)(qkvn, qkvn, qkvn, g_rows, b_rows)


def delta_bwd(qkvn, g_rows, b_rows, states, tinvs, do, key_dim, name):
    T = qkvn.shape[0]
    NK = key_dim // HEAD
    HV = g_rows.shape[0]
    rep = HV // NK
    NC = T // CHUNK
    P = DELTA_HEADS
    kw, vw = HEAD * P // rep, HEAD * P

    def body(q_ref, k_ref, v_ref, g_ref, b_ref, st_ref, ti_ref, do_ref, dq_ref, dk_ref, dv_ref, dg_ref, dbe_ref,
             dstate):
        step = pl.program_id(1)

        @pl.when(step == 0)
        def _():
            dstate[...] = jnp.zeros_like(dstate)

        heads = range(P)
        q_k = [q_ref[:, pl.ds(kh * HEAD, HEAD)] for kh in range(P // rep)]
        k_k = [k_ref[:, pl.ds(kh * HEAD, HEAD)] for kh in range(P // rep)]
        v = [v_ref[:, pl.ds(p * HEAD, HEAD)] for p in heads]
        s0 = [st_ref[p, 0] for p in heads]
        dsn = [dstate[p] for p in heads]
        dov = [do_ref[:, pl.ds(p * HEAD, HEAD)] for p in heads]
        c = _delta_chunk(q_k, k_k, v, [g_ref[p, 0] for p in heads], [b_ref[p, 0] for p in heads], rep,
                         tinv=[ti_ref[p, 0] for p in heads])
        eye, causal, strict = c["eye"], c["causal"], c["strict"]
        q, k, tinv, decay = c["q"], c["k"], c["tinv"], c["decay"]

        def rsum(a, b):
            return jnp.sum(a * b, axis=1, keepdims=True)

        vn = _each(lambda u_, w_, s_: u_ - dot_nn(w_, s_), c["u"], c["w"], s0)
        dvn = _each(lambda at_, do_, ks_, ds_: dot_tn(at_, do_) + dot_nn(ks_, ds_), c["attn"], dov, c["ks"], dsn)
        dattn = _each(lambda do_, vn_: jnp.where(causal, dot_nt(do_, vn_), 0.0), dov, vn)
        dqg = _each(dot_nt, dov, s0)
        dks = _each(dot_nt, vn, dsn)
        ddec = _each(lambda s_, ds_: jnp.sum(rsum(s_, ds_), axis=0, keepdims=True), s0, dsn)
        dw = _each(lambda dvn_, s_: -dot_nt(dvn_, s_), dvn, s0)
        ds0 = _each(lambda qg_, do_, ds_, d_, w_, dvn_: dot_tn(qg_, do_) + ds_ * d_ - dot_tn(w_, dvn_),
                    c["qg"], dov, dsn, c["dec"], c["w"], dvn)
        dvb = _each(lambda t_, x_: dot_tn(t_, x_, hi="split"), tinv, dvn)
        dkbg = _each(lambda t_, x_: dot_tn(t_, x_, hi="split"), tinv, dw)
        dt = _each(lambda dvn_, vb_, dw_, kbg_: dot_nt(dvn_, vb_, hi="split") + dot_nt(dw_, kbg_, hi="split"),
                   dvn, c["vb"], dw, c["kbg"])
        dtt = _each(lambda dt_, t_: dot_nt(dt_, t_, hi="split"), dt, tinv)
        da = _each(lambda t_, x_: jnp.where(strict, -dot_tn(t_, x_, hi="split"), 0.0), tinv, dtt)
        dm = _each(_mul, da, decay)
        dn = _each(_mul, dattn, decay)
        e = _each(lambda da_, m_, dat_, n_, d_: (da_ * m_ + dat_ * n_) * d_, da, c["m"], dattn, c["n"], decay)
        dkb = _each(lambda dm_, k_, dkbg_, egc_: dot_nn(dm_, k_) + dkbg_ * egc_, dm, k, dkbg, c["egc"])
        dk = _each(lambda dm_, kb_, dn_, q_, dks_, ekl_, dkb_, b_: dot_tn(dm_, kb_) + dot_tn(dn_, q_) + dks_ * ekl_
                   + dkb_ * b_, dm, c["kb"], dn, q, dks, c["ekl"], dkb, c["b_col"])
        dq = _each(lambda dn_, k_, dqg_, egc_: dot_nn(dn_, k_) + dqg_ * egc_, dn, k, dqg, c["egc"])
        dks_ks = _each(rsum, dks, c["ks"])
        dgc_col = _each(lambda e_, dkbg_, kbg_, dqg_, qg_, x_: jnp.sum(e_, axis=1, keepdims=True) + rsum(dkbg_, kbg_)
                        + rsum(dqg_, qg_) - x_ - _col(jnp.sum(e_, axis=0, keepdims=True), eye),
                        e, dkbg, c["kbg"], dqg, c["qg"], dks_ks)
        dgl = _each(lambda x_, dd_, d_: jnp.sum(x_, axis=0, keepdims=True) + dd_ * d_, dks_ks, ddec, c["dec"])
        last = c["ii"][:, 0:1] == CHUNK - 1
        dgc_col = _each(lambda g_, l_: g_ + jnp.where(last, l_, 0.0), dgc_col, dgl)
        dbe_col = _each(lambda dvb_, v_, dkb_, k_: rsum(dvb_, v_) + rsum(dkb_, k_), dvb, v, dkb, k)
        for p in heads:
            vc = pl.ds(p * HEAD, HEAD)
            dstate[p] = ds0[p]
            dq_ref[:, vc] = dq[p]
            dk_ref[:, vc] = dk[p]
            dv_ref[:, vc] = dvb[p] * c["b_col"][p]
            dbe_ref[p, 0] = _row(dbe_col[p], eye)
            dg_ref[p, 0] = jnp.sum(jnp.where(causal, dgc_col[p], 0.0), axis=0, keepdims=True)

    rev = lambda n: NC - 1 - n
    vrow = BS((P, 1, 1, CHUNK), lambda h, n: (h, rev(n), 0, 0))
    vblk = BS((CHUNK, vw), lambda h, n: (rev(n), h))
    return pl.pallas_call(
        body, grid=(HV // P, NC),
        in_specs=[BS((CHUNK, kw), lambda h, n: (rev(n), h)),
                  BS((CHUNK, kw), lambda h, n: (rev(n), key_dim // kw + h)),
                  BS((CHUNK, vw), lambda h, n: (rev(n), 2 * key_dim // vw + h)), vrow, vrow,
                  BS((P, 1, HEAD, HEAD), lambda h, n: (h, rev(n), 0, 0)),
                  BS((P, 1, CHUNK, CHUNK), lambda h, n: (h, rev(n), 0, 0)), vblk],
        out_specs=[vblk, vblk, vblk, vrow, vrow],
        out_shape=[S((T, HV * HEAD), F32)] * 3 + [S((HV, NC, 1, CHUNK), F32)] * 2,
        scratch_shapes=[pltpu.VMEM((P, HEAD, HEAD), F32)],
        compiler_params=_cp(("arbitrary", "arbitrary")), name=name)(
            qkvn, qkvn, qkvn, g_rows, b_rows, states, tinvs, do)


def gdn_post_fwd(o, proj, gate_col0, o_gain, name):
    T, V = o.shape
    cb = min(1024, V)
    j0 = gate_col0 // cb
    bt = _pick(T, (256, 128, 64))

    def body(o_ref, g_ref, gain_ref, y_ref):
        for hh in range(cb // HEAD):
            cols = pl.ds(hh * HEAD, HEAD)
            ov = o_ref[:, cols]
            gt = g_ref[:, cols]
            r = lax.rsqrt(jnp.mean(ov * ov, axis=-1, keepdims=True) + EPS)
            y_ref[:, cols] = (((ov * r) * gain_ref[...]) * (gt * _sigmoid(gt))).astype(y_ref.dtype)

    return pl.pallas_call(
        body, grid=(T // bt, V // cb),
        in_specs=[BS((bt, cb), lambda i, j: (i, j)), BS((bt, cb), lambda i, j: (i, j0 + j)),
                  BS((1, HEAD), lambda i, j: (0, 0))],
        out_specs=BS((bt, cb), lambda i, j: (i, j)), out_shape=S((T, V), MXU_DTYPE),
        compiler_params=_cp(("parallel", "parallel")), name=name)(o, proj, o_gain)


def gdn_post_bwd(o, proj, gate_col0, o_gain, dy, name):
    T, V = o.shape
    cb = min(1024, V)
    j0 = gate_col0 // cb
    bt = _pick(T, (256, 128, 64))

    def body(o_ref, g_ref, gain_ref, dy_ref, do_ref, dgt_ref, dgain_ref):
        i, j = pl.program_id(0), pl.program_id(1)

        @pl.when(jnp.logical_and(i == 0, j == 0))
        def _():
            dgain_ref[...] = jnp.zeros_like(dgain_ref)

        gain = gain_ref[...]
        for hh in range(cb // HEAD):
            cols = pl.ds(hh * HEAD, HEAD)
            ov = o_ref[:, cols]
            gt = g_ref[:, cols]
            dyv = dy_ref[:, cols]
            r = lax.rsqrt(jnp.mean(ov * ov, axis=-1, keepdims=True) + EPS)
            ohat = ov * r
            sg = _sigmoid(gt)
            dyn = dyv * (gt * sg)
            dgt_ref[:, cols] = (dyv * (ohat * gain) * (sg * (1.0 + gt * (1.0 - sg)))).astype(dgt_ref.dtype)
            dgain_ref[...] += jnp.sum(dyn * ohat, axis=0, keepdims=True)
            dh = dyn * gain
            do_ref[:, cols] = r * (dh - ohat * jnp.mean(dh * ohat, axis=-1, keepdims=True))

    blk = BS((bt, cb), lambda i, j: (i, j))
    vec = BS((1, HEAD), lambda i, j: (0, 0))
    return pl.pallas_call(
        body, grid=(T // bt, V // cb),
        in_specs=[blk, BS((bt, cb), lambda i, j: (i, j0 + j)), vec, blk],
        out_specs=[blk, blk, vec], out_shape=[S((T, V), F32), S((T, V), MXU_DTYPE), S((1, HEAD), F32)],
        compiler_params=_cp(("arbitrary", "arbitrary")), name=name)(o, proj, o_gain, dy)


def adamw(w, g, m, v, name):
    shape = w.shape
    n = 1
    for d in shape:
        n *= d
    cols = shape[-1] if len(shape) > 1 else n
    rows = n // cols
    br = rows
    for cand in (512, 256, 128, 64, 32, 16, 8):
        if rows % cand == 0 and cand * cols * 4 * 14 <= 36 * MIB:
            br = cand
            break
    c1 = 1.0 - ADAM_B1 ** ADAM_STEP
    c2 = 1.0 - ADAM_B2 ** ADAM_STEP

    def body(w_ref, g_ref, m_ref, v_ref, d_ref, nm_ref, nv_ref):
        gv = g_ref[...]
        nm = ADAM_B1 * m_ref[...] + (1.0 - ADAM_B1) * gv
        nv = ADAM_B2 * v_ref[...] + (1.0 - ADAM_B2) * (gv * gv)
        nm_ref[...] = nm
        nv_ref[...] = nv
        d_ref[...] = -ADAM_LR * ((nm / c1) / (jnp.sqrt(nv / c2) + ADAM_EPS) + ADAM_WD * w_ref[...])

    blk = BS((br, cols), lambda i: (i, 0))
    outs = pl.pallas_call(
        body, grid=(rows // br,), in_specs=[blk] * 4, out_specs=[blk] * 3,
        out_shape=[S((rows, cols), F32)] * 3, compiler_params=_cp(("parallel",), 48), name=name)(
            *[t.reshape(rows, cols) for t in (w, g, m, v)])
    return [t.reshape(shape) for t in outs]


def _ffn_fwd(x, gain, p, tag):
    (h2,) = norm_fwd(x, gain, [MXU_DTYPE], f"{tag}_norm")
    up = matmul(h2, p["w_up_t"], "nt", name=f"{tag}_up")
    act = ffn_act_fwd(up, p["conv"], f"{tag}_act")
    out = matmul(act, p["w_down"], "nn", res=x, name=f"{tag}_down")
    return out, (x, h2, up, act)


def _ffn_bwd(dx, saved, gain, p, tag):
    x, h2, up, act = saved
    g_act = matmul(dx, p["w_down"], "nt", name=f"{tag}_bdown")
    d_down = matmul(act, dx, "tn", name=f"{tag}_wdown")
    du, db, d_conv = ffn_act_bwd_a(up, p["conv"], g_act, f"{tag}_bact_a")
    dup = ffn_act_bwd_b(du, db, p["conv"], f"{tag}_bact_b")
    dh2 = matmul(dup, p["w_up_t"], "nn", name=f"{tag}_bup")
    d_up_t = matmul(dup, h2, "tn", name=f"{tag}_wup")
    dx, d_gain = norm_bwd(x, gain, dh2, dx, f"{tag}_bnorm")
    return dx, d_gain, dict(w_up_t=d_up_t, conv=d_conv, w_down=d_down)


def _joint_bias(bias):
    rows = [jnp.pad(bias, ((0, 0), (0, 0), (qc * CHUNK, (ATT_CHUNKS - 1 - qc) * CHUNK)), constant_values=MASK_VALUE)
            for qc in range(ATT_CHUNKS)]
    return jnp.concatenate(rows, axis=1)


def _joint_bias_grad(dbias):
    return sum(dbias[:, qc * CHUNK:(qc + 1) * CHUNK, qc * CHUNK:qc * CHUNK + BAND] for qc in range(ATT_CHUNKS))


def _att_fwd(x, gain, p, tag):
    (h,) = norm_fwd(x, gain, [MXU_DTYPE], f"{tag}_norm")
    qkv = matmul(h, p["w_qkv_t"], "nt", name=f"{tag}_qkv")
    bias = _joint_bias(rel_bias_expand(p["rel_bias"], f"{tag}_rel").transpose(1, 0, 2))
    o = attn_fwd(qkv, p["q_gain"], p["k_gain"], bias, f"{tag}_core")
    out = matmul(o, p["w_o"], "nn", res=x, name=f"{tag}_out")
    return out, (x, h, qkv, o, bias)


def _att_bwd(dx, saved, gain, p, tag):
    x, h, qkv, o, bias = saved
    do = matmul(dx, p["w_o"], "nt", name=f"{tag}_bout")
    d_wo = matmul(o, dx, "tn", name=f"{tag}_wout")
    dq, dk, dv, d_gq, d_gk, dbias = attn_bwd(qkv, do, p["q_gain"], p["k_gain"], bias, f"{tag}_bcore")
    dqkv = jnp.concatenate([dq, dk, dv], axis=1)
    dh = matmul(dqkv, p["w_qkv_t"], "nn", name=f"{tag}_bqkv")
    d_wqkv_t = matmul(dqkv, h, "tn", name=f"{tag}_wqkv")
    d_rb = rel_bias_reduce(_joint_bias_grad(dbias).transpose(1, 0, 2), p["rel_bias"].shape[1], f"{tag}_brel")
    dx, d_gain = norm_bwd(x, gain, dh, dx, f"{tag}_bnorm")
    return dx, d_gain, dict(w_qkv_t=d_wqkv_t, w_o=d_wo, q_gain=d_gq, k_gain=d_gk, rel_bias=d_rb)


def _pool_fwd(x, gain, p, tag):
    (hf,) = norm_fwd(x, gain, [F32], f"{tag}_norm")
    out, pooled = pool_fwd(hf, x, p["w"], p["scale"], f"{tag}_core")
    return out, (x, pooled)


def _pool_bwd(dx, saved, gain, p, tag):
    x, pooled = saved
    dpooled, d_w, d_scale = pool_bwd_a(dx, pooled, p["w"], p["scale"], f"{tag}_bcore_a")
    dh = pool_bwd_b(dpooled, f"{tag}_bcore_b")
    dx, d_gain = norm_bwd(x, gain, dh, dx, f"{tag}_bnorm")
    return dx, d_gain, dict(w=d_w, scale=d_scale)


def _rows_layout(t, hv):
    return t.T.reshape(hv, t.shape[0] // CHUNK, 1, CHUNK)


def _gdn_fwd(x, gain, p, tag):
    T = x.shape[0]
    hv = p["a_log"].shape[1]
    key_dim = p["key_dim"]
    C = p["conv"].shape[1]
    (h,) = norm_fwd(x, gain, [MXU_DTYPE], f"{tag}_norm")
    proj = matmul(h, p["w_main_t"], "nt", name=f"{tag}_in")
    ab = matmul(h, p["w_ab_t"], "nt", name=f"{tag}_in_ab")
    a, b = ab[:, :hv], ab[:, hv:2 * hv]
    qkvn = gdn_pre_fwd(proj, p["conv"], key_dim, f"{tag}_pre")
    g, beta = gdn_gate_fwd(a, b, p["a_log"], p["dt_bias"], f"{tag}_gate")
    g_rows, b_rows = _rows_layout(g, hv), _rows_layout(beta, hv)
    o, states, tinvs = delta_fwd(qkvn, g_rows, b_rows, key_dim, f"{tag}_delta")
    y = gdn_post_fwd(o, proj, C, p["o_gain"], f"{tag}_post")
    out = matmul(y, p["w_o"], "nn", res=x, name=f"{tag}_out")
    return out, (x, h, proj, a, b, qkvn, g_rows, b_rows, o, states, tinvs, y)


def _gdn_bwd(dx, saved, gain, p, tag):
    x, h, proj, a, b, qkvn, g_rows, b_rows, o, states, tinvs, y = saved
    T = x.shape[0]
    hv = p["a_log"].shape[1]
    key_dim = p["key_dim"]
    C = p["conv"].shape[1]
    dy = matmul(dx, p["w_o"], "nt", name=f"{tag}_bout")
    d_wo = matmul(y, dx, "tn", name=f"{tag}_wout")
    do, dgate, d_ogain = gdn_post_bwd(o, proj, C, p["o_gain"], dy, f"{tag}_bpost")
    dq_v, dk_v, dv, dg_rows, dbe_rows = delta_bwd(qkvn, g_rows, b_rows, states, tinvs, do, key_dim,
                                                  f"{tag}_bdelta")
    dg = dg_rows.reshape(hv, T).T
    dbeta = dbe_rows.reshape(hv, T).T
    da, db, d_alog, d_dtb = gdn_gate_bwd(a, b, p["a_log"], p["dt_bias"], dg, dbeta, f"{tag}_bgate")
    du, d_conv = gdn_pre_bwd_a(proj, p["conv"], dq_v, dk_v, dv, key_dim, f"{tag}_bpre_a")
    dproj = gdn_pre_bwd_b(du, dgate, p["conv"], key_dim, f"{tag}_bpre_b")
    dab = jnp.concatenate([da, db, jnp.zeros((T, LANES - 2 * hv), F32)], axis=1)
    dh = matmul(dproj, p["w_main_t"], "nn", name=f"{tag}_bin")
    dh = matmul(dab, p["w_ab_t"], "nn", res=dh, name=f"{tag}_bin_ab")
    d_main_t = matmul(dproj, h, "tn", name=f"{tag}_win")
    d_ab_t = matmul(dab, h, "tn", name=f"{tag}_win_ab")
    dx, d_gain = norm_bwd(x, gain, dh, dx, f"{tag}_bnorm")
    return dx, d_gain, dict(w_main_t=d_main_t, w_ab_t=d_ab_t, conv=d_conv, a_log=d_alog, dt_bias=d_dtb,
                            o_gain=d_ogain, w_o=d_wo)


_MIXERS = ((_att_fwd, _att_bwd), (_pool_fwd, _pool_bwd), (_gdn_fwd, _gdn_bwd))


def local_step(x, target, W):
    depth = len(W["ffn"])
    saved = []
    for i in range(depth):
        kind, j = i % 3, i // 3
        mp = (W["att"], W["pool"], W["gdn"])[kind][j]
        x, s_mix = _MIXERS[kind][0](x, W["mix_norm"][i:i + 1], mp, f"l{i}_mix")
        x, s_ffn = _ffn_fwd(x, W["ffn_norm"][i:i + 1], W["ffn"][i], f"l{i}_ffn")
        saved.append((s_mix, s_ffn))
    loss, dx = loss_and_grad(x, target, "loss")
    G = dict(mix_norm=[None] * depth, ffn_norm=[None] * depth, ffn=[None] * depth,
             att=[None] * len(W["att"]), pool=[None] * len(W["pool"]), gdn=[None] * len(W["gdn"]))
    for i in reversed(range(depth)):
        kind, j = i % 3, i // 3
        s_mix, s_ffn = saved[i]
        dx, G["ffn_norm"][i], G["ffn"][i] = _ffn_bwd(dx, s_ffn, W["ffn_norm"][i:i + 1], W["ffn"][i], f"l{i}_ffn")
        mp = (W["att"], W["pool"], W["gdn"])[kind][j]
        dx, G["mix_norm"][i], gm = _MIXERS[kind][1](dx, s_mix, W["mix_norm"][i:i + 1], mp, f"l{i}_mix")
        G[("att", "pool", "gdn")[kind]][j] = gm
    return loss, dx, G


SHARD_AXIS = dict(att_w_qkv=2, att_w_o=1, pool_w=2, gdn_w_in=2, gdn_w_o=1, ffn_w_up=2, ffn_w_down=1,
                  att_rel_bias=2, gdn_conv=2, ffn_conv=2)
BIG = ("att_w_qkv", "att_w_o", "pool_w", "gdn_w_in", "gdn_w_o", "ffn_w_up", "ffn_w_down")
SMALL_SHARDED = ("att_rel_bias", "gdn_conv", "ffn_conv")
REPLICATED = ("mix_norm", "ffn_norm", "att_q_gain", "att_k_gain", "pool_scale", "gdn_a_log", "gdn_dt_bias",
              "gdn_o_gain")
WEIGHTS = ("mix_norm", "ffn_norm", "att_w_qkv", "att_q_gain", "att_k_gain", "att_rel_bias", "att_w_o", "pool_w",
           "pool_scale", "gdn_w_in", "gdn_conv", "gdn_a_log", "gdn_dt_bias", "gdn_o_gain", "gdn_w_o", "ffn_w_up",
           "ffn_conv", "ffn_w_down")


def _merge(stacked, axis):
    t = jnp.moveaxis(stacked, 0, axis)
    return t.reshape(t.shape[:axis] + (t.shape[axis] * t.shape[axis + 1],) + t.shape[axis + 2:])


def _pad_to(t, axis, size):
    pad = [(0, 0)] * t.ndim
    pad[axis] = (0, size - t.shape[axis])
    return jnp.pad(t, pad)


def _round_up(n, m):
    return (n + m - 1) // m * m


def to_comm(name, t):
    if name in ("att_w_qkv", "gdn_w_in"):
        return t.T
    if name == "ffn_w_up":
        d, n = t.shape
        return _pad_to(t.T.reshape(2, n // 2, d), 1, _round_up(n // 2, LANES)).reshape(-1, d)
    if name == "ffn_w_down":
        return _pad_to(t, 0, _round_up(t.shape[0], LANES))
    if name == "pool_w":
        return t.reshape(-1, t.shape[-1])
    return t


def from_comm(name, r, shape):
    if name in ("att_w_qkv", "gdn_w_in"):
        return r.T
    if name == "ffn_w_up":
        d, n = shape
        return r.reshape(2, -1, d)[:, :n // 2].reshape(n, d).T
    if name == "ffn_w_down":
        return r[:shape[0]]
    return r.reshape(shape)


def _rows(t):
    return t.reshape(-1, t.shape[-1])


def build_weights(big, full):
    depth = full["ffn_conv"].shape[0]
    F4 = full["ffn_conv"].shape[2] // N_CHIPS
    F4p = _round_up(F4, LANES)
    W = dict(mix_norm=full["mix_norm"], ffn_norm=full["ffn_norm"], att=[], pool=[], gdn=[], ffn=[])
    for i in range(depth):
        conv = _pad_to(full["ffn_conv"][i].reshape(FFN_CONV, N_CHIPS, F4), 2, F4p).reshape(FFN_CONV, -1)
        W["ffn"].append(dict(w_up_t=_rows(big["ffn_w_up", i]), conv=_pad_to(conv, 0, SUBLANES),
                             w_down=_rows(big["ffn_w_down", i])))
    for j in range(full["att_q_gain"].shape[0]):
        W["att"].append(dict(w_qkv_t=_rows(big["att_w_qkv", j]), w_o=_rows(big["att_w_o", j]),
                             q_gain=full["att_q_gain"][j:j + 1], k_gain=full["att_k_gain"][j:j + 1],
                             rel_bias=full["att_rel_bias"][j]))
    for j in range(full["pool_scale"].shape[0]):
        t = big["pool_w", j]
        G = len(POOL_WINDOWS)
        dg = t.shape[-1]
        w = jnp.moveaxis(t.reshape(N_CHIPS, G, dg // N_CHIPS, dg), 0, 1).reshape(G, dg, dg)
        W["pool"].append(dict(w=w, scale=full["pool_scale"][j:j + 1]))
    for j in range(full["gdn_a_log"].shape[0]):
        C = full["gdn_conv"].shape[2]
        wt = _rows(big["gdn_w_in", j])
        V = _rows(big["gdn_w_o", j]).shape[0]
        W["gdn"].append(dict(
            w_main_t=wt[:C + V], w_ab_t=_pad_to(wt[C + V:], 0, LANES),
            conv=_pad_to(full["gdn_conv"][j], 0, SUBLANES), a_log=full["gdn_a_log"][j:j + 1],
            dt_bias=full["gdn_dt_bias"][j:j + 1], o_gain=full["gdn_o_gain"][j:j + 1], w_o=_rows(big["gdn_w_o", j]),
            key_dim=(C - V) // 2))
    return W


def big_grads(G, hv):
    out = {}

    def slots(t):
        return t.reshape(N_CHIPS, t.shape[0] // N_CHIPS, t.shape[1])

    for i, g in enumerate(G["ffn"]):
        out["ffn_w_up", i] = slots(g["w_up_t"])
        out["ffn_w_down", i] = slots(g["w_down"])
    for j, g in enumerate(G["att"]):
        out["att_w_qkv", j] = slots(g["w_qkv_t"])
        out["att_w_o", j] = slots(g["w_o"])
    for j, g in enumerate(G["pool"]):
        n, dg, _ = g["w"].shape
        out["pool_w", j] = jnp.moveaxis(g["w"].reshape(n, N_CHIPS, dg // N_CHIPS, dg), 1, 0).reshape(N_CHIPS, -1, dg)
    for j, g in enumerate(G["gdn"]):
        out["gdn_w_in", j] = slots(jnp.concatenate([g["w_main_t"], g["w_ab_t"][:2 * hv]], axis=0))
        out["gdn_w_o", j] = slots(g["w_o"])
    return out


def small_grads(G, full):
    F = full["ffn_conv"].shape[2]
    F4 = F // N_CHIPS

    def conv(g):
        return g["conv"][:FFN_CONV].reshape(FFN_CONV, N_CHIPS, -1)[:, :, :F4].reshape(FFN_CONV, F)

    return dict(
        mix_norm=jnp.concatenate(G["mix_norm"], axis=0), ffn_norm=jnp.concatenate(G["ffn_norm"], axis=0),
        ffn_conv=jnp.stack([conv(g) for g in G["ffn"]]),
        att_q_gain=jnp.concatenate([g["q_gain"] for g in G["att"]], axis=0),
        att_k_gain=jnp.concatenate([g["k_gain"] for g in G["att"]], axis=0),
        att_rel_bias=jnp.stack([g["rel_bias"] for g in G["att"]]),
        pool_scale=jnp.concatenate([g["scale"] for g in G["pool"]], axis=0),
        gdn_conv=jnp.stack([g["conv"][:GDN_CONV] for g in G["gdn"]]),
        gdn_a_log=jnp.concatenate([g["a_log"] for g in G["gdn"]], axis=0),
        gdn_dt_bias=jnp.concatenate([g["dt_bias"] for g in G["gdn"]], axis=0),
        gdn_o_gain=jnp.concatenate([g["o_gain"] for g in G["gdn"]], axis=0))


ANY = BS(memory_space=pl.ANY)
PACK_COLS = 1024
PACK_ROWS = 32


def _place():
    x, y, c = lax.axis_index("x"), lax.axis_index("y"), lax.axis_index("c")
    chips = [(1 - x, y), (x, 1 - y), (1 - x, 1 - y)]
    return x, y, c, chips


def _remote(src, dst, send_sem, recv_sem, to):
    return pltpu.make_async_remote_copy(src_ref=src, dst_ref=dst, send_sem=send_sem, recv_sem=recv_sem,
                                        device_id=to, device_id_type=MESH)


def gather_chips(shard, name):
    R, C = shard.shape
    half = R // 2

    def body(x_ref, o_ref, send_sems, recv_sems, local_sem):
        x, y, c, chips = _place()
        mine_rows = pl.ds(c * half, half)
        other_rows = pl.ds((1 - c) * half, half)
        own = pltpu.make_async_copy(x_ref, o_ref.at[2 * x + y], local_sem)
        own.start()
        first = [_remote(x_ref.at[mine_rows], o_ref.at[2 * x + y, mine_rows], send_sems.at[j], recv_sems.at[j],
                         (cx, cy, c)) for j, (cx, cy) in enumerate(chips)]
        for cp in first:
            cp.start()
        passed = []
        for j, (cx, cy) in enumerate(chips):
            landed = o_ref.at[2 * cx + cy, mine_rows]
            _remote(landed, landed, send_sems.at[j], recv_sems.at[j], (cx, cy, c)).wait_recv()
            cp = _remote(landed, landed, send_sems.at[3 + j], recv_sems.at[3 + j], (x, y, 1 - c))
            cp.start()
            passed.append(cp)
        for j, (cx, cy) in enumerate(chips):
            landed = o_ref.at[2 * cx + cy, other_rows]
            _remote(landed, landed, send_sems.at[3 + j], recv_sems.at[3 + j], (x, y, 1 - c)).wait_recv()
        for cp in first + passed:
            cp.wait_send()
        own.wait()

    return pl.pallas_call(
        body, out_shape=S((N_CHIPS, R, C), shard.dtype), in_specs=[ANY], out_specs=ANY,
        scratch_shapes=[pltpu.SemaphoreType.DMA((6,)), pltpu.SemaphoreType.DMA((6,)), pltpu.SemaphoreType.DMA],
        name=name)(shard)


def _tile(R, hc):
    if R % 256 == 0:
        return _pick(R, (512, 256)), hc
    return R, _pick(hc, (256, 128))


def _half(c, hc):
    return pl.ds(pl.multiple_of(c * hc, hc), hc)


def prep_slot(t, s_me, name):
    R, C = t.shape
    br, bc = _tile(R, C // 2)

    def body(s_ref, t_ref, o_ref):
        o_ref[0] = t_ref[...].astype(o_ref.dtype)

    return pl.pallas_call(
        body, grid_spec=pltpu.PrefetchScalarGridSpec(
            num_scalar_prefetch=1, grid=(R // br, C // bc),
            in_specs=[BS((br, bc), lambda i, j, s: (i, j))],
            out_specs=BS((1, br, bc), lambda i, j, s: (s[0], i, j))),
        out_shape=S((N_CHIPS, R, C), MXU_DTYPE), compiler_params=_cp(("parallel", "parallel")), name=name)(
            s_me.reshape(1), t)


def gather_slots(arrs, name):
    nt = len(arrs)

    def body(*refs):
        outs = refs[nt:2 * nt]
        send_sems, recv_sems = refs[2 * nt:]
        x, y, c, chips = _place()
        me = 2 * x + y
        first, passed = [], []
        for t, o in enumerate(outs):
            mine = _half(c, o.shape[2] // 2)
            for j, (cx, cy) in enumerate(chips):
                cp = _remote(o.at[me, :, mine], o.at[me, :, mine], send_sems.at[t, j], recv_sems.at[t, j], (cx, cy, c))
                cp.start()
                first.append(cp)
        for t, o in enumerate(outs):
            mine = _half(c, o.shape[2] // 2)
            for j, (cx, cy) in enumerate(chips):
                landed = o.at[2 * cx + cy, :, mine]
                _remote(landed, landed, send_sems.at[t, j], recv_sems.at[t, j], (cx, cy, c)).wait_recv()
                cp = _remote(landed, landed, send_sems.at[t, 3 + j], recv_sems.at[t, 3 + j], (x, y, 1 - c))
                cp.start()
                passed.append(cp)
        for t, o in enumerate(outs):
            other = _half(1 - c, o.shape[2] // 2)
            for j, (cx, cy) in enumerate(chips):
                landed = o.at[2 * cx + cy, :, other]
                _remote(landed, landed, send_sems.at[t, 3 + j], recv_sems.at[t, 3 + j], (x, y, 1 - c)).wait_recv()
        for cp in first + passed:
            cp.wait_send()

    return pl.pallas_call(
        body, out_shape=[S(a.shape, a.dtype) for a in arrs], in_specs=[ANY] * nt, out_specs=[ANY] * nt,
        input_output_aliases={t: t for t in range(nt)},
        scratch_shapes=[pltpu.SemaphoreType.DMA((nt, 6)), pltpu.SemaphoreType.DMA((nt, 6))], name=name)(*arrs)


def swap_cols(gs, name):
    nt = len(gs)

    def body(*refs):
        ins, outs = refs[:nt], refs[nt:2 * nt]
        send_sems, recv_sems = refs[2 * nt:]
        x, y, c, _ = _place()
        sent = []
        for t, (g, o) in enumerate(zip(ins, outs)):
            cp = _remote(g.at[:, :, _half(1 - c, o.shape[2])], o, send_sems.at[t], recv_sems.at[t], (x, y, 1 - c))
            cp.start()
            sent.append(cp)
        for cp in sent:
            cp.wait()

    return pl.pallas_call(
        body, out_shape=[S(g.shape[:2] + (g.shape[2] // 2,), g.dtype) for g in gs], in_specs=[ANY] * nt,
        out_specs=[ANY] * nt, scratch_shapes=[pltpu.SemaphoreType.DMA((nt,)), pltpu.SemaphoreType.DMA((nt,))],
        name=name)(*gs)


def add_cols2(g, other, c, name):
    n, R, C = g.shape
    hc = C // 2
    br, bc = _tile(R, hc)
    nj = hc // bc

    def body(c_ref, g_ref, o_ref, out_ref):
        out_ref[...] = (g_ref[...] + o_ref[...]).astype(out_ref.dtype)

    blk = BS((1, br, bc), lambda s, i, j, c_ref: (s, i, j))
    return pl.pallas_call(
        body, grid_spec=pltpu.PrefetchScalarGridSpec(
            num_scalar_prefetch=1, grid=(n, R // br, nj),
            in_specs=[BS((1, br, bc), lambda s, i, j, c_ref: (s, i, c_ref[0] * nj + j)), blk], out_specs=blk),
        out_shape=S((n, R, hc), BF16), compiler_params=_cp(("parallel", "parallel", "parallel")), name=name)(
            c, g, other)


def scatter_cols(ps, name):
    nt = len(ps)

    def body(*refs):
        ins, outs = refs[:nt], refs[nt:2 * nt]
        send_sems, recv_sems = refs[2 * nt:]
        x, y, c, chips = _place()
        sent = []
        for t, (p, o) in enumerate(zip(ins, outs)):
            for j, (cx, cy) in enumerate(chips):
                cp = _remote(p.at[2 * cx + cy], o.at[j], send_sems.at[t, j], recv_sems.at[t, j], (cx, cy, c))
                cp.start()
                sent.append(cp)
        for cp in sent:
            cp.wait()

    return pl.pallas_call(
        body, out_shape=[S((N_CHIPS - 1,) + p.shape[1:], p.dtype) for p in ps], in_specs=[ANY] * nt,
        out_specs=[ANY] * nt,
        scratch_shapes=[pltpu.SemaphoreType.DMA((nt, 3)), pltpu.SemaphoreType.DMA((nt, 3))], name=name)(*ps)


def add_cols4(p, got, place, name):
    n, R, hc = p.shape
    br, bc = _tile(R, hc)
    nj = hc // bc

    def body(pl_ref, p_ref, g_ref, out_ref):
        acc = p_ref[0].astype(F32)
        for j in range(n - 1):
            acc += g_ref[j].astype(F32)
        out_ref[...] = acc

    return pl.pallas_call(
        body, grid_spec=pltpu.PrefetchScalarGridSpec(
            num_scalar_prefetch=1, grid=(R // br, nj),
            in_specs=[BS((1, br, bc), lambda i, j, pl_ref: (pl_ref[0], i, j)),
                      BS((n - 1, br, bc), lambda i, j, pl_ref: (0, i, j))],
            out_specs=BS((br, bc), lambda i, j, pl_ref: (i, pl_ref[1] * nj + j))),
        out_shape=S((R, 2 * hc), F32), compiler_params=_cp(("parallel", "parallel")), name=name)(place, p, got)


def join_cols(rs, name):
    nt = len(rs)

    def body(*refs):
        outs = refs[nt:2 * nt]
        send_sems, recv_sems = refs[2 * nt:]
        x, y, c, _ = _place()
        sent = []
        for t, o in enumerate(outs):
            mine = o.at[:, _half(c, o.shape[1] // 2)]
            cp = _remote(mine, mine, send_sems.at[t], recv_sems.at[t], (x, y, 1 - c))
            cp.start()
            sent.append(cp)
        for t, o in enumerate(outs):
            theirs = o.at[:, _half(1 - c, o.shape[1] // 2)]
            _remote(theirs, theirs, send_sems.at[t], recv_sems.at[t], (x, y, 1 - c)).wait_recv()
        for cp in sent:
            cp.wait_send()

    return pl.pallas_call(
        body, out_shape=[S(r.shape, r.dtype) for r in rs], in_specs=[ANY] * nt, out_specs=[ANY] * nt,
        input_output_aliases={t: t for t in range(nt)},
        scratch_shapes=[pltpu.SemaphoreType.DMA((nt,)), pltpu.SemaphoreType.DMA((nt,))], name=name)(*rs)


def sum_devices(v, name):
    R, C = v.shape

    def body(v_ref, o_ref, slots, send_sems, recv_sems):
        x, y, c, _ = _place()
        me = 4 * x + 2 * y + c
        slots[me] = v_ref[...]
        sent = []
        for r in range(1, 8):
            peer = (x ^ (r >> 2), y ^ ((r >> 1) & 1), c ^ (r & 1))
            cp = _remote(v_ref, slots.at[me], send_sems.at[r - 1], recv_sems.at[r - 1], peer)
            cp.start()
            sent.append(cp)
        for r in range(1, 8):
            peer = (x ^ (r >> 2), y ^ ((r >> 1) & 1), c ^ (r & 1))
            theirs = slots.at[4 * peer[0] + 2 * peer[1] + peer[2]]
            _remote(v_ref, theirs, send_sems.at[r - 1], recv_sems.at[r - 1], peer).wait_recv()
        for cp in sent:
            cp.wait_send()
        acc = slots[0]
        for k in range(1, 8):
            acc += slots[k]
        o_ref[...] = acc

    vm = BS(memory_space=pltpu.VMEM)
    return pl.pallas_call(
        body, out_shape=S((R, C), F32), in_specs=[vm], out_specs=vm,
        scratch_shapes=[pltpu.VMEM((8, R, C), F32), pltpu.SemaphoreType.DMA((7,)), pltpu.SemaphoreType.DMA((7,))],
        compiler_params=pltpu.CompilerParams(vmem_limit_bytes=32 * MIB), name=name)(v)


def _pack(arrays, dtype, cols, row_mult):
    flat = jnp.concatenate([a.astype(dtype).reshape(-1) for a in arrays])
    n = flat.shape[0]
    total = _round_up(n, cols * row_mult)
    return jnp.pad(flat, (0, total - n)).reshape(total // cols, cols)


def _unpack(flat, shapes):
    out, off = [], 0
    for shp in shapes:
        n = 1
        for d in shp:
            n *= d
        out.append(flat[..., off:off + n].reshape(flat.shape[:-1] + tuple(shp)))
        off += n
    return out


def _layer_groups(depth):
    groups = []
    for i in range(depth):
        kind, j = i % 3, i // 3
        mix = ((("att_w_qkv", j), ("att_w_o", j)), (("pool_w", j),), (("gdn_w_in", j), ("gdn_w_o", j)))[kind]
        groups.append(mix + (("ffn_w_up", i), ("ffn_w_down", i)))
    return groups


def kernel(x, mix_norm, ffn_norm, att_w_qkv, att_q_gain, att_k_gain, att_rel_bias, att_w_o, pool_w, pool_scale, gdn_w_in, gdn_conv, gdn_a_log, gdn_dt_bias, gdn_o_gain, gdn_w_o, ffn_w_up, ffn_conv, ffn_w_down, loss_target, m_mix_norm, m_ffn_norm, m_att_w_qkv, m_att_q_gain, m_att_k_gain, m_att_rel_bias, m_att_w_o, m_pool_w, m_pool_scale, m_gdn_w_in, m_gdn_conv, m_gdn_a_log, m_gdn_dt_bias, m_gdn_o_gain, m_gdn_w_o, m_ffn_w_up, m_ffn_conv, m_ffn_w_down, v_mix_norm, v_ffn_norm, v_att_w_qkv, v_att_q_gain, v_att_k_gain, v_att_rel_bias, v_att_w_o, v_pool_w, v_pool_scale, v_gdn_w_in, v_gdn_conv, v_gdn_a_log, v_gdn_dt_bias, v_gdn_o_gain, v_gdn_w_o, v_ffn_w_up, v_ffn_conv, v_ffn_w_down):
    w = dict(mix_norm=mix_norm, ffn_norm=ffn_norm, att_w_qkv=att_w_qkv, att_q_gain=att_q_gain, att_k_gain=att_k_gain, att_rel_bias=att_rel_bias, att_w_o=att_w_o, pool_w=pool_w, pool_scale=pool_scale, gdn_w_in=gdn_w_in, gdn_conv=gdn_conv, gdn_a_log=gdn_a_log, gdn_dt_bias=gdn_dt_bias, gdn_o_gain=gdn_o_gain, gdn_w_o=gdn_w_o, ffn_w_up=ffn_w_up, ffn_conv=ffn_conv, ffn_w_down=ffn_w_down)
    m = dict(mix_norm=m_mix_norm, ffn_norm=m_ffn_norm, att_w_qkv=m_att_w_qkv, att_q_gain=m_att_q_gain, att_k_gain=m_att_k_gain, att_rel_bias=m_att_rel_bias, att_w_o=m_att_w_o, pool_w=m_pool_w, pool_scale=m_pool_scale, gdn_w_in=m_gdn_w_in, gdn_conv=m_gdn_conv, gdn_a_log=m_gdn_a_log, gdn_dt_bias=m_gdn_dt_bias, gdn_o_gain=m_gdn_o_gain, gdn_w_o=m_gdn_w_o, ffn_w_up=m_ffn_w_up, ffn_conv=m_ffn_conv, ffn_w_down=m_ffn_w_down)
    v = dict(mix_norm=v_mix_norm, ffn_norm=v_ffn_norm, att_w_qkv=v_att_w_qkv, att_q_gain=v_att_q_gain, att_k_gain=v_att_k_gain, att_rel_bias=v_att_rel_bias, att_w_o=v_att_w_o, pool_w=v_pool_w, pool_scale=v_pool_scale, gdn_w_in=v_gdn_w_in, gdn_conv=v_gdn_conv, gdn_a_log=v_gdn_a_log, gdn_dt_bias=v_gdn_dt_bias, gdn_o_gain=v_gdn_o_gain, gdn_w_o=v_gdn_w_o, ffn_w_up=v_ffn_w_up, ffn_conv=v_ffn_conv, ffn_w_down=v_ffn_w_down)
    depth = ffn_w_up.shape[0]
    my_c = lax.axis_index("c").astype(jnp.int32)
    my_chip = (2 * lax.axis_index("x") + lax.axis_index("y")).astype(jnp.int32)
    groups = _layer_groups(depth)

    core = my_c.reshape(1)
    place = jnp.stack([my_chip, my_c])

    full = {n: w[n] for n in REPLICATED}
    big = {}
    for i, group in enumerate(groups):
        slots = [prep_slot(to_comm(n, w[n][j]), my_chip, f"prep_{n}_{j}") for n, j in group]
        for (n, j), t in zip(group, gather_slots(slots, f"gather_l{i}")):
            big[n, j] = t
    small = [w[n] for n in SMALL_SHARDED]
    got = gather_chips(_pack(small, F32, LANES, PACK_ROWS), "gather_small").reshape(N_CHIPS, -1)
    for n, t in zip(SMALL_SHARDED, _unpack(got, [s.shape for s in small])):
        full[n] = _merge(t, SHARD_AXIS[n])

    W = build_weights(big, full)
    loss, grad_x, G = local_step(x[0], loss_target[0], W)
    loss = lax.psum(loss[0, 0], ("x", "y", "c"))
    gfull = small_grads(G, full)
    gbig = big_grads(G, gdn_a_log.shape[1])

    grads = {}
    gparts = {n: [None] * w[n].shape[0] for n in BIG}
    for i, group in enumerate(groups):
        gs = [gbig[n, j] for n, j in group]
        theirs = swap_cols(gs, f"rs_swap_l{i}")
        chip_sums = [add_cols2(g, a, core, f"rs_add2_{n}_{j}") for (n, j), g, a in zip(group, gs, theirs)]
        got = scatter_cols(chip_sums, f"rs_scatter_l{i}")
        mine = [add_cols4(p, b, place, f"rs_add4_{n}_{j}") for (n, j), p, b in zip(group, chip_sums, got)]
        for (n, j), r in zip(group, join_cols(mine, f"rs_join_l{i}")):
            gparts[n][j] = from_comm(n, r, w[n][j].shape)
    for n in BIG:
        grads[n] = jnp.stack(gparts[n])

    small_names = REPLICATED + SMALL_SHARDED
    packed = _pack([gfull[n] for n in small_names], F32, LANES, SUBLANES)
    summed = sum_devices(packed, "sum_small").reshape(-1)
    for n, t in zip(small_names, _unpack(summed, [gfull[n].shape for n in small_names])):
        if n in SHARD_AXIS:
            size = w[n].shape[SHARD_AXIS[n]]
            t = lax.dynamic_slice_in_dim(t, my_chip * size, size, axis=SHARD_AXIS[n])
        grads[n] = t

    delta, new_m, new_v = {}, {}, {}
    for n in WEIGHTS:
        delta[n], new_m[n], new_v[n] = adamw(w[n], grads[n], m[n], v[n], f"adamw_{n}")
    return (loss, grad_x[None], *[grads[n] for n in WEIGHTS], *[delta[n] for n in WEIGHTS],
            *[new_m[n] for n in WEIGHTS], *[new_v[n] for n in WEIGHTS])
```

```python
import functools

import jax
import jax.numpy as jnp
from jax import lax
from jax.experimental import pallas as pl
from jax.experimental.pallas import tpu as pltpu

F32 = jnp.float32
BF16 = jnp.bfloat16
MXU_DTYPE = BF16
HI = lax.Precision.HIGHEST
S = jax.ShapeDtypeStruct
BS = pl.BlockSpec

EPS = 1e-6
MASK_VALUE = -1e30
CHUNK = 64
HEAD = 128
LEFT_CHUNKS = 8
BAND_LEFT = LEFT_CHUNKS * CHUNK
BAND = BAND_LEFT + CHUNK
MAX_REL = 256
ATT_CHUNKS = 4
ATT_QB = ATT_CHUNKS * CHUNK
ATT_BAND = BAND_LEFT + ATT_QB
POOL_WINDOWS = (2, 4, 8, 16)
POOL_HALO = 16
GDN_CONV = 4
DELTA_HEADS = 8
FFN_CONV = 3
SUBLANES = 8
LANES = 128
FF_ALIGN = 512
N_CHIPS = 4
ADAM_LR, ADAM_B1, ADAM_B2, ADAM_EPS, ADAM_WD, ADAM_STEP = 0.001, 0.9, 0.999, 1e-08, 0.01, 10
MIB = 1024 * 1024
MESH = pl.DeviceIdType.MESH


def _cp(sems, vmem_mib=40):
    return pltpu.CompilerParams(dimension_semantics=sems, vmem_limit_bytes=vmem_mib * MIB)


def _pick(n, cands):
    for c in cands:
        if n % c == 0:
            return c
    return n


def _mx(x):
    return x.astype(MXU_DTYPE)


def _hi_lo(x):
    hi = x.astype(MXU_DTYPE)
    return hi, (x - hi.astype(F32)).astype(MXU_DTYPE)


def _dot(a, b, dims, hi=False):
    dn = (dims, ((), ()))
    if hi is True or (hi == "split" and MXU_DTYPE == F32):
        return lax.dot_general(a.astype(F32), b.astype(F32), dn, precision=HI, preferred_element_type=F32)
    if hi == "split":
        ah, al = _hi_lo(a)
        bh, bl = _hi_lo(b)
        return (lax.dot_general(ah, bh, dn, preferred_element_type=F32)
                + (lax.dot_general(ah, bl, dn, preferred_element_type=F32)
                   + lax.dot_general(al, bh, dn, preferred_element_type=F32)))
    return lax.dot_general(_mx(a), _mx(b), dn, preferred_element_type=F32)


def dot_nn(a, b, hi=False):
    return _dot(a, b, ((1,), (0,)), hi)


def dot_nt(a, b, hi=False):
    return _dot(a, b, ((1,), (1,)), hi)


def dot_tn(a, b, hi=False):
    return _dot(a, b, ((0,), (0,)), hi)


def _sigmoid(x):
    return 1.0 / (1.0 + jnp.exp(-x))


def matmul(a, b, mode, *, out_dtype=F32, res=None, job=None, name):
    if mode == "nn":
        (M, K), N = a.shape, b.shape[1]
    elif mode == "nt":
        (M, K), N = a.shape, b.shape[0]
    else:
        (K, M), N = a.shape, b.shape[1]
    bm = _pick(M, (1024, 512, 256, 128))
    bn = _pick(N, (1024, 512, 256, 128))
    bk = _pick(K, (2816, 2048, 1408, 1024, 512, 256, 128))
    nk = K // bk
    if mode == "nn":
        a_spec = BS((bm, bk), lambda i, j, k: (i, k))
        b_spec = BS((bk, bn), lambda i, j, k: (k, j))
        dot = dot_nn
    elif mode == "nt":
        a_spec = BS((bm, bk), lambda i, j, k: (i, k))
        b_spec = BS((bn, bk), lambda i, j, k: (j, k))
        dot = dot_nt
    else:
        a_spec = BS((bk, bm), lambda i, j, k: (k, i))
        b_spec = BS((bk, bn), lambda i, j, k: (k, j))
        dot = dot_tn
    o_spec = BS((bm, bn), lambda i, j, k: (i, j))
    has_res = res is not None
    n_in = 3 if has_res else 2
    nji = len(job["ins"]) if job else 0
    njo = len(job["out_shapes"]) if job else 0
    grid = (M // bm, N // bn, nk)

    def body(*refs):
        a_ref, b_ref = refs[:2]
        r_ref = refs[2] if has_res else None
        o_ref = refs[n_in + nji]
        acc = refs[n_in + nji + 1 + njo]
        k = pl.program_id(2)
        if job:
            j_ins, j_outs = refs[n_in:n_in + nji], refs[n_in + nji + 1:n_in + nji + 1 + njo]
            sems = refs[n_in + nji + 2 + njo:]
            at_step = lambda s: jnp.logical_and(jnp.logical_and(pl.program_id(0) == s[0], pl.program_id(1) == s[1]),
                                                k == s[2])
            pl.when(at_step((0, 0, 0)))(lambda: job["start"](j_ins, j_outs, *sems))
        p = dot(a_ref[...], b_ref[...])

        def finish(total):
            if has_res:
                total = r_ref[...] + total
            o_ref[...] = total.astype(o_ref.dtype)

        if nk == 1:
            finish(p)
        else:
            @pl.when(k == 0)
            def _():
                acc[...] = p

            @pl.when(jnp.logical_and(k > 0, k < nk - 1))
            def _():
                acc[...] += p

            @pl.when(k == nk - 1)
            def _():
                finish(acc[...] + p)

        if job:
            pl.when(at_step((grid[0] - 1, grid[1] - 1, nk - 1)))(lambda: job["finish"](j_ins, j_outs, *sems))

    in_specs = [a_spec, b_spec] + ([o_spec] if has_res else [])
    args = (a, b) + ((res,) if has_res else ())
    if not job:
        return pl.pallas_call(
            body, grid=grid, in_specs=in_specs, out_specs=o_spec,
            out_shape=S((M, N), out_dtype), scratch_shapes=[pltpu.VMEM((bm, bn), F32)],
            compiler_params=_cp(("parallel", "parallel", "arbitrary"), 48), name=name)(*args)
    any_spec = BS(memory_space=pl.ANY)
    outs = pl.pallas_call(
        body, grid=grid, in_specs=in_specs + [any_spec] * nji, out_specs=[o_spec] + [any_spec] * njo,
        out_shape=[S((M, N), out_dtype)] + list(job["out_shapes"]),
        input_output_aliases={n_in + src: 1 + dst for src, dst in job["aliases"]},
        scratch_shapes=[pltpu.VMEM((bm, bn), F32), pltpu.SemaphoreType.DMA((job["nsem"],)),
                        pltpu.SemaphoreType.DMA((job["nsem"],))],
        compiler_params=_cp(("arbitrary", "arbitrary", "arbitrary"), 48), name=name)(*args, *job["ins"])
    return outs[0], list(outs[1:])


def norm_fwd(x, gain, out_dtypes, name):
    T, D = x.shape
    bt = _pick(T, (256, 128, 64))

    def body(x_ref, g_ref, *o_refs):
        xv = x_ref[...]
        r = lax.rsqrt(jnp.mean(xv * xv, axis=-1, keepdims=True) + EPS)
        y = (xv * r) * g_ref[...]
        for o in o_refs:
            o[...] = y.astype(o.dtype)

    row = BS((bt, D), lambda i: (i, 0))
    return pl.pallas_call(
        body, grid=(T // bt,), in_specs=[row, BS((1, D), lambda i: (0, 0))],
        out_specs=[row] * len(out_dtypes), out_shape=[S((T, D), dt) for dt in out_dtypes],
        compiler_params=_cp(("parallel",)), name=name)(x, gain)


def norm_bwd(x, gain, dy, dres, name):
    T, D = x.shape
    bt = _pick(T, (256, 128, 64))

    def body(x_ref, g_ref, dy_ref, dres_ref, dx_ref, dg_ref):
        i = pl.program_id(0)
        xv = x_ref[...]
        dyv = dy_ref[...].astype(F32)
        r = lax.rsqrt(jnp.mean(xv * xv, axis=-1, keepdims=True) + EPS)
        xhat = xv * r
        dxhat = dyv * g_ref[...]
        dx = r * (dxhat - xhat * jnp.mean(dxhat * xhat, axis=-1, keepdims=True))
        dx_ref[...] = dres_ref[...] + dx

        @pl.when(i == 0)
        def _():
            dg_ref[...] = jnp.zeros_like(dg_ref)

        dg_ref[...] += jnp.sum(dyv * xhat, axis=0, keepdims=True)

    row = BS((bt, D), lambda i: (i, 0))
    vec = BS((1, D), lambda i: (0, 0))
    return pl.pallas_call(
        body, grid=(T // bt,), in_specs=[row, vec, row, row], out_specs=[row, vec],
        out_shape=[S((T, D), F32), S((1, D), F32)],
        compiler_params=_cp(("arbitrary",)), name=name)(x, gain, dy, dres)


def loss_and_grad(y, target, name):
    T, D = y.shape
    bt = _pick(T, (256, 128, 64))
    nt = T // bt

    def body(y_ref, t_ref, l_ref, dy_ref, acc):
        i = pl.program_id(0)
        e = y_ref[...] - t_ref[...]
        dy_ref[...] = e * (1.0 / D)

        @pl.when(i == 0)
        def _():
            acc[...] = jnp.zeros_like(acc)

        acc[...] += jnp.sum(e * e, axis=0, keepdims=True)

        @pl.when(i == nt - 1)
        def _():
            l_ref[...] = jnp.sum(acc[...], axis=1, keepdims=True) * (0.5 / D)

    row = BS((bt, D), lambda i: (i, 0))
    return pl.pallas_call(
        body, grid=(nt,), in_specs=[row, row], out_specs=[BS((1, 1), lambda i: (0, 0)), row],
        out_shape=[S((1, 1), F32), S((T, D), F32)], scratch_shapes=[pltpu.VMEM((1, D), F32)],
        compiler_params=_cp(("arbitrary",)), name=name)(y, target)


def _prev_halo(bt):
    return lambda i: (jnp.maximum(i * (bt // SUBLANES) - 1, 0), 0)


def _ffn_u(i, a_ref, halo_ref, w_ref, ext, bt):
    ext[pl.ds(0, SUBLANES), :] = jnp.where(i > 0, halo_ref[...], 0.0)
    ext[pl.ds(SUBLANES, bt), :] = a_ref[...]
    u = w_ref[2:3, :] * ext[pl.ds(SUBLANES, bt), :]
    u += w_ref[1:2, :] * ext[pl.ds(SUBLANES - 1, bt), :]
    u += w_ref[0:1, :] * ext[pl.ds(SUBLANES - 2, bt), :]
    return u


def ffn_act_fwd(up, conv_w, name):
    T, F2 = up.shape
    Fp = F2 // 2
    bt = _pick(T, (128, 64))

    def body(a_ref, b_ref, halo_ref, w_ref, o_ref, ext):
        u = _ffn_u(pl.program_id(0), a_ref, halo_ref, w_ref, ext, bt)
        o_ref[...] = (u * _sigmoid(u) * b_ref[...]).astype(o_ref.dtype)

    return pl.pallas_call(
        body, grid=(T // bt,),
        in_specs=[BS((bt, Fp), lambda i: (i, 0)), BS((bt, Fp), lambda i: (i, 1)),
                  BS((SUBLANES, Fp), _prev_halo(bt)), BS((SUBLANES, Fp), lambda i: (0, 0))],
        out_specs=BS((bt, Fp), lambda i: (i, 0)), out_shape=S((T, Fp), MXU_DTYPE),
        scratch_shapes=[pltpu.VMEM((bt + SUBLANES, Fp), F32)],
        compiler_params=_cp(("arbitrary",)), name=name)(up, up, up, conv_w)


def ffn_act_bwd_a(up, conv_w, g_act, name):
    T, F2 = up.shape
    Fp = F2 // 2
    bt = _pick(T, (128, 64))

    def body(a_ref, b_ref, halo_ref, w_ref, g_ref, du_ref, db_ref, dw_ref, ext):
        i = pl.program_id(0)
        u = _ffn_u(i, a_ref, halo_ref, w_ref, ext, bt)
        sg = _sigmoid(u)
        g = g_ref[...]
        db_ref[...] = (g * (u * sg)).astype(db_ref.dtype)
        du = g * b_ref[...] * (sg * (1.0 + u * (1.0 - sg)))
        du_ref[...] = du

        @pl.when(i == 0)
        def _():
            dw_ref[...] = jnp.zeros_like(dw_ref)

        for j in range(FFN_CONV):
            shifted = ext[pl.ds(SUBLANES - (FFN_CONV - 1) + j, bt), :]
            dw_ref[j:j + 1, :] += jnp.sum(du * shifted, axis=0, keepdims=True)

    blk = BS((bt, Fp), lambda i: (i, 0))
    full = BS((SUBLANES, Fp), lambda i: (0, 0))
    return pl.pallas_call(
        body, grid=(T // bt,),
        in_specs=[blk, BS((bt, Fp), lambda i: (i, 1)), BS((SUBLANES, Fp), _prev_halo(bt)), full, blk],
        out_specs=[blk, blk, full],
        out_shape=[S((T, Fp), F32), S((T, Fp), MXU_DTYPE), S((SUBLANES, Fp), F32)],
        scratch_shapes=[pltpu.VMEM((bt + SUBLANES, Fp), F32)],
        compiler_params=_cp(("arbitrary",)), name=name)(up, up, up, conv_w, g_act)


def ffn_act_bwd_b(du, db, conv_w, name):
    T, Fp = du.shape
    bt = _pick(T, (128, 64))
    nt = T // bt

    def body(du_ref, halo_ref, db_ref, w_ref, o_ref, ext):
        i = pl.program_id(0)
        ext[pl.ds(0, bt), :] = du_ref[...]
        ext[pl.ds(bt, SUBLANES), :] = jnp.where(i < nt - 1, halo_ref[...], 0.0)
        da = w_ref[2:3, :] * ext[pl.ds(0, bt), :]
        da += w_ref[1:2, :] * ext[pl.ds(1, bt), :]
        da += w_ref[0:1, :] * ext[pl.ds(2, bt), :]
        o_ref[:, pl.ds(0, Fp)] = da.astype(o_ref.dtype)
        o_ref[:, pl.ds(Fp, Fp)] = db_ref[...]

    blk = BS((bt, Fp), lambda i: (i, 0))
    nxt = BS((SUBLANES, Fp), lambda i: (jnp.minimum((i + 1) * (bt // SUBLANES), T // SUBLANES - 1), 0))
    return pl.pallas_call(
        body, grid=(nt,), in_specs=[blk, nxt, blk, BS((SUBLANES, Fp), lambda i: (0, 0))],
        out_specs=BS((bt, 2 * Fp), lambda i: (i, 0)), out_shape=S((T, 2 * Fp), MXU_DTYPE),
        scratch_shapes=[pltpu.VMEM((bt + SUBLANES, Fp), F32)],
        compiler_params=_cp(("arbitrary",)), name=name)(du, du, db, conv_w)


def _attn_fill(k_ref, v_ref, gk, kn_scr, vb_scr, T):
    kn_scr[pl.ds(0, BAND_LEFT), :] = jnp.zeros((BAND_LEFT, HEAD), kn_scr.dtype)
    vb_scr[pl.ds(0, BAND_LEFT), :] = jnp.zeros((BAND_LEFT, HEAD), vb_scr.dtype)
    rb = 512

    def fill(r, carry):
        rows = pl.ds(pl.multiple_of(r * rb, rb), rb)
        dst = pl.ds(pl.multiple_of(BAND_LEFT + r * rb, rb), rb)
        k = k_ref[rows, :]
        rk = lax.rsqrt(jnp.mean(k * k, axis=-1, keepdims=True) + EPS)
        kn_scr[dst, :] = ((k * rk) * gk).astype(kn_scr.dtype)
        vb_scr[dst, :] = v_ref[rows, :].astype(vb_scr.dtype)
        return carry

    lax.fori_loop(0, T // rb, fill, 0)


def _attn_probs(c, q_ref, gq, bias_ref, kn_scr):
    q = q_ref[...]
    rq = lax.rsqrt(jnp.mean(q * q, axis=-1, keepdims=True) + EPS)
    qn = (q * rq) * gq
    band = pl.ds(pl.multiple_of(c * ATT_QB, ATT_QB), ATT_BAND)
    kb = kn_scr[band, :]
    s = dot_nt(qn, kb) * (HEAD ** -0.5) + bias_ref[0]
    pos = c * ATT_QB - BAND_LEFT + lax.broadcasted_iota(jnp.int32, (ATT_QB, ATT_BAND), 1)
    s = jnp.where(pos >= 0, s, MASK_VALUE)
    m = jnp.max(s, axis=-1, keepdims=True)
    e = jnp.exp(s - m)
    p = e / jnp.sum(e, axis=-1, keepdims=True)
    return q, rq, qn, kb, p


def attn_fwd(qkv, gq, gk, bias, name):
    T, D3 = qkv.shape
    D = D3 // 3
    H = D // HEAD
    NC = T // ATT_QB

    def body(q_ref, k_ref, v_ref, gq_ref, gk_ref, bias_ref, o_ref, kn_scr, vb_scr):
        c = pl.program_id(1)

        @pl.when(c == 0)
        def _():
            _attn_fill(k_ref, v_ref, gk_ref[...], kn_scr, vb_scr, T)

        _, _, _, _, p = _attn_probs(c, q_ref, gq_ref[...], bias_ref, kn_scr)
        band = pl.ds(pl.multiple_of(c * ATT_QB, ATT_QB), ATT_BAND)
        o_ref[...] = dot_nn(p, vb_scr[band, :]).astype(o_ref.dtype)

    vec = BS((1, HEAD), lambda h, c: (0, 0))
    return pl.pallas_call(
        body, grid=(H, NC),
        in_specs=[BS((ATT_QB, HEAD), lambda h, c: (c, h)), BS((T, HEAD), lambda h, c: (0, H + h)),
                  BS((T, HEAD), lambda h, c: (0, 2 * H + h)), vec, vec,
                  BS((1, ATT_QB, ATT_BAND), lambda h, c: (h, 0, 0))],
        out_specs=BS((ATT_QB, HEAD), lambda h, c: (c, h)), out_shape=S((T, D), MXU_DTYPE),
        scratch_shapes=[pltpu.VMEM((T + BAND_LEFT, HEAD), MXU_DTYPE)] * 2,
        compiler_params=_cp(("arbitrary", "arbitrary"), 48), name=name)(qkv, qkv, qkv, gq, gk, bias)


def attn_bwd(qkv, do, gq, gk, bias, name):
    T, D3 = qkv.shape
    D = D3 // 3
    H = D // HEAD
    NC = T // ATT_QB
    scale = HEAD ** -0.5

    def body(q_ref, k_ref, v_ref, do_ref, gq_ref, gk_ref, bias_ref,
             dq_ref, dk_ref, dv_ref, dgq_ref, dgk_ref, dbias_ref, kn_scr, vb_scr, dkn_acc, dv_acc):
        h = pl.program_id(0)
        c = pl.program_id(1)
        gq = gq_ref[...]
        gk = gk_ref[...]

        @pl.when(c == 0)
        def _():
            _attn_fill(k_ref, v_ref, gk, kn_scr, vb_scr, T)
            dkn_acc[...] = jnp.zeros_like(dkn_acc)
            dv_acc[...] = jnp.zeros_like(dv_acc)
            dbias_ref[...] = jnp.zeros_like(dbias_ref)

        @pl.when(jnp.logical_and(c == 0, h == 0))
        def _():
            dgq_ref[...] = jnp.zeros_like(dgq_ref)
            dgk_ref[...] = jnp.zeros_like(dgk_ref)

        q, rq, qn, kb, p = _attn_probs(c, q_ref, gq, bias_ref, kn_scr)
        band = pl.ds(pl.multiple_of(c * ATT_QB, ATT_QB), ATT_BAND)
        dov = do_ref[...]
        dv_acc[band, :] += dot_tn(p, dov)
        dp = dot_nt(dov, vb_scr[band, :])
        ds = p * (dp - jnp.sum(dp * p, axis=-1, keepdims=True))
        dbias_ref[0] += ds
        dss = ds * scale
        dqn = dot_nn(dss, kb)
        dkn_acc[band, :] += dot_tn(dss, qn)
        xhat = q * rq
        dgq_ref[...] += jnp.sum(dqn * xhat, axis=0, keepdims=True)
        dxhat = dqn * gq
        dq = rq * (dxhat - xhat * jnp.mean(dxhat * xhat, axis=-1, keepdims=True))
        dq_ref[...] = dq.astype(dq_ref.dtype)

        @pl.when(c == NC - 1)
        def _():
            rb = 512

            def fin(r, carry):
                rows = pl.ds(pl.multiple_of(r * rb, rb), rb)
                src = pl.ds(pl.multiple_of(BAND_LEFT + r * rb, rb), rb)
                k = k_ref[rows, :]
                rk = lax.rsqrt(jnp.mean(k * k, axis=-1, keepdims=True) + EPS)
                khat = k * rk
                dkn = dkn_acc[src, :]
                dgk_ref[...] += jnp.sum(dkn * khat, axis=0, keepdims=True)
                dkh = dkn * gk
                dk = rk * (dkh - khat * jnp.mean(dkh * khat, axis=-1, keepdims=True))
                dk_ref[rows, :] = dk.astype(dk_ref.dtype)
                dv_ref[rows, :] = dv_acc[src, :].astype(dv_ref.dtype)
                return carry

            lax.fori_loop(0, T // rb, fin, 0)

    vec = BS((1, HEAD), lambda h, c: (0, 0))
    qblk = BS((ATT_QB, HEAD), lambda h, c: (c, h))
    col = BS((T, HEAD), lambda h, c: (0, h))
    bblk = BS((1, ATT_QB, ATT_BAND), lambda h, c: (h, 0, 0))
    return pl.pallas_call(
        body, grid=(H, NC),
        in_specs=[qblk, BS((T, HEAD), lambda h, c: (0, H + h)), BS((T, HEAD), lambda h, c: (0, 2 * H + h)),
                  qblk, vec, vec, bblk],
        out_specs=[qblk, col, col, vec, vec, bblk],
        out_shape=[S((T, D), MXU_DTYPE)] * 3 + [S((1, HEAD), F32)] * 2 + [S((H, ATT_QB, ATT_BAND), F32)],
        scratch_shapes=[pltpu.VMEM((T + BAND_LEFT, HEAD), MXU_DTYPE)] * 2
        + [pltpu.VMEM((T + BAND_LEFT, HEAD), F32)] * 2,
        compiler_params=_cp(("arbitrary", "arbitrary"), 56), name=name)(qkv, qkv, qkv, do, gq, gk, bias)


def _rel_onehot(qi, num_rel):
    kk = lax.broadcasted_iota(jnp.int32, (BAND, num_rel), 0)
    rr = lax.broadcasted_iota(jnp.int32, (BAND, num_rel), 1)
    idx = jnp.clip(BAND_LEFT + qi - kk, -(CHUNK - 1), MAX_REL) + (CHUNK - 1)
    return (idx == rr).astype(F32)


def rel_bias_expand(table, name):
    H, num_rel = table.shape

    def body(t_ref, o_ref):
        for qi in range(CHUNK):
            o_ref[qi] = dot_nt(t_ref[...], _rel_onehot(qi, num_rel), hi=True)

    return pl.pallas_call(body, out_shape=S((CHUNK, H, BAND), F32), name=name,
                          compiler_params=pltpu.CompilerParams(vmem_limit_bytes=40 * MIB))(table)


def rel_bias_reduce(dbias_t, num_rel, name):
    H = dbias_t.shape[1]

    def body(d_ref, o_ref):
        acc = jnp.zeros((H, num_rel), F32)
        for qi in range(CHUNK):
            acc += dot_nn(d_ref[qi], _rel_onehot(qi, num_rel), hi=True)
        o_ref[...] = acc

    return pl.pallas_call(body, out_shape=S((H, num_rel), F32), name=name,
                          compiler_params=pltpu.CompilerParams(vmem_limit_bytes=40 * MIB))(dbias_t)


def pool_fwd(h, x, w, scale, name):
    T, D = h.shape
    G = len(POOL_WINDOWS)
    Dg = D // G
    bt = _pick(T, (256, 128, 64))

    def body(h_ref, halo_ref, x_ref, w_ref, s_ref, o_ref, p_ref, ext):
        i = pl.program_id(0)
        ext[pl.ds(0, POOL_HALO), :] = jnp.where(i > 0, halo_ref[...], 0.0)
        ext[pl.ds(POOL_HALO, bt), :] = h_ref[...]
        t = i * bt + lax.broadcasted_iota(jnp.int32, (bt, 1), 0)
        for g, win in enumerate(POOL_WINDOWS):
            cols = pl.ds(g * Dg, Dg)
            acc = ext[pl.ds(POOL_HALO, bt), cols]
            for j in range(1, win):
                acc += ext[pl.ds(POOL_HALO - j, bt), cols]
            count = jnp.minimum(t + 1, win).astype(F32)
            pooled = acc / count - h_ref[:, cols]
            p_ref[:, cols] = pooled.astype(p_ref.dtype)
            y = dot_nn(pooled, w_ref[g]) * s_ref[:, cols]
            o_ref[:, cols] = x_ref[:, cols] + y

    row = BS((bt, D), lambda i: (i, 0))
    return pl.pallas_call(
        body, grid=(T // bt,),
        in_specs=[row, BS((POOL_HALO, D), lambda i: (jnp.maximum(i * (bt // POOL_HALO) - 1, 0), 0)), row,
                  BS((G, Dg, Dg), lambda i: (0, 0, 0)), BS((1, D), lambda i: (0, 0))],
        out_specs=[row, row], out_shape=[S((T, D), F32), S((T, D), MXU_DTYPE)],
        scratch_shapes=[pltpu.VMEM((bt + POOL_HALO, D), F32)],
        compiler_params=_cp(("arbitrary",)), name=name)(h, h, x, w, scale)


def pool_bwd_a(dy, pooled, w, scale, name):
    T, D = dy.shape
    G = len(POOL_WINDOWS)
    Dg = D // G
    bt = _pick(T, (256, 128, 64))

    def body(dy_ref, p_ref, w_ref, s_ref, dp_ref, dw_ref, ds_ref):
        i = pl.program_id(0)

        @pl.when(i == 0)
        def _():
            dw_ref[...] = jnp.zeros_like(dw_ref)
            ds_ref[...] = jnp.zeros_like(ds_ref)

        for g in range(G):
            cols = pl.ds(g * Dg, Dg)
            pg = p_ref[:, cols]
            dyg = dy_ref[:, cols]
            ypre = dot_nn(pg, w_ref[g])
            ds_ref[:, cols] += jnp.sum(dyg * ypre, axis=0, keepdims=True)
            dys = dyg * s_ref[:, cols]
            dp_ref[:, cols] = dot_nt(dys, w_ref[g])
            dw_ref[g] += dot_tn(pg, dys)

    row = BS((bt, D), lambda i: (i, 0))
    wspec = BS((G, Dg, Dg), lambda i: (0, 0, 0))
    vec = BS((1, D), lambda i: (0, 0))
    return pl.pallas_call(
        body, grid=(T // bt,), in_specs=[row, row, wspec, vec], out_specs=[row, wspec, vec],
        out_shape=[S((T, D), F32), S((G, Dg, Dg), F32), S((1, D), F32)],
        compiler_params=_cp(("arbitrary",)), name=name)(dy, pooled, w, scale)


def pool_bwd_b(dpooled, name):
    T, D = dpooled.shape
    G = len(POOL_WINDOWS)
    Dg = D // G
    bt = _pick(T, (256, 128, 64))
    nt = T // bt

    def body(d_ref, halo_ref, o_ref, ext):
        i = pl.program_id(0)
        t = i * bt + lax.broadcasted_iota(jnp.int32, (bt, 1), 0)
        for g, win in enumerate(POOL_WINDOWS):
            cols = pl.ds(g * Dg, Dg)
            count = jnp.minimum(t + 1, win).astype(F32)
            ext[pl.ds(0, bt), cols] = d_ref[:, cols] / count
            ext[pl.ds(bt, POOL_HALO), cols] = jnp.where(i < nt - 1, halo_ref[:, cols] * (1.0 / win), 0.0)
            acc = ext[pl.ds(0, bt), cols]
            for j in range(1, win):
                acc += ext[pl.ds(j, bt), cols]
            o_ref[:, cols] = acc - d_ref[:, cols]

    row = BS((bt, D), lambda i: (i, 0))
    nxt = BS((POOL_HALO, D), lambda i: (jnp.minimum((i + 1) * (bt // POOL_HALO), T // POOL_HALO - 1), 0))
    return pl.pallas_call(
        body, grid=(nt,), in_specs=[row, nxt], out_specs=row, out_shape=S((T, D), F32),
        scratch_shapes=[pltpu.VMEM((bt + POOL_HALO, D), F32)],
        compiler_params=_cp(("arbitrary",)), name=name)(dpooled, dpooled)


def _gdn_u(i, x_ref, halo_ref, w_ref, ext, bt):
    ext[pl.ds(0, SUBLANES), :] = jnp.where(i > 0, halo_ref[...], 0.0)
    ext[pl.ds(SUBLANES, bt), :] = x_ref[...]
    u = w_ref[3:4, :] * ext[pl.ds(SUBLANES, bt), :]
    for j in range(GDN_CONV - 1):
        u += w_ref[j:j + 1, :] * ext[pl.ds(SUBLANES - (GDN_CONV - 1) + j, bt), :]
    return u


def gdn_pre_fwd(proj, conv_w, key_dim, name):
    T = proj.shape[0]
    C = conv_w.shape[1]
    cb = min(1024, key_dim)
    nq, nqk, J = key_dim // cb, 2 * key_dim // cb, C // cb
    bt = _pick(T, (256, 128, 64))

    def body(x_ref, halo_ref, w_ref, o_ref, ext):
        i, j = pl.program_id(0), pl.program_id(1)
        u = _gdn_u(i, x_ref, halo_ref, w_ref, ext, bt)
        s = u * _sigmoid(u)

        @pl.when(j < nqk)
        def _():
            sc = jnp.where(j < nq, HEAD ** -0.5, 1.0)
            for hh in range(cb // HEAD):
                cols = pl.ds(hh * HEAD, HEAD)
                blk = s[:, hh * HEAD:(hh + 1) * HEAD]
                r = lax.rsqrt(jnp.sum(blk * blk, axis=-1, keepdims=True) + EPS)
                o_ref[:, cols] = (blk * r) * sc

        @pl.when(j >= nqk)
        def _():
            o_ref[...] = s

    return pl.pallas_call(
        body, grid=(T // bt, J),
        in_specs=[BS((bt, cb), lambda i, j: (i, j)),
                  BS((SUBLANES, cb), lambda i, j: (jnp.maximum(i * (bt // SUBLANES) - 1, 0), j)),
                  BS((SUBLANES, cb), lambda i, j: (0, j))],
        out_specs=BS((bt, cb), lambda i, j: (i, j)), out_shape=S((T, C), F32),
        scratch_shapes=[pltpu.VMEM((bt + SUBLANES, cb), F32)],
        compiler_params=_cp(("arbitrary", "arbitrary")), name=name)(proj, proj, conv_w)


def gdn_pre_bwd_a(proj, conv_w, dq_v, dk_v, dv, key_dim, name):
    T = proj.shape[0]
    C = conv_w.shape[1]
    cb = min(1024, key_dim)
    nq, nqk, J = key_dim // cb, 2 * key_dim // cb, C // cb
    nv = J - nqk
    bt = _pick(T, (256, 128, 64))

    def body(x_ref, halo_ref, w_ref, dq_ref, dk_ref, dv_ref, du_ref, dw_ref, ext, ds_scr):
        j, i = pl.program_id(0), pl.program_id(1)
        u = _gdn_u(i, x_ref, halo_ref, w_ref, ext, bt)
        sg = _sigmoid(u)
        s = u * sg

        @pl.when(j < nqk)
        def _():
            sc = jnp.where(j < nq, HEAD ** -0.5, 1.0)
            for hh in range(cb // HEAD):
                lo = 2 * hh * HEAD
                dq2 = dq_ref[:, lo:lo + HEAD] + dq_ref[:, lo + HEAD:lo + 2 * HEAD]
                dk2 = dk_ref[:, lo:lo + HEAD] + dk_ref[:, lo + HEAD:lo + 2 * HEAD]
                dn = jnp.where(j < nq, dq2, dk2)
                blk = s[:, hh * HEAD:(hh + 1) * HEAD]
                r = lax.rsqrt(jnp.sum(blk * blk, axis=-1, keepdims=True) + EPS)
                shat = blk * r
                ds_scr[:, pl.ds(hh * HEAD, HEAD)] = (sc * r) * (dn - shat * jnp.sum(dn * shat, axis=-1, keepdims=True))

        @pl.when(j >= nqk)
        def _():
            ds_scr[...] = dv_ref[...]

        du = ds_scr[...] * (sg * (1.0 + u * (1.0 - sg)))
        du_ref[...] = du

        @pl.when(i == 0)
        def _():
            dw_ref[...] = jnp.zeros_like(dw_ref)

        for k in range(GDN_CONV):
            shifted = ext[pl.ds(SUBLANES - (GDN_CONV - 1) + k, bt), :]
            dw_ref[k:k + 1, :] += jnp.sum(du * shifted, axis=0, keepdims=True)

    blk = BS((bt, cb), lambda j, i: (i, j))
    return pl.pallas_call(
        body, grid=(J, T // bt),
        in_specs=[blk, BS((SUBLANES, cb), lambda j, i: (jnp.maximum(i * (bt // SUBLANES) - 1, 0), j)),
                  BS((SUBLANES, cb), lambda j, i: (0, j)),
                  BS((bt, 2 * cb), lambda j, i: (i, jnp.minimum(j, nq - 1))),
                  BS((bt, 2 * cb), lambda j, i: (i, jnp.clip(j - nq, 0, nq - 1))),
                  BS((bt, cb), lambda j, i: (i, jnp.clip(j - nqk, 0, nv - 1)))],
        out_specs=[blk, BS((SUBLANES, cb), lambda j, i: (0, j))],
        out_shape=[S((T, C), F32), S((SUBLANES, C), F32)],
        scratch_shapes=[pltpu.VMEM((bt + SUBLANES, cb), F32), pltpu.VMEM((bt, cb), F32)],
        compiler_params=_cp(("arbitrary", "arbitrary")), name=name)(proj, proj, conv_w, dq_v, dk_v, dv)


def gdn_pre_bwd_b(du, dgate, conv_w, key_dim, name):
    T, C = du.shape
    V = dgate.shape[1]
    cb = min(1024, key_dim)
    J = C // cb
    J2 = (C + V) // cb
    bt = _pick(T, (256, 128, 64))
    nt = T // bt

    def body(du_ref, halo_ref, w_ref, dg_ref, o_ref, ext):
        i, j = pl.program_id(0), pl.program_id(1)

        @pl.when(j < J)
        def _():
            ext[pl.ds(0, bt), :] = du_ref[...]
            ext[pl.ds(bt, SUBLANES), :] = jnp.where(i < nt - 1, halo_ref[...], 0.0)
            da = w_ref[3:4, :] * ext[pl.ds(0, bt), :]
            for k in range(GDN_CONV - 1):
                da += w_ref[k:k + 1, :] * ext[pl.ds(GDN_CONV - 1 - k, bt), :]
            o_ref[...] = da.astype(o_ref.dtype)

        @pl.when(j >= J)
        def _():
            o_ref[...] = dg_ref[...]

    jc = lambda j: jnp.minimum(j, J - 1)
    return pl.pallas_call(
        body, grid=(nt, J2),
        in_specs=[BS((bt, cb), lambda i, j: (i, jc(j))),
                  BS((SUBLANES, cb), lambda i, j: (jnp.minimum((i + 1) * (bt // SUBLANES), T // SUBLANES - 1), jc(j))),
                  BS((SUBLANES, cb), lambda i, j: (0, jc(j))),
                  BS((bt, cb), lambda i, j: (i, jnp.maximum(j - J, 0)))],
        out_specs=BS((bt, cb), lambda i, j: (i, j)), out_shape=S((T, C + V), MXU_DTYPE),
        scratch_shapes=[pltpu.VMEM((bt + SUBLANES, cb), F32)],
        compiler_params=_cp(("arbitrary", "arbitrary")), name=name)(du, du, conv_w, dgate)


def _softplus(x):
    return jnp.maximum(x, 0.0) + jnp.log1p(jnp.exp(-jnp.abs(x)))


def gdn_gate_fwd(a, b, a_log, dt_bias, name):
    T, HV = a.shape
    bt = _pick(T, (1024, 512, 256, 128, 64))

    def body(a_ref, b_ref, al_ref, dt_ref, g_ref, be_ref):
        g_ref[...] = -jnp.exp(al_ref[...]) * _softplus(a_ref[...] + dt_ref[...])
        be_ref[...] = _sigmoid(b_ref[...])

    row = BS((bt, HV), lambda i: (i, 0))
    vec = BS((1, HV), lambda i: (0, 0))
    return pl.pallas_call(body, grid=(T // bt,), in_specs=[row, row, vec, vec], out_specs=[row, row],
                          out_shape=[S((T, HV), F32)] * 2, compiler_params=_cp(("parallel",)), name=name)(
                              a, b, a_log, dt_bias)


def gdn_gate_bwd(a, b, a_log, dt_bias, dg, dbeta, name):
    T, HV = a.shape
    bt = _pick(T, (1024, 512, 256, 128, 64))

    def body(a_ref, b_ref, al_ref, dt_ref, dg_ref, dbe_ref, da_ref, db_ref, dal_ref, ddt_ref):
        i = pl.program_id(0)
        x = a_ref[...] + dt_ref[...]
        ea = jnp.exp(al_ref[...])
        dgv = dg_ref[...]
        da = dgv * (-ea * _sigmoid(x))
        da_ref[...] = da
        be = _sigmoid(b_ref[...])
        db_ref[...] = dbe_ref[...] * be * (1.0 - be)

        @pl.when(i == 0)
        def _():
            dal_ref[...] = jnp.zeros_like(dal_ref)
            ddt_ref[...] = jnp.zeros_like(ddt_ref)

        dal_ref[...] += jnp.sum(dgv * (-ea * _softplus(x)), axis=0, keepdims=True)
        ddt_ref[...] += jnp.sum(da, axis=0, keepdims=True)

    row = BS((bt, HV), lambda i: (i, 0))
    vec = BS((1, HV), lambda i: (0, 0))
    return pl.pallas_call(body, grid=(T // bt,), in_specs=[row, row, vec, vec, row, row],
                          out_specs=[row, row, vec, vec],
                          out_shape=[S((T, HV), F32)] * 2 + [S((1, HV), F32)] * 2,
                          compiler_params=_cp(("arbitrary",)), name=name)(a, b, a_log, dt_bias, dg, dbeta)


def _col(row_vec, eye):
    return jnp.sum(jnp.where(eye, row_vec, 0.0), axis=1, keepdims=True)


def _row(col_vec, eye):
    return jnp.sum(jnp.where(eye, col_vec, 0.0), axis=0, keepdims=True)


def _each(f, *lists):
    return [f(*args) for args in zip(*lists)]


def _mul(a, b):
    return a * b


def _hdot_nn(a, b):
    return dot_nn(a, b, hi="split")


def _delta_chunk(q, k, v, g_row, b_row, rep, tinv=None):
    C = CHUNK
    ii = lax.broadcasted_iota(jnp.int32, (C, C), 0)
    jj = lax.broadcasted_iota(jnp.int32, (C, C), 1)
    eye, causal, strict = ii == jj, ii >= jj, ii > jj
    q_v = [q[p // rep] for p in range(len(v))]
    k_v = [k[p // rep] for p in range(len(v))]
    g_col = _each(lambda g: _col(g, eye), g_row)
    gc_row = _each(lambda g: jnp.sum(jnp.where(ii <= jj, g, 0.0), axis=0, keepdims=True), g_col)
    gc_col = _each(lambda g: _col(g, eye), gc_row)
    gl = _each(lambda g: jnp.sum(jnp.where(jj[0:1, :] == C - 1, g, 0.0), axis=1, keepdims=True), gc_row)
    decay = _each(lambda gc, gr: jnp.where(causal, jnp.exp(jnp.where(causal, gc - gr, 0.0)), 0.0), gc_col, gc_row)
    b_col = _each(lambda b: _col(b, eye), b_row)
    kb = _each(_mul, k_v, b_col)
    vb = _each(_mul, v, b_col)
    m = _each(dot_nt, kb, k_v)
    a = _each(lambda m_, d_: jnp.where(strict, m_ * d_, 0.0), m, decay)
    if tinv is None:
        ident = jnp.where(eye, 1.0, 0.0)
        tinv = [ident - a_ for a_ in a]
        pw = _each(_hdot_nn, a, a)
        for step in range(5):
            tinv = _each(lambda t_, p_: t_ + _hdot_nn(t_, p_), tinv, pw)
            if step < 4:
                pw = _each(_hdot_nn, pw, pw)
    egc = _each(jnp.exp, gc_col)
    kbg = _each(_mul, kb, egc)
    u = _each(_hdot_nn, tinv, vb)
    w = _each(_hdot_nn, tinv, kbg)
    n_k = _each(dot_nt, q, k)
    n = [n_k[p // rep] for p in range(len(v))]
    attn = _each(lambda n_, d_: jnp.where(causal, n_ * d_, 0.0), n, decay)
    qg = _each(_mul, q_v, egc)
    ekl = _each(lambda l_, c_: jnp.exp(l_ - c_), gl, gc_col)
    ks = _each(_mul, k_v, ekl)
    dec = _each(jnp.exp, gl)
    return dict(eye=eye, causal=causal, strict=strict, ii=ii, jj=jj, q=q_v, k=k_v, gc_col=gc_col, gl=gl, decay=decay,
                b_col=b_col, kb=kb, vb=vb, m=m, tinv=tinv, egc=egc, kbg=kbg, u=u, w=w, n=n, attn=attn,
                qg=qg, ekl=ekl, ks=ks, dec=dec)


def delta_fwd(qkvn, g_rows, b_rows, key_dim, name):
    T = qkvn.shape[0]
    NK = key_dim // HEAD
    HV = g_rows.shape[0]
    rep = HV // NK
    NC = T // CHUNK

    P = DELTA_HEADS
    kw, vw = HEAD * P // rep, HEAD * P

    def body(q_ref, k_ref, v_ref, g_ref, b_ref, o_ref, st_ref, ti_ref, state):
        n = pl.program_id(1)

        @pl.when(n == 0)
        def _():
            state[...] = jnp.zeros_like(state)

        heads = range(P)
        q = [q_ref[:, pl.ds(kh * HEAD, HEAD)] for kh in range(P // rep)]
        k = [k_ref[:, pl.ds(kh * HEAD, HEAD)] for kh in range(P // rep)]
        v = [v_ref[:, pl.ds(p * HEAD, HEAD)] for p in heads]
        s0 = [state[p] for p in heads]
        c = _delta_chunk(q, k, v, [g_ref[p, 0] for p in heads], [b_ref[p, 0] for p in heads], rep)
        vn = _each(lambda u_, w_, s_: u_ - dot_nn(w_, s_), c["u"], c["w"], s0)
        o = _each(lambda qg_, s_, at_, vn_: dot_nn(qg_, s_) + dot_nn(at_, vn_), c["qg"], s0, c["attn"], vn)
        s1 = _each(lambda s_, d_, ks_, vn_: s_ * d_ + dot_tn(ks_, vn_), s0, c["dec"], c["ks"], vn)
        for p in heads:
            st_ref[p, 0] = s0[p]
            ti_ref[p, 0] = c["tinv"][p]
            o_ref[:, pl.ds(p * HEAD, HEAD)] = o[p]
            state[p] = s1[p]

    vrow = BS((P, 1, 1, CHUNK), lambda h, n: (h, n, 0, 0))
    return pl.pallas_call(
        body, grid=(HV // P, NC),
        in_specs=[BS((CHUNK, kw), lambda h, n: (n, h)),
                  BS((CHUNK, kw), lambda h, n: (n, key_dim // kw + h)),
                  BS((CHUNK, vw), lambda h, n: (n, 2 * key_dim // vw + h)), vrow, vrow],
        out_specs=[BS((CHUNK, vw), lambda h, n: (n, h)), BS((P, 1, HEAD, HEAD), lambda h, n: (h, n, 0, 0)),
                   BS((P, 1, CHUNK, CHUNK), lambda h, n: (h, n, 0, 0))],
        out_shape=[S((T, HV * HEAD), F32), S((HV, NC, HEAD, HEAD), F32), S((HV, NC, CHUNK, CHUNK), F32)],
        scratch_shapes=[pltpu.VMEM((P, HEAD, HEAD), F32)],
        compiler_params=_cp(("arbitrary", "arbitrary")), name=name)(qkvn, qkvn, qkvn, g_rows, b_rows)


def delta_bwd(qkvn, g_rows, b_rows, states, tinvs, do, key_dim, name):
    T = qkvn.shape[0]
    NK = key_dim // HEAD
    HV = g_rows.shape[0]
    rep = HV // NK
    NC = T // CHUNK
    P = DELTA_HEADS
    kw, vw = HEAD * P // rep, HEAD * P

    def body(q_ref, k_ref, v_ref, g_ref, b_ref, st_ref, ti_ref, do_ref, dq_ref, dk_ref, dv_ref, dg_ref, dbe_ref,
             dstate):
        step = pl.program_id(1)

        @pl.when(step == 0)
        def _():
            dstate[...] = jnp.zeros_like(dstate)

        heads = range(P)
        q_k = [q_ref[:, pl.ds(kh * HEAD, HEAD)] for kh in range(P // rep)]
        k_k = [k_ref[:, pl.ds(kh * HEAD, HEAD)] for kh in range(P // rep)]
        v = [v_ref[:, pl.ds(p * HEAD, HEAD)] for p in heads]
        s0 = [st_ref[p, 0] for p in heads]
        dsn = [dstate[p] for p in heads]
        dov = [do_ref[:, pl.ds(p * HEAD, HEAD)] for p in heads]
        c = _delta_chunk(q_k, k_k, v, [g_ref[p, 0] for p in heads], [b_ref[p, 0] for p in heads], rep,
                         tinv=[ti_ref[p, 0] for p in heads])
        eye, causal, strict = c["eye"], c["causal"], c["strict"]
        q, k, tinv, decay = c["q"], c["k"], c["tinv"], c["decay"]

        def rsum(a, b):
            return jnp.sum(a * b, axis=1, keepdims=True)

        vn = _each(lambda u_, w_, s_: u_ - dot_nn(w_, s_), c["u"], c["w"], s0)
        dvn = _each(lambda at_, do_, ks_, ds_: dot_tn(at_, do_) + dot_nn(ks_, ds_), c["attn"], dov, c["ks"], dsn)
        dattn = _each(lambda do_, vn_: jnp.where(causal, dot_nt(do_, vn_), 0.0), dov, vn)
        dqg = _each(dot_nt, dov, s0)
        dks = _each(dot_nt, vn, dsn)
        ddec = _each(lambda s_, ds_: jnp.sum(rsum(s_, ds_), axis=0, keepdims=True), s0, dsn)
        dw = _each(lambda dvn_, s_: -dot_nt(dvn_, s_), dvn, s0)
        ds0 = _each(lambda qg_, do_, ds_, d_, w_, dvn_: dot_tn(qg_, do_) + ds_ * d_ - dot_tn(w_, dvn_),
                    c["qg"], dov, dsn, c["dec"], c["w"], dvn)
        dvb = _each(lambda t_, x_: dot_tn(t_, x_, hi="split"), tinv, dvn)
        dkbg = _each(lambda t_, x_: dot_tn(t_, x_, hi="split"), tinv, dw)
        dt = _each(lambda dvn_, vb_, dw_, kbg_: dot_nt(dvn_, vb_, hi="split") + dot_nt(dw_, kbg_, hi="split"),
                   dvn, c["vb"], dw, c["kbg"])
        dtt = _each(lambda dt_, t_: dot_nt(dt_, t_, hi="split"), dt, tinv)
        da = _each(lambda t_, x_: jnp.where(strict, -dot_tn(t_, x_, hi="split"), 0.0), tinv, dtt)
        dm = _each(_mul, da, decay)
        dn = _each(_mul, dattn, decay)
        e = _each(lambda da_, m_, dat_, n_, d_: (da_ * m_ + dat_ * n_) * d_, da, c["m"], dattn, c["n"], decay)
        dkb = _each(lambda dm_, k_, dkbg_, egc_: dot_nn(dm_, k_) + dkbg_ * egc_, dm, k, dkbg, c["egc"])
        dk = _each(lambda dm_, kb_, dn_, q_, dks_, ekl_, dkb_, b_: dot_tn(dm_, kb_) + dot_tn(dn_, q_) + dks_ * ekl_
                   + dkb_ * b_, dm, c["kb"], dn, q, dks, c["ekl"], dkb, c["b_col"])
        dq = _each(lambda dn_, k_, dqg_, egc_: dot_nn(dn_, k_) + dqg_ * egc_, dn, k, dqg, c["egc"])
        dks_ks = _each(rsum, dks, c["ks"])
        dgc_col = _each(lambda e_, dkbg_, kbg_, dqg_, qg_, x_: jnp.sum(e_, axis=1, keepdims=True) + rsum(dkbg_, kbg_)
                        + rsum(dqg_, qg_) - x_ - _col(jnp.sum(e_, axis=0, keepdims=True), eye),
                        e, dkbg, c["kbg"], dqg, c["qg"], dks_ks)
        dgl = _each(lambda x_, dd_, d_: jnp.sum(x_, axis=0, keepdims=True) + dd_ * d_, dks_ks, ddec, c["dec"])
        last = c["ii"][:, 0:1] == CHUNK - 1
        dgc_col = _each(lambda g_, l_: g_ + jnp.where(last, l_, 0.0), dgc_col, dgl)
        dbe_col = _each(lambda dvb_, v_, dkb_, k_: rsum(dvb_, v_) + rsum(dkb_, k_), dvb, v, dkb, k)
        for p in heads:
            vc = pl.ds(p * HEAD, HEAD)
            dstate[p] = ds0[p]
            dq_ref[:, vc] = dq[p]
            dk_ref[:, vc] = dk[p]
            dv_ref[:, vc] = dvb[p] * c["b_col"][p]
            dbe_ref[p, 0] = _row(dbe_col[p], eye)
            dg_ref[p, 0] = jnp.sum(jnp.where(causal, dgc_col[p], 0.0), axis=0, keepdims=True)

    rev = lambda n: NC - 1 - n
    vrow = BS((P, 1, 1, CHUNK), lambda h, n: (h, rev(n), 0, 0))
    vblk = BS((CHUNK, vw), lambda h, n: (rev(n), h))
    return pl.pallas_call(
        body, grid=(HV // P, NC),
        in_specs=[BS((CHUNK, kw), lambda h, n: (rev(n), h)),
                  BS((CHUNK, kw), lambda h, n: (rev(n), key_dim // kw + h)),
                  BS((CHUNK, vw), lambda h, n: (rev(n), 2 * key_dim // vw + h)), vrow, vrow,
                  BS((P, 1, HEAD, HEAD), lambda h, n: (h, rev(n), 0, 0)),
                  BS((P, 1, CHUNK, CHUNK), lambda h, n: (h, rev(n), 0, 0)), vblk],
        out_specs=[vblk, vblk, vblk, vrow, vrow],
        out_shape=[S((T, HV * HEAD), F32)] * 3 + [S((HV, NC, 1, CHUNK), F32)] * 2,
        scratch_shapes=[pltpu.VMEM((P, HEAD, HEAD), F32)],
        compiler_params=_cp(("arbitrary", "arbitrary")), name=name)(
            qkvn, qkvn, qkvn, g_rows, b_rows, states, tinvs, do)


def gdn_post_fwd(o, proj, gate_col0, o_gain, name):
    T, V = o.shape
    cb = min(1024, V)
    j0 = gate_col0 // cb
    bt = _pick(T, (256, 128, 64))

    def body(o_ref, g_ref, gain_ref, y_ref):
        for hh in range(cb // HEAD):
            cols = pl.ds(hh * HEAD, HEAD)
            ov = o_ref[:, cols]
            gt = g_ref[:, cols]
            r = lax.rsqrt(jnp.mean(ov * ov, axis=-1, keepdims=True) + EPS)
            y_ref[:, cols] = (((ov * r) * gain_ref[...]) * (gt * _sigmoid(gt))).astype(y_ref.dtype)

    return pl.pallas_call(
        body, grid=(T // bt, V // cb),
        in_specs=[BS((bt, cb), lambda i, j: (i, j)), BS((bt, cb), lambda i, j: (i, j0 + j)),
                  BS((1, HEAD), lambda i, j: (0, 0))],
        out_specs=BS((bt, cb), lambda i, j: (i, j)), out_shape=S((T, V), MXU_DTYPE),
        compiler_params=_cp(("parallel", "parallel")), name=name)(o, proj, o_gain)


def gdn_post_bwd(o, proj, gate_col0, o_gain, dy, name):
    T, V = o.shape
    cb = min(1024, V)
    j0 = gate_col0 // cb
    bt = _pick(T, (256, 128, 64))

    def body(o_ref, g_ref, gain_ref, dy_ref, do_ref, dgt_ref, dgain_ref):
        i, j = pl.program_id(0), pl.program_id(1)

        @pl.when(jnp.logical_and(i == 0, j == 0))
        def _():
            dgain_ref[...] = jnp.zeros_like(dgain_ref)

        gain = gain_ref[...]
        for hh in range(cb // HEAD):
            cols = pl.ds(hh * HEAD, HEAD)
            ov = o_ref[:, cols]
            gt = g_ref[:, cols]
            dyv = dy_ref[:, cols]
            r = lax.rsqrt(jnp.mean(ov * ov, axis=-1, keepdims=True) + EPS)
            ohat = ov * r
            sg = _sigmoid(gt)
            dyn = dyv * (gt * sg)
            dgt_ref[:, cols] = (dyv * (ohat * gain) * (sg * (1.0 + gt * (1.0 - sg)))).astype(dgt_ref.dtype)
            dgain_ref[...] += jnp.sum(dyn * ohat, axis=0, keepdims=True)
            dh = dyn * gain
            do_ref[:, cols] = r * (dh - ohat * jnp.mean(dh * ohat, axis=-1, keepdims=True))

    blk = BS((bt, cb), lambda i, j: (i, j))
    vec = BS((1, HEAD), lambda i, j: (0, 0))
    return pl.pallas_call(
        body, grid=(T // bt, V // cb),
        in_specs=[blk, BS((bt, cb), lambda i, j: (i, j0 + j)), vec, blk],
        out_specs=[blk, blk, vec], out_shape=[S((T, V), F32), S((T, V), MXU_DTYPE), S((1, HEAD), F32)],
        compiler_params=_cp(("arbitrary", "arbitrary")), name=name)(o, proj, o_gain, dy)


def adamw(w, g, m, v, name):
    shape = w.shape
    n = 1
    for d in shape:
        n *= d
    cols = shape[-1] if len(shape) > 1 else n
    rows = n // cols
    br = rows
    for cand in (512, 256, 128, 64, 32, 16, 8):
        if rows % cand == 0 and cand * cols * 4 * 14 <= 36 * MIB:
            br = cand
            break
    c1 = 1.0 - ADAM_B1 ** ADAM_STEP
    c2 = 1.0 - ADAM_B2 ** ADAM_STEP

    def body(w_ref, g_ref, m_ref, v_ref, d_ref, nm_ref, nv_ref):
        gv = g_ref[...]
        nm = ADAM_B1 * m_ref[...] + (1.0 - ADAM_B1) * gv
        nv = ADAM_B2 * v_ref[...] + (1.0 - ADAM_B2) * (gv * gv)
        nm_ref[...] = nm
        nv_ref[...] = nv
        d_ref[...] = -ADAM_LR * ((nm / c1) / (jnp.sqrt(nv / c2) + ADAM_EPS) + ADAM_WD * w_ref[...])

    blk = BS((br, cols), lambda i: (i, 0))
    outs = pl.pallas_call(
        body, grid=(rows // br,), in_specs=[blk] * 4, out_specs=[blk] * 3,
        out_shape=[S((rows, cols), F32)] * 3, compiler_params=_cp(("parallel",), 48), name=name)(
            *[t.reshape(rows, cols) for t in (w, g, m, v)])
    return [t.reshape(shape) for t in outs]


def _with_job(job, *args, **kwargs):
    if job is None:
        return matmul(*args, **kwargs), []
    return matmul(*args, job=job, **kwargs)


def _ffn_fwd(x, gain, p, tag, jobs=(None, None)):
    (h2,) = norm_fwd(x, gain, [MXU_DTYPE], f"{tag}_norm")
    up, got_a = _with_job(jobs[0], h2, p["w_up_t"], "nt", name=f"{tag}_up")
    act = ffn_act_fwd(up, p["conv"], f"{tag}_act")
    out, got_b = _with_job(jobs[1], act, p["w_down"], "nn", res=x, name=f"{tag}_down")
    return out, (x, h2, up, act), got_a + got_b


def _ffn_bwd(dx, saved, gain, p, tag, jobs=(None, None)):
    x, h2, up, act = saved
    g_act = matmul(dx, p["w_down"], "nt", name=f"{tag}_bdown")
    d_down = matmul(act, dx, "tn", name=f"{tag}_wdown")
    du, db, d_conv = ffn_act_bwd_a(up, p["conv"], g_act, f"{tag}_bact_a")
    dup = ffn_act_bwd_b(du, db, p["conv"], f"{tag}_bact_b")
    dh2, got_a = _with_job(jobs[0], dup, p["w_up_t"], "nn", name=f"{tag}_bup")
    d_up_t, got_b = _with_job(jobs[1], dup, h2, "tn", name=f"{tag}_wup")
    dx, d_gain = norm_bwd(x, gain, dh2, dx, f"{tag}_bnorm")
    return dx, d_gain, dict(w_up_t=d_up_t, conv=d_conv, w_down=d_down), got_a + got_b


def _joint_bias(bias):
    rows = [jnp.pad(bias, ((0, 0), (0, 0), (qc * CHUNK, (ATT_CHUNKS - 1 - qc) * CHUNK)), constant_values=MASK_VALUE)
            for qc in range(ATT_CHUNKS)]
    return jnp.concatenate(rows, axis=1)


def _joint_bias_grad(dbias):
    return sum(dbias[:, qc * CHUNK:(qc + 1) * CHUNK, qc * CHUNK:qc * CHUNK + BAND] for qc in range(ATT_CHUNKS))


def _att_fwd(x, gain, p, tag):
    (h,) = norm_fwd(x, gain, [MXU_DTYPE], f"{tag}_norm")
    qkv = matmul(h, p["w_qkv_t"], "nt", name=f"{tag}_qkv")
    bias = _joint_bias(rel_bias_expand(p["rel_bias"], f"{tag}_rel").transpose(1, 0, 2))
    o = attn_fwd(qkv, p["q_gain"], p["k_gain"], bias, f"{tag}_core")
    out = matmul(o, p["w_o"], "nn", res=x, name=f"{tag}_out")
    return out, (x, h, qkv, o, bias)


def _att_bwd(dx, saved, gain, p, tag):
    x, h, qkv, o, bias = saved
    do = matmul(dx, p["w_o"], "nt", name=f"{tag}_bout")
    d_wo = matmul(o, dx, "tn", name=f"{tag}_wout")
    dq, dk, dv, d_gq, d_gk, dbias = attn_bwd(qkv, do, p["q_gain"], p["k_gain"], bias, f"{tag}_bcore")
    dqkv = jnp.concatenate([dq, dk, dv], axis=1)
    dh = matmul(dqkv, p["w_qkv_t"], "nn", name=f"{tag}_bqkv")
    d_wqkv_t = matmul(dqkv, h, "tn", name=f"{tag}_wqkv")
    d_rb = rel_bias_reduce(_joint_bias_grad(dbias).transpose(1, 0, 2), p["rel_bias"].shape[1], f"{tag}_brel")
    dx, d_gain = norm_bwd(x, gain, dh, dx, f"{tag}_bnorm")
    return dx, d_gain, dict(w_qkv_t=d_wqkv_t, w_o=d_wo, q_gain=d_gq, k_gain=d_gk, rel_bias=d_rb)


def _pool_fwd(x, gain, p, tag):
    (hf,) = norm_fwd(x, gain, [F32], f"{tag}_norm")
    out, pooled = pool_fwd(hf, x, p["w"], p["scale"], f"{tag}_core")
    return out, (x, pooled)


def _pool_bwd(dx, saved, gain, p, tag):
    x, pooled = saved
    dpooled, d_w, d_scale = pool_bwd_a(dx, pooled, p["w"], p["scale"], f"{tag}_bcore_a")
    dh = pool_bwd_b(dpooled, f"{tag}_bcore_b")
    dx, d_gain = norm_bwd(x, gain, dh, dx, f"{tag}_bnorm")
    return dx, d_gain, dict(w=d_w, scale=d_scale)


def _rows_layout(t, hv):
    return t.T.reshape(hv, t.shape[0] // CHUNK, 1, CHUNK)


def _gdn_fwd(x, gain, p, tag):
    T = x.shape[0]
    hv = p["a_log"].shape[1]
    key_dim = p["key_dim"]
    C = p["conv"].shape[1]
    (h,) = norm_fwd(x, gain, [MXU_DTYPE], f"{tag}_norm")
    proj = matmul(h, p["w_main_t"], "nt", name=f"{tag}_in")
    ab = matmul(h, p["w_ab_t"], "nt", name=f"{tag}_in_ab")
    a, b = ab[:, :hv], ab[:, hv:2 * hv]
    qkvn = gdn_pre_fwd(proj, p["conv"], key_dim, f"{tag}_pre")
    g, beta = gdn_gate_fwd(a, b, p["a_log"], p["dt_bias"], f"{tag}_gate")
    g_rows, b_rows = _rows_layout(g, hv), _rows_layout(beta, hv)
    o, states, tinvs = delta_fwd(qkvn, g_rows, b_rows, key_dim, f"{tag}_delta")
    y = gdn_post_fwd(o, proj, C, p["o_gain"], f"{tag}_post")
    out = matmul(y, p["w_o"], "nn", res=x, name=f"{tag}_out")
    return out, (x, h, proj, a, b, qkvn, g_rows, b_rows, o, states, tinvs, y)


def _gdn_bwd(dx, saved, gain, p, tag):
    x, h, proj, a, b, qkvn, g_rows, b_rows, o, states, tinvs, y = saved
    T = x.shape[0]
    hv = p["a_log"].shape[1]
    key_dim = p["key_dim"]
    C = p["conv"].shape[1]
    dy = matmul(dx, p["w_o"], "nt", name=f"{tag}_bout")
    d_wo = matmul(y, dx, "tn", name=f"{tag}_wout")
    do, dgate, d_ogain = gdn_post_bwd(o, proj, C, p["o_gain"], dy, f"{tag}_bpost")
    dq_v, dk_v, dv, dg_rows, dbe_rows = delta_bwd(qkvn, g_rows, b_rows, states, tinvs, do, key_dim,
                                                  f"{tag}_bdelta")
    dg = dg_rows.reshape(hv, T).T
    dbeta = dbe_rows.reshape(hv, T).T
    da, db, d_alog, d_dtb = gdn_gate_bwd(a, b, p["a_log"], p["dt_bias"], dg, dbeta, f"{tag}_bgate")
    du, d_conv = gdn_pre_bwd_a(proj, p["conv"], dq_v, dk_v, dv, key_dim, f"{tag}_bpre_a")
    dproj = gdn_pre_bwd_b(du, dgate, p["conv"], key_dim, f"{tag}_bpre_b")
    dab = jnp.concatenate([da, db, jnp.zeros((T, LANES - 2 * hv), F32)], axis=1)
    dh = matmul(dproj, p["w_main_t"], "nn", name=f"{tag}_bin")
    dh = matmul(dab, p["w_ab_t"], "nn", res=dh, name=f"{tag}_bin_ab")
    d_main_t = matmul(dproj, h, "tn", name=f"{tag}_win")
    d_ab_t = matmul(dab, h, "tn", name=f"{tag}_win_ab")
    dx, d_gain = norm_bwd(x, gain, dh, dx, f"{tag}_bnorm")
    return dx, d_gain, dict(w_main_t=d_main_t, w_ab_t=d_ab_t, conv=d_conv, a_log=d_alog, dt_bias=d_dtb,
                            o_gain=d_ogain, w_o=d_wo)


_MIXERS = ((_att_fwd, _att_bwd), (_pool_fwd, _pool_bwd), (_gdn_fwd, _gdn_bwd))


def local_step(x, target, full, slots, place):
    depth = full["ffn_conv"].shape[0]
    groups = _layer_groups(depth)
    core = place[1:2]
    big = dict(zip(groups[0], gather_slots([slots[k] for k in groups[0]], "gather_l0")))
    saved = []
    for i in range(depth):
        kind, j = i % 3, i // 3
        mp, fp = layer_weights(i, big, full)
        x, s_mix = _MIXERS[kind][0](x, full["mix_norm"][i:i + 1], mp, f"l{i}_mix")
        jobs, keys = (None, None), ()
        if i + 1 < depth:
            keys = groups[i + 1][0::2] + groups[i + 1][1::2]
            jobs = (gather_job([slots[k] for k in groups[i + 1][0::2]]),
                    gather_job([slots[k] for k in groups[i + 1][1::2]]))
        x, s_ffn, got = _ffn_fwd(x, full["ffn_norm"][i:i + 1], fp, f"l{i}_ffn", jobs)
        saved.append((s_mix, s_ffn, mp, fp))
        if keys:
            big = dict(zip(keys, gather_pass(got, f"gather_pass_l{i + 1}")))
    loss, dx = loss_and_grad(x, target, "loss")
    n_mix = [len([i for i in range(depth) if i % 3 == kind]) for kind in range(3)]
    G = dict(mix_norm=[None] * depth, ffn_norm=[None] * depth, ffn=[None] * depth,
             att=[None] * n_mix[0], pool=[None] * n_mix[1], gdn=[None] * n_mix[2])
    summed = {}

    def finish(keys, chip_sums, got, tag):
        mine = [add_cols4(p, b, place, f"rs_add4_{n}_{j}") for (n, j), p, b in zip(keys, chip_sums, got)]
        summed.update(zip(keys, join_cols(mine, f"rs_join_{tag}")))

    pending = None
    for i in reversed(range(depth)):
        kind, j = i % 3, i // 3
        s_mix, s_ffn, mp, fp = saved[i]
        jobs = (None, None)
        if pending:
            keys, chip_sums = pending
            jobs = (scatter_job(chip_sums[0::2]), scatter_job(chip_sums[1::2]))
        dx, G["ffn_norm"][i], gf, got = _ffn_bwd(dx, s_ffn, full["ffn_norm"][i:i + 1], fp, f"l{i}_ffn", jobs)
        if pending:
            finish(keys[0::2] + keys[1::2], chip_sums[0::2] + chip_sums[1::2], got, f"l{i + 1}")
        dx, G["mix_norm"][i], gm = _MIXERS[kind][1](dx, s_mix, full["mix_norm"][i:i + 1], mp, f"l{i}_mix")
        G["ffn"][i] = gf
        G[("att", "pool", "gdn")[kind]][j] = gm
        gbig = layer_big_grads(i, gm, gf, full["gdn_a_log"].shape[1])
        keys = groups[i]
        gs = [gbig[k] for k in keys]
        theirs = swap_cols(gs, f"rs_swap_l{i}")
        pending = (keys, [add_cols2(g, a, core, f"rs_add2_{n}_{j2}") for (n, j2), g, a in zip(keys, gs, theirs)])
    keys, chip_sums = pending
    finish(keys, chip_sums, scatter_cols(chip_sums, "rs_scatter_l0"), "l0")
    return loss, dx, summed, G


SHARD_AXIS = dict(att_w_qkv=2, att_w_o=1, pool_w=2, gdn_w_in=2, gdn_w_o=1, ffn_w_up=2, ffn_w_down=1,
                  att_rel_bias=2, gdn_conv=2, ffn_conv=2)
BIG = ("att_w_qkv", "att_w_o", "pool_w", "gdn_w_in", "gdn_w_o", "ffn_w_up", "ffn_w_down")
SMALL_SHARDED = ("att_rel_bias", "gdn_conv", "ffn_conv")
REPLICATED = ("mix_norm", "ffn_norm", "att_q_gain", "att_k_gain", "pool_scale", "gdn_a_log", "gdn_dt_bias",
              "gdn_o_gain")
WEIGHTS = ("mix_norm", "ffn_norm", "att_w_qkv", "att_q_gain", "att_k_gain", "att_rel_bias", "att_w_o", "pool_w",
           "pool_scale", "gdn_w_in", "gdn_conv", "gdn_a_log", "gdn_dt_bias", "gdn_o_gain", "gdn_w_o", "ffn_w_up",
           "ffn_conv", "ffn_w_down")


def _merge(stacked, axis):
    t = jnp.moveaxis(stacked, 0, axis)
    return t.reshape(t.shape[:axis] + (t.shape[axis] * t.shape[axis + 1],) + t.shape[axis + 2:])


def _pad_to(t, axis, size):
    pad = [(0, 0)] * t.ndim
    pad[axis] = (0, size - t.shape[axis])
    return jnp.pad(t, pad)


def _round_up(n, m):
    return (n + m - 1) // m * m


def to_comm(name, t):
    if name in ("att_w_qkv", "gdn_w_in"):
        return t.T
    if name == "ffn_w_up":
        d, n = t.shape
        return _pad_to(t.T.reshape(2, n // 2, d), 1, _round_up(n // 2, LANES)).reshape(-1, d)
    if name == "ffn_w_down":
        return _pad_to(t, 0, _round_up(t.shape[0], LANES))
    if name == "pool_w":
        return t.reshape(-1, t.shape[-1])
    return t


def from_comm(name, r, shape):
    if name in ("att_w_qkv", "gdn_w_in"):
        return r.T
    if name == "ffn_w_up":
        d, n = shape
        return r.reshape(2, -1, d)[:, :n // 2].reshape(n, d).T
    if name == "ffn_w_down":
        return r[:shape[0]]
    return r.reshape(shape)


def _rows(t):
    return t.reshape(-1, t.shape[-1])


def layer_weights(i, big, full):
    kind, j = i % 3, i // 3
    F4 = full["ffn_conv"].shape[2] // N_CHIPS
    conv = _pad_to(full["ffn_conv"][i].reshape(FFN_CONV, N_CHIPS, F4), 2, _round_up(F4, LANES)).reshape(FFN_CONV, -1)
    fp = dict(w_up_t=_rows(big["ffn_w_up", i]), conv=_pad_to(conv, 0, SUBLANES), w_down=_rows(big["ffn_w_down", i]))
    if kind == 0:
        mp = dict(w_qkv_t=_rows(big["att_w_qkv", j]), w_o=_rows(big["att_w_o", j]),
                  q_gain=full["att_q_gain"][j:j + 1], k_gain=full["att_k_gain"][j:j + 1],
                  rel_bias=full["att_rel_bias"][j])
    elif kind == 1:
        t = big["pool_w", j]
        G = len(POOL_WINDOWS)
        dg = t.shape[-1]
        w = jnp.moveaxis(t.reshape(N_CHIPS, G, dg // N_CHIPS, dg), 0, 1).reshape(G, dg, dg)
        mp = dict(w=w, scale=full["pool_scale"][j:j + 1])
    else:
        C = full["gdn_conv"].shape[2]
        wt = _rows(big["gdn_w_in", j])
        V = _rows(big["gdn_w_o", j]).shape[0]
        mp = dict(w_main_t=wt[:C + V], w_ab_t=_pad_to(wt[C + V:], 0, LANES),
                  conv=_pad_to(full["gdn_conv"][j], 0, SUBLANES), a_log=full["gdn_a_log"][j:j + 1],
                  dt_bias=full["gdn_dt_bias"][j:j + 1], o_gain=full["gdn_o_gain"][j:j + 1],
                  w_o=_rows(big["gdn_w_o", j]), key_dim=(C - V) // 2)
    return mp, fp


def layer_big_grads(i, gm, gf, hv):
    kind, j = i % 3, i // 3

    def slots(t):
        return t.reshape(N_CHIPS, t.shape[0] // N_CHIPS, t.shape[1])

    out = {("ffn_w_up", i): slots(gf["w_up_t"]), ("ffn_w_down", i): slots(gf["w_down"])}
    if kind == 0:
        out["att_w_qkv", j] = slots(gm["w_qkv_t"])
        out["att_w_o", j] = slots(gm["w_o"])
    elif kind == 1:
        n, dg, _ = gm["w"].shape
        out["pool_w", j] = jnp.moveaxis(gm["w"].reshape(n, N_CHIPS, dg // N_CHIPS, dg), 1, 0).reshape(N_CHIPS, -1, dg)
    else:
        out["gdn_w_in", j] = slots(jnp.concatenate([gm["w_main_t"], gm["w_ab_t"][:2 * hv]], axis=0))
        out["gdn_w_o", j] = slots(gm["w_o"])
    return out


def small_grads(G, full):
    F = full["ffn_conv"].shape[2]
    F4 = F // N_CHIPS

    def conv(g):
        return g["conv"][:FFN_CONV].reshape(FFN_CONV, N_CHIPS, -1)[:, :, :F4].reshape(FFN_CONV, F)

    return dict(
        mix_norm=jnp.concatenate(G["mix_norm"], axis=0), ffn_norm=jnp.concatenate(G["ffn_norm"], axis=0),
        ffn_conv=jnp.stack([conv(g) for g in G["ffn"]]),
        att_q_gain=jnp.concatenate([g["q_gain"] for g in G["att"]], axis=0),
        att_k_gain=jnp.concatenate([g["k_gain"] for g in G["att"]], axis=0),
        att_rel_bias=jnp.stack([g["rel_bias"] for g in G["att"]]),
        pool_scale=jnp.concatenate([g["scale"] for g in G["pool"]], axis=0),
        gdn_conv=jnp.stack([g["conv"][:GDN_CONV] for g in G["gdn"]]),
        gdn_a_log=jnp.concatenate([g["a_log"] for g in G["gdn"]], axis=0),
        gdn_dt_bias=jnp.concatenate([g["dt_bias"] for g in G["gdn"]], axis=0),
        gdn_o_gain=jnp.concatenate([g["o_gain"] for g in G["gdn"]], axis=0))


ANY = BS(memory_space=pl.ANY)
PACK_COLS = 1024
PACK_ROWS = 32


def _place():
    x, y, c = lax.axis_index("x"), lax.axis_index("y"), lax.axis_index("c")
    chips = [(1 - x, y), (x, 1 - y), (1 - x, 1 - y)]
    return x, y, c, chips


def _remote(src, dst, send_sem, recv_sem, to):
    return pltpu.make_async_remote_copy(src_ref=src, dst_ref=dst, send_sem=send_sem, recv_sem=recv_sem,
                                        device_id=to, device_id_type=MESH)


def gather_chips(shard, name):
    R, C = shard.shape
    half = R // 2

    def body(x_ref, o_ref, send_sems, recv_sems, local_sem):
        x, y, c, chips = _place()
        mine_rows = pl.ds(c * half, half)
        other_rows = pl.ds((1 - c) * half, half)
        own = pltpu.make_async_copy(x_ref, o_ref.at[2 * x + y], local_sem)
        own.start()
        first = [_remote(x_ref.at[mine_rows], o_ref.at[2 * x + y, mine_rows], send_sems.at[j], recv_sems.at[j],
                         (cx, cy, c)) for j, (cx, cy) in enumerate(chips)]
        for cp in first:
            cp.start()
        passed = []
        for j, (cx, cy) in enumerate(chips):
            landed = o_ref.at[2 * cx + cy, mine_rows]
            _remote(landed, landed, send_sems.at[j], recv_sems.at[j], (cx, cy, c)).wait_recv()
            cp = _remote(landed, landed, send_sems.at[3 + j], recv_sems.at[3 + j], (x, y, 1 - c))
            cp.start()
            passed.append(cp)
        for j, (cx, cy) in enumerate(chips):
            landed = o_ref.at[2 * cx + cy, other_rows]
            _remote(landed, landed, send_sems.at[3 + j], recv_sems.at[3 + j], (x, y, 1 - c)).wait_recv()
        for cp in first + passed:
            cp.wait_send()
        own.wait()

    return pl.pallas_call(
        body, out_shape=S((N_CHIPS, R, C), shard.dtype), in_specs=[ANY], out_specs=ANY,
        scratch_shapes=[pltpu.SemaphoreType.DMA((6,)), pltpu.SemaphoreType.DMA((6,)), pltpu.SemaphoreType.DMA],
        name=name)(shard)


def _tile(R, hc):
    if R % 256 == 0:
        return _pick(R, (512, 256)), hc
    return R, _pick(hc, (256, 128))


def _half(c, hc):
    return pl.ds(pl.multiple_of(c * hc, hc), hc)


def prep_slot(t, s_me, name):
    R, C = t.shape
    br, bc = _tile(R, C // 2)

    def body(s_ref, t_ref, o_ref):
        o_ref[0] = t_ref[...].astype(o_ref.dtype)

    return pl.pallas_call(
        body, grid_spec=pltpu.PrefetchScalarGridSpec(
            num_scalar_prefetch=1, grid=(R // br, C // bc),
            in_specs=[BS((br, bc), lambda i, j, s: (i, j))],
            out_specs=BS((1, br, bc), lambda i, j, s: (s[0], i, j))),
        out_shape=S((N_CHIPS, R, C), MXU_DTYPE), compiler_params=_cp(("parallel", "parallel")), name=name)(
            s_me.reshape(1), t)


def gather_slots(arrs, name):
    nt = len(arrs)

    def body(*refs):
        outs = refs[nt:2 * nt]
        send_sems, recv_sems = refs[2 * nt:]
        x, y, c, chips = _place()
        me = 2 * x + y
        first, passed = [], []
        for t, o in enumerate(outs):
            mine = _half(c, o.shape[2] // 2)
            for j, (cx, cy) in enumerate(chips):
                cp = _remote(o.at[me, :, mine], o.at[me, :, mine], send_sems.at[t, j], recv_sems.at[t, j], (cx, cy, c))
                cp.start()
                first.append(cp)
        for t, o in enumerate(outs):
            mine = _half(c, o.shape[2] // 2)
            for j, (cx, cy) in enumerate(chips):
                landed = o.at[2 * cx + cy, :, mine]
                _remote(landed, landed, send_sems.at[t, j], recv_sems.at[t, j], (cx, cy, c)).wait_recv()
                cp = _remote(landed, landed, send_sems.at[t, 3 + j], recv_sems.at[t, 3 + j], (x, y, 1 - c))
                cp.start()
                passed.append(cp)
        for t, o in enumerate(outs):
            other = _half(1 - c, o.shape[2] // 2)
            for j, (cx, cy) in enumerate(chips):
                landed = o.at[2 * cx + cy, :, other]
                _remote(landed, landed, send_sems.at[t, 3 + j], recv_sems.at[t, 3 + j], (x, y, 1 - c)).wait_recv()
        for cp in first + passed:
            cp.wait_send()

    return pl.pallas_call(
        body, out_shape=[S(a.shape, a.dtype) for a in arrs], in_specs=[ANY] * nt, out_specs=[ANY] * nt,
        input_output_aliases={t: t for t in range(nt)},
        scratch_shapes=[pltpu.SemaphoreType.DMA((nt, 6)), pltpu.SemaphoreType.DMA((nt, 6))], name=name)(*arrs)


def swap_cols(gs, name):
    nt = len(gs)

    def body(*refs):
        ins, outs = refs[:nt], refs[nt:2 * nt]
        send_sems, recv_sems = refs[2 * nt:]
        x, y, c, _ = _place()
        sent = []
        for t, (g, o) in enumerate(zip(ins, outs)):
            cp = _remote(g.at[:, :, _half(1 - c, o.shape[2])], o, send_sems.at[t], recv_sems.at[t], (x, y, 1 - c))
            cp.start()
            sent.append(cp)
        for cp in sent:
            cp.wait()

    return pl.pallas_call(
        body, out_shape=[S(g.shape[:2] + (g.shape[2] // 2,), g.dtype) for g in gs], in_specs=[ANY] * nt,
        out_specs=[ANY] * nt, scratch_shapes=[pltpu.SemaphoreType.DMA((nt,)), pltpu.SemaphoreType.DMA((nt,))],
        name=name)(*gs)


def add_cols2(g, other, c, name):
    n, R, C = g.shape
    hc = C // 2
    br, bc = _tile(R, hc)
    nj = hc // bc

    def body(c_ref, g_ref, o_ref, out_ref):
        out_ref[...] = (g_ref[...] + o_ref[...]).astype(out_ref.dtype)

    blk = BS((1, br, bc), lambda s, i, j, c_ref: (s, i, j))
    return pl.pallas_call(
        body, grid_spec=pltpu.PrefetchScalarGridSpec(
            num_scalar_prefetch=1, grid=(n, R // br, nj),
            in_specs=[BS((1, br, bc), lambda s, i, j, c_ref: (s, i, c_ref[0] * nj + j)), blk], out_specs=blk),
        out_shape=S((n, R, hc), BF16), compiler_params=_cp(("parallel", "parallel", "parallel")), name=name)(
            c, g, other)


def scatter_cols(ps, name):
    nt = len(ps)

    def body(*refs):
        ins, outs = refs[:nt], refs[nt:2 * nt]
        send_sems, recv_sems = refs[2 * nt:]
        x, y, c, chips = _place()
        sent = []
        for t, (p, o) in enumerate(zip(ins, outs)):
            for j, (cx, cy) in enumerate(chips):
                cp = _remote(p.at[2 * cx + cy], o.at[j], send_sems.at[t, j], recv_sems.at[t, j], (cx, cy, c))
                cp.start()
                sent.append(cp)
        for cp in sent:
            cp.wait()

    return pl.pallas_call(
        body, out_shape=[S((N_CHIPS - 1,) + p.shape[1:], p.dtype) for p in ps], in_specs=[ANY] * nt,
        out_specs=[ANY] * nt,
        scratch_shapes=[pltpu.SemaphoreType.DMA((nt, 3)), pltpu.SemaphoreType.DMA((nt, 3))], name=name)(*ps)


def add_cols4(p, got, place, name):
    n, R, hc = p.shape
    br, bc = _tile(R, hc)
    nj = hc // bc

    def body(pl_ref, p_ref, g_ref, out_ref):
        acc = p_ref[0].astype(F32)
        for j in range(n - 1):
            acc += g_ref[j].astype(F32)
        out_ref[...] = acc

    return pl.pallas_call(
        body, grid_spec=pltpu.PrefetchScalarGridSpec(
            num_scalar_prefetch=1, grid=(R // br, nj),
            in_specs=[BS((1, br, bc), lambda i, j, pl_ref: (pl_ref[0], i, j)),
                      BS((n - 1, br, bc), lambda i, j, pl_ref: (0, i, j))],
            out_specs=BS((br, bc), lambda i, j, pl_ref: (i, pl_ref[1] * nj + j))),
        out_shape=S((R, 2 * hc), F32), compiler_params=_cp(("parallel", "parallel")), name=name)(place, p, got)


def join_cols(rs, name):
    nt = len(rs)

    def body(*refs):
        outs = refs[nt:2 * nt]
        send_sems, recv_sems = refs[2 * nt:]
        x, y, c, _ = _place()
        sent = []
        for t, o in enumerate(outs):
            mine = o.at[:, _half(c, o.shape[1] // 2)]
            cp = _remote(mine, mine, send_sems.at[t], recv_sems.at[t], (x, y, 1 - c))
            cp.start()
            sent.append(cp)
        for t, o in enumerate(outs):
            theirs = o.at[:, _half(1 - c, o.shape[1] // 2)]
            _remote(theirs, theirs, send_sems.at[t], recv_sems.at[t], (x, y, 1 - c)).wait_recv()
        for cp in sent:
            cp.wait_send()

    return pl.pallas_call(
        body, out_shape=[S(r.shape, r.dtype) for r in rs], in_specs=[ANY] * nt, out_specs=[ANY] * nt,
        input_output_aliases={t: t for t in range(nt)},
        scratch_shapes=[pltpu.SemaphoreType.DMA((nt,)), pltpu.SemaphoreType.DMA((nt,))], name=name)(*rs)


def gather_job(slots):
    nt = len(slots)

    def copies(outs, send_sems, recv_sems, sent):
        x, y, c, chips = _place()
        me = 2 * x + y
        out = []
        for t, o in enumerate(outs):
            mine = _half(c, o.shape[2] // 2)
            for j, (cx, cy) in enumerate(chips):
                rows = o.at[me if sent else 2 * cx + cy, :, mine]
                out.append(_remote(rows, rows, send_sems.at[3 * t + j], recv_sems.at[3 * t + j], (cx, cy, c)))
        return out

    def start(ins, outs, send_sems, recv_sems):
        for cp in copies(outs, send_sems, recv_sems, True):
            cp.start()

    def finish(ins, outs, send_sems, recv_sems):
        for cp in copies(outs, send_sems, recv_sems, False):
            cp.wait_recv()
        for cp in copies(outs, send_sems, recv_sems, True):
            cp.wait_send()

    return dict(ins=list(slots), out_shapes=[S(a.shape, a.dtype) for a in slots], aliases=[(t, t) for t in range(nt)],
                nsem=3 * nt, start=start, finish=finish)


def gather_pass(arrs, name):
    nt = len(arrs)

    def body(*refs):
        outs = refs[nt:2 * nt]
        send_sems, recv_sems = refs[2 * nt:]
        x, y, c, chips = _place()
        sent = []
        for t, o in enumerate(outs):
            mine = _half(c, o.shape[2] // 2)
            for j, (cx, cy) in enumerate(chips):
                landed = o.at[2 * cx + cy, :, mine]
                cp = _remote(landed, landed, send_sems.at[t, j], recv_sems.at[t, j], (x, y, 1 - c))
                cp.start()
                sent.append(cp)
        for t, o in enumerate(outs):
            other = _half(1 - c, o.shape[2] // 2)
            for j, (cx, cy) in enumerate(chips):
                landed = o.at[2 * cx + cy, :, other]
                _remote(landed, landed, send_sems.at[t, j], recv_sems.at[t, j], (x, y, 1 - c)).wait_recv()
        for cp in sent:
            cp.wait_send()

    return pl.pallas_call(
        body, out_shape=[S(a.shape, a.dtype) for a in arrs], in_specs=[ANY] * nt, out_specs=[ANY] * nt,
        input_output_aliases={t: t for t in range(nt)},
        scratch_shapes=[pltpu.SemaphoreType.DMA((nt, 3)), pltpu.SemaphoreType.DMA((nt, 3))], name=name)(*arrs)


def scatter_job(ps):
    nt = len(ps)

    def copies(ins, outs, send_sems, recv_sems):
        x, y, c, chips = _place()
        return [_remote(p.at[2 * cx + cy], o.at[j], send_sems.at[3 * t + j], recv_sems.at[3 * t + j], (cx, cy, c))
                for t, (p, o) in enumerate(zip(ins, outs)) for j, (cx, cy) in enumerate(chips)]

    def start(ins, outs, send_sems, recv_sems):
        for cp in copies(ins, outs, send_sems, recv_sems):
            cp.start()

    def finish(ins, outs, send_sems, recv_sems):
        for cp in copies(ins, outs, send_sems, recv_sems):
            cp.wait()

    return dict(ins=list(ps), out_shapes=[S((N_CHIPS - 1,) + p.shape[1:], p.dtype) for p in ps], aliases=[],
                nsem=3 * nt, start=start, finish=finish)


def sum_devices(v, name):
    R, C = v.shape

    def body(v_ref, o_ref, slots, send_sems, recv_sems):
        x, y, c, _ = _place()
        me = 4 * x + 2 * y + c
        slots[me] = v_ref[...]
        sent = []
        for r in range(1, 8):
            peer = (x ^ (r >> 2), y ^ ((r >> 1) & 1), c ^ (r & 1))
            cp = _remote(v_ref, slots.at[me], send_sems.at[r - 1], recv_sems.at[r - 1], peer)
            cp.start()
            sent.append(cp)
        for r in range(1, 8):
            peer = (x ^ (r >> 2), y ^ ((r >> 1) & 1), c ^ (r & 1))
            theirs = slots.at[4 * peer[0] + 2 * peer[1] + peer[2]]
            _remote(v_ref, theirs, send_sems.at[r - 1], recv_sems.at[r - 1], peer).wait_recv()
        for cp in sent:
            cp.wait_send()
        acc = slots[0]
        for k in range(1, 8):
            acc += slots[k]
        o_ref[...] = acc

    vm = BS(memory_space=pltpu.VMEM)
    return pl.pallas_call(
        body, out_shape=S((R, C), F32), in_specs=[vm], out_specs=vm,
        scratch_shapes=[pltpu.VMEM((8, R, C), F32), pltpu.SemaphoreType.DMA((7,)), pltpu.SemaphoreType.DMA((7,))],
        compiler_params=pltpu.CompilerParams(vmem_limit_bytes=32 * MIB), name=name)(v)


def _pack(arrays, dtype, cols, row_mult):
    flat = jnp.concatenate([a.astype(dtype).reshape(-1) for a in arrays])
    n = flat.shape[0]
    total = _round_up(n, cols * row_mult)
    return jnp.pad(flat, (0, total - n)).reshape(total // cols, cols)


def _unpack(flat, shapes):
    out, off = [], 0
    for shp in shapes:
        n = 1
        for d in shp:
            n *= d
        out.append(flat[..., off:off + n].reshape(flat.shape[:-1] + tuple(shp)))
        off += n
    return out


def _layer_groups(depth):
    groups = []
    for i in range(depth):
        kind, j = i % 3, i // 3
        mix = ((("att_w_qkv", j), ("att_w_o", j)), (("pool_w", j),), (("gdn_w_in", j), ("gdn_w_o", j)))[kind]
        groups.append(mix + (("ffn_w_up", i), ("ffn_w_down", i)))
    return groups


def kernel(x, mix_norm, ffn_norm, att_w_qkv, att_q_gain, att_k_gain, att_rel_bias, att_w_o, pool_w, pool_scale, gdn_w_in, gdn_conv, gdn_a_log, gdn_dt_bias, gdn_o_gain, gdn_w_o, ffn_w_up, ffn_conv, ffn_w_down, loss_target, m_mix_norm, m_ffn_norm, m_att_w_qkv, m_att_q_gain, m_att_k_gain, m_att_rel_bias, m_att_w_o, m_pool_w, m_pool_scale, m_gdn_w_in, m_gdn_conv, m_gdn_a_log, m_gdn_dt_bias, m_gdn_o_gain, m_gdn_w_o, m_ffn_w_up, m_ffn_conv, m_ffn_w_down, v_mix_norm, v_ffn_norm, v_att_w_qkv, v_att_q_gain, v_att_k_gain, v_att_rel_bias, v_att_w_o, v_pool_w, v_pool_scale, v_gdn_w_in, v_gdn_conv, v_gdn_a_log, v_gdn_dt_bias, v_gdn_o_gain, v_gdn_w_o, v_ffn_w_up, v_ffn_conv, v_ffn_w_down):
    w = dict(mix_norm=mix_norm, ffn_norm=ffn_norm, att_w_qkv=att_w_qkv, att_q_gain=att_q_gain, att_k_gain=att_k_gain, att_rel_bias=att_rel_bias, att_w_o=att_w_o, pool_w=pool_w, pool_scale=pool_scale, gdn_w_in=gdn_w_in, gdn_conv=gdn_conv, gdn_a_log=gdn_a_log, gdn_dt_bias=gdn_dt_bias, gdn_o_gain=gdn_o_gain, gdn_w_o=gdn_w_o, ffn_w_up=ffn_w_up, ffn_conv=ffn_conv, ffn_w_down=ffn_w_down)
    m = dict(mix_norm=m_mix_norm, ffn_norm=m_ffn_norm, att_w_qkv=m_att_w_qkv, att_q_gain=m_att_q_gain, att_k_gain=m_att_k_gain, att_rel_bias=m_att_rel_bias, att_w_o=m_att_w_o, pool_w=m_pool_w, pool_scale=m_pool_scale, gdn_w_in=m_gdn_w_in, gdn_conv=m_gdn_conv, gdn_a_log=m_gdn_a_log, gdn_dt_bias=m_gdn_dt_bias, gdn_o_gain=m_gdn_o_gain, gdn_w_o=m_gdn_w_o, ffn_w_up=m_ffn_w_up, ffn_conv=m_ffn_conv, ffn_w_down=m_ffn_w_down)
    v = dict(mix_norm=v_mix_norm, ffn_norm=v_ffn_norm, att_w_qkv=v_att_w_qkv, att_q_gain=v_att_q_gain, att_k_gain=v_att_k_gain, att_rel_bias=v_att_rel_bias, att_w_o=v_att_w_o, pool_w=v_pool_w, pool_scale=v_pool_scale, gdn_w_in=v_gdn_w_in, gdn_conv=v_gdn_conv, gdn_a_log=v_gdn_a_log, gdn_dt_bias=v_gdn_dt_bias, gdn_o_gain=v_gdn_o_gain, gdn_w_o=v_gdn_w_o, ffn_w_up=v_ffn_w_up, ffn_conv=v_ffn_conv, ffn_w_down=v_ffn_w_down)
    depth = ffn_w_up.shape[0]
    my_c = lax.axis_index("c").astype(jnp.int32)
    my_chip = (2 * lax.axis_index("x") + lax.axis_index("y")).astype(jnp.int32)
    groups = _layer_groups(depth)

    place = jnp.stack([my_chip, my_c])

    full = {n: w[n] for n in REPLICATED}
    slots = {(n, j): prep_slot(to_comm(n, w[n][j]), my_chip, f"prep_{n}_{j}") for group in groups for n, j in group}
    small = [w[n] for n in SMALL_SHARDED]
    got = gather_chips(_pack(small, F32, LANES, PACK_ROWS), "gather_small").reshape(N_CHIPS, -1)
    for n, t in zip(SMALL_SHARDED, _unpack(got, [s.shape for s in small])):
        full[n] = _merge(t, SHARD_AXIS[n])

    loss, grad_x, summed, G = local_step(x[0], loss_target[0], full, slots, place)
    loss = lax.psum(loss[0, 0], ("x", "y", "c"))
    gfull = small_grads(G, full)

    grads = {n: jnp.stack([from_comm(n, summed[n, j], w[n][j].shape) for j in range(w[n].shape[0])]) for n in BIG}

    small_names = REPLICATED + SMALL_SHARDED
    packed = _pack([gfull[n] for n in small_names], F32, LANES, SUBLANES)
    summed = sum_devices(packed, "sum_small").reshape(-1)
    for n, t in zip(small_names, _unpack(summed, [gfull[n].shape for n in small_names])):
        if n in SHARD_AXIS:
            size = w[n].shape[SHARD_AXIS[n]]
            t = lax.dynamic_slice_in_dim(t, my_chip * size, size, axis=SHARD_AXIS[n])
        grads[n] = t

    delta, new_m, new_v = {}, {}, {}
    for n in WEIGHTS:
        delta[n], new_m[n], new_v[n] = adamw(w[n], grads[n], m[n], v[n], f"adamw_{n}")
    return (loss, grad_x[None], *[grads[n] for n in WEIGHTS], *[delta[n] for n in WEIGHTS],
            *[new_m[n] for n in WEIGHTS], *[new_v[n] for n in WEIGHTS])
```

```python
import functools

import jax
import jax.numpy as jnp
from jax import lax
from jax.experimental import pallas as pl
from jax.experimental.pallas import tpu as pltpu

F32 = jnp.float32
BF16 = jnp.bfloat16
MXU_DTYPE = BF16
HI = lax.Precision.HIGHEST
S = jax.ShapeDtypeStruct
BS = pl.BlockSpec

EPS = 1e-6
MASK_VALUE = -1e30
CHUNK = 64
HEAD = 128
LEFT_CHUNKS = 8
BAND_LEFT = LEFT_CHUNKS * CHUNK
BAND = BAND_LEFT + CHUNK
MAX_REL = 256
ATT_CHUNKS = 4
ATT_QB = ATT_CHUNKS * CHUNK
ATT_BAND = BAND_LEFT + ATT_QB
POOL_WINDOWS = (2, 4, 8, 16)
POOL_HALO = 16
GDN_CONV = 4
DELTA_HEADS = 8
FFN_CONV = 3
SUBLANES = 8
LANES = 128
FF_ALIGN = 512
N_CHIPS = 4
ADAM_LR, ADAM_B1, ADAM_B2, ADAM_EPS, ADAM_WD, ADAM_STEP = 0.001, 0.9, 0.999, 1e-08, 0.01, 10
MIB = 1024 * 1024
MESH = pl.DeviceIdType.MESH


def _cp(sems, vmem_mib=40):
    return pltpu.CompilerParams(dimension_semantics=sems, vmem_limit_bytes=vmem_mib * MIB)


def _pick(n, cands):
    for c in cands:
        if n % c == 0:
            return c
    return n


def _mx(x):
    return x.astype(MXU_DTYPE)


def _hi_lo(x):
    hi = x.astype(MXU_DTYPE)
    return hi, (x - hi.astype(F32)).astype(MXU_DTYPE)


def _dot(a, b, dims, hi=False):
    dn = (dims, ((), ()))
    if hi is True or (hi == "split" and MXU_DTYPE == F32):
        return lax.dot_general(a.astype(F32), b.astype(F32), dn, precision=HI, preferred_element_type=F32)
    if hi == "split":
        ah, al = _hi_lo(a)
        bh, bl = _hi_lo(b)
        return (lax.dot_general(ah, bh, dn, preferred_element_type=F32)
                + (lax.dot_general(ah, bl, dn, preferred_element_type=F32)
                   + lax.dot_general(al, bh, dn, preferred_element_type=F32)))
    return lax.dot_general(_mx(a), _mx(b), dn, preferred_element_type=F32)


def dot_nn(a, b, hi=False):
    return _dot(a, b, ((1,), (0,)), hi)


def dot_nt(a, b, hi=False):
    return _dot(a, b, ((1,), (1,)), hi)


def dot_tn(a, b, hi=False):
    return _dot(a, b, ((0,), (0,)), hi)


def _sigmoid(x):
    return 1.0 / (1.0 + jnp.exp(-x))


def matmul(a, b, mode, *, out_dtype=F32, res=None, job=None, name):
    if mode == "nn":
        (M, K), N = a.shape, b.shape[1]
    elif mode == "nt":
        (M, K), N = a.shape, b.shape[0]
    else:
        (K, M), N = a.shape, b.shape[1]
    bm = _pick(M, (1024, 512, 256, 128))
    bn = _pick(N, (1024, 512, 256, 128))
    bk = _pick(K, (2816, 2048, 1408, 1024, 512, 256, 128))
    nk = K // bk
    if mode == "nn":
        a_spec = BS((bm, bk), lambda i, j, k: (i, k))
        b_spec = BS((bk, bn), lambda i, j, k: (k, j))
        dot = dot_nn
    elif mode == "nt":
        a_spec = BS((bm, bk), lambda i, j, k: (i, k))
        b_spec = BS((bn, bk), lambda i, j, k: (j, k))
        dot = dot_nt
    else:
        a_spec = BS((bk, bm), lambda i, j, k: (k, i))
        b_spec = BS((bk, bn), lambda i, j, k: (k, j))
        dot = dot_tn
    o_spec = BS((bm, bn), lambda i, j, k: (i, j))
    has_res = res is not None
    n_in = 3 if has_res else 2
    nji = len(job["ins"]) if job else 0
    njo = len(job["out_shapes"]) if job else 0
    grid = (M // bm, N // bn, nk)

    def body(*refs):
        a_ref, b_ref = refs[:2]
        r_ref = refs[2] if has_res else None
        o_ref = refs[n_in + nji]
        acc = refs[n_in + nji + 1 + njo]
        k = pl.program_id(2)
        if job:
            j_ins, j_outs = refs[n_in:n_in + nji], refs[n_in + nji + 1:n_in + nji + 1 + njo]
            sems = refs[n_in + nji + 2 + njo:]
            at_step = lambda s: jnp.logical_and(jnp.logical_and(pl.program_id(0) == s[0], pl.program_id(1) == s[1]),
                                                k == s[2])
            pl.when(at_step((0, 0, 0)))(lambda: job["start"](j_ins, j_outs, *sems))
        p = dot(a_ref[...], b_ref[...])

        def finish(total):
            if has_res:
                total = r_ref[...] + total
            o_ref[...] = total.astype(o_ref.dtype)

        if nk == 1:
            finish(p)
        else:
            @pl.when(k == 0)
            def _():
                acc[...] = p

            @pl.when(jnp.logical_and(k > 0, k < nk - 1))
            def _():
                acc[...] += p

            @pl.when(k == nk - 1)
            def _():
                finish(acc[...] + p)

        if job:
            pl.when(at_step((grid[0] - 1, grid[1] - 1, nk - 1)))(lambda: job["finish"](j_ins, j_outs, *sems))

    in_specs = [a_spec, b_spec] + ([o_spec] if has_res else [])
    args = (a, b) + ((res,) if has_res else ())
    if not job:
        return pl.pallas_call(
            body, grid=grid, in_specs=in_specs, out_specs=o_spec,
            out_shape=S((M, N), out_dtype), scratch_shapes=[pltpu.VMEM((bm, bn), F32)],
            compiler_params=_cp(("parallel", "parallel", "arbitrary"), 48), name=name)(*args)
    any_spec = BS(memory_space=pl.ANY)
    outs = pl.pallas_call(
        body, grid=grid, in_specs=in_specs + [any_spec] * nji, out_specs=[o_spec] + [any_spec] * njo,
        out_shape=[S((M, N), out_dtype)] + list(job["out_shapes"]),
        input_output_aliases={n_in + src: 1 + dst for src, dst in job["aliases"]},
        scratch_shapes=[pltpu.VMEM((bm, bn), F32), pltpu.SemaphoreType.DMA((job["nsem"],)),
                        pltpu.SemaphoreType.DMA((job["nsem"],))],
        compiler_params=_cp(("arbitrary", "arbitrary", "arbitrary"), 48), name=name)(*args, *job["ins"])
    return outs[0], list(outs[1:])


def norm_fwd(x, gain, out_dtypes, name):
    T, D = x.shape
    bt = _pick(T, (256, 128, 64))

    def body(x_ref, g_ref, *o_refs):
        xv = x_ref[...]
        r = lax.rsqrt(jnp.mean(xv * xv, axis=-1, keepdims=True) + EPS)
        y = (xv * r) * g_ref[...]
        for o in o_refs:
            o[...] = y.astype(o.dtype)

    row = BS((bt, D), lambda i: (i, 0))
    return pl.pallas_call(
        body, grid=(T // bt,), in_specs=[row, BS((1, D), lambda i: (0, 0))],
        out_specs=[row] * len(out_dtypes), out_shape=[S((T, D), dt) for dt in out_dtypes],
        compiler_params=_cp(("parallel",)), name=name)(x, gain)


def norm_bwd(x, gain, dy, dres, name):
    T, D = x.shape
    bt = _pick(T, (256, 128, 64))

    def body(x_ref, g_ref, dy_ref, dres_ref, dx_ref, dxm_ref, dg_ref):
        i = pl.program_id(0)
        xv = x_ref[...]
        dyv = dy_ref[...].astype(F32)
        r = lax.rsqrt(jnp.mean(xv * xv, axis=-1, keepdims=True) + EPS)
        xhat = xv * r
        dxhat = dyv * g_ref[...]
        dx = dres_ref[...] + r * (dxhat - xhat * jnp.mean(dxhat * xhat, axis=-1, keepdims=True))
        dx_ref[...] = dx
        dxm_ref[...] = dx.astype(dxm_ref.dtype)

        @pl.when(i == 0)
        def _():
            dg_ref[...] = jnp.zeros_like(dg_ref)

        dg_ref[...] += jnp.sum(dyv * xhat, axis=0, keepdims=True)

    row = BS((bt, D), lambda i: (i, 0))
    vec = BS((1, D), lambda i: (0, 0))
    dx, dxm, dg = pl.pallas_call(
        body, grid=(T // bt,), in_specs=[row, vec, row, row], out_specs=[row, row, vec],
        out_shape=[S((T, D), F32), S((T, D), MXU_DTYPE), S((1, D), F32)],
        compiler_params=_cp(("arbitrary",)), name=name)(x, gain, dy, dres)
    return (dx, dxm), dg


def loss_and_grad(y, target, name):
    T, D = y.shape
    bt = _pick(T, (256, 128, 64))
    nt = T // bt

    def body(y_ref, t_ref, l_ref, dy_ref, dym_ref, acc):
        i = pl.program_id(0)
        e = y_ref[...] - t_ref[...]
        dy_ref[...] = e * (1.0 / D)
        dym_ref[...] = (e * (1.0 / D)).astype(dym_ref.dtype)

        @pl.when(i == 0)
        def _():
            acc[...] = jnp.zeros_like(acc)

        acc[...] += jnp.sum(e * e, axis=0, keepdims=True)

        @pl.when(i == nt - 1)
        def _():
            l_ref[...] = jnp.sum(acc[...], axis=1, keepdims=True) * (0.5 / D)

    row = BS((bt, D), lambda i: (i, 0))
    loss, dy, dym = pl.pallas_call(
        body, grid=(nt,), in_specs=[row, row], out_specs=[BS((1, 1), lambda i: (0, 0)), row, row],
        out_shape=[S((1, 1), F32), S((T, D), F32), S((T, D), MXU_DTYPE)], scratch_shapes=[pltpu.VMEM((1, D), F32)],
        compiler_params=_cp(("arbitrary",)), name=name)(y, target)
    return loss, (dy, dym)


FFN_HALO = 16


def _prev_halo(bt):
    return lambda i: (jnp.maximum(i * (bt // FFN_HALO) - 1, 0), 0)


def _ffn_u(i, a_ref, halo_ref, w_ref, ext, bt):
    ext[pl.ds(0, FFN_HALO), :] = jnp.where(i > 0, halo_ref[...].astype(F32), 0.0)
    ext[pl.ds(FFN_HALO, bt), :] = a_ref[...].astype(F32)
    u = w_ref[2:3, :] * ext[pl.ds(FFN_HALO, bt), :]
    u += w_ref[1:2, :] * ext[pl.ds(FFN_HALO - 1, bt), :]
    u += w_ref[0:1, :] * ext[pl.ds(FFN_HALO - 2, bt), :]
    return u


def ffn_act_fwd(up, conv_w, name):
    T, F2 = up.shape
    Fp = F2 // 2
    bt = _pick(T, (128, 64))

    def body(a_ref, b_ref, halo_ref, w_ref, o_ref, ext):
        u = _ffn_u(pl.program_id(0), a_ref, halo_ref, w_ref, ext, bt)
        o_ref[...] = (u * _sigmoid(u) * b_ref[...]).astype(o_ref.dtype)

    return pl.pallas_call(
        body, grid=(T // bt,),
        in_specs=[BS((bt, Fp), lambda i: (i, 0)), BS((bt, Fp), lambda i: (i, 1)),
                  BS((FFN_HALO, Fp), _prev_halo(bt)), BS((SUBLANES, Fp), lambda i: (0, 0))],
        out_specs=BS((bt, Fp), lambda i: (i, 0)), out_shape=S((T, Fp), MXU_DTYPE),
        scratch_shapes=[pltpu.VMEM((bt + FFN_HALO, Fp), F32)],
        compiler_params=_cp(("arbitrary",)), name=name)(up, up, up, conv_w)


def ffn_act_bwd_a(up, conv_w, g_act, name):
    T, F2 = up.shape
    Fp = F2 // 2
    bt = _pick(T, (128, 64))

    def body(a_ref, b_ref, halo_ref, w_ref, g_ref, du_ref, db_ref, dw_ref, ext):
        i = pl.program_id(0)
        u = _ffn_u(i, a_ref, halo_ref, w_ref, ext, bt)
        sg = _sigmoid(u)
        g = g_ref[...].astype(F32)
        db_ref[...] = (g * (u * sg)).astype(db_ref.dtype)
        du = g * b_ref[...] * (sg * (1.0 + u * (1.0 - sg)))
        du_ref[...] = du

        @pl.when(i == 0)
        def _():
            dw_ref[...] = jnp.zeros_like(dw_ref)

        for j in range(FFN_CONV):
            shifted = ext[pl.ds(FFN_HALO - (FFN_CONV - 1) + j, bt), :]
            dw_ref[j:j + 1, :] += jnp.sum(du * shifted, axis=0, keepdims=True)

    blk = BS((bt, Fp), lambda i: (i, 0))
    full = BS((SUBLANES, Fp), lambda i: (0, 0))
    return pl.pallas_call(
        body, grid=(T // bt,),
        in_specs=[blk, BS((bt, Fp), lambda i: (i, 1)), BS((FFN_HALO, Fp), _prev_halo(bt)), full, blk],
        out_specs=[blk, blk, full],
        out_shape=[S((T, Fp), F32), S((T, Fp), MXU_DTYPE), S((SUBLANES, Fp), F32)],
        scratch_shapes=[pltpu.VMEM((bt + FFN_HALO, Fp), F32)],
        compiler_params=_cp(("arbitrary",)), name=name)(up, up, up, conv_w, g_act)


def ffn_act_bwd_b(du, db, conv_w, name):
    T, Fp = du.shape
    bt = _pick(T, (128, 64))
    nt = T // bt

    def body(du_ref, halo_ref, db_ref, w_ref, o_ref, ext):
        i = pl.program_id(0)
        ext[pl.ds(0, bt), :] = du_ref[...]
        ext[pl.ds(bt, SUBLANES), :] = jnp.where(i < nt - 1, halo_ref[...], 0.0)
        da = w_ref[2:3, :] * ext[pl.ds(0, bt), :]
        da += w_ref[1:2, :] * ext[pl.ds(1, bt), :]
        da += w_ref[0:1, :] * ext[pl.ds(2, bt), :]
        o_ref[:, pl.ds(0, Fp)] = da.astype(o_ref.dtype)
        o_ref[:, pl.ds(Fp, Fp)] = db_ref[...]

    blk = BS((bt, Fp), lambda i: (i, 0))
    nxt = BS((SUBLANES, Fp), lambda i: (jnp.minimum((i + 1) * (bt // SUBLANES), T // SUBLANES - 1), 0))
    return pl.pallas_call(
        body, grid=(nt,), in_specs=[blk, nxt, blk, BS((SUBLANES, Fp), lambda i: (0, 0))],
        out_specs=BS((bt, 2 * Fp), lambda i: (i, 0)), out_shape=S((T, 2 * Fp), MXU_DTYPE),
        scratch_shapes=[pltpu.VMEM((bt + SUBLANES, Fp), F32)],
        compiler_params=_cp(("arbitrary",)), name=name)(du, du, db, conv_w)


def _attn_fill(k_ref, v_ref, gk, kn_scr, vb_scr, T):
    kn_scr[pl.ds(0, BAND_LEFT), :] = jnp.zeros((BAND_LEFT, HEAD), kn_scr.dtype)
    vb_scr[pl.ds(0, BAND_LEFT), :] = jnp.zeros((BAND_LEFT, HEAD), vb_scr.dtype)
    rb = 512

    def fill(r, carry):
        rows = pl.ds(pl.multiple_of(r * rb, rb), rb)
        dst = pl.ds(pl.multiple_of(BAND_LEFT + r * rb, rb), rb)
        k = k_ref[rows, :]
        rk = lax.rsqrt(jnp.mean(k * k, axis=-1, keepdims=True) + EPS)
        kn_scr[dst, :] = ((k * rk) * gk).astype(kn_scr.dtype)
        vb_scr[dst, :] = v_ref[rows, :].astype(vb_scr.dtype)
        return carry

    lax.fori_loop(0, T // rb, fill, 0)


def _attn_probs(c, q_ref, gq, bias_ref, kn_scr):
    q = q_ref[...]
    rq = lax.rsqrt(jnp.mean(q * q, axis=-1, keepdims=True) + EPS)
    qn = (q * rq) * gq
    band = pl.ds(pl.multiple_of(c * ATT_QB, ATT_QB), ATT_BAND)
    kb = kn_scr[band, :]
    s = dot_nt(qn, kb) * (HEAD ** -0.5) + bias_ref[0]
    pos = c * ATT_QB - BAND_LEFT + lax.broadcasted_iota(jnp.int32, (ATT_QB, ATT_BAND), 1)
    s = jnp.where(pos >= 0, s, MASK_VALUE)
    m = jnp.max(s, axis=-1, keepdims=True)
    e = jnp.exp(s - m)
    p = e / jnp.sum(e, axis=-1, keepdims=True)
    return q, rq, qn, kb, p


def attn_fwd(qkv, gq, gk, bias, name):
    T, D3 = qkv.shape
    D = D3 // 3
    H = D // HEAD
    NC = T // ATT_QB

    def body(q_ref, k_ref, v_ref, gq_ref, gk_ref, bias_ref, o_ref, kn_scr, vb_scr):
        c = pl.program_id(1)

        @pl.when(c == 0)
        def _():
            _attn_fill(k_ref, v_ref, gk_ref[...], kn_scr, vb_scr, T)

        _, _, _, _, p = _attn_probs(c, q_ref, gq_ref[...], bias_ref, kn_scr)
        band = pl.ds(pl.multiple_of(c * ATT_QB, ATT_QB), ATT_BAND)
        o_ref[...] = dot_nn(p, vb_scr[band, :]).astype(o_ref.dtype)

    vec = BS((1, HEAD), lambda h, c: (0, 0))
    return pl.pallas_call(
        body, grid=(H, NC),
        in_specs=[BS((ATT_QB, HEAD), lambda h, c: (c, h)), BS((T, HEAD), lambda h, c: (0, H + h)),
                  BS((T, HEAD), lambda h, c: (0, 2 * H + h)), vec, vec,
                  BS((1, ATT_QB, ATT_BAND), lambda h, c: (h, 0, 0))],
        out_specs=BS((ATT_QB, HEAD), lambda h, c: (c, h)), out_shape=S((T, D), MXU_DTYPE),
        scratch_shapes=[pltpu.VMEM((T + BAND_LEFT, HEAD), MXU_DTYPE)] * 2,
        compiler_params=_cp(("arbitrary", "arbitrary"), 48), name=name)(qkv, qkv, qkv, gq, gk, bias)


def attn_bwd(qkv, do, gq, gk, bias, name):
    T, D3 = qkv.shape
    D = D3 // 3
    H = D // HEAD
    NC = T // ATT_QB
    scale = HEAD ** -0.5

    def body(q_ref, k_ref, v_ref, do_ref, gq_ref, gk_ref, bias_ref,
             dq_ref, dk_ref, dv_ref, dgq_ref, dgk_ref, dbias_ref, kn_scr, vb_scr, dkn_acc, dv_acc):
        h = pl.program_id(0)
        c = pl.program_id(1)
        gq = gq_ref[...]
        gk = gk_ref[...]

        @pl.when(c == 0)
        def _():
            _attn_fill(k_ref, v_ref, gk, kn_scr, vb_scr, T)
            dkn_acc[...] = jnp.zeros_like(dkn_acc)
            dv_acc[...] = jnp.zeros_like(dv_acc)
            dbias_ref[...] = jnp.zeros_like(dbias_ref)

        @pl.when(jnp.logical_and(c == 0, h == 0))
        def _():
            dgq_ref[...] = jnp.zeros_like(dgq_ref)
            dgk_ref[...] = jnp.zeros_like(dgk_ref)

        q, rq, qn, kb, p = _attn_probs(c, q_ref, gq, bias_ref, kn_scr)
        band = pl.ds(pl.multiple_of(c * ATT_QB, ATT_QB), ATT_BAND)
        dov = do_ref[...]
        dv_acc[band, :] += dot_tn(p, dov)
        dp = dot_nt(dov, vb_scr[band, :])
        ds = p * (dp - jnp.sum(dp * p, axis=-1, keepdims=True))
        dbias_ref[0] += ds
        dss = ds * scale
        dqn = dot_nn(dss, kb)
        dkn_acc[band, :] += dot_tn(dss, qn)
        xhat = q * rq
        dgq_ref[...] += jnp.sum(dqn * xhat, axis=0, keepdims=True)
        dxhat = dqn * gq
        dq = rq * (dxhat - xhat * jnp.mean(dxhat * xhat, axis=-1, keepdims=True))
        dq_ref[...] = dq.astype(dq_ref.dtype)

        @pl.when(c == NC - 1)
        def _():
            rb = 512

            def fin(r, carry):
                rows = pl.ds(pl.multiple_of(r * rb, rb), rb)
                src = pl.ds(pl.multiple_of(BAND_LEFT + r * rb, rb), rb)
                k = k_ref[rows, :]
                rk = lax.rsqrt(jnp.mean(k * k, axis=-1, keepdims=True) + EPS)
                khat = k * rk
                dkn = dkn_acc[src, :]
                dgk_ref[...] += jnp.sum(dkn * khat, axis=0, keepdims=True)
                dkh = dkn * gk
                dk = rk * (dkh - khat * jnp.mean(dkh * khat, axis=-1, keepdims=True))
                dk_ref[rows, :] = dk.astype(dk_ref.dtype)
                dv_ref[rows, :] = dv_acc[src, :].astype(dv_ref.dtype)
                return carry

            lax.fori_loop(0, T // rb, fin, 0)

    vec = BS((1, HEAD), lambda h, c: (0, 0))
    qblk = BS((ATT_QB, HEAD), lambda h, c: (c, h))
    col = BS((T, HEAD), lambda h, c: (0, h))
    bblk = BS((1, ATT_QB, ATT_BAND), lambda h, c: (h, 0, 0))
    return pl.pallas_call(
        body, grid=(H, NC),
        in_specs=[qblk, BS((T, HEAD), lambda h, c: (0, H + h)), BS((T, HEAD), lambda h, c: (0, 2 * H + h)),
                  qblk, vec, vec, bblk],
        out_specs=[qblk, col, col, vec, vec, bblk],
        out_shape=[S((T, D), MXU_DTYPE)] * 3 + [S((1, HEAD), F32)] * 2 + [S((H, ATT_QB, ATT_BAND), F32)],
        scratch_shapes=[pltpu.VMEM((T + BAND_LEFT, HEAD), MXU_DTYPE)] * 2
        + [pltpu.VMEM((T + BAND_LEFT, HEAD), F32)] * 2,
        compiler_params=_cp(("arbitrary", "arbitrary"), 56), name=name)(qkv, qkv, qkv, do, gq, gk, bias)


def _rel_onehot(qi, num_rel):
    kk = lax.broadcasted_iota(jnp.int32, (BAND, num_rel), 0)
    rr = lax.broadcasted_iota(jnp.int32, (BAND, num_rel), 1)
    idx = jnp.clip(BAND_LEFT + qi - kk, -(CHUNK - 1), MAX_REL) + (CHUNK - 1)
    return (idx == rr).astype(F32)


def rel_bias_expand(table, name):
    H, num_rel = table.shape

    def body(t_ref, o_ref):
        for qi in range(CHUNK):
            o_ref[qi] = dot_nt(t_ref[...], _rel_onehot(qi, num_rel), hi=True)

    return pl.pallas_call(body, out_shape=S((CHUNK, H, BAND), F32), name=name,
                          compiler_params=pltpu.CompilerParams(vmem_limit_bytes=40 * MIB))(table)


def rel_bias_reduce(dbias_t, num_rel, name):
    H = dbias_t.shape[1]

    def body(d_ref, o_ref):
        acc = jnp.zeros((H, num_rel), F32)
        for qi in range(CHUNK):
            acc += dot_nn(d_ref[qi], _rel_onehot(qi, num_rel), hi=True)
        o_ref[...] = acc

    return pl.pallas_call(body, out_shape=S((H, num_rel), F32), name=name,
                          compiler_params=pltpu.CompilerParams(vmem_limit_bytes=40 * MIB))(dbias_t)


def pool_fwd(h, x, w, scale, name):
    T, D = h.shape
    G = len(POOL_WINDOWS)
    Dg = D // G
    bt = _pick(T, (256, 128, 64))

    def body(h_ref, halo_ref, x_ref, w_ref, s_ref, o_ref, p_ref, ext):
        i = pl.program_id(0)
        ext[pl.ds(0, POOL_HALO), :] = jnp.where(i > 0, halo_ref[...], 0.0)
        ext[pl.ds(POOL_HALO, bt), :] = h_ref[...]
        t = i * bt + lax.broadcasted_iota(jnp.int32, (bt, 1), 0)
        for g, win in enumerate(POOL_WINDOWS):
            cols = pl.ds(g * Dg, Dg)
            acc = ext[pl.ds(POOL_HALO, bt), cols]
            for j in range(1, win):
                acc += ext[pl.ds(POOL_HALO - j, bt), cols]
            count = jnp.minimum(t + 1, win).astype(F32)
            pooled = acc / count - h_ref[:, cols]
            p_ref[:, cols] = pooled.astype(p_ref.dtype)
            y = dot_nn(pooled, w_ref[g]) * s_ref[:, cols]
            o_ref[:, cols] = x_ref[:, cols] + y

    row = BS((bt, D), lambda i: (i, 0))
    return pl.pallas_call(
        body, grid=(T // bt,),
        in_specs=[row, BS((POOL_HALO, D), lambda i: (jnp.maximum(i * (bt // POOL_HALO) - 1, 0), 0)), row,
                  BS((G, Dg, Dg), lambda i: (0, 0, 0)), BS((1, D), lambda i: (0, 0))],
        out_specs=[row, row], out_shape=[S((T, D), F32), S((T, D), MXU_DTYPE)],
        scratch_shapes=[pltpu.VMEM((bt + POOL_HALO, D), F32)],
        compiler_params=_cp(("arbitrary",)), name=name)(h, h, x, w, scale)


def pool_bwd_a(dy, pooled, w, scale, name):
    T, D = dy.shape
    G = len(POOL_WINDOWS)
    Dg = D // G
    bt = _pick(T, (256, 128, 64))

    def body(dy_ref, p_ref, w_ref, s_ref, dp_ref, dw_ref, ds_ref):
        i = pl.program_id(0)

        @pl.when(i == 0)
        def _():
            dw_ref[...] = jnp.zeros_like(dw_ref)
            ds_ref[...] = jnp.zeros_like(ds_ref)

        for g in range(G):
            cols = pl.ds(g * Dg, Dg)
            pg = p_ref[:, cols]
            dyg = dy_ref[:, cols]
            ypre = dot_nn(pg, w_ref[g])
            ds_ref[:, cols] += jnp.sum(dyg * ypre, axis=0, keepdims=True)
            dys = dyg * s_ref[:, cols]
            dp_ref[:, cols] = dot_nt(dys, w_ref[g])
            dw_ref[g] += dot_tn(pg, dys)

    row = BS((bt, D), lambda i: (i, 0))
    wspec = BS((G, Dg, Dg), lambda i: (0, 0, 0))
    vec = BS((1, D), lambda i: (0, 0))
    return pl.pallas_call(
        body, grid=(T // bt,), in_specs=[row, row, wspec, vec], out_specs=[row, wspec, vec],
        out_shape=[S((T, D), F32), S((G, Dg, Dg), F32), S((1, D), F32)],
        compiler_params=_cp(("arbitrary",)), name=name)(dy, pooled, w, scale)


def pool_bwd_b(dpooled, name):
    T, D = dpooled.shape
    G = len(POOL_WINDOWS)
    Dg = D // G
    bt = _pick(T, (256, 128, 64))
    nt = T // bt

    def body(d_ref, halo_ref, o_ref, ext):
        i = pl.program_id(0)
        t = i * bt + lax.broadcasted_iota(jnp.int32, (bt, 1), 0)
        for g, win in enumerate(POOL_WINDOWS):
            cols = pl.ds(g * Dg, Dg)
            count = jnp.minimum(t + 1, win).astype(F32)
            ext[pl.ds(0, bt), cols] = d_ref[:, cols] / count
            ext[pl.ds(bt, POOL_HALO), cols] = jnp.where(i < nt - 1, halo_ref[:, cols] * (1.0 / win), 0.0)
            acc = ext[pl.ds(0, bt), cols]
            for j in range(1, win):
                acc += ext[pl.ds(j, bt), cols]
            o_ref[:, cols] = acc - d_ref[:, cols]

    row = BS((bt, D), lambda i: (i, 0))
    nxt = BS((POOL_HALO, D), lambda i: (jnp.minimum((i + 1) * (bt // POOL_HALO), T // POOL_HALO - 1), 0))
    return pl.pallas_call(
        body, grid=(nt,), in_specs=[row, nxt], out_specs=row, out_shape=S((T, D), F32),
        scratch_shapes=[pltpu.VMEM((bt + POOL_HALO, D), F32)],
        compiler_params=_cp(("arbitrary",)), name=name)(dpooled, dpooled)


def _gdn_u(i, x_ref, halo_ref, w_ref, ext, bt):
    ext[pl.ds(0, SUBLANES), :] = jnp.where(i > 0, halo_ref[...], 0.0)
    ext[pl.ds(SUBLANES, bt), :] = x_ref[...]
    u = w_ref[3:4, :] * ext[pl.ds(SUBLANES, bt), :]
    for j in range(GDN_CONV - 1):
        u += w_ref[j:j + 1, :] * ext[pl.ds(SUBLANES - (GDN_CONV - 1) + j, bt), :]
    return u


def gdn_pre_fwd(proj, conv_w, key_dim, name):
    T = proj.shape[0]
    C = conv_w.shape[1]
    cb = min(1024, key_dim)
    nq, nqk, J = key_dim // cb, 2 * key_dim // cb, C // cb
    bt = _pick(T, (256, 128, 64))

    def body(x_ref, halo_ref, w_ref, o_ref, ext):
        i, j = pl.program_id(0), pl.program_id(1)
        u = _gdn_u(i, x_ref, halo_ref, w_ref, ext, bt)
        s = u * _sigmoid(u)

        @pl.when(j < nqk)
        def _():
            sc = jnp.where(j < nq, HEAD ** -0.5, 1.0)
            for hh in range(cb // HEAD):
                cols = pl.ds(hh * HEAD, HEAD)
                blk = s[:, hh * HEAD:(hh + 1) * HEAD]
                r = lax.rsqrt(jnp.sum(blk * blk, axis=-1, keepdims=True) + EPS)
                o_ref[:, cols] = (blk * r) * sc

        @pl.when(j >= nqk)
        def _():
            o_ref[...] = s

    return pl.pallas_call(
        body, grid=(T // bt, J),
        in_specs=[BS((bt, cb), lambda i, j: (i, j)),
                  BS((SUBLANES, cb), lambda i, j: (jnp.maximum(i * (bt // SUBLANES) - 1, 0), j)),
                  BS((SUBLANES, cb), lambda i, j: (0, j))],
        out_specs=BS((bt, cb), lambda i, j: (i, j)), out_shape=S((T, C), F32),
        scratch_shapes=[pltpu.VMEM((bt + SUBLANES, cb), F32)],
        compiler_params=_cp(("arbitrary", "arbitrary")), name=name)(proj, proj, conv_w)


def gdn_pre_bwd_a(proj, conv_w, dq_v, dk_v, dv, key_dim, name):
    T = proj.shape[0]
    C = conv_w.shape[1]
    cb = min(1024, key_dim)
    nq, nqk, J = key_dim // cb, 2 * key_dim // cb, C // cb
    nv = J - nqk
    bt = _pick(T, (256, 128, 64))

    def body(x_ref, halo_ref, w_ref, dq_ref, dk_ref, dv_ref, du_ref, dw_ref, ext, ds_scr):
        j, i = pl.program_id(0), pl.program_id(1)
        u = _gdn_u(i, x_ref, halo_ref, w_ref, ext, bt)
        sg = _sigmoid(u)
        s = u * sg

        @pl.when(j < nqk)
        def _():
            sc = jnp.where(j < nq, HEAD ** -0.5, 1.0)
            for hh in range(cb // HEAD):
                lo = 2 * hh * HEAD
                dq2 = dq_ref[:, lo:lo + HEAD] + dq_ref[:, lo + HEAD:lo + 2 * HEAD]
                dk2 = dk_ref[:, lo:lo + HEAD] + dk_ref[:, lo + HEAD:lo + 2 * HEAD]
                dn = jnp.where(j < nq, dq2, dk2)
                blk = s[:, hh * HEAD:(hh + 1) * HEAD]
                r = lax.rsqrt(jnp.sum(blk * blk, axis=-1, keepdims=True) + EPS)
                shat = blk * r
                ds_scr[:, pl.ds(hh * HEAD, HEAD)] = (sc * r) * (dn - shat * jnp.sum(dn * shat, axis=-1, keepdims=True))

        @pl.when(j >= nqk)
        def _():
            ds_scr[...] = dv_ref[...]

        du = ds_scr[...] * (sg * (1.0 + u * (1.0 - sg)))
        du_ref[...] = du

        @pl.when(i == 0)
        def _():
            dw_ref[...] = jnp.zeros_like(dw_ref)

        for k in range(GDN_CONV):
            shifted = ext[pl.ds(SUBLANES - (GDN_CONV - 1) + k, bt), :]
            dw_ref[k:k + 1, :] += jnp.sum(du * shifted, axis=0, keepdims=True)

    blk = BS((bt, cb), lambda j, i: (i, j))
    return pl.pallas_call(
        body, grid=(J, T // bt),
        in_specs=[blk, BS((SUBLANES, cb), lambda j, i: (jnp.maximum(i * (bt // SUBLANES) - 1, 0), j)),
                  BS((SUBLANES, cb), lambda j, i: (0, j)),
                  BS((bt, 2 * cb), lambda j, i: (i, jnp.minimum(j, nq - 1))),
                  BS((bt, 2 * cb), lambda j, i: (i, jnp.clip(j - nq, 0, nq - 1))),
                  BS((bt, cb), lambda j, i: (i, jnp.clip(j - nqk, 0, nv - 1)))],
        out_specs=[blk, BS((SUBLANES, cb), lambda j, i: (0, j))],
        out_shape=[S((T, C), F32), S((SUBLANES, C), F32)],
        scratch_shapes=[pltpu.VMEM((bt + SUBLANES, cb), F32), pltpu.VMEM((bt, cb), F32)],
        compiler_params=_cp(("arbitrary", "arbitrary")), name=name)(proj, proj, conv_w, dq_v, dk_v, dv)


def gdn_pre_bwd_b(du, dgate, conv_w, key_dim, name):
    T, C = du.shape
    V = dgate.shape[1]
    cb = min(1024, key_dim)
    J = C // cb
    J2 = (C + V) // cb
    bt = _pick(T, (256, 128, 64))
    nt = T // bt

    def body(du_ref, halo_ref, w_ref, dg_ref, o_ref, ext):
        i, j = pl.program_id(0), pl.program_id(1)

        @pl.when(j < J)
        def _():
            ext[pl.ds(0, bt), :] = du_ref[...]
            ext[pl.ds(bt, SUBLANES), :] = jnp.where(i < nt - 1, halo_ref[...], 0.0)
            da = w_ref[3:4, :] * ext[pl.ds(0, bt), :]
            for k in range(GDN_CONV - 1):
                da += w_ref[k:k + 1, :] * ext[pl.ds(GDN_CONV - 1 - k, bt), :]
            o_ref[...] = da.astype(o_ref.dtype)

        @pl.when(j >= J)
        def _():
            o_ref[...] = dg_ref[...]

    jc = lambda j: jnp.minimum(j, J - 1)
    return pl.pallas_call(
        body, grid=(nt, J2),
        in_specs=[BS((bt, cb), lambda i, j: (i, jc(j))),
                  BS((SUBLANES, cb), lambda i, j: (jnp.minimum((i + 1) * (bt // SUBLANES), T // SUBLANES - 1), jc(j))),
                  BS((SUBLANES, cb), lambda i, j: (0, jc(j))),
                  BS((bt, cb), lambda i, j: (i, jnp.maximum(j - J, 0)))],
        out_specs=BS((bt, cb), lambda i, j: (i, j)), out_shape=S((T, C + V), MXU_DTYPE),
        scratch_shapes=[pltpu.VMEM((bt + SUBLANES, cb), F32)],
        compiler_params=_cp(("arbitrary", "arbitrary")), name=name)(du, du, conv_w, dgate)


def _softplus(x):
    return jnp.maximum(x, 0.0) + jnp.log1p(jnp.exp(-jnp.abs(x)))


def gdn_gate_fwd(a, b, a_log, dt_bias, name):
    T, HV = a.shape
    bt = _pick(T, (1024, 512, 256, 128, 64))

    def body(a_ref, b_ref, al_ref, dt_ref, g_ref, be_ref):
        g_ref[...] = -jnp.exp(al_ref[...]) * _softplus(a_ref[...] + dt_ref[...])
        be_ref[...] = _sigmoid(b_ref[...])

    row = BS((bt, HV), lambda i: (i, 0))
    vec = BS((1, HV), lambda i: (0, 0))
    return pl.pallas_call(body, grid=(T // bt,), in_specs=[row, row, vec, vec], out_specs=[row, row],
                          out_shape=[S((T, HV), F32)] * 2, compiler_params=_cp(("parallel",)), name=name)(
                              a, b, a_log, dt_bias)


def gdn_gate_bwd(a, b, a_log, dt_bias, dg, dbeta, name):
    T, HV = a.shape
    bt = _pick(T, (1024, 512, 256, 128, 64))

    def body(a_ref, b_ref, al_ref, dt_ref, dg_ref, dbe_ref, da_ref, db_ref, dal_ref, ddt_ref):
        i = pl.program_id(0)
        x = a_ref[...] + dt_ref[...]
        ea = jnp.exp(al_ref[...])
        dgv = dg_ref[...]
        da = dgv * (-ea * _sigmoid(x))
        da_ref[...] = da
        be = _sigmoid(b_ref[...])
        db_ref[...] = dbe_ref[...] * be * (1.0 - be)

        @pl.when(i == 0)
        def _():
            dal_ref[...] = jnp.zeros_like(dal_ref)
            ddt_ref[...] = jnp.zeros_like(ddt_ref)

        dal_ref[...] += jnp.sum(dgv * (-ea * _softplus(x)), axis=0, keepdims=True)
        ddt_ref[...] += jnp.sum(da, axis=0, keepdims=True)

    row = BS((bt, HV), lambda i: (i, 0))
    vec = BS((1, HV), lambda i: (0, 0))
    return pl.pallas_call(body, grid=(T // bt,), in_specs=[row, row, vec, vec, row, row],
                          out_specs=[row, row, vec, vec],
                          out_shape=[S((T, HV), F32)] * 2 + [S((1, HV), F32)] * 2,
                          compiler_params=_cp(("arbitrary",)), name=name)(a, b, a_log, dt_bias, dg, dbeta)


def _col(row_vec, eye):
    return jnp.sum(jnp.where(eye, row_vec, 0.0), axis=1, keepdims=True)


def _row(col_vec, eye):
    return jnp.sum(jnp.where(eye, col_vec, 0.0), axis=0, keepdims=True)


def _each(f, *lists):
    return [f(*args) for args in zip(*lists)]


def _mul(a, b):
    return a * b


def _hdot_nn(a, b):
    return dot_nn(a, b, hi="split")


def _delta_chunk(q, k, v, g_row, b_row, rep, tinv=None):
    C = CHUNK
    ii = lax.broadcasted_iota(jnp.int32, (C, C), 0)
    jj = lax.broadcasted_iota(jnp.int32, (C, C), 1)
    eye, causal, strict = ii == jj, ii >= jj, ii > jj
    q_v = [q[p // rep] for p in range(len(v))]
    k_v = [k[p // rep] for p in range(len(v))]
    g_col = _each(lambda g: _col(g, eye), g_row)
    gc_row = _each(lambda g: jnp.sum(jnp.where(ii <= jj, g, 0.0), axis=0, keepdims=True), g_col)
    gc_col = _each(lambda g: _col(g, eye), gc_row)
    gl = _each(lambda g: jnp.sum(jnp.where(jj[0:1, :] == C - 1, g, 0.0), axis=1, keepdims=True), gc_row)
    decay = _each(lambda gc, gr: jnp.where(causal, jnp.exp(jnp.where(causal, gc - gr, 0.0)), 0.0), gc_col, gc_row)
    b_col = _each(lambda b: _col(b, eye), b_row)
    kb = _each(_mul, k_v, b_col)
    vb = _each(_mul, v, b_col)
    m = _each(dot_nt, kb, k_v)
    a = _each(lambda m_, d_: jnp.where(strict, m_ * d_, 0.0), m, decay)
    if tinv is None:
        ident = jnp.where(eye, 1.0, 0.0)
        tinv = [ident - a_ for a_ in a]
        pw = _each(_hdot_nn, a, a)
        for step in range(5):
            tinv = _each(lambda t_, p_: t_ + _hdot_nn(t_, p_), tinv, pw)
            if step < 4:
                pw = _each(_hdot_nn, pw, pw)
    egc = _each(jnp.exp, gc_col)
    kbg = _each(_mul, kb, egc)
    u = _each(_hdot_nn, tinv, vb)
    w = _each(_hdot_nn, tinv, kbg)
    n_k = _each(dot_nt, q, k)
    n = [n_k[p // rep] for p in range(len(v))]
    attn = _each(lambda n_, d_: jnp.where(causal, n_ * d_, 0.0), n, decay)
    qg = _each(_mul, q_v, egc)
    ekl = _each(lambda l_, c_: jnp.exp(l_ - c_), gl, gc_col)
    ks = _each(_mul, k_v, ekl)
    dec = _each(jnp.exp, gl)
    return dict(eye=eye, causal=causal, strict=strict, ii=ii, jj=jj, q=q_v, k=k_v, gc_col=gc_col, gl=gl, decay=decay,
                b_col=b_col, kb=kb, vb=vb, m=m, tinv=tinv, egc=egc, kbg=kbg, u=u, w=w, n=n, attn=attn,
                qg=qg, ekl=ekl, ks=ks, dec=dec)


def delta_fwd(qkvn, g_rows, b_rows, key_dim, name):
    T = qkvn.shape[0]
    NK = key_dim // HEAD
    HV = g_rows.shape[0]
    rep = HV // NK
    NC = T // CHUNK

    P = DELTA_HEADS
    kw, vw = HEAD * P // rep, HEAD * P

    def body(q_ref, k_ref, v_ref, g_ref, b_ref, o_ref, st_ref, ti_ref, state):
        n = pl.program_id(1)

        @pl.when(n == 0)
        def _():
            state[...] = jnp.zeros_like(state)

        heads = range(P)
        q = [q_ref[:, pl.ds(kh * HEAD, HEAD)] for kh in range(P // rep)]
        k = [k_ref[:, pl.ds(kh * HEAD, HEAD)] for kh in range(P // rep)]
        v = [v_ref[:, pl.ds(p * HEAD, HEAD)] for p in heads]
        s0 = [state[p] for p in heads]
        c = _delta_chunk(q, k, v, [g_ref[p, 0] for p in heads], [b_ref[p, 0] for p in heads], rep)
        vn = _each(lambda u_, w_, s_: u_ - dot_nn(w_, s_), c["u"], c["w"], s0)
        o = _each(lambda qg_, s_, at_, vn_: dot_nn(qg_, s_) + dot_nn(at_, vn_), c["qg"], s0, c["attn"], vn)
        s1 = _each(lambda s_, d_, ks_, vn_: s_ * d_ + dot_tn(ks_, vn_), s0, c["dec"], c["ks"], vn)
        for p in heads:
            st_ref[p, 0] = s0[p]
            ti_ref[p, 0] = c["tinv"][p]
            o_ref[:, pl.ds(p * HEAD, HEAD)] = o[p]
            state[p] = s1[p]

    vrow = BS((P, 1, 1, CHUNK), lambda h, n: (h, n, 0, 0))
    return pl.pallas_call(
        body, grid=(HV // P, NC),
        in_specs=[BS((CHUNK, kw), lambda h, n: (n, h)),
                  BS((CHUNK, kw), lambda h, n: (n, key_dim // kw + h)),
                  BS((CHUNK, vw), lambda h, n: (n, 2 * key_dim // vw + h)), vrow, vrow],
        out_specs=[BS((CHUNK, vw), lambda h, n: (n, h)), BS((P, 1, HEAD, HEAD), lambda h, n: (h, n, 0, 0)),
                   BS((P, 1, CHUNK, CHUNK), lambda h, n: (h, n, 0, 0))],
        out_shape=[S((T, HV * HEAD), F32), S((HV, NC, HEAD, HEAD), F32), S((HV, NC, CHUNK, CHUNK), F32)],
        scratch_shapes=[pltpu.VMEM((P, HEAD, HEAD), F32)],
        compiler_params=_cp(("arbitrary", "arbitrary")), name=name)(qkvn, qkvn, qkvn, g_rows, b_rows)


def delta_bwd(qkvn, g_rows, b_rows, states, tinvs, do, key_dim, name):
    T = qkvn.shape[0]
    NK = key_dim // HEAD
    HV = g_rows.shape[0]
    rep = HV // NK
    NC = T // CHUNK
    P = DELTA_HEADS
    kw, vw = HEAD * P // rep, HEAD * P

    def body(q_ref, k_ref, v_ref, g_ref, b_ref, st_ref, ti_ref, do_ref, dq_ref, dk_ref, dv_ref, dg_ref, dbe_ref,
             dstate):
        step = pl.program_id(1)

        @pl.when(step == 0)
        def _():
            dstate[...] = jnp.zeros_like(dstate)

        heads = range(P)
        q_k = [q_ref[:, pl.ds(kh * HEAD, HEAD)] for kh in range(P // rep)]
        k_k = [k_ref[:, pl.ds(kh * HEAD, HEAD)] for kh in range(P // rep)]
        v = [v_ref[:, pl.ds(p * HEAD, HEAD)] for p in heads]
        s0 = [st_ref[p, 0] for p in heads]
        dsn = [dstate[p] for p in heads]
        dov = [do_ref[:, pl.ds(p * HEAD, HEAD)] for p in heads]
        c = _delta_chunk(q_k, k_k, v, [g_ref[p, 0] for p in heads], [b_ref[p, 0] for p in heads], rep,
                         tinv=[ti_ref[p, 0] for p in heads])
        eye, causal, strict = c["eye"], c["causal"], c["strict"]
        q, k, tinv, decay = c["q"], c["k"], c["tinv"], c["decay"]

        def rsum(a, b):
            return jnp.sum(a * b, axis=1, keepdims=True)

        vn = _each(lambda u_, w_, s_: u_ - dot_nn(w_, s_), c["u"], c["w"], s0)
        dvn = _each(lambda at_, do_, ks_, ds_: dot_tn(at_, do_) + dot_nn(ks_, ds_), c["attn"], dov, c["ks"], dsn)
        dattn = _each(lambda do_, vn_: jnp.where(causal, dot_nt(do_, vn_), 0.0), dov, vn)
        dqg = _each(dot_nt, dov, s0)
        dks = _each(dot_nt, vn, dsn)
        ddec = _each(lambda s_, ds_: jnp.sum(rsum(s_, ds_), axis=0, keepdims=True), s0, dsn)
        dw = _each(lambda dvn_, s_: -dot_nt(dvn_, s_), dvn, s0)
        ds0 = _each(lambda qg_, do_, ds_, d_, w_, dvn_: dot_tn(qg_, do_) + ds_ * d_ - dot_tn(w_, dvn_),
                    c["qg"], dov, dsn, c["dec"], c["w"], dvn)
        dvb = _each(lambda t_, x_: dot_tn(t_, x_, hi="split"), tinv, dvn)
        dkbg = _each(lambda t_, x_: dot_tn(t_, x_, hi="split"), tinv, dw)
        dt = _each(lambda dvn_, vb_, dw_, kbg_: dot_nt(dvn_, vb_, hi="split") + dot_nt(dw_, kbg_, hi="split"),
                   dvn, c["vb"], dw, c["kbg"])
        dtt = _each(lambda dt_, t_: dot_nt(dt_, t_, hi="split"), dt, tinv)
        da = _each(lambda t_, x_: jnp.where(strict, -dot_tn(t_, x_, hi="split"), 0.0), tinv, dtt)
        dm = _each(_mul, da, decay)
        dn = _each(_mul, dattn, decay)
        e = _each(lambda da_, m_, dat_, n_, d_: (da_ * m_ + dat_ * n_) * d_, da, c["m"], dattn, c["n"], decay)
        dkb = _each(lambda dm_, k_, dkbg_, egc_: dot_nn(dm_, k_) + dkbg_ * egc_, dm, k, dkbg, c["egc"])
        dk = _each(lambda dm_, kb_, dn_, q_, dks_, ekl_, dkb_, b_: dot_tn(dm_, kb_) + dot_tn(dn_, q_) + dks_ * ekl_
                   + dkb_ * b_, dm, c["kb"], dn, q, dks, c["ekl"], dkb, c["b_col"])
        dq = _each(lambda dn_, k_, dqg_, egc_: dot_nn(dn_, k_) + dqg_ * egc_, dn, k, dqg, c["egc"])
        dks_ks = _each(rsum, dks, c["ks"])
        dgc_col = _each(lambda e_, dkbg_, kbg_, dqg_, qg_, x_: jnp.sum(e_, axis=1, keepdims=True) + rsum(dkbg_, kbg_)
                        + rsum(dqg_, qg_) - x_ - _col(jnp.sum(e_, axis=0, keepdims=True), eye),
                        e, dkbg, c["kbg"], dqg, c["qg"], dks_ks)
        dgl = _each(lambda x_, dd_, d_: jnp.sum(x_, axis=0, keepdims=True) + dd_ * d_, dks_ks, ddec, c["dec"])
        last = c["ii"][:, 0:1] == CHUNK - 1
        dgc_col = _each(lambda g_, l_: g_ + jnp.where(last, l_, 0.0), dgc_col, dgl)
        dbe_col = _each(lambda dvb_, v_, dkb_, k_: rsum(dvb_, v_) + rsum(dkb_, k_), dvb, v, dkb, k)
        for p in heads:
            vc = pl.ds(p * HEAD, HEAD)
            dstate[p] = ds0[p]
            dq_ref[:, vc] = dq[p]
            dk_ref[:, vc] = dk[p]
            dv_ref[:, vc] = dvb[p] * c["b_col"][p]
            dbe_ref[p, 0] = _row(dbe_col[p], eye)
            dg_ref[p, 0] = jnp.sum(jnp.where(causal, dgc_col[p], 0.0), axis=0, keepdims=True)

    rev = lambda n: NC - 1 - n
    vrow = BS((P, 1, 1, CHUNK), lambda h, n: (h, rev(n), 0, 0))
    vblk = BS((CHUNK, vw), lambda h, n: (rev(n), h))
    return pl.pallas_call(
        body, grid=(HV // P, NC),
        in_specs=[BS((CHUNK, kw), lambda h, n: (rev(n), h)),
                  BS((CHUNK, kw), lambda h, n: (rev(n), key_dim // kw + h)),
                  BS((CHUNK, vw), lambda h, n: (rev(n), 2 * key_dim // vw + h)), vrow, vrow,
                  BS((P, 1, HEAD, HEAD), lambda h, n: (h, rev(n), 0, 0)),
                  BS((P, 1, CHUNK, CHUNK), lambda h, n: (h, rev(n), 0, 0)), vblk],
        out_specs=[vblk, vblk, vblk, vrow, vrow],
        out_shape=[S((T, HV * HEAD), F32)] * 3 + [S((HV, NC, 1, CHUNK), F32)] * 2,
        scratch_shapes=[pltpu.VMEM((P, HEAD, HEAD), F32)],
        compiler_params=_cp(("arbitrary", "arbitrary")), name=name)(
            qkvn, qkvn, qkvn, g_rows, b_rows, states, tinvs, do)


def gdn_post_fwd(o, proj, gate_col0, o_gain, name):
    T, V = o.shape
    cb = min(1024, V)
    j0 = gate_col0 // cb
    bt = _pick(T, (256, 128, 64))

    def body(o_ref, g_ref, gain_ref, y_ref):
        for hh in range(cb // HEAD):
            cols = pl.ds(hh * HEAD, HEAD)
            ov = o_ref[:, cols]
            gt = g_ref[:, cols]
            r = lax.rsqrt(jnp.mean(ov * ov, axis=-1, keepdims=True) + EPS)
            y_ref[:, cols] = (((ov * r) * gain_ref[...]) * (gt * _sigmoid(gt))).astype(y_ref.dtype)

    return pl.pallas_call(
        body, grid=(T // bt, V // cb),
        in_specs=[BS((bt, cb), lambda i, j: (i, j)), BS((bt, cb), lambda i, j: (i, j0 + j)),
                  BS((1, HEAD), lambda i, j: (0, 0))],
        out_specs=BS((bt, cb), lambda i, j: (i, j)), out_shape=S((T, V), MXU_DTYPE),
        compiler_params=_cp(("parallel", "parallel")), name=name)(o, proj, o_gain)


def gdn_post_bwd(o, proj, gate_col0, o_gain, dy, name):
    T, V = o.shape
    cb = min(1024, V)
    j0 = gate_col0 // cb
    bt = _pick(T, (256, 128, 64))

    def body(o_ref, g_ref, gain_ref, dy_ref, do_ref, dgt_ref, dgain_ref):
        i, j = pl.program_id(0), pl.program_id(1)

        @pl.when(jnp.logical_and(i == 0, j == 0))
        def _():
            dgain_ref[...] = jnp.zeros_like(dgain_ref)

        gain = gain_ref[...]
        for hh in range(cb // HEAD):
            cols = pl.ds(hh * HEAD, HEAD)
            ov = o_ref[:, cols]
            gt = g_ref[:, cols]
            dyv = dy_ref[:, cols]
            r = lax.rsqrt(jnp.mean(ov * ov, axis=-1, keepdims=True) + EPS)
            ohat = ov * r
            sg = _sigmoid(gt)
            dyn = dyv * (gt * sg)
            dgt_ref[:, cols] = (dyv * (ohat * gain) * (sg * (1.0 + gt * (1.0 - sg)))).astype(dgt_ref.dtype)
            dgain_ref[...] += jnp.sum(dyn * ohat, axis=0, keepdims=True)
            dh = dyn * gain
            do_ref[:, cols] = r * (dh - ohat * jnp.mean(dh * ohat, axis=-1, keepdims=True))

    blk = BS((bt, cb), lambda i, j: (i, j))
    vec = BS((1, HEAD), lambda i, j: (0, 0))
    return pl.pallas_call(
        body, grid=(T // bt, V // cb),
        in_specs=[blk, BS((bt, cb), lambda i, j: (i, j0 + j)), vec, blk],
        out_specs=[blk, blk, vec], out_shape=[S((T, V), F32), S((T, V), MXU_DTYPE), S((1, HEAD), F32)],
        compiler_params=_cp(("arbitrary", "arbitrary")), name=name)(o, proj, o_gain, dy)


def adamw(w, g, m, v, name):
    shape = w.shape
    n = 1
    for d in shape:
        n *= d
    cols = shape[-1] if len(shape) > 1 else n
    rows = n // cols
    br = rows
    for cand in (512, 256, 128, 64, 32, 16, 8):
        if rows % cand == 0 and cand * cols * 4 * 14 <= 36 * MIB:
            br = cand
            break
    c1 = 1.0 - ADAM_B1 ** ADAM_STEP
    c2 = 1.0 - ADAM_B2 ** ADAM_STEP

    def body(w_ref, g_ref, m_ref, v_ref, d_ref, nm_ref, nv_ref):
        gv = g_ref[...]
        nm = ADAM_B1 * m_ref[...] + (1.0 - ADAM_B1) * gv
        nv = ADAM_B2 * v_ref[...] + (1.0 - ADAM_B2) * (gv * gv)
        nm_ref[...] = nm
        nv_ref[...] = nv
        d_ref[...] = -ADAM_LR * ((nm / c1) / (jnp.sqrt(nv / c2) + ADAM_EPS) + ADAM_WD * w_ref[...])

    blk = BS((br, cols), lambda i: (i, 0))
    outs = pl.pallas_call(
        body, grid=(rows // br,), in_specs=[blk] * 4, out_specs=[blk] * 3,
        out_shape=[S((rows, cols), F32)] * 3, compiler_params=_cp(("parallel",), 48), name=name)(
            *[t.reshape(rows, cols) for t in (w, g, m, v)])
    return [t.reshape(shape) for t in outs]


def _with_job(job, *args, **kwargs):
    if job is None:
        return matmul(*args, **kwargs), []
    return matmul(*args, job=job, **kwargs)


def _ffn_fwd(x, gain, p, tag, jobs=(None, None)):
    (h2,) = norm_fwd(x, gain, [MXU_DTYPE], f"{tag}_norm")
    up, got_a = _with_job(jobs[0], h2, p["w_up_t"], "nt", out_dtype=MXU_DTYPE, name=f"{tag}_up")
    act = ffn_act_fwd(up, p["conv"], f"{tag}_act")
    out, got_b = _with_job(jobs[1], act, p["w_down"], "nn", res=x, name=f"{tag}_down")
    return out, (x, h2, up, act), got_a + got_b


def _ffn_bwd(dx, saved, gain, p, tag, first_job=None, then_jobs=None):
    x, h2, up, act = saved
    g_act, got = _with_job(first_job, dx[1], p["w_down"], "nt", out_dtype=MXU_DTYPE, name=f"{tag}_bdown")
    jobs = then_jobs(got) if then_jobs else (None, None)
    d_down = matmul(act, dx[1], "tn", name=f"{tag}_wdown")
    du, db, d_conv = ffn_act_bwd_a(up, p["conv"], g_act, f"{tag}_bact_a")
    dup = ffn_act_bwd_b(du, db, p["conv"], f"{tag}_bact_b")
    dh2, got_a = _with_job(jobs[0], dup, p["w_up_t"], "nn", name=f"{tag}_bup")
    d_up_t, got_b = _with_job(jobs[1], dup, h2, "tn", name=f"{tag}_wup")
    dx, d_gain = norm_bwd(x, gain, dh2, dx[0], f"{tag}_bnorm")
    return dx, d_gain, dict(w_up_t=d_up_t, conv=d_conv, w_down=d_down), got_a + got_b


def _joint_bias(bias):
    rows = [jnp.pad(bias, ((0, 0), (0, 0), (qc * CHUNK, (ATT_CHUNKS - 1 - qc) * CHUNK)), constant_values=MASK_VALUE)
            for qc in range(ATT_CHUNKS)]
    return jnp.concatenate(rows, axis=1)


def _joint_bias_grad(dbias):
    return sum(dbias[:, qc * CHUNK:(qc + 1) * CHUNK, qc * CHUNK:qc * CHUNK + BAND] for qc in range(ATT_CHUNKS))


def _att_fwd(x, gain, p, tag):
    (h,) = norm_fwd(x, gain, [MXU_DTYPE], f"{tag}_norm")
    qkv = matmul(h, p["w_qkv_t"], "nt", name=f"{tag}_qkv")
    bias = _joint_bias(rel_bias_expand(p["rel_bias"], f"{tag}_rel").transpose(1, 0, 2))
    o = attn_fwd(qkv, p["q_gain"], p["k_gain"], bias, f"{tag}_core")
    out = matmul(o, p["w_o"], "nn", res=x, name=f"{tag}_out")
    return out, (x, h, qkv, o, bias)


def _att_bwd(dx, saved, gain, p, tag):
    x, h, qkv, o, bias = saved
    do = matmul(dx[1], p["w_o"], "nt", name=f"{tag}_bout")
    d_wo = matmul(o, dx[1], "tn", name=f"{tag}_wout")
    dq, dk, dv, d_gq, d_gk, dbias = attn_bwd(qkv, do, p["q_gain"], p["k_gain"], bias, f"{tag}_bcore")
    dqkv = jnp.concatenate([dq, dk, dv], axis=1)
    dh = matmul(dqkv, p["w_qkv_t"], "nn", name=f"{tag}_bqkv")
    d_wqkv_t = matmul(dqkv, h, "tn", name=f"{tag}_wqkv")
    d_rb = rel_bias_reduce(_joint_bias_grad(dbias).transpose(1, 0, 2), p["rel_bias"].shape[1], f"{tag}_brel")
    dx, d_gain = norm_bwd(x, gain, dh, dx[0], f"{tag}_bnorm")
    return dx, d_gain, dict(w_qkv_t=d_wqkv_t, w_o=d_wo, q_gain=d_gq, k_gain=d_gk, rel_bias=d_rb)


def _pool_fwd(x, gain, p, tag):
    (hf,) = norm_fwd(x, gain, [F32], f"{tag}_norm")
    out, pooled = pool_fwd(hf, x, p["w"], p["scale"], f"{tag}_core")
    return out, (x, pooled)


def _pool_bwd(dx, saved, gain, p, tag):
    x, pooled = saved
    dpooled, d_w, d_scale = pool_bwd_a(dx[0], pooled, p["w"], p["scale"], f"{tag}_bcore_a")
    dh = pool_bwd_b(dpooled, f"{tag}_bcore_b")
    dx, d_gain = norm_bwd(x, gain, dh, dx[0], f"{tag}_bnorm")
    return dx, d_gain, dict(w=d_w, scale=d_scale)


def _rows_layout(t, hv):
    return t.T.reshape(hv, t.shape[0] // CHUNK, 1, CHUNK)


def _gdn_fwd(x, gain, p, tag):
    T = x.shape[0]
    hv = p["a_log"].shape[1]
    key_dim = p["key_dim"]
    C = p["conv"].shape[1]
    (h,) = norm_fwd(x, gain, [MXU_DTYPE], f"{tag}_norm")
    proj = matmul(h, p["w_main_t"], "nt", name=f"{tag}_in")
    ab = matmul(h, p["w_ab_t"], "nt", name=f"{tag}_in_ab")
    a, b = ab[:, :hv], ab[:, hv:2 * hv]
    qkvn = gdn_pre_fwd(proj, p["conv"], key_dim, f"{tag}_pre")
    g, beta = gdn_gate_fwd(a, b, p["a_log"], p["dt_bias"], f"{tag}_gate")
    g_rows, b_rows = _rows_layout(g, hv), _rows_layout(beta, hv)
    o, states, tinvs = delta_fwd(qkvn, g_rows, b_rows, key_dim, f"{tag}_delta")
    y = gdn_post_fwd(o, proj, C, p["o_gain"], f"{tag}_post")
    out = matmul(y, p["w_o"], "nn", res=x, name=f"{tag}_out")
    return out, (x, h, proj, a, b, qkvn, g_rows, b_rows, o, states, tinvs, y)


def _gdn_bwd(dx, saved, gain, p, tag):
    x, h, proj, a, b, qkvn, g_rows, b_rows, o, states, tinvs, y = saved
    T = x.shape[0]
    hv = p["a_log"].shape[1]
    key_dim = p["key_dim"]
    C = p["conv"].shape[1]
    dy = matmul(dx[1], p["w_o"], "nt", name=f"{tag}_bout")
    d_wo = matmul(y, dx[1], "tn", name=f"{tag}_wout")
    do, dgate, d_ogain = gdn_post_bwd(o, proj, C, p["o_gain"], dy, f"{tag}_bpost")
    dq_v, dk_v, dv, dg_rows, dbe_rows = delta_bwd(qkvn, g_rows, b_rows, states, tinvs, do, key_dim,
                                                  f"{tag}_bdelta")
    dg = dg_rows.reshape(hv, T).T
    dbeta = dbe_rows.reshape(hv, T).T
    da, db, d_alog, d_dtb = gdn_gate_bwd(a, b, p["a_log"], p["dt_bias"], dg, dbeta, f"{tag}_bgate")
    du, d_conv = gdn_pre_bwd_a(proj, p["conv"], dq_v, dk_v, dv, key_dim, f"{tag}_bpre_a")
    dproj = gdn_pre_bwd_b(du, dgate, p["conv"], key_dim, f"{tag}_bpre_b")
    dab = jnp.concatenate([da, db, jnp.zeros((T, LANES - 2 * hv), F32)], axis=1)
    dh = matmul(dproj, p["w_main_t"], "nn", name=f"{tag}_bin")
    dh = matmul(dab, p["w_ab_t"], "nn", res=dh, name=f"{tag}_bin_ab")
    d_main_t = matmul(dproj, h, "tn", name=f"{tag}_win")
    d_ab_t = matmul(dab, h, "tn", name=f"{tag}_win_ab")
    dx, d_gain = norm_bwd(x, gain, dh, dx[0], f"{tag}_bnorm")
    return dx, d_gain, dict(w_main_t=d_main_t, w_ab_t=d_ab_t, conv=d_conv, a_log=d_alog, dt_bias=d_dtb,
                            o_gain=d_ogain, w_o=d_wo)


_MIXERS = ((_att_fwd, _att_bwd), (_pool_fwd, _pool_bwd), (_gdn_fwd, _gdn_bwd))


def local_step(x, target, full, slots, place):
    depth = full["ffn_conv"].shape[0]
    groups = _layer_groups(depth)
    core = place[1:2]
    big = dict(zip(groups[0], gather_slots([slots[k] for k in groups[0]], "gather_l0")))
    saved = []
    for i in range(depth):
        kind, j = i % 3, i // 3
        mp, fp = layer_weights(i, big, full)
        x, s_mix = _MIXERS[kind][0](x, full["mix_norm"][i:i + 1], mp, f"l{i}_mix")
        jobs, keys = (None, None), ()
        if i + 1 < depth:
            keys = groups[i + 1][0::2] + groups[i + 1][1::2]
            jobs = (gather_job([slots[k] for k in groups[i + 1][0::2]]),
                    gather_job([slots[k] for k in groups[i + 1][1::2]]))
        x, s_ffn, got = _ffn_fwd(x, full["ffn_norm"][i:i + 1], fp, f"l{i}_ffn", jobs)
        saved.append((s_mix, s_ffn, mp, fp))
        if keys:
            big = dict(zip(keys, gather_pass(got, f"gather_pass_l{i + 1}")))
    loss, dx = loss_and_grad(x, target, "loss")
    n_mix = [len([i for i in range(depth) if i % 3 == kind]) for kind in range(3)]
    G = dict(mix_norm=[None] * depth, ffn_norm=[None] * depth, ffn=[None] * depth,
             att=[None] * n_mix[0], pool=[None] * n_mix[1], gdn=[None] * n_mix[2])
    summed = {}

    def finish(keys, chip_sums, got, tag):
        mine = [add_cols4(p, b, place, f"rs_add4_{n}_{j}") for (n, j), p, b in zip(keys, chip_sums, got)]
        summed.update(zip(keys, join_cols(mine, f"rs_join_{tag}")))

    def chip_sum(keys, gs, theirs):
        return [add_cols2(g, a, core, f"rs_add2_{n}_{j}") for (n, j), g, a in zip(keys, gs, theirs)]

    pending = None
    for i in reversed(range(depth)):
        kind, j = i % 3, i // 3
        s_mix, s_ffn, mp, fp = saved[i]
        if pending:
            keys, gs = pending
            order = keys[0::2] + keys[1::2]
            sums = []

            def then_jobs(theirs, keys=keys, gs=gs, sums=sums):
                sums.extend(chip_sum(keys, gs, theirs))
                return scatter_job(sums[0::2]), scatter_job(sums[1::2])

            dx, G["ffn_norm"][i], gf, got = _ffn_bwd(dx, s_ffn, full["ffn_norm"][i:i + 1], fp, f"l{i}_ffn",
                                                     swap_job(gs), then_jobs)
            finish(order, sums[0::2] + sums[1::2], got, f"l{i + 1}")
        else:
            dx, G["ffn_norm"][i], gf, _ = _ffn_bwd(dx, s_ffn, full["ffn_norm"][i:i + 1], fp, f"l{i}_ffn")
        dx, G["mix_norm"][i], gm = _MIXERS[kind][1](dx, s_mix, full["mix_norm"][i:i + 1], mp, f"l{i}_mix")
        G["ffn"][i] = gf
        G[("att", "pool", "gdn")[kind]][j] = gm
        gbig = layer_big_grads(i, gm, gf, full["gdn_a_log"].shape[1])
        pending = (groups[i], [gbig[k] for k in groups[i]])
    keys, gs = pending
    sums = chip_sum(keys, gs, swap_cols(gs, "rs_swap_l0"))
    finish(keys, sums, scatter_cols(sums, "rs_scatter_l0"), "l0")
    return loss, dx[0], summed, G


SHARD_AXIS = dict(att_w_qkv=2, att_w_o=1, pool_w=2, gdn_w_in=2, gdn_w_o=1, ffn_w_up=2, ffn_w_down=1,
                  att_rel_bias=2, gdn_conv=2, ffn_conv=2)
BIG = ("att_w_qkv", "att_w_o", "pool_w", "gdn_w_in", "gdn_w_o", "ffn_w_up", "ffn_w_down")
SMALL_SHARDED = ("att_rel_bias", "gdn_conv", "ffn_conv")
REPLICATED = ("mix_norm", "ffn_norm", "att_q_gain", "att_k_gain", "pool_scale", "gdn_a_log", "gdn_dt_bias",
              "gdn_o_gain")
WEIGHTS = ("mix_norm", "ffn_norm", "att_w_qkv", "att_q_gain", "att_k_gain", "att_rel_bias", "att_w_o", "pool_w",
           "pool_scale", "gdn_w_in", "gdn_conv", "gdn_a_log", "gdn_dt_bias", "gdn_o_gain", "gdn_w_o", "ffn_w_up",
           "ffn_conv", "ffn_w_down")


def _merge(stacked, axis):
    t = jnp.moveaxis(stacked, 0, axis)
    return t.reshape(t.shape[:axis] + (t.shape[axis] * t.shape[axis + 1],) + t.shape[axis + 2:])


def _pad_to(t, axis, size):
    pad = [(0, 0)] * t.ndim
    pad[axis] = (0, size - t.shape[axis])
    return jnp.pad(t, pad)


def _round_up(n, m):
    return (n + m - 1) // m * m


def to_comm(name, t):
    if name in ("att_w_qkv", "gdn_w_in"):
        return t.T
    if name == "ffn_w_up":
        d, n = t.shape
        return _pad_to(t.T.reshape(2, n // 2, d), 1, _round_up(n // 2, LANES)).reshape(-1, d)
    if name == "ffn_w_down":
        return _pad_to(t, 0, _round_up(t.shape[0], LANES))
    if name == "pool_w":
        return t.reshape(-1, t.shape[-1])
    return t


def from_comm(name, r, shape):
    if name in ("att_w_qkv", "gdn_w_in"):
        return r.T
    if name == "ffn_w_up":
        d, n = shape
        return r.reshape(2, -1, d)[:, :n // 2].reshape(n, d).T
    if name == "ffn_w_down":
        return r[:shape[0]]
    return r.reshape(shape)


def _rows(t):
    return t.reshape(-1, t.shape[-1])


def layer_weights(i, big, full):
    kind, j = i % 3, i // 3
    F4 = full["ffn_conv"].shape[2] // N_CHIPS
    conv = _pad_to(full["ffn_conv"][i].reshape(FFN_CONV, N_CHIPS, F4), 2, _round_up(F4, LANES)).reshape(FFN_CONV, -1)
    fp = dict(w_up_t=_rows(big["ffn_w_up", i]), conv=_pad_to(conv, 0, SUBLANES), w_down=_rows(big["ffn_w_down", i]))
    if kind == 0:
        mp = dict(w_qkv_t=_rows(big["att_w_qkv", j]), w_o=_rows(big["att_w_o", j]),
                  q_gain=full["att_q_gain"][j:j + 1], k_gain=full["att_k_gain"][j:j + 1],
                  rel_bias=full["att_rel_bias"][j])
    elif kind == 1:
        t = big["pool_w", j]
        G = len(POOL_WINDOWS)
        dg = t.shape[-1]
        w = jnp.moveaxis(t.reshape(N_CHIPS, G, dg // N_CHIPS, dg), 0, 1).reshape(G, dg, dg)
        mp = dict(w=w, scale=full["pool_scale"][j:j + 1])
    else:
        C = full["gdn_conv"].shape[2]
        wt = _rows(big["gdn_w_in", j])
        V = _rows(big["gdn_w_o", j]).shape[0]
        mp = dict(w_main_t=wt[:C + V], w_ab_t=_pad_to(wt[C + V:], 0, LANES),
                  conv=_pad_to(full["gdn_conv"][j], 0, SUBLANES), a_log=full["gdn_a_log"][j:j + 1],
                  dt_bias=full["gdn_dt_bias"][j:j + 1], o_gain=full["gdn_o_gain"][j:j + 1],
                  w_o=_rows(big["gdn_w_o", j]), key_dim=(C - V) // 2)
    return mp, fp


def layer_big_grads(i, gm, gf, hv):
    kind, j = i % 3, i // 3

    def slots(t):
        return t.reshape(N_CHIPS, t.shape[0] // N_CHIPS, t.shape[1])

    out = {("ffn_w_up", i): slots(gf["w_up_t"]), ("ffn_w_down", i): slots(gf["w_down"])}
    if kind == 0:
        out["att_w_qkv", j] = slots(gm["w_qkv_t"])
        out["att_w_o", j] = slots(gm["w_o"])
    elif kind == 1:
        n, dg, _ = gm["w"].shape
        out["pool_w", j] = jnp.moveaxis(gm["w"].reshape(n, N_CHIPS, dg // N_CHIPS, dg), 1, 0).reshape(N_CHIPS, -1, dg)
    else:
        out["gdn_w_in", j] = slots(jnp.concatenate([gm["w_main_t"], gm["w_ab_t"][:2 * hv]], axis=0))
        out["gdn_w_o", j] = slots(gm["w_o"])
    return out


def small_grads(G, full):
    F = full["ffn_conv"].shape[2]
    F4 = F // N_CHIPS

    def conv(g):
        return g["conv"][:FFN_CONV].reshape(FFN_CONV, N_CHIPS, -1)[:, :, :F4].reshape(FFN_CONV, F)

    return dict(
        mix_norm=jnp.concatenate(G["mix_norm"], axis=0), ffn_norm=jnp.concatenate(G["ffn_norm"], axis=0),
        ffn_conv=jnp.stack([conv(g) for g in G["ffn"]]),
        att_q_gain=jnp.concatenate([g["q_gain"] for g in G["att"]], axis=0),
        att_k_gain=jnp.concatenate([g["k_gain"] for g in G["att"]], axis=0),
        att_rel_bias=jnp.stack([g["rel_bias"] for g in G["att"]]),
        pool_scale=jnp.concatenate([g["scale"] for g in G["pool"]], axis=0),
        gdn_conv=jnp.stack([g["conv"][:GDN_CONV] for g in G["gdn"]]),
        gdn_a_log=jnp.concatenate([g["a_log"] for g in G["gdn"]], axis=0),
        gdn_dt_bias=jnp.concatenate([g["dt_bias"] for g in G["gdn"]], axis=0),
        gdn_o_gain=jnp.concatenate([g["o_gain"] for g in G["gdn"]], axis=0))


ANY = BS(memory_space=pl.ANY)
PACK_COLS = 1024
PACK_ROWS = 32


def _place():
    x, y, c = lax.axis_index("x"), lax.axis_index("y"), lax.axis_index("c")
    chips = [(1 - x, y), (x, 1 - y), (1 - x, 1 - y)]
    return x, y, c, chips


def _remote(src, dst, send_sem, recv_sem, to):
    return pltpu.make_async_remote_copy(src_ref=src, dst_ref=dst, send_sem=send_sem, recv_sem=recv_sem,
                                        device_id=to, device_id_type=MESH)


def gather_chips(shard, name):
    R, C = shard.shape
    half = R // 2

    def body(x_ref, o_ref, send_sems, recv_sems, local_sem):
        x, y, c, chips = _place()
        mine_rows = pl.ds(c * half, half)
        other_rows = pl.ds((1 - c) * half, half)
        own = pltpu.make_async_copy(x_ref, o_ref.at[2 * x + y], local_sem)
        own.start()
        first = [_remote(x_ref.at[mine_rows], o_ref.at[2 * x + y, mine_rows], send_sems.at[j], recv_sems.at[j],
                         (cx, cy, c)) for j, (cx, cy) in enumerate(chips)]
        for cp in first:
            cp.start()
        passed = []
        for j, (cx, cy) in enumerate(chips):
            landed = o_ref.at[2 * cx + cy, mine_rows]
            _remote(landed, landed, send_sems.at[j], recv_sems.at[j], (cx, cy, c)).wait_recv()
            cp = _remote(landed, landed, send_sems.at[3 + j], recv_sems.at[3 + j], (x, y, 1 - c))
            cp.start()
            passed.append(cp)
        for j, (cx, cy) in enumerate(chips):
            landed = o_ref.at[2 * cx + cy, other_rows]
            _remote(landed, landed, send_sems.at[3 + j], recv_sems.at[3 + j], (x, y, 1 - c)).wait_recv()
        for cp in first + passed:
            cp.wait_send()
        own.wait()

    return pl.pallas_call(
        body, out_shape=S((N_CHIPS, R, C), shard.dtype), in_specs=[ANY], out_specs=ANY,
        scratch_shapes=[pltpu.SemaphoreType.DMA((6,)), pltpu.SemaphoreType.DMA((6,)), pltpu.SemaphoreType.DMA],
        name=name)(shard)


def _tile(R, hc):
    if R % 256 == 0:
        return _pick(R, (512, 256)), hc
    return R, _pick(hc, (256, 128))


def _half(c, hc):
    return pl.ds(pl.multiple_of(c * hc, hc), hc)


def prep_slot(t, s_me, name):
    R, C = t.shape
    br, bc = _tile(R, C // 2)

    def body(s_ref, t_ref, o_ref):
        o_ref[0] = t_ref[...].astype(o_ref.dtype)

    return pl.pallas_call(
        body, grid_spec=pltpu.PrefetchScalarGridSpec(
            num_scalar_prefetch=1, grid=(R // br, C // bc),
            in_specs=[BS((br, bc), lambda i, j, s: (i, j))],
            out_specs=BS((1, br, bc), lambda i, j, s: (s[0], i, j))),
        out_shape=S((N_CHIPS, R, C), MXU_DTYPE), compiler_params=_cp(("parallel", "parallel")), name=name)(
            s_me.reshape(1), t)


def gather_slots(arrs, name):
    nt = len(arrs)

    def body(*refs):
        outs = refs[nt:2 * nt]
        send_sems, recv_sems = refs[2 * nt:]
        x, y, c, chips = _place()
        me = 2 * x + y
        first, passed = [], []
        for t, o in enumerate(outs):
            mine = _half(c, o.shape[2] // 2)
            for j, (cx, cy) in enumerate(chips):
                cp = _remote(o.at[me, :, mine], o.at[me, :, mine], send_sems.at[t, j], recv_sems.at[t, j], (cx, cy, c))
                cp.start()
                first.append(cp)
        for t, o in enumerate(outs):
            mine = _half(c, o.shape[2] // 2)
            for j, (cx, cy) in enumerate(chips):
                landed = o.at[2 * cx + cy, :, mine]
                _remote(landed, landed, send_sems.at[t, j], recv_sems.at[t, j], (cx, cy, c)).wait_recv()
                cp = _remote(landed, landed, send_sems.at[t, 3 + j], recv_sems.at[t, 3 + j], (x, y, 1 - c))
                cp.start()
                passed.append(cp)
        for t, o in enumerate(outs):
            other = _half(1 - c, o.shape[2] // 2)
            for j, (cx, cy) in enumerate(chips):
                landed = o.at[2 * cx + cy, :, other]
                _remote(landed, landed, send_sems.at[t, 3 + j], recv_sems.at[t, 3 + j], (x, y, 1 - c)).wait_recv()
        for cp in first + passed:
            cp.wait_send()

    return pl.pallas_call(
        body, out_shape=[S(a.shape, a.dtype) for a in arrs], in_specs=[ANY] * nt, out_specs=[ANY] * nt,
        input_output_aliases={t: t for t in range(nt)},
        scratch_shapes=[pltpu.SemaphoreType.DMA((nt, 6)), pltpu.SemaphoreType.DMA((nt, 6))], name=name)(*arrs)


def swap_cols(gs, name):
    nt = len(gs)

    def body(*refs):
        ins, outs = refs[:nt], refs[nt:2 * nt]
        send_sems, recv_sems = refs[2 * nt:]
        x, y, c, _ = _place()
        sent = []
        for t, (g, o) in enumerate(zip(ins, outs)):
            cp = _remote(g.at[:, :, _half(1 - c, o.shape[2])], o, send_sems.at[t], recv_sems.at[t], (x, y, 1 - c))
            cp.start()
            sent.append(cp)
        for cp in sent:
            cp.wait()

    return pl.pallas_call(
        body, out_shape=[S(g.shape[:2] + (g.shape[2] // 2,), g.dtype) for g in gs], in_specs=[ANY] * nt,
        out_specs=[ANY] * nt, scratch_shapes=[pltpu.SemaphoreType.DMA((nt,)), pltpu.SemaphoreType.DMA((nt,))],
        name=name)(*gs)


def add_cols2(g, other, c, name):
    n, R, C = g.shape
    hc = C // 2
    br, bc = _tile(R, hc)
    nj = hc // bc

    def body(c_ref, g_ref, o_ref, out_ref):
        out_ref[...] = (g_ref[...] + o_ref[...]).astype(out_ref.dtype)

    blk = BS((1, br, bc), lambda s, i, j, c_ref: (s, i, j))
    return pl.pallas_call(
        body, grid_spec=pltpu.PrefetchScalarGridSpec(
            num_scalar_prefetch=1, grid=(n, R // br, nj),
            in_specs=[BS((1, br, bc), lambda s, i, j, c_ref: (s, i, c_ref[0] * nj + j)), blk], out_specs=blk),
        out_shape=S((n, R, hc), BF16), compiler_params=_cp(("parallel", "parallel", "parallel")), name=name)(
            c, g, other)


def scatter_cols(ps, name):
    nt = len(ps)

    def body(*refs):
        ins, outs = refs[:nt], refs[nt:2 * nt]
        send_sems, recv_sems = refs[2 * nt:]
        x, y, c, chips = _place()
        sent = []
        for t, (p, o) in enumerate(zip(ins, outs)):
            for j, (cx, cy) in enumerate(chips):
                cp = _remote(p.at[2 * cx + cy], o.at[j], send_sems.at[t, j], recv_sems.at[t, j], (cx, cy, c))
                cp.start()
                sent.append(cp)
        for cp in sent:
            cp.wait()

    return pl.pallas_call(
        body, out_shape=[S((N_CHIPS - 1,) + p.shape[1:], p.dtype) for p in ps], in_specs=[ANY] * nt,
        out_specs=[ANY] * nt,
        scratch_shapes=[pltpu.SemaphoreType.DMA((nt, 3)), pltpu.SemaphoreType.DMA((nt, 3))], name=name)(*ps)


def add_cols4(p, got, place, name):
    n, R, hc = p.shape
    br, bc = _tile(R, hc)
    nj = hc // bc

    def body(pl_ref, p_ref, g_ref, out_ref):
        acc = p_ref[0].astype(F32)
        for j in range(n - 1):
            acc += g_ref[j].astype(F32)
        out_ref[...] = acc

    return pl.pallas_call(
        body, grid_spec=pltpu.PrefetchScalarGridSpec(
            num_scalar_prefetch=1, grid=(R // br, nj),
            in_specs=[BS((1, br, bc), lambda i, j, pl_ref: (pl_ref[0], i, j)),
                      BS((n - 1, br, bc), lambda i, j, pl_ref: (0, i, j))],
            out_specs=BS((br, bc), lambda i, j, pl_ref: (i, pl_ref[1] * nj + j))),
        out_shape=S((R, 2 * hc), F32), compiler_params=_cp(("parallel", "parallel")), name=name)(place, p, got)


def join_cols(rs, name):
    nt = len(rs)

    def body(*refs):
        outs = refs[nt:2 * nt]
        send_sems, recv_sems = refs[2 * nt:]
        x, y, c, _ = _place()
        sent = []
        for t, o in enumerate(outs):
            mine = o.at[:, _half(c, o.shape[1] // 2)]
            cp = _remote(mine, mine, send_sems.at[t], recv_sems.at[t], (x, y, 1 - c))
            cp.start()
            sent.append(cp)
        for t, o in enumerate(outs):
            theirs = o.at[:, _half(1 - c, o.shape[1] // 2)]
            _remote(theirs, theirs, send_sems.at[t], recv_sems.at[t], (x, y, 1 - c)).wait_recv()
        for cp in sent:
            cp.wait_send()

    return pl.pallas_call(
        body, out_shape=[S(r.shape, r.dtype) for r in rs], in_specs=[ANY] * nt, out_specs=[ANY] * nt,
        input_output_aliases={t: t for t in range(nt)},
        scratch_shapes=[pltpu.SemaphoreType.DMA((nt,)), pltpu.SemaphoreType.DMA((nt,))], name=name)(*rs)


def gather_job(slots):
    nt = len(slots)

    def copies(outs, send_sems, recv_sems, sent):
        x, y, c, chips = _place()
        me = 2 * x + y
        out = []
        for t, o in enumerate(outs):
            mine = _half(c, o.shape[2] // 2)
            for j, (cx, cy) in enumerate(chips):
                rows = o.at[me if sent else 2 * cx + cy, :, mine]
                out.append(_remote(rows, rows, send_sems.at[3 * t + j], recv_sems.at[3 * t + j], (cx, cy, c)))
        return out

    def start(ins, outs, send_sems, recv_sems):
        for cp in copies(outs, send_sems, recv_sems, True):
            cp.start()

    def finish(ins, outs, send_sems, recv_sems):
        for cp in copies(outs, send_sems, recv_sems, False):
            cp.wait_recv()
        for cp in copies(outs, send_sems, recv_sems, True):
            cp.wait_send()

    return dict(ins=list(slots), out_shapes=[S(a.shape, a.dtype) for a in slots], aliases=[(t, t) for t in range(nt)],
                nsem=3 * nt, start=start, finish=finish)


def gather_pass(arrs, name):
    nt = len(arrs)

    def body(*refs):
        outs = refs[nt:2 * nt]
        send_sems, recv_sems = refs[2 * nt:]
        x, y, c, chips = _place()
        sent = []
        for t, o in enumerate(outs):
            mine = _half(c, o.shape[2] // 2)
            for j, (cx, cy) in enumerate(chips):
                landed = o.at[2 * cx + cy, :, mine]
                cp = _remote(landed, landed, send_sems.at[t, j], recv_sems.at[t, j], (x, y, 1 - c))
                cp.start()
                sent.append(cp)
        for t, o in enumerate(outs):
            other = _half(1 - c, o.shape[2] // 2)
            for j, (cx, cy) in enumerate(chips):
                landed = o.at[2 * cx + cy, :, other]
                _remote(landed, landed, send_sems.at[t, j], recv_sems.at[t, j], (x, y, 1 - c)).wait_recv()
        for cp in sent:
            cp.wait_send()

    return pl.pallas_call(
        body, out_shape=[S(a.shape, a.dtype) for a in arrs], in_specs=[ANY] * nt, out_specs=[ANY] * nt,
        input_output_aliases={t: t for t in range(nt)},
        scratch_shapes=[pltpu.SemaphoreType.DMA((nt, 3)), pltpu.SemaphoreType.DMA((nt, 3))], name=name)(*arrs)


def swap_job(gs):
    def copies(ins, outs, send_sems, recv_sems):
        x, y, c, _ = _place()
        return [_remote(g.at[:, :, _half(1 - c, o.shape[2])], o, send_sems.at[t], recv_sems.at[t], (x, y, 1 - c))
                for t, (g, o) in enumerate(zip(ins, outs))]

    def start(ins, outs, send_sems, recv_sems):
        for cp in copies(ins, outs, send_sems, recv_sems):
            cp.start()

    def finish(ins, outs, send_sems, recv_sems):
        for cp in copies(ins, outs, send_sems, recv_sems):
            cp.wait()

    return dict(ins=list(gs), out_shapes=[S(g.shape[:2] + (g.shape[2] // 2,), g.dtype) for g in gs], aliases=[],
                nsem=len(gs), start=start, finish=finish)


def scatter_job(ps):
    nt = len(ps)

    def copies(ins, outs, send_sems, recv_sems):
        x, y, c, chips = _place()
        return [_remote(p.at[2 * cx + cy], o.at[j], send_sems.at[3 * t + j], recv_sems.at[3 * t + j], (cx, cy, c))
                for t, (p, o) in enumerate(zip(ins, outs)) for j, (cx, cy) in enumerate(chips)]

    def start(ins, outs, send_sems, recv_sems):
        for cp in copies(ins, outs, send_sems, recv_sems):
            cp.start()

    def finish(ins, outs, send_sems, recv_sems):
        for cp in copies(ins, outs, send_sems, recv_sems):
            cp.wait()

    return dict(ins=list(ps), out_shapes=[S((N_CHIPS - 1,) + p.shape[1:], p.dtype) for p in ps], aliases=[],
                nsem=3 * nt, start=start, finish=finish)


def sum_devices(v, name):
    R, C = v.shape

    def body(v_ref, o_ref, slots, send_sems, recv_sems):
        x, y, c, _ = _place()
        me = 4 * x + 2 * y + c
        slots[me] = v_ref[...]
        sent = []
        for r in range(1, 8):
            peer = (x ^ (r >> 2), y ^ ((r >> 1) & 1), c ^ (r & 1))
            cp = _remote(v_ref, slots.at[me], send_sems.at[r - 1], recv_sems.at[r - 1], peer)
            cp.start()
            sent.append(cp)
        for r in range(1, 8):
            peer = (x ^ (r >> 2), y ^ ((r >> 1) & 1), c ^ (r & 1))
            theirs = slots.at[4 * peer[0] + 2 * peer[1] + peer[2]]
            _remote(v_ref, theirs, send_sems.at[r - 1], recv_sems.at[r - 1], peer).wait_recv()
        for cp in sent:
            cp.wait_send()
        acc = slots[0]
        for k in range(1, 8):
            acc += slots[k]
        o_ref[...] = acc

    vm = BS(memory_space=pltpu.VMEM)
    return pl.pallas_call(
        body, out_shape=S((R, C), F32), in_specs=[vm], out_specs=vm,
        scratch_shapes=[pltpu.VMEM((8, R, C), F32), pltpu.SemaphoreType.DMA((7,)), pltpu.SemaphoreType.DMA((7,))],
        compiler_params=pltpu.CompilerParams(vmem_limit_bytes=32 * MIB), name=name)(v)


def _pack(arrays, dtype, cols, row_mult):
    flat = jnp.concatenate([a.astype(dtype).reshape(-1) for a in arrays])
    n = flat.shape[0]
    total = _round_up(n, cols * row_mult)
    return jnp.pad(flat, (0, total - n)).reshape(total // cols, cols)


def _unpack(flat, shapes):
    out, off = [], 0
    for shp in shapes:
        n = 1
        for d in shp:
            n *= d
        out.append(flat[..., off:off + n].reshape(flat.shape[:-1] + tuple(shp)))
        off += n
    return out


def _layer_groups(depth):
    groups = []
    for i in range(depth):
        kind, j = i % 3, i // 3
        mix = ((("att_w_qkv", j), ("att_w_o", j)), (("pool_w", j),), (("gdn_w_in", j), ("gdn_w_o", j)))[kind]
        groups.append(mix + (("ffn_w_up", i), ("ffn_w_down", i)))
    return groups


def kernel(x, mix_norm, ffn_norm, att_w_qkv, att_q_gain, att_k_gain, att_rel_bias, att_w_o, pool_w, pool_scale, gdn_w_in, gdn_conv, gdn_a_log, gdn_dt_bias, gdn_o_gain, gdn_w_o, ffn_w_up, ffn_conv, ffn_w_down, loss_target, m_mix_norm, m_ffn_norm, m_att_w_qkv, m_att_q_gain, m_att_k_gain, m_att_rel_bias, m_att_w_o, m_pool_w, m_pool_scale, m_gdn_w_in, m_gdn_conv, m_gdn_a_log, m_gdn_dt_bias, m_gdn_o_gain, m_gdn_w_o, m_ffn_w_up, m_ffn_conv, m_ffn_w_down, v_mix_norm, v_ffn_norm, v_att_w_qkv, v_att_q_gain, v_att_k_gain, v_att_rel_bias, v_att_w_o, v_pool_w, v_pool_scale, v_gdn_w_in, v_gdn_conv, v_gdn_a_log, v_gdn_dt_bias, v_gdn_o_gain, v_gdn_w_o, v_ffn_w_up, v_ffn_conv, v_ffn_w_down):
    w = dict(mix_norm=mix_norm, ffn_norm=ffn_norm, att_w_qkv=att_w_qkv, att_q_gain=att_q_gain, att_k_gain=att_k_gain, att_rel_bias=att_rel_bias, att_w_o=att_w_o, pool_w=pool_w, pool_scale=pool_scale, gdn_w_in=gdn_w_in, gdn_conv=gdn_conv, gdn_a_log=gdn_a_log, gdn_dt_bias=gdn_dt_bias, gdn_o_gain=gdn_o_gain, gdn_w_o=gdn_w_o, ffn_w_up=ffn_w_up, ffn_conv=ffn_conv, ffn_w_down=ffn_w_down)
    m = dict(mix_norm=m_mix_norm, ffn_norm=m_ffn_norm, att_w_qkv=m_att_w_qkv, att_q_gain=m_att_q_gain, att_k_gain=m_att_k_gain, att_rel_bias=m_att_rel_bias, att_w_o=m_att_w_o, pool_w=m_pool_w, pool_scale=m_pool_scale, gdn_w_in=m_gdn_w_in, gdn_conv=m_gdn_conv, gdn_a_log=m_gdn_a_log, gdn_dt_bias=m_gdn_dt_bias, gdn_o_gain=m_gdn_o_gain, gdn_w_o=m_gdn_w_o, ffn_w_up=m_ffn_w_up, ffn_conv=m_ffn_conv, ffn_w_down=m_ffn_w_down)
    v = dict(mix_norm=v_mix_norm, ffn_norm=v_ffn_norm, att_w_qkv=v_att_w_qkv, att_q_gain=v_att_q_gain, att_k_gain=v_att_k_gain, att_rel_bias=v_att_rel_bias, att_w_o=v_att_w_o, pool_w=v_pool_w, pool_scale=v_pool_scale, gdn_w_in=v_gdn_w_in, gdn_conv=v_gdn_conv, gdn_a_log=v_gdn_a_log, gdn_dt_bias=v_gdn_dt_bias, gdn_o_gain=v_gdn_o_gain, gdn_w_o=v_gdn_w_o, ffn_w_up=v_ffn_w_up, ffn_conv=v_ffn_conv, ffn_w_down=v_ffn_w_down)
    depth = ffn_w_up.shape[0]
    my_c = lax.axis_index("c").astype(jnp.int32)
    my_chip = (2 * lax.axis_index("x") + lax.axis_index("y")).astype(jnp.int32)
    groups = _layer_groups(depth)

    place = jnp.stack([my_chip, my_c])

    full = {n: w[n] for n in REPLICATED}
    slots = {(n, j): prep_slot(to_comm(n, w[n][j]), my_chip, f"prep_{n}_{j}") for group in groups for n, j in group}
    small = [w[n] for n in SMALL_SHARDED]
    got = gather_chips(_pack(small, F32, LANES, PACK_ROWS), "gather_small").reshape(N_CHIPS, -1)
    for n, t in zip(SMALL_SHARDED, _unpack(got, [s.shape for s in small])):
        full[n] = _merge(t, SHARD_AXIS[n])

    loss, grad_x, summed, G = local_step(x[0], loss_target[0], full, slots, place)
    loss = lax.psum(loss[0, 0], ("x", "y", "c"))
    gfull = small_grads(G, full)

    grads = {n: jnp.stack([from_comm(n, summed[n, j], w[n][j].shape) for j in range(w[n].shape[0])]) for n in BIG}

    small_names = REPLICATED + SMALL_SHARDED
    packed = _pack([gfull[n] for n in small_names], F32, LANES, SUBLANES)
    summed = sum_devices(packed, "sum_small").reshape(-1)
    for n, t in zip(small_names, _unpack(summed, [gfull[n].shape for n in small_names])):
        if n in SHARD_AXIS:
            size = w[n].shape[SHARD_AXIS[n]]
            t = lax.dynamic_slice_in_dim(t, my_chip * size, size, axis=SHARD_AXIS[n])
        grads[n] = t

    delta, new_m, new_v = {}, {}, {}
    for n in WEIGHTS:
        delta[n], new_m[n], new_v[n] = adamw(w[n], grads[n], m[n], v[n], f"adamw_{n}")
    return (loss, grad_x[None], *[grads[n] for n in WEIGHTS], *[delta[n] for n in WEIGHTS],
            *[new_m[n] for n in WEIGHTS], *[new_v[n] for n in WEIGHTS])
```

```python
import functools

import jax
import jax.numpy as jnp
from jax import lax
from jax.experimental import pallas as pl
from jax.experimental.pallas import tpu as pltpu

F32 = jnp.float32
BF16 = jnp.bfloat16
MXU_DTYPE = BF16
HI = lax.Precision.HIGHEST
S = jax.ShapeDtypeStruct
BS = pl.BlockSpec

EPS = 1e-6
MASK_VALUE = -1e30
CHUNK = 64
HEAD = 128
LEFT_CHUNKS = 8
BAND_LEFT = LEFT_CHUNKS * CHUNK
BAND = BAND_LEFT + CHUNK
MAX_REL = 256
ATT_CHUNKS = 4
ATT_QB = ATT_CHUNKS * CHUNK
ATT_BAND = BAND_LEFT + ATT_QB
POOL_WINDOWS = (2, 4, 8, 16)
POOL_HALO = 16
GDN_CONV = 4
DELTA_HEADS = 8
FFN_CONV = 3
SUBLANES = 8
LANES = 128
FF_ALIGN = 512
N_CHIPS = 4
ADAM_LR, ADAM_B1, ADAM_B2, ADAM_EPS, ADAM_WD, ADAM_STEP = 0.001, 0.9, 0.999, 1e-08, 0.01, 10
MIB = 1024 * 1024
MESH = pl.DeviceIdType.MESH


def _cp(sems, vmem_mib=40):
    return pltpu.CompilerParams(dimension_semantics=sems, vmem_limit_bytes=vmem_mib * MIB)


def _pick(n, cands):
    for c in cands:
        if n % c == 0:
            return c
    return n


def _mx(x):
    return x.astype(MXU_DTYPE)


def _hi_lo(x):
    hi = x.astype(MXU_DTYPE)
    return hi, (x - hi.astype(F32)).astype(MXU_DTYPE)


def _dot(a, b, dims, hi=False):
    dn = (dims, ((), ()))
    if hi is True or (hi == "split" and MXU_DTYPE == F32):
        return lax.dot_general(a.astype(F32), b.astype(F32), dn, precision=HI, preferred_element_type=F32)
    if hi == "split":
        ah, al = _hi_lo(a)
        bh, bl = _hi_lo(b)
        return (lax.dot_general(ah, bh, dn, preferred_element_type=F32)
                + (lax.dot_general(ah, bl, dn, preferred_element_type=F32)
                   + lax.dot_general(al, bh, dn, preferred_element_type=F32)))
    return lax.dot_general(_mx(a), _mx(b), dn, preferred_element_type=F32)


def dot_nn(a, b, hi=False):
    return _dot(a, b, ((1,), (0,)), hi)


def dot_nt(a, b, hi=False):
    return _dot(a, b, ((1,), (1,)), hi)


def dot_tn(a, b, hi=False):
    return _dot(a, b, ((0,), (0,)), hi)


def _sigmoid(x):
    return 0.5 * jnp.tanh(0.5 * x) + 0.5


def matmul(a, b, mode, *, out_dtype=F32, res=None, job=None, name):
    if mode == "nn":
        (M, K), N = a.shape, b.shape[1]
    elif mode == "nt":
        (M, K), N = a.shape, b.shape[0]
    else:
        (K, M), N = a.shape, b.shape[1]
    bm = _pick(M, (1024, 1408, 512, 256, 128))
    bn = _pick(N, (1024, 1408, 512, 256, 128))
    bk = _pick(K, (2816, 2048, 1408, 1024, 512, 256, 128))
    nk = K // bk
    if mode == "nn":
        a_spec = BS((bm, bk), lambda i, j, k: (i, k))
        b_spec = BS((bk, bn), lambda i, j, k: (k, j))
        dot = dot_nn
    elif mode == "nt":
        a_spec = BS((bm, bk), lambda i, j, k: (i, k))
        b_spec = BS((bn, bk), lambda i, j, k: (j, k))
        dot = dot_nt
    else:
        a_spec = BS((bk, bm), lambda i, j, k: (k, i))
        b_spec = BS((bk, bn), lambda i, j, k: (k, j))
        dot = dot_tn
    o_spec = BS((bm, bn), lambda i, j, k: (i, j))
    has_res = res is not None
    n_in = 3 if has_res else 2
    nji = len(job["ins"]) if job else 0
    njo = len(job["out_shapes"]) if job else 0
    grid = (M // bm, N // bn, nk)

    def body(*refs):
        a_ref, b_ref = refs[:2]
        r_ref = refs[2] if has_res else None
        o_ref = refs[n_in + nji]
        acc = refs[n_in + nji + 1 + njo]
        k = pl.program_id(2)
        if job:
            j_ins, j_outs = refs[n_in:n_in + nji], refs[n_in + nji + 1:n_in + nji + 1 + njo]
            sems = refs[n_in + nji + 2 + njo:]
            at_step = lambda s: jnp.logical_and(jnp.logical_and(pl.program_id(0) == s[0], pl.program_id(1) == s[1]),
                                                k == s[2])
            pl.when(at_step((0, 0, 0)))(lambda: job["start"](j_ins, j_outs, *sems))
        p = dot(a_ref[...], b_ref[...])

        def finish(total):
            if has_res:
                total = r_ref[...] + total
            o_ref[...] = total.astype(o_ref.dtype)

        if nk == 1:
            finish(p)
        else:
            @pl.when(k == 0)
            def _():
                acc[...] = p

            @pl.when(jnp.logical_and(k > 0, k < nk - 1))
            def _():
                acc[...] += p

            @pl.when(k == nk - 1)
            def _():
                finish(acc[...] + p)

        if job:
            pl.when(at_step((grid[0] - 1, grid[1] - 1, nk - 1)))(lambda: job["finish"](j_ins, j_outs, *sems))

    in_specs = [a_spec, b_spec] + ([o_spec] if has_res else [])
    args = (a, b) + ((res,) if has_res else ())
    if not job:
        return pl.pallas_call(
            body, grid=grid, in_specs=in_specs, out_specs=o_spec,
            out_shape=S((M, N), out_dtype), scratch_shapes=[pltpu.VMEM((bm, bn), F32)],
            compiler_params=_cp(("parallel", "parallel", "arbitrary"), 48), name=name)(*args)
    any_spec = BS(memory_space=pl.ANY)
    outs = pl.pallas_call(
        body, grid=grid, in_specs=in_specs + [any_spec] * nji, out_specs=[o_spec] + [any_spec] * njo,
        out_shape=[S((M, N), out_dtype)] + list(job["out_shapes"]),
        input_output_aliases={n_in + src: 1 + dst for src, dst in job["aliases"]},
        scratch_shapes=[pltpu.VMEM((bm, bn), F32), pltpu.SemaphoreType.DMA((job["nsem"],)),
                        pltpu.SemaphoreType.DMA((job["nsem"],))],
        compiler_params=_cp(("arbitrary", "arbitrary", "arbitrary"), 48), name=name)(*args, *job["ins"])
    return outs[0], list(outs[1:])


def norm_fwd(x, gain, out_dtypes, name):
    T, D = x.shape
    bt = _pick(T, (256, 128, 64))

    def body(x_ref, g_ref, *o_refs):
        xv = x_ref[...]
        r = lax.rsqrt(jnp.mean(xv * xv, axis=-1, keepdims=True) + EPS)
        y = (xv * r) * g_ref[...]
        for o in o_refs:
            o[...] = y.astype(o.dtype)

    row = BS((bt, D), lambda i: (i, 0))
    return pl.pallas_call(
        body, grid=(T // bt,), in_specs=[row, BS((1, D), lambda i: (0, 0))],
        out_specs=[row] * len(out_dtypes), out_shape=[S((T, D), dt) for dt in out_dtypes],
        compiler_params=_cp(("parallel",)), name=name)(x, gain)


def norm_bwd(x, gain, dy, dres, name):
    T, D = x.shape
    bt = _pick(T, (256, 128, 64))

    def body(x_ref, g_ref, dy_ref, dres_ref, dx_ref, dxm_ref, dg_ref):
        i = pl.program_id(0)
        xv = x_ref[...]
        dyv = dy_ref[...].astype(F32)
        r = lax.rsqrt(jnp.mean(xv * xv, axis=-1, keepdims=True) + EPS)
        xhat = xv * r
        dxhat = dyv * g_ref[...]
        dx = dres_ref[...] + r * (dxhat - xhat * jnp.mean(dxhat * xhat, axis=-1, keepdims=True))
        dx_ref[...] = dx
        dxm_ref[...] = dx.astype(dxm_ref.dtype)

        @pl.when(i == 0)
        def _():
            dg_ref[...] = jnp.zeros_like(dg_ref)

        dg_ref[...] += jnp.sum(dyv * xhat, axis=0, keepdims=True)

    row = BS((bt, D), lambda i: (i, 0))
    vec = BS((1, D), lambda i: (0, 0))
    dx, dxm, dg = pl.pallas_call(
        body, grid=(T // bt,), in_specs=[row, vec, row, row], out_specs=[row, row, vec],
        out_shape=[S((T, D), F32), S((T, D), MXU_DTYPE), S((1, D), F32)],
        compiler_params=_cp(("arbitrary",)), name=name)(x, gain, dy, dres)
    return (dx, dxm), dg


def loss_and_grad(y, target, name):
    T, D = y.shape
    bt = _pick(T, (256, 128, 64))
    nt = T // bt

    def body(y_ref, t_ref, l_ref, dy_ref, dym_ref, acc):
        i = pl.program_id(0)
        e = y_ref[...] - t_ref[...]
        dy_ref[...] = e * (1.0 / D)
        dym_ref[...] = (e * (1.0 / D)).astype(dym_ref.dtype)

        @pl.when(i == 0)
        def _():
            acc[...] = jnp.zeros_like(acc)

        acc[...] += jnp.sum(e * e, axis=0, keepdims=True)

        @pl.when(i == nt - 1)
        def _():
            l_ref[...] = jnp.sum(acc[...], axis=1, keepdims=True) * (0.5 / D)

    row = BS((bt, D), lambda i: (i, 0))
    loss, dy, dym = pl.pallas_call(
        body, grid=(nt,), in_specs=[row, row], out_specs=[BS((1, 1), lambda i: (0, 0)), row, row],
        out_shape=[S((1, 1), F32), S((T, D), F32), S((T, D), MXU_DTYPE)], scratch_shapes=[pltpu.VMEM((1, D), F32)],
        compiler_params=_cp(("arbitrary",)), name=name)(y, target)
    return loss, (dy, dym)


FFN_HALO = 16


def _prev_halo(bt):
    return lambda i: (jnp.maximum(i * (bt // FFN_HALO) - 1, 0), 0)


def _ffn_u(i, a_ref, halo_ref, w_ref, ext, bt):
    ext[pl.ds(0, FFN_HALO), :] = jnp.where(i > 0, halo_ref[...].astype(F32), 0.0)
    ext[pl.ds(FFN_HALO, bt), :] = a_ref[...].astype(F32)
    u = w_ref[2:3, :] * ext[pl.ds(FFN_HALO, bt), :]
    u += w_ref[1:2, :] * ext[pl.ds(FFN_HALO - 1, bt), :]
    u += w_ref[0:1, :] * ext[pl.ds(FFN_HALO - 2, bt), :]
    return u


def ffn_act_fwd(up, conv_w, name):
    T, F2 = up.shape
    Fp = F2 // 2
    bt = _pick(T, (128, 64))

    def body(a_ref, b_ref, halo_ref, w_ref, o_ref, ext):
        u = _ffn_u(pl.program_id(0), a_ref, halo_ref, w_ref, ext, bt)
        o_ref[...] = (u * _sigmoid(u) * b_ref[...]).astype(o_ref.dtype)

    return pl.pallas_call(
        body, grid=(T // bt,),
        in_specs=[BS((bt, Fp), lambda i: (i, 0)), BS((bt, Fp), lambda i: (i, 1)),
                  BS((FFN_HALO, Fp), _prev_halo(bt)), BS((SUBLANES, Fp), lambda i: (0, 0))],
        out_specs=BS((bt, Fp), lambda i: (i, 0)), out_shape=S((T, Fp), MXU_DTYPE),
        scratch_shapes=[pltpu.VMEM((bt + FFN_HALO, Fp), F32)],
        compiler_params=_cp(("arbitrary",)), name=name)(up, up, up, conv_w)


def ffn_act_bwd_a(up, conv_w, g_act, name):
    T, F2 = up.shape
    Fp = F2 // 2
    bt = _pick(T, (128, 64))

    def body(a_ref, b_ref, halo_ref, w_ref, g_ref, du_ref, db_ref, dw_ref, ext):
        i = pl.program_id(0)
        u = _ffn_u(i, a_ref, halo_ref, w_ref, ext, bt)
        sg = _sigmoid(u)
        g = g_ref[...].astype(F32)
        db_ref[...] = (g * (u * sg)).astype(db_ref.dtype)
        du = g * b_ref[...] * (sg * (1.0 + u * (1.0 - sg)))
        du_ref[...] = du

        @pl.when(i == 0)
        def _():
            dw_ref[...] = jnp.zeros_like(dw_ref)

        for j in range(FFN_CONV):
            shifted = ext[pl.ds(FFN_HALO - (FFN_CONV - 1) + j, bt), :]
            dw_ref[j:j + 1, :] += jnp.sum(du * shifted, axis=0, keepdims=True)

    blk = BS((bt, Fp), lambda i: (i, 0))
    full = BS((SUBLANES, Fp), lambda i: (0, 0))
    return pl.pallas_call(
        body, grid=(T // bt,),
        in_specs=[blk, BS((bt, Fp), lambda i: (i, 1)), BS((FFN_HALO, Fp), _prev_halo(bt)), full, blk],
        out_specs=[blk, blk, full],
        out_shape=[S((T, Fp), F32), S((T, Fp), MXU_DTYPE), S((SUBLANES, Fp), F32)],
        scratch_shapes=[pltpu.VMEM((bt + FFN_HALO, Fp), F32)],
        compiler_params=_cp(("arbitrary",)), name=name)(up, up, up, conv_w, g_act)


def ffn_act_bwd_b(du, db, conv_w, name):
    T, Fp = du.shape
    bt = _pick(T, (128, 64))
    nt = T // bt

    def body(du_ref, halo_ref, db_ref, w_ref, o_ref, ext):
        i = pl.program_id(0)
        ext[pl.ds(0, bt), :] = du_ref[...]
        ext[pl.ds(bt, SUBLANES), :] = jnp.where(i < nt - 1, halo_ref[...], 0.0)
        da = w_ref[2:3, :] * ext[pl.ds(0, bt), :]
        da += w_ref[1:2, :] * ext[pl.ds(1, bt), :]
        da += w_ref[0:1, :] * ext[pl.ds(2, bt), :]
        o_ref[:, pl.ds(0, Fp)] = da.astype(o_ref.dtype)
        o_ref[:, pl.ds(Fp, Fp)] = db_ref[...]

    blk = BS((bt, Fp), lambda i: (i, 0))
    nxt = BS((SUBLANES, Fp), lambda i: (jnp.minimum((i + 1) * (bt // SUBLANES), T // SUBLANES - 1), 0))
    return pl.pallas_call(
        body, grid=(nt,), in_specs=[blk, nxt, blk, BS((SUBLANES, Fp), lambda i: (0, 0))],
        out_specs=BS((bt, 2 * Fp), lambda i: (i, 0)), out_shape=S((T, 2 * Fp), MXU_DTYPE),
        scratch_shapes=[pltpu.VMEM((bt + SUBLANES, Fp), F32)],
        compiler_params=_cp(("arbitrary",)), name=name)(du, du, db, conv_w)


def _attn_fill(k_ref, v_ref, gk, kn_scr, vb_scr, T):
    kn_scr[pl.ds(0, BAND_LEFT), :] = jnp.zeros((BAND_LEFT, HEAD), kn_scr.dtype)
    vb_scr[pl.ds(0, BAND_LEFT), :] = jnp.zeros((BAND_LEFT, HEAD), vb_scr.dtype)
    rb = 512

    def fill(r, carry):
        rows = pl.ds(pl.multiple_of(r * rb, rb), rb)
        dst = pl.ds(pl.multiple_of(BAND_LEFT + r * rb, rb), rb)
        k = k_ref[rows, :]
        rk = lax.rsqrt(jnp.mean(k * k, axis=-1, keepdims=True) + EPS)
        kn_scr[dst, :] = ((k * rk) * gk).astype(kn_scr.dtype)
        vb_scr[dst, :] = v_ref[rows, :].astype(vb_scr.dtype)
        return carry

    lax.fori_loop(0, T // rb, fill, 0)


def _attn_probs(c, q_ref, gq, bias_ref, kn_scr):
    q = q_ref[...]
    rq = lax.rsqrt(jnp.mean(q * q, axis=-1, keepdims=True) + EPS)
    qn = (q * rq) * gq
    band = pl.ds(pl.multiple_of(c * ATT_QB, ATT_QB), ATT_BAND)
    kb = kn_scr[band, :]
    s = dot_nt(qn, kb) * (HEAD ** -0.5) + bias_ref[0]
    pos = c * ATT_QB - BAND_LEFT + lax.broadcasted_iota(jnp.int32, (ATT_QB, ATT_BAND), 1)
    s = jnp.where(pos >= 0, s, MASK_VALUE)
    m = jnp.max(s, axis=-1, keepdims=True)
    e = jnp.exp(s - m)
    p = e * (1.0 / jnp.sum(e, axis=-1, keepdims=True))
    return q, rq, qn, kb, p


def attn_fwd(qkv, gq, gk, bias, name):
    T, D3 = qkv.shape
    D = D3 // 3
    H = D // HEAD
    NC = T // ATT_QB

    def body(q_ref, k_ref, v_ref, gq_ref, gk_ref, bias_ref, o_ref, kn_scr, vb_scr):
        c = pl.program_id(1)

        @pl.when(c == 0)
        def _():
            _attn_fill(k_ref, v_ref, gk_ref[...], kn_scr, vb_scr, T)

        _, _, _, _, p = _attn_probs(c, q_ref, gq_ref[...], bias_ref, kn_scr)
        band = pl.ds(pl.multiple_of(c * ATT_QB, ATT_QB), ATT_BAND)
        o_ref[...] = dot_nn(p, vb_scr[band, :]).astype(o_ref.dtype)

    vec = BS((1, HEAD), lambda h, c: (0, 0))
    return pl.pallas_call(
        body, grid=(H, NC),
        in_specs=[BS((ATT_QB, HEAD), lambda h, c: (c, h)), BS((T, HEAD), lambda h, c: (0, H + h)),
                  BS((T, HEAD), lambda h, c: (0, 2 * H + h)), vec, vec,
                  BS((1, ATT_QB, ATT_BAND), lambda h, c: (h, 0, 0))],
        out_specs=BS((ATT_QB, HEAD), lambda h, c: (c, h)), out_shape=S((T, D), MXU_DTYPE),
        scratch_shapes=[pltpu.VMEM((T + BAND_LEFT, HEAD), MXU_DTYPE)] * 2,
        compiler_params=_cp(("arbitrary", "arbitrary"), 48), name=name)(qkv, qkv, qkv, gq, gk, bias)


def attn_bwd(qkv, do, gq, gk, bias, name):
    T, D3 = qkv.shape
    D = D3 // 3
    H = D // HEAD
    NC = T // ATT_QB
    scale = HEAD ** -0.5

    def body(q_ref, k_ref, v_ref, do_ref, gq_ref, gk_ref, bias_ref,
             dq_ref, dk_ref, dv_ref, dgq_ref, dgk_ref, dbias_ref, kn_scr, vb_scr, dkn_acc, dv_acc):
        h = pl.program_id(0)
        c = pl.program_id(1)
        gq = gq_ref[...]
        gk = gk_ref[...]

        @pl.when(c == 0)
        def _():
            _attn_fill(k_ref, v_ref, gk, kn_scr, vb_scr, T)
            dkn_acc[...] = jnp.zeros_like(dkn_acc)
            dv_acc[...] = jnp.zeros_like(dv_acc)
            dbias_ref[...] = jnp.zeros_like(dbias_ref)

        @pl.when(jnp.logical_and(c == 0, h == 0))
        def _():
            dgq_ref[...] = jnp.zeros_like(dgq_ref)
            dgk_ref[...] = jnp.zeros_like(dgk_ref)

        q, rq, qn, kb, p = _attn_probs(c, q_ref, gq, bias_ref, kn_scr)
        band = pl.ds(pl.multiple_of(c * ATT_QB, ATT_QB), ATT_BAND)
        dov = do_ref[...]
        dv_acc[band, :] += dot_tn(p, dov)
        dp = dot_nt(dov, vb_scr[band, :])
        ds = p * (dp - jnp.sum(dp * p, axis=-1, keepdims=True))
        dbias_ref[0] += ds
        dss = ds * scale
        dqn = dot_nn(dss, kb)
        dkn_acc[band, :] += dot_tn(dss, qn)
        xhat = q * rq
        dgq_ref[...] += jnp.sum(dqn * xhat, axis=0, keepdims=True)
        dxhat = dqn * gq
        dq = rq * (dxhat - xhat * jnp.mean(dxhat * xhat, axis=-1, keepdims=True))
        dq_ref[...] = dq.astype(dq_ref.dtype)

        @pl.when(c == NC - 1)
        def _():
            rb = 512

            def fin(r, carry):
                rows = pl.ds(pl.multiple_of(r * rb, rb), rb)
                src = pl.ds(pl.multiple_of(BAND_LEFT + r * rb, rb), rb)
                k = k_ref[rows, :]
                rk = lax.rsqrt(jnp.mean(k * k, axis=-1, keepdims=True) + EPS)
                khat = k * rk
                dkn = dkn_acc[src, :]
                dgk_ref[...] += jnp.sum(dkn * khat, axis=0, keepdims=True)
                dkh = dkn * gk
                dk = rk * (dkh - khat * jnp.mean(dkh * khat, axis=-1, keepdims=True))
                dk_ref[rows, :] = dk.astype(dk_ref.dtype)
                dv_ref[rows, :] = dv_acc[src, :].astype(dv_ref.dtype)
                return carry

            lax.fori_loop(0, T // rb, fin, 0)

    vec = BS((1, HEAD), lambda h, c: (0, 0))
    qblk = BS((ATT_QB, HEAD), lambda h, c: (c, h))
    col = BS((T, HEAD), lambda h, c: (0, h))
    bblk = BS((1, ATT_QB, ATT_BAND), lambda h, c: (h, 0, 0))
    return pl.pallas_call(
        body, grid=(H, NC),
        in_specs=[qblk, BS((T, HEAD), lambda h, c: (0, H + h)), BS((T, HEAD), lambda h, c: (0, 2 * H + h)),
                  qblk, vec, vec, bblk],
        out_specs=[qblk, col, col, vec, vec, bblk],
        out_shape=[S((T, D), MXU_DTYPE)] * 3 + [S((1, HEAD), F32)] * 2 + [S((H, ATT_QB, ATT_BAND), F32)],
        scratch_shapes=[pltpu.VMEM((T + BAND_LEFT, HEAD), MXU_DTYPE)] * 2
        + [pltpu.VMEM((T + BAND_LEFT, HEAD), F32)] * 2,
        compiler_params=_cp(("arbitrary", "arbitrary"), 56), name=name)(qkv, qkv, qkv, do, gq, gk, bias)


def _rel_onehot(qi, num_rel):
    kk = lax.broadcasted_iota(jnp.int32, (BAND, num_rel), 0)
    rr = lax.broadcasted_iota(jnp.int32, (BAND, num_rel), 1)
    idx = jnp.clip(BAND_LEFT + qi - kk, -(CHUNK - 1), MAX_REL) + (CHUNK - 1)
    return (idx == rr).astype(F32)


def rel_bias_expand(table, name):
    H, num_rel = table.shape

    def body(t_ref, o_ref):
        for qi in range(CHUNK):
            o_ref[qi] = dot_nt(t_ref[...], _rel_onehot(qi, num_rel), hi=True)

    return pl.pallas_call(body, out_shape=S((CHUNK, H, BAND), F32), name=name,
                          compiler_params=pltpu.CompilerParams(vmem_limit_bytes=40 * MIB))(table)


def rel_bias_reduce(dbias_t, num_rel, name):
    H = dbias_t.shape[1]

    def body(d_ref, o_ref):
        acc = jnp.zeros((H, num_rel), F32)
        for qi in range(CHUNK):
            acc += dot_nn(d_ref[qi], _rel_onehot(qi, num_rel), hi=True)
        o_ref[...] = acc

    return pl.pallas_call(body, out_shape=S((H, num_rel), F32), name=name,
                          compiler_params=pltpu.CompilerParams(vmem_limit_bytes=40 * MIB))(dbias_t)


def pool_fwd(h, x, w, scale, name):
    T, D = h.shape
    G = len(POOL_WINDOWS)
    Dg = D // G
    bt = _pick(T, (256, 128, 64))

    def body(h_ref, halo_ref, x_ref, w_ref, s_ref, o_ref, p_ref, ext):
        i = pl.program_id(0)
        ext[pl.ds(0, POOL_HALO), :] = jnp.where(i > 0, halo_ref[...], 0.0)
        ext[pl.ds(POOL_HALO, bt), :] = h_ref[...]
        t = i * bt + lax.broadcasted_iota(jnp.int32, (bt, 1), 0)
        for g, win in enumerate(POOL_WINDOWS):
            cols = pl.ds(g * Dg, Dg)
            acc = ext[pl.ds(POOL_HALO, bt), cols]
            for j in range(1, win):
                acc += ext[pl.ds(POOL_HALO - j, bt), cols]
            count = jnp.minimum(t + 1, win).astype(F32)
            pooled = acc / count - h_ref[:, cols]
            p_ref[:, cols] = pooled.astype(p_ref.dtype)
            y = dot_nn(pooled, w_ref[g]) * s_ref[:, cols]
            o_ref[:, cols] = x_ref[:, cols] + y

    row = BS((bt, D), lambda i: (i, 0))
    return pl.pallas_call(
        body, grid=(T // bt,),
        in_specs=[row, BS((POOL_HALO, D), lambda i: (jnp.maximum(i * (bt // POOL_HALO) - 1, 0), 0)), row,
                  BS((G, Dg, Dg), lambda i: (0, 0, 0)), BS((1, D), lambda i: (0, 0))],
        out_specs=[row, row], out_shape=[S((T, D), F32), S((T, D), MXU_DTYPE)],
        scratch_shapes=[pltpu.VMEM((bt + POOL_HALO, D), F32)],
        compiler_params=_cp(("arbitrary",)), name=name)(h, h, x, w, scale)


def pool_bwd_a(dy, pooled, w, scale, name):
    T, D = dy.shape
    G = len(POOL_WINDOWS)
    Dg = D // G
    bt = _pick(T, (256, 128, 64))

    def body(dy_ref, p_ref, w_ref, s_ref, dp_ref, dw_ref, ds_ref):
        i = pl.program_id(0)

        @pl.when(i == 0)
        def _():
            dw_ref[...] = jnp.zeros_like(dw_ref)
            ds_ref[...] = jnp.zeros_like(ds_ref)

        for g in range(G):
            cols = pl.ds(g * Dg, Dg)
            pg = p_ref[:, cols]
            dyg = dy_ref[:, cols]
            ypre = dot_nn(pg, w_ref[g])
            ds_ref[:, cols] += jnp.sum(dyg * ypre, axis=0, keepdims=True)
            dys = dyg * s_ref[:, cols]
            dp_ref[:, cols] = dot_nt(dys, w_ref[g])
            dw_ref[g] += dot_tn(pg, dys)

    row = BS((bt, D), lambda i: (i, 0))
    wspec = BS((G, Dg, Dg), lambda i: (0, 0, 0))
    vec = BS((1, D), lambda i: (0, 0))
    return pl.pallas_call(
        body, grid=(T // bt,), in_specs=[row, row, wspec, vec], out_specs=[row, wspec, vec],
        out_shape=[S((T, D), F32), S((G, Dg, Dg), F32), S((1, D), F32)],
        compiler_params=_cp(("arbitrary",)), name=name)(dy, pooled, w, scale)


def pool_bwd_b(dpooled, name):
    T, D = dpooled.shape
    G = len(POOL_WINDOWS)
    Dg = D // G
    bt = _pick(T, (256, 128, 64))
    nt = T // bt

    def body(d_ref, halo_ref, o_ref, ext):
        i = pl.program_id(0)
        t = i * bt + lax.broadcasted_iota(jnp.int32, (bt, 1), 0)
        for g, win in enumerate(POOL_WINDOWS):
            cols = pl.ds(g * Dg, Dg)
            count = jnp.minimum(t + 1, win).astype(F32)
            ext[pl.ds(0, bt), cols] = d_ref[:, cols] / count
            ext[pl.ds(bt, POOL_HALO), cols] = jnp.where(i < nt - 1, halo_ref[:, cols] * (1.0 / win), 0.0)
            acc = ext[pl.ds(0, bt), cols]
            for j in range(1, win):
                acc += ext[pl.ds(j, bt), cols]
            o_ref[:, cols] = acc - d_ref[:, cols]

    row = BS((bt, D), lambda i: (i, 0))
    nxt = BS((POOL_HALO, D), lambda i: (jnp.minimum((i + 1) * (bt // POOL_HALO), T // POOL_HALO - 1), 0))
    return pl.pallas_call(
        body, grid=(nt,), in_specs=[row, nxt], out_specs=row, out_shape=S((T, D), F32),
        scratch_shapes=[pltpu.VMEM((bt + POOL_HALO, D), F32)],
        compiler_params=_cp(("arbitrary",)), name=name)(dpooled, dpooled)


def _gdn_u(i, x_ref, halo_ref, w_ref, ext, bt):
    ext[pl.ds(0, SUBLANES), :] = jnp.where(i > 0, halo_ref[...], 0.0)
    ext[pl.ds(SUBLANES, bt), :] = x_ref[...]
    u = w_ref[3:4, :] * ext[pl.ds(SUBLANES, bt), :]
    for j in range(GDN_CONV - 1):
        u += w_ref[j:j + 1, :] * ext[pl.ds(SUBLANES - (GDN_CONV - 1) + j, bt), :]
    return u


def gdn_pre_fwd(proj, conv_w, key_dim, name):
    T = proj.shape[0]
    C = conv_w.shape[1]
    cb = min(1024, key_dim)
    nq, nqk, J = key_dim // cb, 2 * key_dim // cb, C // cb
    bt = _pick(T, (256, 128, 64))

    def body(x_ref, halo_ref, w_ref, o_ref, ext):
        i, j = pl.program_id(0), pl.program_id(1)
        u = _gdn_u(i, x_ref, halo_ref, w_ref, ext, bt)
        s = u * _sigmoid(u)

        @pl.when(j < nqk)
        def _():
            sc = jnp.where(j < nq, HEAD ** -0.5, 1.0)
            for hh in range(cb // HEAD):
                cols = pl.ds(hh * HEAD, HEAD)
                blk = s[:, hh * HEAD:(hh + 1) * HEAD]
                r = lax.rsqrt(jnp.sum(blk * blk, axis=-1, keepdims=True) + EPS)
                o_ref[:, cols] = (blk * r) * sc

        @pl.when(j >= nqk)
        def _():
            o_ref[...] = s

    return pl.pallas_call(
        body, grid=(T // bt, J),
        in_specs=[BS((bt, cb), lambda i, j: (i, j)),
                  BS((SUBLANES, cb), lambda i, j: (jnp.maximum(i * (bt // SUBLANES) - 1, 0), j)),
                  BS((SUBLANES, cb), lambda i, j: (0, j))],
        out_specs=BS((bt, cb), lambda i, j: (i, j)), out_shape=S((T, C), F32),
        scratch_shapes=[pltpu.VMEM((bt + SUBLANES, cb), F32)],
        compiler_params=_cp(("arbitrary", "arbitrary")), name=name)(proj, proj, conv_w)


def gdn_pre_bwd_a(proj, conv_w, dq_v, dk_v, dv, key_dim, name):
    T = proj.shape[0]
    C = conv_w.shape[1]
    cb = min(1024, key_dim)
    nq, nqk, J = key_dim // cb, 2 * key_dim // cb, C // cb
    nv = J - nqk
    bt = _pick(T, (256, 128, 64))

    def body(x_ref, halo_ref, w_ref, dq_ref, dk_ref, dv_ref, du_ref, dw_ref, ext, ds_scr):
        j, i = pl.program_id(0), pl.program_id(1)
        u = _gdn_u(i, x_ref, halo_ref, w_ref, ext, bt)
        sg = _sigmoid(u)
        s = u * sg

        @pl.when(j < nqk)
        def _():
            sc = jnp.where(j < nq, HEAD ** -0.5, 1.0)
            for hh in range(cb // HEAD):
                lo = 2 * hh * HEAD
                dq2 = dq_ref[:, lo:lo + HEAD] + dq_ref[:, lo + HEAD:lo + 2 * HEAD]
                dk2 = dk_ref[:, lo:lo + HEAD] + dk_ref[:, lo + HEAD:lo + 2 * HEAD]
                dn = jnp.where(j < nq, dq2, dk2)
                blk = s[:, hh * HEAD:(hh + 1) * HEAD]
                r = lax.rsqrt(jnp.sum(blk * blk, axis=-1, keepdims=True) + EPS)
                shat = blk * r
                ds_scr[:, pl.ds(hh * HEAD, HEAD)] = (sc * r) * (dn - shat * jnp.sum(dn * shat, axis=-1, keepdims=True))

        @pl.when(j >= nqk)
        def _():
            ds_scr[...] = dv_ref[...]

        du = ds_scr[...] * (sg * (1.0 + u * (1.0 - sg)))
        du_ref[...] = du

        @pl.when(i == 0)
        def _():
            dw_ref[...] = jnp.zeros_like(dw_ref)

        for k in range(GDN_CONV):
            shifted = ext[pl.ds(SUBLANES - (GDN_CONV - 1) + k, bt), :]
            dw_ref[k:k + 1, :] += jnp.sum(du * shifted, axis=0, keepdims=True)

    blk = BS((bt, cb), lambda j, i: (i, j))
    return pl.pallas_call(
        body, grid=(J, T // bt),
        in_specs=[blk, BS((SUBLANES, cb), lambda j, i: (jnp.maximum(i * (bt // SUBLANES) - 1, 0), j)),
                  BS((SUBLANES, cb), lambda j, i: (0, j)),
                  BS((bt, 2 * cb), lambda j, i: (i, jnp.minimum(j, nq - 1))),
                  BS((bt, 2 * cb), lambda j, i: (i, jnp.clip(j - nq, 0, nq - 1))),
                  BS((bt, cb), lambda j, i: (i, jnp.clip(j - nqk, 0, nv - 1)))],
        out_specs=[blk, BS((SUBLANES, cb), lambda j, i: (0, j))],
        out_shape=[S((T, C), F32), S((SUBLANES, C), F32)],
        scratch_shapes=[pltpu.VMEM((bt + SUBLANES, cb), F32), pltpu.VMEM((bt, cb), F32)],
        compiler_params=_cp(("arbitrary", "arbitrary")), name=name)(proj, proj, conv_w, dq_v, dk_v, dv)


def gdn_pre_bwd_b(du, dgate, conv_w, key_dim, name):
    T, C = du.shape
    V = dgate.shape[1]
    cb = min(1024, key_dim)
    J = C // cb
    J2 = (C + V) // cb
    bt = _pick(T, (256, 128, 64))
    nt = T // bt

    def body(du_ref, halo_ref, w_ref, dg_ref, o_ref, ext):
        i, j = pl.program_id(0), pl.program_id(1)

        @pl.when(j < J)
        def _():
            ext[pl.ds(0, bt), :] = du_ref[...]
            ext[pl.ds(bt, SUBLANES), :] = jnp.where(i < nt - 1, halo_ref[...], 0.0)
            da = w_ref[3:4, :] * ext[pl.ds(0, bt), :]
            for k in range(GDN_CONV - 1):
                da += w_ref[k:k + 1, :] * ext[pl.ds(GDN_CONV - 1 - k, bt), :]
            o_ref[...] = da.astype(o_ref.dtype)

        @pl.when(j >= J)
        def _():
            o_ref[...] = dg_ref[...]

    jc = lambda j: jnp.minimum(j, J - 1)
    return pl.pallas_call(
        body, grid=(nt, J2),
        in_specs=[BS((bt, cb), lambda i, j: (i, jc(j))),
                  BS((SUBLANES, cb), lambda i, j: (jnp.minimum((i + 1) * (bt // SUBLANES), T // SUBLANES - 1), jc(j))),
                  BS((SUBLANES, cb), lambda i, j: (0, jc(j))),
                  BS((bt, cb), lambda i, j: (i, jnp.maximum(j - J, 0)))],
        out_specs=BS((bt, cb), lambda i, j: (i, j)), out_shape=S((T, C + V), MXU_DTYPE),
        scratch_shapes=[pltpu.VMEM((bt + SUBLANES, cb), F32)],
        compiler_params=_cp(("arbitrary", "arbitrary")), name=name)(du, du, conv_w, dgate)


def _softplus(x):
    return jnp.maximum(x, 0.0) + jnp.log1p(jnp.exp(-jnp.abs(x)))


def gdn_gate_fwd(a, b, a_log, dt_bias, name):
    T, HV = a.shape
    bt = _pick(T, (1024, 512, 256, 128, 64))

    def body(a_ref, b_ref, al_ref, dt_ref, g_ref, be_ref):
        g_ref[...] = -jnp.exp(al_ref[...]) * _softplus(a_ref[...] + dt_ref[...])
        be_ref[...] = _sigmoid(b_ref[...])

    row = BS((bt, HV), lambda i: (i, 0))
    vec = BS((1, HV), lambda i: (0, 0))
    return pl.pallas_call(body, grid=(T // bt,), in_specs=[row, row, vec, vec], out_specs=[row, row],
                          out_shape=[S((T, HV), F32)] * 2, compiler_params=_cp(("parallel",)), name=name)(
                              a, b, a_log, dt_bias)


def gdn_gate_bwd(a, b, a_log, dt_bias, dg, dbeta, name):
    T, HV = a.shape
    bt = _pick(T, (1024, 512, 256, 128, 64))

    def body(a_ref, b_ref, al_ref, dt_ref, dg_ref, dbe_ref, da_ref, db_ref, dal_ref, ddt_ref):
        i = pl.program_id(0)
        x = a_ref[...] + dt_ref[...]
        ea = jnp.exp(al_ref[...])
        dgv = dg_ref[...]
        da = dgv * (-ea * _sigmoid(x))
        da_ref[...] = da
        be = _sigmoid(b_ref[...])
        db_ref[...] = dbe_ref[...] * be * (1.0 - be)

        @pl.when(i == 0)
        def _():
            dal_ref[...] = jnp.zeros_like(dal_ref)
            ddt_ref[...] = jnp.zeros_like(ddt_ref)

        dal_ref[...] += jnp.sum(dgv * (-ea * _softplus(x)), axis=0, keepdims=True)
        ddt_ref[...] += jnp.sum(da, axis=0, keepdims=True)

    row = BS((bt, HV), lambda i: (i, 0))
    vec = BS((1, HV), lambda i: (0, 0))
    return pl.pallas_call(body, grid=(T // bt,), in_specs=[row, row, vec, vec, row, row],
                          out_specs=[row, row, vec, vec],
                          out_shape=[S((T, HV), F32)] * 2 + [S((1, HV), F32)] * 2,
                          compiler_params=_cp(("arbitrary",)), name=name)(a, b, a_log, dt_bias, dg, dbeta)


def _col(row_vec, eye):
    return jnp.sum(jnp.where(eye, row_vec, 0.0), axis=1, keepdims=True)


def _row(col_vec, eye):
    return jnp.sum(jnp.where(eye, col_vec, 0.0), axis=0, keepdims=True)


def _each(f, *lists):
    return [f(*args) for args in zip(*lists)]


def _mul(a, b):
    return a * b


def _hdot_nn(a, b):
    return dot_nn(a, b, hi="split")


def _delta_chunk(q, k, v, g_row, b_row, rep, tinv=None):
    C = CHUNK
    ii = lax.broadcasted_iota(jnp.int32, (C, C), 0)
    jj = lax.broadcasted_iota(jnp.int32, (C, C), 1)
    eye, causal, strict = ii == jj, ii >= jj, ii > jj
    q_v = [q[p // rep] for p in range(len(v))]
    k_v = [k[p // rep] for p in range(len(v))]
    g_col = _each(lambda g: _col(g, eye), g_row)
    gc_row = _each(lambda g: jnp.sum(jnp.where(ii <= jj, g, 0.0), axis=0, keepdims=True), g_col)
    gc_col = _each(lambda g: _col(g, eye), gc_row)
    gl = _each(lambda g: jnp.sum(jnp.where(jj[0:1, :] == C - 1, g, 0.0), axis=1, keepdims=True), gc_row)
    decay = _each(lambda gc, gr: jnp.where(causal, jnp.exp(jnp.where(causal, gc - gr, 0.0)), 0.0), gc_col, gc_row)
    b_col = _each(lambda b: _col(b, eye), b_row)
    kb = _each(_mul, k_v, b_col)
    vb = _each(_mul, v, b_col)
    m = _each(dot_nt, kb, k_v)
    a = _each(lambda m_, d_: jnp.where(strict, m_ * d_, 0.0), m, decay)
    if tinv is None:
        ident = jnp.where(eye, 1.0, 0.0)
        tinv = [ident - a_ for a_ in a]
        pw = _each(_hdot_nn, a, a)
        for step in range(5):
            tinv = _each(lambda t_, p_: t_ + _hdot_nn(t_, p_), tinv, pw)
            if step < 4:
                pw = _each(_hdot_nn, pw, pw)
    egc = _each(jnp.exp, gc_col)
    kbg = _each(_mul, kb, egc)
    u = _each(_hdot_nn, tinv, vb)
    w = _each(_hdot_nn, tinv, kbg)
    n_k = _each(dot_nt, q, k)
    n = [n_k[p // rep] for p in range(len(v))]
    attn = _each(lambda n_, d_: jnp.where(causal, n_ * d_, 0.0), n, decay)
    qg = _each(_mul, q_v, egc)
    ekl = _each(lambda l_, c_: jnp.exp(l_ - c_), gl, gc_col)
    ks = _each(_mul, k_v, ekl)
    dec = _each(jnp.exp, gl)
    return dict(eye=eye, causal=causal, strict=strict, ii=ii, jj=jj, q=q_v, k=k_v, gc_col=gc_col, gl=gl, decay=decay,
                b_col=b_col, kb=kb, vb=vb, m=m, tinv=tinv, egc=egc, kbg=kbg, u=u, w=w, n=n, attn=attn,
                qg=qg, ekl=ekl, ks=ks, dec=dec)


def delta_fwd(qkvn, g_rows, b_rows, key_dim, name):
    T = qkvn.shape[0]
    NK = key_dim // HEAD
    HV = g_rows.shape[0]
    rep = HV // NK
    NC = T // CHUNK

    P = DELTA_HEADS
    kw, vw = HEAD * P // rep, HEAD * P

    def body(q_ref, k_ref, v_ref, g_ref, b_ref, o_ref, st_ref, ti_ref, state):
        n = pl.program_id(1)

        @pl.when(n == 0)
        def _():
            state[...] = jnp.zeros_like(state)

        heads = range(P)
        q = [q_ref[:, pl.ds(kh * HEAD, HEAD)] for kh in range(P // rep)]
        k = [k_ref[:, pl.ds(kh * HEAD, HEAD)] for kh in range(P // rep)]
        v = [v_ref[:, pl.ds(p * HEAD, HEAD)] for p in heads]
        s0 = [state[p] for p in heads]
        c = _delta_chunk(q, k, v, [g_ref[p, 0] for p in heads], [b_ref[p, 0] for p in heads], rep)
        vn = _each(lambda u_, w_, s_: u_ - dot_nn(w_, s_), c["u"], c["w"], s0)
        o = _each(lambda qg_, s_, at_, vn_: dot_nn(qg_, s_) + dot_nn(at_, vn_), c["qg"], s0, c["attn"], vn)
        s1 = _each(lambda s_, d_, ks_, vn_: s_ * d_ + dot_tn(ks_, vn_), s0, c["dec"], c["ks"], vn)
        for p in heads:
            st_ref[p, 0] = s0[p]
            ti_ref[p, 0] = c["tinv"][p]
            o_ref[:, pl.ds(p * HEAD, HEAD)] = o[p]
            state[p] = s1[p]

    vrow = BS((P, 1, 1, CHUNK), lambda h, n: (h, n, 0, 0))
    return pl.pallas_call(
        body, grid=(HV // P, NC),
        in_specs=[BS((CHUNK, kw), lambda h, n: (n, h)),
                  BS((CHUNK, kw), lambda h, n: (n, key_dim // kw + h)),
                  BS((CHUNK, vw), lambda h, n: (n, 2 * key_dim // vw + h)), vrow, vrow],
        out_specs=[BS((CHUNK, vw), lambda h, n: (n, h)), BS((P, 1, HEAD, HEAD), lambda h, n: (h, n, 0, 0)),
                   BS((P, 1, CHUNK, CHUNK), lambda h, n: (h, n, 0, 0))],
        out_shape=[S((T, HV * HEAD), F32), S((HV, NC, HEAD, HEAD), F32), S((HV, NC, CHUNK, CHUNK), F32)],
        scratch_shapes=[pltpu.VMEM((P, HEAD, HEAD), F32)],
        compiler_params=_cp(("arbitrary", "arbitrary")), name=name)(qkvn, qkvn, qkvn, g_rows, b_rows)


def delta_bwd(qkvn, g_rows, b_rows, states, tinvs, do, key_dim, name):
    T = qkvn.shape[0]
    NK = key_dim // HEAD
    HV = g_rows.shape[0]
    rep = HV // NK
    NC = T // CHUNK
    P = DELTA_HEADS
    kw, vw = HEAD * P // rep, HEAD * P

    def body(q_ref, k_ref, v_ref, g_ref, b_ref, st_ref, ti_ref, do_ref, dq_ref, dk_ref, dv_ref, dg_ref, dbe_ref,
             dstate):
        step = pl.program_id(1)

        @pl.when(step == 0)
        def _():
            dstate[...] = jnp.zeros_like(dstate)

        heads = range(P)
        q_k = [q_ref[:, pl.ds(kh * HEAD, HEAD)] for kh in range(P // rep)]
        k_k = [k_ref[:, pl.ds(kh * HEAD, HEAD)] for kh in range(P // rep)]
        v = [v_ref[:, pl.ds(p * HEAD, HEAD)] for p in heads]
        s0 = [st_ref[p, 0] for p in heads]
        dsn = [dstate[p] for p in heads]
        dov = [do_ref[:, pl.ds(p * HEAD, HEAD)] for p in heads]
        c = _delta_chunk(q_k, k_k, v, [g_ref[p, 0] for p in heads], [b_ref[p, 0] for p in heads], rep,
                         tinv=[ti_ref[p, 0] for p in heads])
        eye, causal, strict = c["eye"], c["causal"], c["strict"]
        q, k, tinv, decay = c["q"], c["k"], c["tinv"], c["decay"]

        def rsum(a, b):
            return jnp.sum(a * b, axis=1, keepdims=True)

        vn = _each(lambda u_, w_, s_: u_ - dot_nn(w_, s_), c["u"], c["w"], s0)
        dvn = _each(lambda at_, do_, ks_, ds_: dot_tn(at_, do_) + dot_nn(ks_, ds_), c["attn"], dov, c["ks"], dsn)
        dattn = _each(lambda do_, vn_: jnp.where(causal, dot_nt(do_, vn_), 0.0), dov, vn)
        dqg = _each(dot_nt, dov, s0)
        dks = _each(dot_nt, vn, dsn)
        ddec = _each(lambda s_, ds_: jnp.sum(rsum(s_, ds_), axis=0, keepdims=True), s0, dsn)
        dw = _each(lambda dvn_, s_: -dot_nt(dvn_, s_), dvn, s0)
        ds0 = _each(lambda qg_, do_, ds_, d_, w_, dvn_: dot_tn(qg_, do_) + ds_ * d_ - dot_tn(w_, dvn_),
                    c["qg"], dov, dsn, c["dec"], c["w"], dvn)
        dvb = _each(lambda t_, x_: dot_tn(t_, x_, hi="split"), tinv, dvn)
        dkbg = _each(lambda t_, x_: dot_tn(t_, x_, hi="split"), tinv, dw)
        dt = _each(lambda dvn_, vb_, dw_, kbg_: dot_nt(dvn_, vb_, hi="split") + dot_nt(dw_, kbg_, hi="split"),
                   dvn, c["vb"], dw, c["kbg"])
        dtt = _each(lambda dt_, t_: dot_nt(dt_, t_, hi="split"), dt, tinv)
        da = _each(lambda t_, x_: jnp.where(strict, -dot_tn(t_, x_, hi="split"), 0.0), tinv, dtt)
        dm = _each(_mul, da, decay)
        dn = _each(_mul, dattn, decay)
        e = _each(lambda da_, m_, dat_, n_, d_: (da_ * m_ + dat_ * n_) * d_, da, c["m"], dattn, c["n"], decay)
        dkb = _each(lambda dm_, k_, dkbg_, egc_: dot_nn(dm_, k_) + dkbg_ * egc_, dm, k, dkbg, c["egc"])
        dk = _each(lambda dm_, kb_, dn_, q_, dks_, ekl_, dkb_, b_: dot_tn(dm_, kb_) + dot_tn(dn_, q_) + dks_ * ekl_
                   + dkb_ * b_, dm, c["kb"], dn, q, dks, c["ekl"], dkb, c["b_col"])
        dq = _each(lambda dn_, k_, dqg_, egc_: dot_nn(dn_, k_) + dqg_ * egc_, dn, k, dqg, c["egc"])
        dks_ks = _each(rsum, dks, c["ks"])
        dgc_col = _each(lambda e_, dkbg_, kbg_, dqg_, qg_, x_: jnp.sum(e_, axis=1, keepdims=True) + rsum(dkbg_, kbg_)
                        + rsum(dqg_, qg_) - x_ - _col(jnp.sum(e_, axis=0, keepdims=True), eye),
                        e, dkbg, c["kbg"], dqg, c["qg"], dks_ks)
        dgl = _each(lambda x_, dd_, d_: jnp.sum(x_, axis=0, keepdims=True) + dd_ * d_, dks_ks, ddec, c["dec"])
        last = c["ii"][:, 0:1] == CHUNK - 1
        dgc_col = _each(lambda g_, l_: g_ + jnp.where(last, l_, 0.0), dgc_col, dgl)
        dbe_col = _each(lambda dvb_, v_, dkb_, k_: rsum(dvb_, v_) + rsum(dkb_, k_), dvb, v, dkb, k)
        for p in heads:
            vc = pl.ds(p * HEAD, HEAD)
            dstate[p] = ds0[p]
            dq_ref[:, vc] = dq[p]
            dk_ref[:, vc] = dk[p]
            dv_ref[:, vc] = dvb[p] * c["b_col"][p]
            dbe_ref[p, 0] = _row(dbe_col[p], eye)
            dg_ref[p, 0] = jnp.sum(jnp.where(causal, dgc_col[p], 0.0), axis=0, keepdims=True)

    rev = lambda n: NC - 1 - n
    vrow = BS((P, 1, 1, CHUNK), lambda h, n: (h, rev(n), 0, 0))
    vblk = BS((CHUNK, vw), lambda h, n: (rev(n), h))
    return pl.pallas_call(
        body, grid=(HV // P, NC),
        in_specs=[BS((CHUNK, kw), lambda h, n: (rev(n), h)),
                  BS((CHUNK, kw), lambda h, n: (rev(n), key_dim // kw + h)),
                  BS((CHUNK, vw), lambda h, n: (rev(n), 2 * key_dim // vw + h)), vrow, vrow,
                  BS((P, 1, HEAD, HEAD), lambda h, n: (h, rev(n), 0, 0)),
                  BS((P, 1, CHUNK, CHUNK), lambda h, n: (h, rev(n), 0, 0)), vblk],
        out_specs=[vblk, vblk, vblk, vrow, vrow],
        out_shape=[S((T, HV * HEAD), F32)] * 3 + [S((HV, NC, 1, CHUNK), F32)] * 2,
        scratch_shapes=[pltpu.VMEM((P, HEAD, HEAD), F32)],
        compiler_params=_cp(("arbitrary", "arbitrary")), name=name)(
            qkvn, qkvn, qkvn, g_rows, b_rows, states, tinvs, do)


def gdn_post_fwd(o, proj, gate_col0, o_gain, name):
    T, V = o.shape
    cb = min(1024, V)
    j0 = gate_col0 // cb
    bt = _pick(T, (256, 128, 64))

    def body(o_ref, g_ref, gain_ref, y_ref):
        for hh in range(cb // HEAD):
            cols = pl.ds(hh * HEAD, HEAD)
            ov = o_ref[:, cols]
            gt = g_ref[:, cols]
            r = lax.rsqrt(jnp.mean(ov * ov, axis=-1, keepdims=True) + EPS)
            y_ref[:, cols] = (((ov * r) * gain_ref[...]) * (gt * _sigmoid(gt))).astype(y_ref.dtype)

    return pl.pallas_call(
        body, grid=(T // bt, V // cb),
        in_specs=[BS((bt, cb), lambda i, j: (i, j)), BS((bt, cb), lambda i, j: (i, j0 + j)),
                  BS((1, HEAD), lambda i, j: (0, 0))],
        out_specs=BS((bt, cb), lambda i, j: (i, j)), out_shape=S((T, V), MXU_DTYPE),
        compiler_params=_cp(("parallel", "parallel")), name=name)(o, proj, o_gain)


def gdn_post_bwd(o, proj, gate_col0, o_gain, dy, name):
    T, V = o.shape
    cb = min(1024, V)
    j0 = gate_col0 // cb
    bt = _pick(T, (256, 128, 64))

    def body(o_ref, g_ref, gain_ref, dy_ref, do_ref, dgt_ref, dgain_ref):
        i, j = pl.program_id(0), pl.program_id(1)

        @pl.when(jnp.logical_and(i == 0, j == 0))
        def _():
            dgain_ref[...] = jnp.zeros_like(dgain_ref)

        gain = gain_ref[...]
        for hh in range(cb // HEAD):
            cols = pl.ds(hh * HEAD, HEAD)
            ov = o_ref[:, cols]
            gt = g_ref[:, cols]
            dyv = dy_ref[:, cols]
            r = lax.rsqrt(jnp.mean(ov * ov, axis=-1, keepdims=True) + EPS)
            ohat = ov * r
            sg = _sigmoid(gt)
            dyn = dyv * (gt * sg)
            dgt_ref[:, cols] = (dyv * (ohat * gain) * (sg * (1.0 + gt * (1.0 - sg)))).astype(dgt_ref.dtype)
            dgain_ref[...] += jnp.sum(dyn * ohat, axis=0, keepdims=True)
            dh = dyn * gain
            do_ref[:, cols] = r * (dh - ohat * jnp.mean(dh * ohat, axis=-1, keepdims=True))

    blk = BS((bt, cb), lambda i, j: (i, j))
    vec = BS((1, HEAD), lambda i, j: (0, 0))
    return pl.pallas_call(
        body, grid=(T // bt, V // cb),
        in_specs=[blk, BS((bt, cb), lambda i, j: (i, j0 + j)), vec, blk],
        out_specs=[blk, blk, vec], out_shape=[S((T, V), F32), S((T, V), MXU_DTYPE), S((1, HEAD), F32)],
        compiler_params=_cp(("arbitrary", "arbitrary")), name=name)(o, proj, o_gain, dy)


def adamw(w, g, m, v, name):
    shape = w.shape
    n = 1
    for d in shape:
        n *= d
    cols = shape[-1] if len(shape) > 1 else n
    rows = n // cols
    br = rows
    for cand in (512, 256, 128, 64, 32, 16, 8):
        if rows % cand == 0 and cand * cols * 4 * 14 <= 36 * MIB:
            br = cand
            break
    c1 = 1.0 - ADAM_B1 ** ADAM_STEP
    c2 = 1.0 - ADAM_B2 ** ADAM_STEP

    def body(w_ref, g_ref, m_ref, v_ref, d_ref, nm_ref, nv_ref):
        gv = g_ref[...]
        nm = ADAM_B1 * m_ref[...] + (1.0 - ADAM_B1) * gv
        nv = ADAM_B2 * v_ref[...] + (1.0 - ADAM_B2) * (gv * gv)
        nm_ref[...] = nm
        nv_ref[...] = nv
        d_ref[...] = -ADAM_LR * ((nm / c1) / (jnp.sqrt(nv / c2) + ADAM_EPS) + ADAM_WD * w_ref[...])

    blk = BS((br, cols), lambda i: (i, 0))
    outs = pl.pallas_call(
        body, grid=(rows // br,), in_specs=[blk] * 4, out_specs=[blk] * 3,
        out_shape=[S((rows, cols), F32)] * 3, compiler_params=_cp(("parallel",), 48), name=name)(
            *[t.reshape(rows, cols) for t in (w, g, m, v)])
    return [t.reshape(shape) for t in outs]


def _with_job(job, *args, **kwargs):
    if job is None:
        return matmul(*args, **kwargs), []
    return matmul(*args, job=job, **kwargs)


def _ffn_fwd(x, gain, p, tag, jobs=(None, None)):
    (h2,) = norm_fwd(x, gain, [MXU_DTYPE], f"{tag}_norm")
    up, got_a = _with_job(jobs[0], h2, p["w_up_t"], "nt", out_dtype=MXU_DTYPE, name=f"{tag}_up")
    act = ffn_act_fwd(up, p["conv"], f"{tag}_act")
    out, got_b = _with_job(jobs[1], act, p["w_down"], "nn", res=x, name=f"{tag}_down")
    return out, (x, h2, up, act), got_a + got_b


def _ffn_bwd(dx, saved, gain, p, tag, first_job=None, then_jobs=None):
    x, h2, up, act = saved
    g_act, got = _with_job(first_job, dx[1], p["w_down"], "nt", out_dtype=MXU_DTYPE, name=f"{tag}_bdown")
    jobs = then_jobs(got) if then_jobs else (None, None)
    d_down = matmul(act, dx[1], "tn", name=f"{tag}_wdown")
    du, db, d_conv = ffn_act_bwd_a(up, p["conv"], g_act, f"{tag}_bact_a")
    dup = ffn_act_bwd_b(du, db, p["conv"], f"{tag}_bact_b")
    dh2, got_a = _with_job(jobs[0], dup, p["w_up_t"], "nn", name=f"{tag}_bup")
    d_up_t, got_b = _with_job(jobs[1], dup, h2, "tn", name=f"{tag}_wup")
    dx, d_gain = norm_bwd(x, gain, dh2, dx[0], f"{tag}_bnorm")
    return dx, d_gain, dict(w_up_t=d_up_t, conv=d_conv, w_down=d_down), got_a + got_b


def _joint_bias(bias):
    rows = [jnp.pad(bias, ((0, 0), (0, 0), (qc * CHUNK, (ATT_CHUNKS - 1 - qc) * CHUNK)), constant_values=MASK_VALUE)
            for qc in range(ATT_CHUNKS)]
    return jnp.concatenate(rows, axis=1)


def _joint_bias_grad(dbias):
    return sum(dbias[:, qc * CHUNK:(qc + 1) * CHUNK, qc * CHUNK:qc * CHUNK + BAND] for qc in range(ATT_CHUNKS))


def _att_fwd(x, gain, p, tag):
    (h,) = norm_fwd(x, gain, [MXU_DTYPE], f"{tag}_norm")
    qkv = matmul(h, p["w_qkv_t"], "nt", name=f"{tag}_qkv")
    bias = _joint_bias(rel_bias_expand(p["rel_bias"], f"{tag}_rel").transpose(1, 0, 2))
    o = attn_fwd(qkv, p["q_gain"], p["k_gain"], bias, f"{tag}_core")
    out = matmul(o, p["w_o"], "nn", res=x, name=f"{tag}_out")
    return out, (x, h, qkv, o, bias)


def _att_bwd(dx, saved, gain, p, tag):
    x, h, qkv, o, bias = saved
    do = matmul(dx[1], p["w_o"], "nt", name=f"{tag}_bout")
    d_wo = matmul(o, dx[1], "tn", name=f"{tag}_wout")
    dq, dk, dv, d_gq, d_gk, dbias = attn_bwd(qkv, do, p["q_gain"], p["k_gain"], bias, f"{tag}_bcore")
    dqkv = jnp.concatenate([dq, dk, dv], axis=1)
    dh = matmul(dqkv, p["w_qkv_t"], "nn", name=f"{tag}_bqkv")
    d_wqkv_t = matmul(dqkv, h, "tn", name=f"{tag}_wqkv")
    d_rb = rel_bias_reduce(_joint_bias_grad(dbias).transpose(1, 0, 2), p["rel_bias"].shape[1], f"{tag}_brel")
    dx, d_gain = norm_bwd(x, gain, dh, dx[0], f"{tag}_bnorm")
    return dx, d_gain, dict(w_qkv_t=d_wqkv_t, w_o=d_wo, q_gain=d_gq, k_gain=d_gk, rel_bias=d_rb)


def _pool_fwd(x, gain, p, tag):
    (hf,) = norm_fwd(x, gain, [F32], f"{tag}_norm")
    out, pooled = pool_fwd(hf, x, p["w"], p["scale"], f"{tag}_core")
    return out, (x, pooled)


def _pool_bwd(dx, saved, gain, p, tag):
    x, pooled = saved
    dpooled, d_w, d_scale = pool_bwd_a(dx[0], pooled, p["w"], p["scale"], f"{tag}_bcore_a")
    dh = pool_bwd_b(dpooled, f"{tag}_bcore_b")
    dx, d_gain = norm_bwd(x, gain, dh, dx[0], f"{tag}_bnorm")
    return dx, d_gain, dict(w=d_w, scale=d_scale)


def _rows_layout(t, hv):
    return t.T.reshape(hv, t.shape[0] // CHUNK, 1, CHUNK)


def _gdn_fwd(x, gain, p, tag):
    T = x.shape[0]
    hv = p["a_log"].shape[1]
    key_dim = p["key_dim"]
    C = p["conv"].shape[1]
    (h,) = norm_fwd(x, gain, [MXU_DTYPE], f"{tag}_norm")
    proj = matmul(h, p["w_main_t"], "nt", name=f"{tag}_in")
    ab = matmul(h, p["w_ab_t"], "nt", name=f"{tag}_in_ab")
    a, b = ab[:, :hv], ab[:, hv:2 * hv]
    qkvn = gdn_pre_fwd(proj, p["conv"], key_dim, f"{tag}_pre")
    g, beta = gdn_gate_fwd(a, b, p["a_log"], p["dt_bias"], f"{tag}_gate")
    g_rows, b_rows = _rows_layout(g, hv), _rows_layout(beta, hv)
    o, states, tinvs = delta_fwd(qkvn, g_rows, b_rows, key_dim, f"{tag}_delta")
    y = gdn_post_fwd(o, proj, C, p["o_gain"], f"{tag}_post")
    out = matmul(y, p["w_o"], "nn", res=x, name=f"{tag}_out")
    return out, (x, h, proj, a, b, qkvn, g_rows, b_rows, o, states, tinvs, y)


def _gdn_bwd(dx, saved, gain, p, tag):
    x, h, proj, a, b, qkvn, g_rows, b_rows, o, states, tinvs, y = saved
    T = x.shape[0]
    hv = p["a_log"].shape[1]
    key_dim = p["key_dim"]
    C = p["conv"].shape[1]
    dy = matmul(dx[1], p["w_o"], "nt", name=f"{tag}_bout")
    d_wo = matmul(y, dx[1], "tn", name=f"{tag}_wout")
    do, dgate, d_ogain = gdn_post_bwd(o, proj, C, p["o_gain"], dy, f"{tag}_bpost")
    dq_v, dk_v, dv, dg_rows, dbe_rows = delta_bwd(qkvn, g_rows, b_rows, states, tinvs, do, key_dim,
                                                  f"{tag}_bdelta")
    dg = dg_rows.reshape(hv, T).T
    dbeta = dbe_rows.reshape(hv, T).T
    da, db, d_alog, d_dtb = gdn_gate_bwd(a, b, p["a_log"], p["dt_bias"], dg, dbeta, f"{tag}_bgate")
    du, d_conv = gdn_pre_bwd_a(proj, p["conv"], dq_v, dk_v, dv, key_dim, f"{tag}_bpre_a")
    dproj = gdn_pre_bwd_b(du, dgate, p["conv"], key_dim, f"{tag}_bpre_b")
    dab = jnp.concatenate([da, db, jnp.zeros((T, LANES - 2 * hv), F32)], axis=1)
    dh = matmul(dproj, p["w_main_t"], "nn", name=f"{tag}_bin")
    dh = matmul(dab, p["w_ab_t"], "nn", res=dh, name=f"{tag}_bin_ab")
    d_main_t = matmul(dproj, h, "tn", name=f"{tag}_win")
    d_ab_t = matmul(dab, h, "tn", name=f"{tag}_win_ab")
    dx, d_gain = norm_bwd(x, gain, dh, dx[0], f"{tag}_bnorm")
    return dx, d_gain, dict(w_main_t=d_main_t, w_ab_t=d_ab_t, conv=d_conv, a_log=d_alog, dt_bias=d_dtb,
                            o_gain=d_ogain, w_o=d_wo)


_MIXERS = ((_att_fwd, _att_bwd), (_pool_fwd, _pool_bwd), (_gdn_fwd, _gdn_bwd))


def local_step(x, target, full, slots, place):
    depth = full["ffn_conv"].shape[0]
    groups = _layer_groups(depth)
    core = place[1:2]
    big = dict(zip(groups[0], gather_slots([slots[k] for k in groups[0]], "gather_l0")))
    saved = []
    for i in range(depth):
        kind, j = i % 3, i // 3
        mp, fp = layer_weights(i, big, full)
        x, s_mix = _MIXERS[kind][0](x, full["mix_norm"][i:i + 1], mp, f"l{i}_mix")
        jobs, keys = (None, None), ()
        if i + 1 < depth:
            keys = groups[i + 1][0::2] + groups[i + 1][1::2]
            jobs = (gather_job([slots[k] for k in groups[i + 1][0::2]]),
                    gather_job([slots[k] for k in groups[i + 1][1::2]]))
        x, s_ffn, got = _ffn_fwd(x, full["ffn_norm"][i:i + 1], fp, f"l{i}_ffn", jobs)
        saved.append((s_mix, s_ffn, mp, fp))
        if keys:
            big = dict(zip(keys, gather_pass(got, f"gather_pass_l{i + 1}")))
    loss, dx = loss_and_grad(x, target, "loss")
    n_mix = [len([i for i in range(depth) if i % 3 == kind]) for kind in range(3)]
    G = dict(mix_norm=[None] * depth, ffn_norm=[None] * depth, ffn=[None] * depth,
             att=[None] * n_mix[0], pool=[None] * n_mix[1], gdn=[None] * n_mix[2])
    summed = {}

    def finish(keys, chip_sums, got, tag):
        mine = [add_cols4(p, b, place, f"rs_add4_{n}_{j}") for (n, j), p, b in zip(keys, chip_sums, got)]
        summed.update(zip(keys, join_cols(mine, f"rs_join_{tag}")))

    def chip_sum(keys, gs, theirs):
        return [add_cols2(g, a, core, f"rs_add2_{n}_{j}") for (n, j), g, a in zip(keys, gs, theirs)]

    pending = None
    for i in reversed(range(depth)):
        kind, j = i % 3, i // 3
        s_mix, s_ffn, mp, fp = saved[i]
        if pending:
            keys, gs = pending
            order = keys[0::2] + keys[1::2]
            sums = []

            def then_jobs(theirs, keys=keys, gs=gs, sums=sums):
                sums.extend(chip_sum(keys, gs, theirs))
                return scatter_job(sums[0::2]), scatter_job(sums[1::2])

            dx, G["ffn_norm"][i], gf, got = _ffn_bwd(dx, s_ffn, full["ffn_norm"][i:i + 1], fp, f"l{i}_ffn",
                                                     swap_job(gs), then_jobs)
            finish(order, sums[0::2] + sums[1::2], got, f"l{i + 1}")
        else:
            dx, G["ffn_norm"][i], gf, _ = _ffn_bwd(dx, s_ffn, full["ffn_norm"][i:i + 1], fp, f"l{i}_ffn")
        dx, G["mix_norm"][i], gm = _MIXERS[kind][1](dx, s_mix, full["mix_norm"][i:i + 1], mp, f"l{i}_mix")
        G["ffn"][i] = gf
        G[("att", "pool", "gdn")[kind]][j] = gm
        gbig = layer_big_grads(i, gm, gf, full["gdn_a_log"].shape[1])
        pending = (groups[i], [gbig[k] for k in groups[i]])
    keys, gs = pending
    sums = chip_sum(keys, gs, swap_cols(gs, "rs_swap_l0"))
    finish(keys, sums, scatter_cols(sums, "rs_scatter_l0"), "l0")
    return loss, dx[0], summed, G


SHARD_AXIS = dict(att_w_qkv=2, att_w_o=1, pool_w=2, gdn_w_in=2, gdn_w_o=1, ffn_w_up=2, ffn_w_down=1,
                  att_rel_bias=2, gdn_conv=2, ffn_conv=2)
BIG = ("att_w_qkv", "att_w_o", "pool_w", "gdn_w_in", "gdn_w_o", "ffn_w_up", "ffn_w_down")
SMALL_SHARDED = ("att_rel_bias", "gdn_conv", "ffn_conv")
REPLICATED = ("mix_norm", "ffn_norm", "att_q_gain", "att_k_gain", "pool_scale", "gdn_a_log", "gdn_dt_bias",
              "gdn_o_gain")
WEIGHTS = ("mix_norm", "ffn_norm", "att_w_qkv", "att_q_gain", "att_k_gain", "att_rel_bias", "att_w_o", "pool_w",
           "pool_scale", "gdn_w_in", "gdn_conv", "gdn_a_log", "gdn_dt_bias", "gdn_o_gain", "gdn_w_o", "ffn_w_up",
           "ffn_conv", "ffn_w_down")


def _merge(stacked, axis):
    t = jnp.moveaxis(stacked, 0, axis)
    return t.reshape(t.shape[:axis] + (t.shape[axis] * t.shape[axis + 1],) + t.shape[axis + 2:])


def _pad_to(t, axis, size):
    pad = [(0, 0)] * t.ndim
    pad[axis] = (0, size - t.shape[axis])
    return jnp.pad(t, pad)


def _round_up(n, m):
    return (n + m - 1) // m * m


def to_comm(name, t):
    if name in ("att_w_qkv", "gdn_w_in"):
        return t.T
    if name == "ffn_w_up":
        d, n = t.shape
        return _pad_to(t.T.reshape(2, n // 2, d), 1, _round_up(n // 2, LANES)).reshape(-1, d)
    if name == "ffn_w_down":
        return _pad_to(t, 0, _round_up(t.shape[0], LANES))
    if name == "pool_w":
        return t.reshape(-1, t.shape[-1])
    return t


def from_comm(name, r, shape):
    if name in ("att_w_qkv", "gdn_w_in"):
        return r.T
    if name == "ffn_w_up":
        d, n = shape
        return r.reshape(2, -1, d)[:, :n // 2].reshape(n, d).T
    if name == "ffn_w_down":
        return r[:shape[0]]
    return r.reshape(shape)


def _rows(t):
    return t.reshape(-1, t.shape[-1])


def layer_weights(i, big, full):
    kind, j = i % 3, i // 3
    F4 = full["ffn_conv"].shape[2] // N_CHIPS
    conv = _pad_to(full["ffn_conv"][i].reshape(FFN_CONV, N_CHIPS, F4), 2, _round_up(F4, LANES)).reshape(FFN_CONV, -1)
    fp = dict(w_up_t=_rows(big["ffn_w_up", i]), conv=_pad_to(conv, 0, SUBLANES), w_down=_rows(big["ffn_w_down", i]))
    if kind == 0:
        mp = dict(w_qkv_t=_rows(big["att_w_qkv", j]), w_o=_rows(big["att_w_o", j]),
                  q_gain=full["att_q_gain"][j:j + 1], k_gain=full["att_k_gain"][j:j + 1],
                  rel_bias=full["att_rel_bias"][j])
    elif kind == 1:
        t = big["pool_w", j]
        G = len(POOL_WINDOWS)
        dg = t.shape[-1]
        w = jnp.moveaxis(t.reshape(N_CHIPS, G, dg // N_CHIPS, dg), 0, 1).reshape(G, dg, dg)
        mp = dict(w=w, scale=full["pool_scale"][j:j + 1])
    else:
        C = full["gdn_conv"].shape[2]
        wt = _rows(big["gdn_w_in", j])
        V = _rows(big["gdn_w_o", j]).shape[0]
        mp = dict(w_main_t=wt[:C + V], w_ab_t=_pad_to(wt[C + V:], 0, LANES),
                  conv=_pad_to(full["gdn_conv"][j], 0, SUBLANES), a_log=full["gdn_a_log"][j:j + 1],
                  dt_bias=full["gdn_dt_bias"][j:j + 1], o_gain=full["gdn_o_gain"][j:j + 1],
                  w_o=_rows(big["gdn_w_o", j]), key_dim=(C - V) // 2)
    return mp, fp


def layer_big_grads(i, gm, gf, hv):
    kind, j = i % 3, i // 3

    def slots(t):
        return t.reshape(N_CHIPS, t.shape[0] // N_CHIPS, t.shape[1])

    out = {("ffn_w_up", i): slots(gf["w_up_t"]), ("ffn_w_down", i): slots(gf["w_down"])}
    if kind == 0:
        out["att_w_qkv", j] = slots(gm["w_qkv_t"])
        out["att_w_o", j] = slots(gm["w_o"])
    elif kind == 1:
        n, dg, _ = gm["w"].shape
        out["pool_w", j] = jnp.moveaxis(gm["w"].reshape(n, N_CHIPS, dg // N_CHIPS, dg), 1, 0).reshape(N_CHIPS, -1, dg)
    else:
        out["gdn_w_in", j] = slots(jnp.concatenate([gm["w_main_t"], gm["w_ab_t"][:2 * hv]], axis=0))
        out["gdn_w_o", j] = slots(gm["w_o"])
    return out


def small_grads(G, full):
    F = full["ffn_conv"].shape[2]
    F4 = F // N_CHIPS

    def conv(g):
        return g["conv"][:FFN_CONV].reshape(FFN_CONV, N_CHIPS, -1)[:, :, :F4].reshape(FFN_CONV, F)

    return dict(
        mix_norm=jnp.concatenate(G["mix_norm"], axis=0), ffn_norm=jnp.concatenate(G["ffn_norm"], axis=0),
        ffn_conv=jnp.stack([conv(g) for g in G["ffn"]]),
        att_q_gain=jnp.concatenate([g["q_gain"] for g in G["att"]], axis=0),
        att_k_gain=jnp.concatenate([g["k_gain"] for g in G["att"]], axis=0),
        att_rel_bias=jnp.stack([g["rel_bias"] for g in G["att"]]),
        pool_scale=jnp.concatenate([g["scale"] for g in G["pool"]], axis=0),
        gdn_conv=jnp.stack([g["conv"][:GDN_CONV] for g in G["gdn"]]),
        gdn_a_log=jnp.concatenate([g["a_log"] for g in G["gdn"]], axis=0),
        gdn_dt_bias=jnp.concatenate([g["dt_bias"] for g in G["gdn"]], axis=0),
        gdn_o_gain=jnp.concatenate([g["o_gain"] for g in G["gdn"]], axis=0))


ANY = BS(memory_space=pl.ANY)
PACK_COLS = 1024
PACK_ROWS = 32


def _place():
    x, y, c = lax.axis_index("x"), lax.axis_index("y"), lax.axis_index("c")
    chips = [(1 - x, y), (x, 1 - y), (1 - x, 1 - y)]
    return x, y, c, chips


def _remote(src, dst, send_sem, recv_sem, to):
    return pltpu.make_async_remote_copy(src_ref=src, dst_ref=dst, send_sem=send_sem, recv_sem=recv_sem,
                                        device_id=to, device_id_type=MESH)


def gather_chips(shard, name):
    R, C = shard.shape
    half = R // 2

    def body(x_ref, o_ref, send_sems, recv_sems, local_sem):
        x, y, c, chips = _place()
        mine_rows = pl.ds(c * half, half)
        other_rows = pl.ds((1 - c) * half, half)
        own = pltpu.make_async_copy(x_ref, o_ref.at[2 * x + y], local_sem)
        own.start()
        first = [_remote(x_ref.at[mine_rows], o_ref.at[2 * x + y, mine_rows], send_sems.at[j], recv_sems.at[j],
                         (cx, cy, c)) for j, (cx, cy) in enumerate(chips)]
        for cp in first:
            cp.start()
        passed = []
        for j, (cx, cy) in enumerate(chips):
            landed = o_ref.at[2 * cx + cy, mine_rows]
            _remote(landed, landed, send_sems.at[j], recv_sems.at[j], (cx, cy, c)).wait_recv()
            cp = _remote(landed, landed, send_sems.at[3 + j], recv_sems.at[3 + j], (x, y, 1 - c))
            cp.start()
            passed.append(cp)
        for j, (cx, cy) in enumerate(chips):
            landed = o_ref.at[2 * cx + cy, other_rows]
            _remote(landed, landed, send_sems.at[3 + j], recv_sems.at[3 + j], (x, y, 1 - c)).wait_recv()
        for cp in first + passed:
            cp.wait_send()
        own.wait()

    return pl.pallas_call(
        body, out_shape=S((N_CHIPS, R, C), shard.dtype), in_specs=[ANY], out_specs=ANY,
        scratch_shapes=[pltpu.SemaphoreType.DMA((6,)), pltpu.SemaphoreType.DMA((6,)), pltpu.SemaphoreType.DMA],
        name=name)(shard)


def _tile(R, hc):
    if R % 256 == 0:
        return _pick(R, (512, 256)), hc
    return R, _pick(hc, (256, 128))


def _half(c, hc):
    return pl.ds(pl.multiple_of(c * hc, hc), hc)


def prep_slot(t, s_me, name):
    R, C = t.shape
    br, bc = _tile(R, C // 2)

    def body(s_ref, t_ref, o_ref):
        o_ref[0] = t_ref[...].astype(o_ref.dtype)

    return pl.pallas_call(
        body, grid_spec=pltpu.PrefetchScalarGridSpec(
            num_scalar_prefetch=1, grid=(R // br, C // bc),
            in_specs=[BS((br, bc), lambda i, j, s: (i, j))],
            out_specs=BS((1, br, bc), lambda i, j, s: (s[0], i, j))),
        out_shape=S((N_CHIPS, R, C), MXU_DTYPE), compiler_params=_cp(("parallel", "parallel")), name=name)(
            s_me.reshape(1), t)


def gather_slots(arrs, name):
    nt = len(arrs)

    def body(*refs):
        outs = refs[nt:2 * nt]
        send_sems, recv_sems = refs[2 * nt:]
        x, y, c, chips = _place()
        me = 2 * x + y
        first, passed = [], []
        for t, o in enumerate(outs):
            mine = _half(c, o.shape[2] // 2)
            for j, (cx, cy) in enumerate(chips):
                cp = _remote(o.at[me, :, mine], o.at[me, :, mine], send_sems.at[t, j], recv_sems.at[t, j], (cx, cy, c))
                cp.start()
                first.append(cp)
        for t, o in enumerate(outs):
            mine = _half(c, o.shape[2] // 2)
            for j, (cx, cy) in enumerate(chips):
                landed = o.at[2 * cx + cy, :, mine]
                _remote(landed, landed, send_sems.at[t, j], recv_sems.at[t, j], (cx, cy, c)).wait_recv()
                cp = _remote(landed, landed, send_sems.at[t, 3 + j], recv_sems.at[t, 3 + j], (x, y, 1 - c))
                cp.start()
                passed.append(cp)
        for t, o in enumerate(outs):
            other = _half(1 - c, o.shape[2] // 2)
            for j, (cx, cy) in enumerate(chips):
                landed = o.at[2 * cx + cy, :, other]
                _remote(landed, landed, send_sems.at[t, 3 + j], recv_sems.at[t, 3 + j], (x, y, 1 - c)).wait_recv()
        for cp in first + passed:
            cp.wait_send()

    return pl.pallas_call(
        body, out_shape=[S(a.shape, a.dtype) for a in arrs], in_specs=[ANY] * nt, out_specs=[ANY] * nt,
        input_output_aliases={t: t for t in range(nt)},
        scratch_shapes=[pltpu.SemaphoreType.DMA((nt, 6)), pltpu.SemaphoreType.DMA((nt, 6))], name=name)(*arrs)


def swap_cols(gs, name):
    nt = len(gs)

    def body(*refs):
        ins, outs = refs[:nt], refs[nt:2 * nt]
        send_sems, recv_sems = refs[2 * nt:]
        x, y, c, _ = _place()
        sent = []
        for t, (g, o) in enumerate(zip(ins, outs)):
            cp = _remote(g.at[:, :, _half(1 - c, o.shape[2])], o, send_sems.at[t], recv_sems.at[t], (x, y, 1 - c))
            cp.start()
            sent.append(cp)
        for cp in sent:
            cp.wait()

    return pl.pallas_call(
        body, out_shape=[S(g.shape[:2] + (g.shape[2] // 2,), g.dtype) for g in gs], in_specs=[ANY] * nt,
        out_specs=[ANY] * nt, scratch_shapes=[pltpu.SemaphoreType.DMA((nt,)), pltpu.SemaphoreType.DMA((nt,))],
        name=name)(*gs)


def add_cols2(g, other, c, name):
    n, R, C = g.shape
    hc = C // 2
    br, bc = _tile(R, hc)
    nj = hc // bc

    def body(c_ref, g_ref, o_ref, out_ref):
        out_ref[...] = (g_ref[...] + o_ref[...]).astype(out_ref.dtype)

    blk = BS((1, br, bc), lambda s, i, j, c_ref: (s, i, j))
    return pl.pallas_call(
        body, grid_spec=pltpu.PrefetchScalarGridSpec(
            num_scalar_prefetch=1, grid=(n, R // br, nj),
            in_specs=[BS((1, br, bc), lambda s, i, j, c_ref: (s, i, c_ref[0] * nj + j)), blk], out_specs=blk),
        out_shape=S((n, R, hc), BF16), compiler_params=_cp(("parallel", "parallel", "parallel")), name=name)(
            c, g, other)


def scatter_cols(ps, name):
    nt = len(ps)

    def body(*refs):
        ins, outs = refs[:nt], refs[nt:2 * nt]
        send_sems, recv_sems = refs[2 * nt:]
        x, y, c, chips = _place()
        sent = []
        for t, (p, o) in enumerate(zip(ins, outs)):
            for j, (cx, cy) in enumerate(chips):
                cp = _remote(p.at[2 * cx + cy], o.at[j], send_sems.at[t, j], recv_sems.at[t, j], (cx, cy, c))
                cp.start()
                sent.append(cp)
        for cp in sent:
            cp.wait()

    return pl.pallas_call(
        body, out_shape=[S((N_CHIPS - 1,) + p.shape[1:], p.dtype) for p in ps], in_specs=[ANY] * nt,
        out_specs=[ANY] * nt,
        scratch_shapes=[pltpu.SemaphoreType.DMA((nt, 3)), pltpu.SemaphoreType.DMA((nt, 3))], name=name)(*ps)


def add_cols4(p, got, place, name):
    n, R, hc = p.shape
    br, bc = _tile(R, hc)
    nj = hc // bc

    def body(pl_ref, p_ref, g_ref, out_ref):
        acc = p_ref[0].astype(F32)
        for j in range(n - 1):
            acc += g_ref[j].astype(F32)
        out_ref[...] = acc

    return pl.pallas_call(
        body, grid_spec=pltpu.PrefetchScalarGridSpec(
            num_scalar_prefetch=1, grid=(R // br, nj),
            in_specs=[BS((1, br, bc), lambda i, j, pl_ref: (pl_ref[0], i, j)),
                      BS((n - 1, br, bc), lambda i, j, pl_ref: (0, i, j))],
            out_specs=BS((br, bc), lambda i, j, pl_ref: (i, pl_ref[1] * nj + j))),
        out_shape=S((R, 2 * hc), F32), compiler_params=_cp(("parallel", "parallel")), name=name)(place, p, got)


def join_cols(rs, name):
    nt = len(rs)

    def body(*refs):
        outs = refs[nt:2 * nt]
        send_sems, recv_sems = refs[2 * nt:]
        x, y, c, _ = _place()
        sent = []
        for t, o in enumerate(outs):
            mine = o.at[:, _half(c, o.shape[1] // 2)]
            cp = _remote(mine, mine, send_sems.at[t], recv_sems.at[t], (x, y, 1 - c))
            cp.start()
            sent.append(cp)
        for t, o in enumerate(outs):
            theirs = o.at[:, _half(1 - c, o.shape[1] // 2)]
            _remote(theirs, theirs, send_sems.at[t], recv_sems.at[t], (x, y, 1 - c)).wait_recv()
        for cp in sent:
            cp.wait_send()

    return pl.pallas_call(
        body, out_shape=[S(r.shape, r.dtype) for r in rs], in_specs=[ANY] * nt, out_specs=[ANY] * nt,
        input_output_aliases={t: t for t in range(nt)},
        scratch_shapes=[pltpu.SemaphoreType.DMA((nt,)), pltpu.SemaphoreType.DMA((nt,))], name=name)(*rs)


def gather_job(slots):
    nt = len(slots)

    def copies(outs, send_sems, recv_sems, sent):
        x, y, c, chips = _place()
        me = 2 * x + y
        out = []
        for t, o in enumerate(outs):
            mine = _half(c, o.shape[2] // 2)
            for j, (cx, cy) in enumerate(chips):
                rows = o.at[me if sent else 2 * cx + cy, :, mine]
                out.append(_remote(rows, rows, send_sems.at[3 * t + j], recv_sems.at[3 * t + j], (cx, cy, c)))
        return out

    def start(ins, outs, send_sems, recv_sems):
        for cp in copies(outs, send_sems, recv_sems, True):
            cp.start()

    def finish(ins, outs, send_sems, recv_sems):
        for cp in copies(outs, send_sems, recv_sems, False):
            cp.wait_recv()
        for cp in copies(outs, send_sems, recv_sems, True):
            cp.wait_send()

    return dict(ins=list(slots), out_shapes=[S(a.shape, a.dtype) for a in slots], aliases=[(t, t) for t in range(nt)],
                nsem=3 * nt, start=start, finish=finish)


def gather_pass(arrs, name):
    nt = len(arrs)

    def body(*refs):
        outs = refs[nt:2 * nt]
        send_sems, recv_sems = refs[2 * nt:]
        x, y, c, chips = _place()
        sent = []
        for t, o in enumerate(outs):
            mine = _half(c, o.shape[2] // 2)
            for j, (cx, cy) in enumerate(chips):
                landed = o.at[2 * cx + cy, :, mine]
                cp = _remote(landed, landed, send_sems.at[t, j], recv_sems.at[t, j], (x, y, 1 - c))
                cp.start()
                sent.append(cp)
        for t, o in enumerate(outs):
            other = _half(1 - c, o.shape[2] // 2)
            for j, (cx, cy) in enumerate(chips):
                landed = o.at[2 * cx + cy, :, other]
                _remote(landed, landed, send_sems.at[t, j], recv_sems.at[t, j], (x, y, 1 - c)).wait_recv()
        for cp in sent:
            cp.wait_send()

    return pl.pallas_call(
        body, out_shape=[S(a.shape, a.dtype) for a in arrs], in_specs=[ANY] * nt, out_specs=[ANY] * nt,
        input_output_aliases={t: t for t in range(nt)},
        scratch_shapes=[pltpu.SemaphoreType.DMA((nt, 3)), pltpu.SemaphoreType.DMA((nt, 3))], name=name)(*arrs)


def swap_job(gs):
    def copies(ins, outs, send_sems, recv_sems):
        x, y, c, _ = _place()
        return [_remote(g.at[:, :, _half(1 - c, o.shape[2])], o, send_sems.at[t], recv_sems.at[t], (x, y, 1 - c))
                for t, (g, o) in enumerate(zip(ins, outs))]

    def start(ins, outs, send_sems, recv_sems):
        for cp in copies(ins, outs, send_sems, recv_sems):
            cp.start()

    def finish(ins, outs, send_sems, recv_sems):
        for cp in copies(ins, outs, send_sems, recv_sems):
            cp.wait()

    return dict(ins=list(gs), out_shapes=[S(g.shape[:2] + (g.shape[2] // 2,), g.dtype) for g in gs], aliases=[],
                nsem=len(gs), start=start, finish=finish)


def scatter_job(ps):
    nt = len(ps)

    def copies(ins, outs, send_sems, recv_sems):
        x, y, c, chips = _place()
        return [_remote(p.at[2 * cx + cy], o.at[j], send_sems.at[3 * t + j], recv_sems.at[3 * t + j], (cx, cy, c))
                for t, (p, o) in enumerate(zip(ins, outs)) for j, (cx, cy) in enumerate(chips)]

    def start(ins, outs, send_sems, recv_sems):
        for cp in copies(ins, outs, send_sems, recv_sems):
            cp.start()

    def finish(ins, outs, send_sems, recv_sems):
        for cp in copies(ins, outs, send_sems, recv_sems):
            cp.wait()

    return dict(ins=list(ps), out_shapes=[S((N_CHIPS - 1,) + p.shape[1:], p.dtype) for p in ps], aliases=[],
                nsem=3 * nt, start=start, finish=finish)


def sum_devices(v, name):
    R, C = v.shape

    def body(v_ref, o_ref, slots, send_sems, recv_sems):
        x, y, c, _ = _place()
        me = 4 * x + 2 * y + c
        slots[me] = v_ref[...]
        sent = []
        for r in range(1, 8):
            peer = (x ^ (r >> 2), y ^ ((r >> 1) & 1), c ^ (r & 1))
            cp = _remote(v_ref, slots.at[me], send_sems.at[r - 1], recv_sems.at[r - 1], peer)
            cp.start()
            sent.append(cp)
        for r in range(1, 8):
            peer = (x ^ (r >> 2), y ^ ((r >> 1) & 1), c ^ (r & 1))
            theirs = slots.at[4 * peer[0] + 2 * peer[1] + peer[2]]
            _remote(v_ref, theirs, send_sems.at[r - 1], recv_sems.at[r - 1], peer).wait_recv()
        for cp in sent:
            cp.wait_send()
        acc = slots[0]
        for k in range(1, 8):
            acc += slots[k]
        o_ref[...] = acc

    vm = BS(memory_space=pltpu.VMEM)
    return pl.pallas_call(
        body, out_shape=S((R, C), F32), in_specs=[vm], out_specs=vm,
        scratch_shapes=[pltpu.VMEM((8, R, C), F32), pltpu.SemaphoreType.DMA((7,)), pltpu.SemaphoreType.DMA((7,))],
        compiler_params=pltpu.CompilerParams(vmem_limit_bytes=32 * MIB), name=name)(v)


def _pack(arrays, dtype, cols, row_mult):
    flat = jnp.concatenate([a.astype(dtype).reshape(-1) for a in arrays])
    n = flat.shape[0]
    total = _round_up(n, cols * row_mult)
    return jnp.pad(flat, (0, total - n)).reshape(total // cols, cols)


def _unpack(flat, shapes):
    out, off = [], 0
    for shp in shapes:
        n = 1
        for d in shp:
            n *= d
        out.append(flat[..., off:off + n].reshape(flat.shape[:-1] + tuple(shp)))
        off += n
    return out


def _layer_groups(depth):
    groups = []
    for i in range(depth):
        kind, j = i % 3, i // 3
        mix = ((("att_w_qkv", j), ("att_w_o", j)), (("pool_w", j),), (("gdn_w_in", j), ("gdn_w_o", j)))[kind]
        groups.append(mix + (("ffn_w_up", i), ("ffn_w_down", i)))
    return groups


def kernel(x, mix_norm, ffn_norm, att_w_qkv, att_q_gain, att_k_gain, att_rel_bias, att_w_o, pool_w, pool_scale, gdn_w_in, gdn_conv, gdn_a_log, gdn_dt_bias, gdn_o_gain, gdn_w_o, ffn_w_up, ffn_conv, ffn_w_down, loss_target, m_mix_norm, m_ffn_norm, m_att_w_qkv, m_att_q_gain, m_att_k_gain, m_att_rel_bias, m_att_w_o, m_pool_w, m_pool_scale, m_gdn_w_in, m_gdn_conv, m_gdn_a_log, m_gdn_dt_bias, m_gdn_o_gain, m_gdn_w_o, m_ffn_w_up, m_ffn_conv, m_ffn_w_down, v_mix_norm, v_ffn_norm, v_att_w_qkv, v_att_q_gain, v_att_k_gain, v_att_rel_bias, v_att_w_o, v_pool_w, v_pool_scale, v_gdn_w_in, v_gdn_conv, v_gdn_a_log, v_gdn_dt_bias, v_gdn_o_gain, v_gdn_w_o, v_ffn_w_up, v_ffn_conv, v_ffn_w_down):
    w = dict(mix_norm=mix_norm, ffn_norm=ffn_norm, att_w_qkv=att_w_qkv, att_q_gain=att_q_gain, att_k_gain=att_k_gain, att_rel_bias=att_rel_bias, att_w_o=att_w_o, pool_w=pool_w, pool_scale=pool_scale, gdn_w_in=gdn_w_in, gdn_conv=gdn_conv, gdn_a_log=gdn_a_log, gdn_dt_bias=gdn_dt_bias, gdn_o_gain=gdn_o_gain, gdn_w_o=gdn_w_o, ffn_w_up=ffn_w_up, ffn_conv=ffn_conv, ffn_w_down=ffn_w_down)
    m = dict(mix_norm=m_mix_norm, ffn_norm=m_ffn_norm, att_w_qkv=m_att_w_qkv, att_q_gain=m_att_q_gain, att_k_gain=m_att_k_gain, att_rel_bias=m_att_rel_bias, att_w_o=m_att_w_o, pool_w=m_pool_w, pool_scale=m_pool_scale, gdn_w_in=m_gdn_w_in, gdn_conv=m_gdn_conv, gdn_a_log=m_gdn_a_log, gdn_dt_bias=m_gdn_dt_bias, gdn_o_gain=m_gdn_o_gain, gdn_w_o=m_gdn_w_o, ffn_w_up=m_ffn_w_up, ffn_conv=m_ffn_conv, ffn_w_down=m_ffn_w_down)
    v = dict(mix_norm=v_mix_norm, ffn_norm=v_ffn_norm, att_w_qkv=v_att_w_qkv, att_q_gain=v_att_q_gain, att_k_gain=v_att_k_gain, att_rel_bias=v_att_rel_bias, att_w_o=v_att_w_o, pool_w=v_pool_w, pool_scale=v_pool_scale, gdn_w_in=v_gdn_w_in, gdn_conv=v_gdn_conv, gdn_a_log=v_gdn_a_log, gdn_dt_bias=v_gdn_dt_bias, gdn_o_gain=v_gdn_o_gain, gdn_w_o=v_gdn_w_o, ffn_w_up=v_ffn_w_up, ffn_conv=v_ffn_conv, ffn_w_down=v_ffn_w_down)
    depth = ffn_w_up.shape[0]
    my_c = lax.axis_index("c").astype(jnp.int32)
    my_chip = (2 * lax.axis_index("x") + lax.axis_index("y")).astype(jnp.int32)
    groups = _layer_groups(depth)

    place = jnp.stack([my_chip, my_c])

    full = {n: w[n] for n in REPLICATED}
    slots = {(n, j): prep_slot(to_comm(n, w[n][j]), my_chip, f"prep_{n}_{j}") for group in groups for n, j in group}
    small = [w[n] for n in SMALL_SHARDED]
    got = gather_chips(_pack(small, F32, LANES, PACK_ROWS), "gather_small").reshape(N_CHIPS, -1)
    for n, t in zip(SMALL_SHARDED, _unpack(got, [s.shape for s in small])):
        full[n] = _merge(t, SHARD_AXIS[n])

    loss, grad_x, summed, G = local_step(x[0], loss_target[0], full, slots, place)
    loss = lax.psum(loss[0, 0], ("x", "y", "c"))
    gfull = small_grads(G, full)

    grads = {n: jnp.stack([from_comm(n, summed[n, j], w[n][j].shape) for j in range(w[n].shape[0])]) for n in BIG}

    small_names = REPLICATED + SMALL_SHARDED
    packed = _pack([gfull[n] for n in small_names], F32, LANES, SUBLANES)
    summed = sum_devices(packed, "sum_small").reshape(-1)
    for n, t in zip(small_names, _unpack(summed, [gfull[n].shape for n in small_names])):
        if n in SHARD_AXIS:
            size = w[n].shape[SHARD_AXIS[n]]
            t = lax.dynamic_slice_in_dim(t, my_chip * size, size, axis=SHARD_AXIS[n])
        grads[n] = t

    delta, new_m, new_v = {}, {}, {}
    for n in WEIGHTS:
        delta[n], new_m[n], new_v[n] = adamw(w[n], grads[n], m[n], v[n], f"adamw_{n}")
    return (loss, grad_x[None], *[grads[n] for n in WEIGHTS], *[delta[n] for n in WEIGHTS],
            *[new_m[n] for n in WEIGHTS], *[new_v[n] for n in WEIGHTS])
```

```python
import functools

import jax
import jax.numpy as jnp
from jax import lax
from jax.experimental import pallas as pl
from jax.experimental.pallas import tpu as pltpu

F32 = jnp.float32
BF16 = jnp.bfloat16
MXU_DTYPE = BF16
HI = lax.Precision.HIGHEST
S = jax.ShapeDtypeStruct
BS = pl.BlockSpec

EPS = 1e-6
MASK_VALUE = -1e30
CHUNK = 64
HEAD = 128
LEFT_CHUNKS = 8
BAND_LEFT = LEFT_CHUNKS * CHUNK
BAND = BAND_LEFT + CHUNK
MAX_REL = 256
ATT_CHUNKS = 4
ATT_QB = ATT_CHUNKS * CHUNK
ATT_BAND = BAND_LEFT + ATT_QB
POOL_WINDOWS = (2, 4, 8, 16)
POOL_HALO = 16
GDN_CONV = 4
DELTA_HEADS = 8
FFN_CONV = 3
SUBLANES = 8
LANES = 128
FF_ALIGN = 512
N_CHIPS = 4
ADAM_LR, ADAM_B1, ADAM_B2, ADAM_EPS, ADAM_WD, ADAM_STEP = 0.001, 0.9, 0.999, 1e-08, 0.01, 10
MIB = 1024 * 1024
MESH = pl.DeviceIdType.MESH


def _cp(sems, vmem_mib=40):
    return pltpu.CompilerParams(dimension_semantics=sems, vmem_limit_bytes=vmem_mib * MIB)


def _pick(n, cands):
    for c in cands:
        if n % c == 0:
            return c
    return n


def _mx(x):
    return x.astype(MXU_DTYPE)


def _hi_lo(x):
    hi = x.astype(MXU_DTYPE)
    return hi, (x - hi.astype(F32)).astype(MXU_DTYPE)


def _dot(a, b, dims, hi=False):
    dn = (dims, ((), ()))
    if hi is True or (hi == "split" and MXU_DTYPE == F32):
        return lax.dot_general(a.astype(F32), b.astype(F32), dn, precision=HI, preferred_element_type=F32)
    if hi == "split":
        ah, al = _hi_lo(a)
        bh, bl = _hi_lo(b)
        return (lax.dot_general(ah, bh, dn, preferred_element_type=F32)
                + (lax.dot_general(ah, bl, dn, preferred_element_type=F32)
                   + lax.dot_general(al, bh, dn, preferred_element_type=F32)))
    return lax.dot_general(_mx(a), _mx(b), dn, preferred_element_type=F32)


def dot_nn(a, b, hi=False):
    return _dot(a, b, ((1,), (0,)), hi)


def dot_nt(a, b, hi=False):
    return _dot(a, b, ((1,), (1,)), hi)


def dot_tn(a, b, hi=False):
    return _dot(a, b, ((0,), (0,)), hi)


def _sigmoid(x):
    return 0.5 * jnp.tanh(0.5 * x) + 0.5


def matmul(a, b, mode, *, out_dtype=F32, res=None, job=None, name):
    if mode == "nn":
        (M, K), N = a.shape, b.shape[1]
    elif mode == "nt":
        (M, K), N = a.shape, b.shape[0]
    else:
        (K, M), N = a.shape, b.shape[1]
    bm = _pick(M, (1024, 1408, 512, 256, 128))
    bn = _pick(N, (1024, 1408, 512, 256, 128))
    bk = _pick(K, (2816, 2048, 1408, 1024, 512, 256, 128))
    nk = K // bk
    if mode == "nn":
        a_spec = BS((bm, bk), lambda i, j, k: (i, k))
        b_spec = BS((bk, bn), lambda i, j, k: (k, j))
        dot = dot_nn
    elif mode == "nt":
        a_spec = BS((bm, bk), lambda i, j, k: (i, k))
        b_spec = BS((bn, bk), lambda i, j, k: (j, k))
        dot = dot_nt
    else:
        a_spec = BS((bk, bm), lambda i, j, k: (k, i))
        b_spec = BS((bk, bn), lambda i, j, k: (k, j))
        dot = dot_tn
    o_spec = BS((bm, bn), lambda i, j, k: (i, j))
    has_res = res is not None
    n_in = 3 if has_res else 2
    nji = len(job["ins"]) if job else 0
    njo = len(job["out_shapes"]) if job else 0
    grid = (M // bm, N // bn, nk)

    def body(*refs):
        a_ref, b_ref = refs[:2]
        r_ref = refs[2] if has_res else None
        o_ref = refs[n_in + nji]
        acc = refs[n_in + nji + 1 + njo]
        k = pl.program_id(2)
        if job:
            j_ins, j_outs = refs[n_in:n_in + nji], refs[n_in + nji + 1:n_in + nji + 1 + njo]
            sems = refs[n_in + nji + 2 + njo:]
            at_step = lambda s: jnp.logical_and(jnp.logical_and(pl.program_id(0) == s[0], pl.program_id(1) == s[1]),
                                                k == s[2])
            pl.when(at_step((0, 0, 0)))(lambda: job["start"](j_ins, j_outs, *sems))
        p = dot(a_ref[...], b_ref[...])

        def finish(total):
            if has_res:
                total = r_ref[...] + total
            o_ref[...] = total.astype(o_ref.dtype)

        if nk == 1:
            finish(p)
        else:
            @pl.when(k == 0)
            def _():
                acc[...] = p

            @pl.when(jnp.logical_and(k > 0, k < nk - 1))
            def _():
                acc[...] += p

            @pl.when(k == nk - 1)
            def _():
                finish(acc[...] + p)

        if job:
            pl.when(at_step((grid[0] - 1, grid[1] - 1, nk - 1)))(lambda: job["finish"](j_ins, j_outs, *sems))

    in_specs = [a_spec, b_spec] + ([o_spec] if has_res else [])
    args = (a, b) + ((res,) if has_res else ())
    if not job:
        return pl.pallas_call(
            body, grid=grid, in_specs=in_specs, out_specs=o_spec,
            out_shape=S((M, N), out_dtype), scratch_shapes=[pltpu.VMEM((bm, bn), F32)],
            compiler_params=_cp(("parallel", "parallel", "arbitrary"), 48), name=name)(*args)
    any_spec = BS(memory_space=pl.ANY)
    outs = pl.pallas_call(
        body, grid=grid, in_specs=in_specs + [any_spec] * nji, out_specs=[o_spec] + [any_spec] * njo,
        out_shape=[S((M, N), out_dtype)] + list(job["out_shapes"]),
        input_output_aliases={n_in + src: 1 + dst for src, dst in job["aliases"]},
        scratch_shapes=[pltpu.VMEM((bm, bn), F32), pltpu.SemaphoreType.DMA((job["nsem"],)),
                        pltpu.SemaphoreType.DMA((job["nsem"],))],
        compiler_params=_cp(("arbitrary", "arbitrary", "arbitrary"), 48), name=name)(*args, *job["ins"])
    return outs[0], list(outs[1:])


def norm_fwd(x, gain, out_dtypes, name):
    T, D = x.shape
    bt = _pick(T, (256, 128, 64))

    def body(x_ref, g_ref, *o_refs):
        xv = x_ref[...]
        r = lax.rsqrt(jnp.mean(xv * xv, axis=-1, keepdims=True) + EPS)
        y = (xv * r) * g_ref[...]
        for o in o_refs:
            o[...] = y.astype(o.dtype)

    row = BS((bt, D), lambda i: (i, 0))
    return pl.pallas_call(
        body, grid=(T // bt,), in_specs=[row, BS((1, D), lambda i: (0, 0))],
        out_specs=[row] * len(out_dtypes), out_shape=[S((T, D), dt) for dt in out_dtypes],
        compiler_params=_cp(("parallel",)), name=name)(x, gain)


def norm_bwd(x, gain, dy, dres, name):
    T, D = x.shape
    bt = _pick(T, (256, 128, 64))

    def body(x_ref, g_ref, dy_ref, dres_ref, dx_ref, dxm_ref, dg_ref):
        i = pl.program_id(0)
        xv = x_ref[...]
        dyv = dy_ref[...].astype(F32)
        r = lax.rsqrt(jnp.mean(xv * xv, axis=-1, keepdims=True) + EPS)
        xhat = xv * r
        dxhat = dyv * g_ref[...]
        dx = dres_ref[...] + r * (dxhat - xhat * jnp.mean(dxhat * xhat, axis=-1, keepdims=True))
        dx_ref[...] = dx
        dxm_ref[...] = dx.astype(dxm_ref.dtype)

        @pl.when(i == 0)
        def _():
            dg_ref[...] = jnp.zeros_like(dg_ref)

        dg_ref[...] += jnp.sum(dyv * xhat, axis=0, keepdims=True)

    row = BS((bt, D), lambda i: (i, 0))
    vec = BS((1, D), lambda i: (0, 0))
    dx, dxm, dg = pl.pallas_call(
        body, grid=(T // bt,), in_specs=[row, vec, row, row], out_specs=[row, row, vec],
        out_shape=[S((T, D), F32), S((T, D), MXU_DTYPE), S((1, D), F32)],
        compiler_params=_cp(("arbitrary",)), name=name)(x, gain, dy, dres)
    return (dx, dxm), dg


def loss_and_grad(y, target, name):
    T, D = y.shape
    bt = _pick(T, (256, 128, 64))
    nt = T // bt

    def body(y_ref, t_ref, l_ref, dy_ref, dym_ref, acc):
        i = pl.program_id(0)
        e = y_ref[...] - t_ref[...]
        dy_ref[...] = e * (1.0 / D)
        dym_ref[...] = (e * (1.0 / D)).astype(dym_ref.dtype)

        @pl.when(i == 0)
        def _():
            acc[...] = jnp.zeros_like(acc)

        acc[...] += jnp.sum(e * e, axis=0, keepdims=True)

        @pl.when(i == nt - 1)
        def _():
            l_ref[...] = jnp.sum(acc[...], axis=1, keepdims=True) * (0.5 / D)

    row = BS((bt, D), lambda i: (i, 0))
    loss, dy, dym = pl.pallas_call(
        body, grid=(nt,), in_specs=[row, row], out_specs=[BS((1, 1), lambda i: (0, 0)), row, row],
        out_shape=[S((1, 1), F32), S((T, D), F32), S((T, D), MXU_DTYPE)], scratch_shapes=[pltpu.VMEM((1, D), F32)],
        compiler_params=_cp(("arbitrary",)), name=name)(y, target)
    return loss, (dy, dym)


FFN_HALO = 16


def _prev_halo(bt):
    return lambda i: (jnp.maximum(i * (bt // FFN_HALO) - 1, 0), 0)


FFN_STRIP = 128


def _strips(width):
    return [pl.ds(c0, FFN_STRIP) for c0 in range(0, width, FFN_STRIP)]


def _ffn_u(i, a_ref, halo_ref, w_ref, ext, bt, cols):
    ext[pl.ds(0, FFN_HALO), cols] = jnp.where(i > 0, halo_ref[:, cols].astype(F32), 0.0)
    ext[pl.ds(FFN_HALO, bt), cols] = a_ref[:, cols].astype(F32)
    u = w_ref[2:3, cols] * ext[pl.ds(FFN_HALO, bt), cols]
    u += w_ref[1:2, cols] * ext[pl.ds(FFN_HALO - 1, bt), cols]
    u += w_ref[0:1, cols] * ext[pl.ds(FFN_HALO - 2, bt), cols]
    return u


def ffn_act_fwd(up, conv_w, name):
    T, F2 = up.shape
    Fp = F2 // 2
    bt = _pick(T, (128, 64))

    def body(a_ref, b_ref, halo_ref, w_ref, o_ref, ext):
        for cols in _strips(Fp):
            u = _ffn_u(pl.program_id(0), a_ref, halo_ref, w_ref, ext, bt, cols)
            o_ref[:, cols] = (u * _sigmoid(u) * b_ref[:, cols]).astype(o_ref.dtype)

    return pl.pallas_call(
        body, grid=(T // bt,),
        in_specs=[BS((bt, Fp), lambda i: (i, 0)), BS((bt, Fp), lambda i: (i, 1)),
                  BS((FFN_HALO, Fp), _prev_halo(bt)), BS((SUBLANES, Fp), lambda i: (0, 0))],
        out_specs=BS((bt, Fp), lambda i: (i, 0)), out_shape=S((T, Fp), MXU_DTYPE),
        scratch_shapes=[pltpu.VMEM((bt + FFN_HALO, Fp), F32)],
        compiler_params=_cp(("arbitrary",)), name=name)(up, up, up, conv_w)


def ffn_act_bwd_a(up, conv_w, g_act, name):
    T, F2 = up.shape
    Fp = F2 // 2
    bt = _pick(T, (128, 64))

    def body(a_ref, b_ref, halo_ref, w_ref, g_ref, du_ref, db_ref, dw_ref, ext):
        i = pl.program_id(0)

        @pl.when(i == 0)
        def _():
            dw_ref[...] = jnp.zeros_like(dw_ref)

        for cols in _strips(Fp):
            u = _ffn_u(i, a_ref, halo_ref, w_ref, ext, bt, cols)
            sg = _sigmoid(u)
            g = g_ref[:, cols].astype(F32)
            db_ref[:, cols] = (g * (u * sg)).astype(db_ref.dtype)
            du = g * b_ref[:, cols] * (sg * (1.0 + u * (1.0 - sg)))
            du_ref[:, cols] = du
            for j in range(FFN_CONV):
                shifted = ext[pl.ds(FFN_HALO - (FFN_CONV - 1) + j, bt), cols]
                dw_ref[j:j + 1, cols] += jnp.sum(du * shifted, axis=0, keepdims=True)

    blk = BS((bt, Fp), lambda i: (i, 0))
    full = BS((SUBLANES, Fp), lambda i: (0, 0))
    return pl.pallas_call(
        body, grid=(T // bt,),
        in_specs=[blk, BS((bt, Fp), lambda i: (i, 1)), BS((FFN_HALO, Fp), _prev_halo(bt)), full, blk],
        out_specs=[blk, blk, full],
        out_shape=[S((T, Fp), F32), S((T, Fp), MXU_DTYPE), S((SUBLANES, Fp), F32)],
        scratch_shapes=[pltpu.VMEM((bt + FFN_HALO, Fp), F32)],
        compiler_params=_cp(("arbitrary",)), name=name)(up, up, up, conv_w, g_act)


def ffn_act_bwd_b(du, db, conv_w, name):
    T, Fp = du.shape
    bt = _pick(T, (128, 64))
    nt = T // bt

    def body(du_ref, halo_ref, db_ref, w_ref, o_ref, ext):
        i = pl.program_id(0)
        for cols in _strips(Fp):
            ext[pl.ds(0, bt), cols] = du_ref[:, cols]
            ext[pl.ds(bt, SUBLANES), cols] = jnp.where(i < nt - 1, halo_ref[:, cols], 0.0)
            da = w_ref[2:3, cols] * ext[pl.ds(0, bt), cols]
            da += w_ref[1:2, cols] * ext[pl.ds(1, bt), cols]
            da += w_ref[0:1, cols] * ext[pl.ds(2, bt), cols]
            o_ref[:, cols] = da.astype(o_ref.dtype)
        o_ref[:, pl.ds(Fp, Fp)] = db_ref[...]

    blk = BS((bt, Fp), lambda i: (i, 0))
    nxt = BS((SUBLANES, Fp), lambda i: (jnp.minimum((i + 1) * (bt // SUBLANES), T // SUBLANES - 1), 0))
    return pl.pallas_call(
        body, grid=(nt,), in_specs=[blk, nxt, blk, BS((SUBLANES, Fp), lambda i: (0, 0))],
        out_specs=BS((bt, 2 * Fp), lambda i: (i, 0)), out_shape=S((T, 2 * Fp), MXU_DTYPE),
        scratch_shapes=[pltpu.VMEM((bt + SUBLANES, Fp), F32)],
        compiler_params=_cp(("arbitrary",)), name=name)(du, du, db, conv_w)


def _attn_fill(k_ref, v_ref, gk, kn_scr, vb_scr, T):
    kn_scr[pl.ds(0, BAND_LEFT), :] = jnp.zeros((BAND_LEFT, HEAD), kn_scr.dtype)
    vb_scr[pl.ds(0, BAND_LEFT), :] = jnp.zeros((BAND_LEFT, HEAD), vb_scr.dtype)
    rb = 512

    def fill(r, carry):
        rows = pl.ds(pl.multiple_of(r * rb, rb), rb)
        dst = pl.ds(pl.multiple_of(BAND_LEFT + r * rb, rb), rb)
        k = k_ref[rows, :]
        rk = lax.rsqrt(jnp.mean(k * k, axis=-1, keepdims=True) + EPS)
        kn_scr[dst, :] = ((k * rk) * gk).astype(kn_scr.dtype)
        vb_scr[dst, :] = v_ref[rows, :].astype(vb_scr.dtype)
        return carry

    lax.fori_loop(0, T // rb, fill, 0)


def _attn_probs(c, q_ref, gq, bias_ref, kn_scr):
    q = q_ref[...]
    rq = lax.rsqrt(jnp.mean(q * q, axis=-1, keepdims=True) + EPS)
    qn = (q * rq) * gq
    band = pl.ds(pl.multiple_of(c * ATT_QB, ATT_QB), ATT_BAND)
    kb = kn_scr[band, :]
    s = dot_nt(qn, kb) * (HEAD ** -0.5) + bias_ref[0]
    pos = c * ATT_QB - BAND_LEFT + lax.broadcasted_iota(jnp.int32, (ATT_QB, ATT_BAND), 1)
    s = jnp.where(pos >= 0, s, MASK_VALUE)
    m = jnp.max(s, axis=-1, keepdims=True)
    e = jnp.exp(s - m)
    p = e * (1.0 / jnp.sum(e, axis=-1, keepdims=True))
    return q, rq, qn, kb, p


def attn_fwd(qkv, gq, gk, bias, name):
    T, D3 = qkv.shape
    D = D3 // 3
    H = D // HEAD
    NC = T // ATT_QB

    def body(q_ref, k_ref, v_ref, gq_ref, gk_ref, bias_ref, o_ref, kn_scr, vb_scr):
        c = pl.program_id(1)

        @pl.when(c == 0)
        def _():
            _attn_fill(k_ref, v_ref, gk_ref[...], kn_scr, vb_scr, T)

        _, _, _, _, p = _attn_probs(c, q_ref, gq_ref[...], bias_ref, kn_scr)
        band = pl.ds(pl.multiple_of(c * ATT_QB, ATT_QB), ATT_BAND)
        o_ref[...] = dot_nn(p, vb_scr[band, :]).astype(o_ref.dtype)

    vec = BS((1, HEAD), lambda h, c: (0, 0))
    return pl.pallas_call(
        body, grid=(H, NC),
        in_specs=[BS((ATT_QB, HEAD), lambda h, c: (c, h)), BS((T, HEAD), lambda h, c: (0, H + h)),
                  BS((T, HEAD), lambda h, c: (0, 2 * H + h)), vec, vec,
                  BS((1, ATT_QB, ATT_BAND), lambda h, c: (h, 0, 0))],
        out_specs=BS((ATT_QB, HEAD), lambda h, c: (c, h)), out_shape=S((T, D), MXU_DTYPE),
        scratch_shapes=[pltpu.VMEM((T + BAND_LEFT, HEAD), MXU_DTYPE)] * 2,
        compiler_params=_cp(("arbitrary", "arbitrary"), 48), name=name)(qkv, qkv, qkv, gq, gk, bias)


def attn_bwd(qkv, do, gq, gk, bias, name):
    T, D3 = qkv.shape
    D = D3 // 3
    H = D // HEAD
    NC = T // ATT_QB
    scale = HEAD ** -0.5

    def body(q_ref, k_ref, v_ref, do_ref, gq_ref, gk_ref, bias_ref,
             dq_ref, dk_ref, dv_ref, dgq_ref, dgk_ref, dbias_ref, kn_scr, vb_scr, dkn_acc, dv_acc):
        h = pl.program_id(0)
        c = pl.program_id(1)
        gq = gq_ref[...]
        gk = gk_ref[...]

        @pl.when(c == 0)
        def _():
            _attn_fill(k_ref, v_ref, gk, kn_scr, vb_scr, T)
            dkn_acc[...] = jnp.zeros_like(dkn_acc)
            dv_acc[...] = jnp.zeros_like(dv_acc)
            dbias_ref[...] = jnp.zeros_like(dbias_ref)

        @pl.when(jnp.logical_and(c == 0, h == 0))
        def _():
            dgq_ref[...] = jnp.zeros_like(dgq_ref)
            dgk_ref[...] = jnp.zeros_like(dgk_ref)

        q, rq, qn, kb, p = _attn_probs(c, q_ref, gq, bias_ref, kn_scr)
        band = pl.ds(pl.multiple_of(c * ATT_QB, ATT_QB), ATT_BAND)
        dov = do_ref[...]
        dv_acc[band, :] += dot_tn(p, dov)
        dp = dot_nt(dov, vb_scr[band, :])
        ds = p * (dp - jnp.sum(dp * p, axis=-1, keepdims=True))
        dbias_ref[0] += ds
        dss = ds * scale
        dqn = dot_nn(dss, kb)
        dkn_acc[band, :] += dot_tn(dss, qn)
        xhat = q * rq
        dgq_ref[...] += jnp.sum(dqn * xhat, axis=0, keepdims=True)
        dxhat = dqn * gq
        dq = rq * (dxhat - xhat * jnp.mean(dxhat * xhat, axis=-1, keepdims=True))
        dq_ref[...] = dq.astype(dq_ref.dtype)

        @pl.when(c == NC - 1)
        def _():
            rb = 512

            def fin(r, carry):
                rows = pl.ds(pl.multiple_of(r * rb, rb), rb)
                src = pl.ds(pl.multiple_of(BAND_LEFT + r * rb, rb), rb)
                k = k_ref[rows, :]
                rk = lax.rsqrt(jnp.mean(k * k, axis=-1, keepdims=True) + EPS)
                khat = k * rk
                dkn = dkn_acc[src, :]
                dgk_ref[...] += jnp.sum(dkn * khat, axis=0, keepdims=True)
                dkh = dkn * gk
                dk = rk * (dkh - khat * jnp.mean(dkh * khat, axis=-1, keepdims=True))
                dk_ref[rows, :] = dk.astype(dk_ref.dtype)
                dv_ref[rows, :] = dv_acc[src, :].astype(dv_ref.dtype)
                return carry

            lax.fori_loop(0, T // rb, fin, 0)

    vec = BS((1, HEAD), lambda h, c: (0, 0))
    qblk = BS((ATT_QB, HEAD), lambda h, c: (c, h))
    col = BS((T, HEAD), lambda h, c: (0, h))
    bblk = BS((1, ATT_QB, ATT_BAND), lambda h, c: (h, 0, 0))
    return pl.pallas_call(
        body, grid=(H, NC),
        in_specs=[qblk, BS((T, HEAD), lambda h, c: (0, H + h)), BS((T, HEAD), lambda h, c: (0, 2 * H + h)),
                  qblk, vec, vec, bblk],
        out_specs=[qblk, col, col, vec, vec, bblk],
        out_shape=[S((T, D), MXU_DTYPE)] * 3 + [S((1, HEAD), F32)] * 2 + [S((H, ATT_QB, ATT_BAND), F32)],
        scratch_shapes=[pltpu.VMEM((T + BAND_LEFT, HEAD), MXU_DTYPE)] * 2
        + [pltpu.VMEM((T + BAND_LEFT, HEAD), F32)] * 2,
        compiler_params=_cp(("arbitrary", "arbitrary"), 56), name=name)(qkv, qkv, qkv, do, gq, gk, bias)


def _rel_onehot(qi, num_rel):
    kk = lax.broadcasted_iota(jnp.int32, (BAND, num_rel), 0)
    rr = lax.broadcasted_iota(jnp.int32, (BAND, num_rel), 1)
    idx = jnp.clip(BAND_LEFT + qi - kk, -(CHUNK - 1), MAX_REL) + (CHUNK - 1)
    return (idx == rr).astype(F32)


def rel_bias_expand(table, name):
    H, num_rel = table.shape

    def body(t_ref, o_ref):
        for qi in range(CHUNK):
            o_ref[qi] = dot_nt(t_ref[...], _rel_onehot(qi, num_rel), hi=True)

    return pl.pallas_call(body, out_shape=S((CHUNK, H, BAND), F32), name=name,
                          compiler_params=pltpu.CompilerParams(vmem_limit_bytes=40 * MIB))(table)


def rel_bias_reduce(dbias_t, num_rel, name):
    H = dbias_t.shape[1]

    def body(d_ref, o_ref):
        acc = jnp.zeros((H, num_rel), F32)
        for qi in range(CHUNK):
            acc += dot_nn(d_ref[qi], _rel_onehot(qi, num_rel), hi=True)
        o_ref[...] = acc

    return pl.pallas_call(body, out_shape=S((H, num_rel), F32), name=name,
                          compiler_params=pltpu.CompilerParams(vmem_limit_bytes=40 * MIB))(dbias_t)


def pool_fwd(h, x, w, scale, name):
    T, D = h.shape
    G = len(POOL_WINDOWS)
    Dg = D // G
    bt = _pick(T, (256, 128, 64))

    def body(h_ref, halo_ref, x_ref, w_ref, s_ref, o_ref, p_ref, ext):
        i = pl.program_id(0)
        ext[pl.ds(0, POOL_HALO), :] = jnp.where(i > 0, halo_ref[...], 0.0)
        ext[pl.ds(POOL_HALO, bt), :] = h_ref[...]
        t = i * bt + lax.broadcasted_iota(jnp.int32, (bt, 1), 0)
        for g, win in enumerate(POOL_WINDOWS):
            cols = pl.ds(g * Dg, Dg)
            acc = ext[pl.ds(POOL_HALO, bt), cols]
            for j in range(1, win):
                acc += ext[pl.ds(POOL_HALO - j, bt), cols]
            count = jnp.minimum(t + 1, win).astype(F32)
            pooled = acc / count - h_ref[:, cols]
            p_ref[:, cols] = pooled.astype(p_ref.dtype)
            y = dot_nn(pooled, w_ref[g]) * s_ref[:, cols]
            o_ref[:, cols] = x_ref[:, cols] + y

    row = BS((bt, D), lambda i: (i, 0))
    return pl.pallas_call(
        body, grid=(T // bt,),
        in_specs=[row, BS((POOL_HALO, D), lambda i: (jnp.maximum(i * (bt // POOL_HALO) - 1, 0), 0)), row,
                  BS((G, Dg, Dg), lambda i: (0, 0, 0)), BS((1, D), lambda i: (0, 0))],
        out_specs=[row, row], out_shape=[S((T, D), F32), S((T, D), MXU_DTYPE)],
        scratch_shapes=[pltpu.VMEM((bt + POOL_HALO, D), F32)],
        compiler_params=_cp(("arbitrary",)), name=name)(h, h, x, w, scale)


def pool_bwd_a(dy, pooled, w, scale, name):
    T, D = dy.shape
    G = len(POOL_WINDOWS)
    Dg = D // G
    bt = _pick(T, (256, 128, 64))

    def body(dy_ref, p_ref, w_ref, s_ref, dp_ref, dw_ref, ds_ref):
        i = pl.program_id(0)

        @pl.when(i == 0)
        def _():
            dw_ref[...] = jnp.zeros_like(dw_ref)
            ds_ref[...] = jnp.zeros_like(ds_ref)

        for g in range(G):
            cols = pl.ds(g * Dg, Dg)
            pg = p_ref[:, cols]
            dyg = dy_ref[:, cols]
            ypre = dot_nn(pg, w_ref[g])
            ds_ref[:, cols] += jnp.sum(dyg * ypre, axis=0, keepdims=True)
            dys = dyg * s_ref[:, cols]
            dp_ref[:, cols] = dot_nt(dys, w_ref[g])
            dw_ref[g] += dot_tn(pg, dys)

    row = BS((bt, D), lambda i: (i, 0))
    wspec = BS((G, Dg, Dg), lambda i: (0, 0, 0))
    vec = BS((1, D), lambda i: (0, 0))
    return pl.pallas_call(
        body, grid=(T // bt,), in_specs=[row, row, wspec, vec], out_specs=[row, wspec, vec],
        out_shape=[S((T, D), F32), S((G, Dg, Dg), F32), S((1, D), F32)],
        compiler_params=_cp(("arbitrary",)), name=name)(dy, pooled, w, scale)


def pool_bwd_b(dpooled, name):
    T, D = dpooled.shape
    G = len(POOL_WINDOWS)
    Dg = D // G
    bt = _pick(T, (256, 128, 64))
    nt = T // bt

    def body(d_ref, halo_ref, o_ref, ext):
        i = pl.program_id(0)
        t = i * bt + lax.broadcasted_iota(jnp.int32, (bt, 1), 0)
        for g, win in enumerate(POOL_WINDOWS):
            cols = pl.ds(g * Dg, Dg)
            count = jnp.minimum(t + 1, win).astype(F32)
            ext[pl.ds(0, bt), cols] = d_ref[:, cols] / count
            ext[pl.ds(bt, POOL_HALO), cols] = jnp.where(i < nt - 1, halo_ref[:, cols] * (1.0 / win), 0.0)
            acc = ext[pl.ds(0, bt), cols]
            for j in range(1, win):
                acc += ext[pl.ds(j, bt), cols]
            o_ref[:, cols] = acc - d_ref[:, cols]

    row = BS((bt, D), lambda i: (i, 0))
    nxt = BS((POOL_HALO, D), lambda i: (jnp.minimum((i + 1) * (bt // POOL_HALO), T // POOL_HALO - 1), 0))
    return pl.pallas_call(
        body, grid=(nt,), in_specs=[row, nxt], out_specs=row, out_shape=S((T, D), F32),
        scratch_shapes=[pltpu.VMEM((bt + POOL_HALO, D), F32)],
        compiler_params=_cp(("arbitrary",)), name=name)(dpooled, dpooled)


def _gdn_u(i, x_ref, halo_ref, w_ref, ext, bt):
    ext[pl.ds(0, SUBLANES), :] = jnp.where(i > 0, halo_ref[...], 0.0)
    ext[pl.ds(SUBLANES, bt), :] = x_ref[...]
    u = w_ref[3:4, :] * ext[pl.ds(SUBLANES, bt), :]
    for j in range(GDN_CONV - 1):
        u += w_ref[j:j + 1, :] * ext[pl.ds(SUBLANES - (GDN_CONV - 1) + j, bt), :]
    return u


def gdn_pre_fwd(proj, conv_w, key_dim, name):
    T = proj.shape[0]
    C = conv_w.shape[1]
    cb = min(1024, key_dim)
    nq, nqk, J = key_dim // cb, 2 * key_dim // cb, C // cb
    bt = _pick(T, (256, 128, 64))

    def body(x_ref, halo_ref, w_ref, o_ref, ext):
        i, j = pl.program_id(0), pl.program_id(1)
        u = _gdn_u(i, x_ref, halo_ref, w_ref, ext, bt)
        s = u * _sigmoid(u)

        @pl.when(j < nqk)
        def _():
            sc = jnp.where(j < nq, HEAD ** -0.5, 1.0)
            for hh in range(cb // HEAD):
                cols = pl.ds(hh * HEAD, HEAD)
                blk = s[:, hh * HEAD:(hh + 1) * HEAD]
                r = lax.rsqrt(jnp.sum(blk * blk, axis=-1, keepdims=True) + EPS)
                o_ref[:, cols] = (blk * r) * sc

        @pl.when(j >= nqk)
        def _():
            o_ref[...] = s

    return pl.pallas_call(
        body, grid=(T // bt, J),
        in_specs=[BS((bt, cb), lambda i, j: (i, j)),
                  BS((SUBLANES, cb), lambda i, j: (jnp.maximum(i * (bt // SUBLANES) - 1, 0), j)),
                  BS((SUBLANES, cb), lambda i, j: (0, j))],
        out_specs=BS((bt, cb), lambda i, j: (i, j)), out_shape=S((T, C), F32),
        scratch_shapes=[pltpu.VMEM((bt + SUBLANES, cb), F32)],
        compiler_params=_cp(("arbitrary", "arbitrary")), name=name)(proj, proj, conv_w)


def gdn_pre_bwd_a(proj, conv_w, dq_v, dk_v, dv, key_dim, name):
    T = proj.shape[0]
    C = conv_w.shape[1]
    cb = min(1024, key_dim)
    nq, nqk, J = key_dim // cb, 2 * key_dim // cb, C // cb
    nv = J - nqk
    bt = _pick(T, (256, 128, 64))

    def body(x_ref, halo_ref, w_ref, dq_ref, dk_ref, dv_ref, du_ref, dw_ref, ext, ds_scr):
        j, i = pl.program_id(0), pl.program_id(1)
        u = _gdn_u(i, x_ref, halo_ref, w_ref, ext, bt)
        sg = _sigmoid(u)
        s = u * sg

        @pl.when(j < nqk)
        def _():
            sc = jnp.where(j < nq, HEAD ** -0.5, 1.0)
            for hh in range(cb // HEAD):
                lo = 2 * hh * HEAD
                dq2 = dq_ref[:, lo:lo + HEAD] + dq_ref[:, lo + HEAD:lo + 2 * HEAD]
                dk2 = dk_ref[:, lo:lo + HEAD] + dk_ref[:, lo + HEAD:lo + 2 * HEAD]
                dn = jnp.where(j < nq, dq2, dk2)
                blk = s[:, hh * HEAD:(hh + 1) * HEAD]
                r = lax.rsqrt(jnp.sum(blk * blk, axis=-1, keepdims=True) + EPS)
                shat = blk * r
                ds_scr[:, pl.ds(hh * HEAD, HEAD)] = (sc * r) * (dn - shat * jnp.sum(dn * shat, axis=-1, keepdims=True))

        @pl.when(j >= nqk)
        def _():
            ds_scr[...] = dv_ref[...]

        du = ds_scr[...] * (sg * (1.0 + u * (1.0 - sg)))
        du_ref[...] = du

        @pl.when(i == 0)
        def _():
            dw_ref[...] = jnp.zeros_like(dw_ref)

        for k in range(GDN_CONV):
            shifted = ext[pl.ds(SUBLANES - (GDN_CONV - 1) + k, bt), :]
            dw_ref[k:k + 1, :] += jnp.sum(du * shifted, axis=0, keepdims=True)

    blk = BS((bt, cb), lambda j, i: (i, j))
    return pl.pallas_call(
        body, grid=(J, T // bt),
        in_specs=[blk, BS((SUBLANES, cb), lambda j, i: (jnp.maximum(i * (bt // SUBLANES) - 1, 0), j)),
                  BS((SUBLANES, cb), lambda j, i: (0, j)),
                  BS((bt, 2 * cb), lambda j, i: (i, jnp.minimum(j, nq - 1))),
                  BS((bt, 2 * cb), lambda j, i: (i, jnp.clip(j - nq, 0, nq - 1))),
                  BS((bt, cb), lambda j, i: (i, jnp.clip(j - nqk, 0, nv - 1)))],
        out_specs=[blk, BS((SUBLANES, cb), lambda j, i: (0, j))],
        out_shape=[S((T, C), F32), S((SUBLANES, C), F32)],
        scratch_shapes=[pltpu.VMEM((bt + SUBLANES, cb), F32), pltpu.VMEM((bt, cb), F32)],
        compiler_params=_cp(("arbitrary", "arbitrary")), name=name)(proj, proj, conv_w, dq_v, dk_v, dv)


def gdn_pre_bwd_b(du, dgate, conv_w, key_dim, name):
    T, C = du.shape
    V = dgate.shape[1]
    cb = min(1024, key_dim)
    J = C // cb
    J2 = (C + V) // cb
    bt = _pick(T, (256, 128, 64))
    nt = T // bt

    def body(du_ref, halo_ref, w_ref, dg_ref, o_ref, ext):
        i, j = pl.program_id(0), pl.program_id(1)

        @pl.when(j < J)
        def _():
            ext[pl.ds(0, bt), :] = du_ref[...]
            ext[pl.ds(bt, SUBLANES), :] = jnp.where(i < nt - 1, halo_ref[...], 0.0)
            da = w_ref[3:4, :] * ext[pl.ds(0, bt), :]
            for k in range(GDN_CONV - 1):
                da += w_ref[k:k + 1, :] * ext[pl.ds(GDN_CONV - 1 - k, bt), :]
            o_ref[...] = da.astype(o_ref.dtype)

        @pl.when(j >= J)
        def _():
            o_ref[...] = dg_ref[...]

    jc = lambda j: jnp.minimum(j, J - 1)
    return pl.pallas_call(
        body, grid=(nt, J2),
        in_specs=[BS((bt, cb), lambda i, j: (i, jc(j))),
                  BS((SUBLANES, cb), lambda i, j: (jnp.minimum((i + 1) * (bt // SUBLANES), T // SUBLANES - 1), jc(j))),
                  BS((SUBLANES, cb), lambda i, j: (0, jc(j))),
                  BS((bt, cb), lambda i, j: (i, jnp.maximum(j - J, 0)))],
        out_specs=BS((bt, cb), lambda i, j: (i, j)), out_shape=S((T, C + V), MXU_DTYPE),
        scratch_shapes=[pltpu.VMEM((bt + SUBLANES, cb), F32)],
        compiler_params=_cp(("arbitrary", "arbitrary")), name=name)(du, du, conv_w, dgate)


def _softplus(x):
    return jnp.maximum(x, 0.0) + jnp.log1p(jnp.exp(-jnp.abs(x)))


def gdn_gate_fwd(a, b, a_log, dt_bias, name):
    T, HV = a.shape
    bt = _pick(T, (1024, 512, 256, 128, 64))

    def body(a_ref, b_ref, al_ref, dt_ref, g_ref, be_ref):
        g_ref[...] = -jnp.exp(al_ref[...]) * _softplus(a_ref[...] + dt_ref[...])
        be_ref[...] = _sigmoid(b_ref[...])

    row = BS((bt, HV), lambda i: (i, 0))
    vec = BS((1, HV), lambda i: (0, 0))
    return pl.pallas_call(body, grid=(T // bt,), in_specs=[row, row, vec, vec], out_specs=[row, row],
                          out_shape=[S((T, HV), F32)] * 2, compiler_params=_cp(("parallel",)), name=name)(
                              a, b, a_log, dt_bias)


def gdn_gate_bwd(a, b, a_log, dt_bias, dg, dbeta, name):
    T, HV = a.shape
    bt = _pick(T, (1024, 512, 256, 128, 64))

    def body(a_ref, b_ref, al_ref, dt_ref, dg_ref, dbe_ref, da_ref, db_ref, dal_ref, ddt_ref):
        i = pl.program_id(0)
        x = a_ref[...] + dt_ref[...]
        ea = jnp.exp(al_ref[...])
        dgv = dg_ref[...]
        da = dgv * (-ea * _sigmoid(x))
        da_ref[...] = da
        be = _sigmoid(b_ref[...])
        db_ref[...] = dbe_ref[...] * be * (1.0 - be)

        @pl.when(i == 0)
        def _():
            dal_ref[...] = jnp.zeros_like(dal_ref)
            ddt_ref[...] = jnp.zeros_like(ddt_ref)

        dal_ref[...] += jnp.sum(dgv * (-ea * _softplus(x)), axis=0, keepdims=True)
        ddt_ref[...] += jnp.sum(da, axis=0, keepdims=True)

    row = BS((bt, HV), lambda i: (i, 0))
    vec = BS((1, HV), lambda i: (0, 0))
    return pl.pallas_call(body, grid=(T // bt,), in_specs=[row, row, vec, vec, row, row],
                          out_specs=[row, row, vec, vec],
                          out_shape=[S((T, HV), F32)] * 2 + [S((1, HV), F32)] * 2,
                          compiler_params=_cp(("arbitrary",)), name=name)(a, b, a_log, dt_bias, dg, dbeta)


def _col(row_vec, eye):
    return jnp.sum(jnp.where(eye, row_vec, 0.0), axis=1, keepdims=True)


def _row(col_vec, eye):
    return jnp.sum(jnp.where(eye, col_vec, 0.0), axis=0, keepdims=True)


def _each(f, *lists):
    return [f(*args) for args in zip(*lists)]


def _mul(a, b):
    return a * b


def _hdot_nn(a, b):
    return dot_nn(a, b, hi="split")


def _delta_chunk(q, k, v, g_row, b_row, rep, tinv=None):
    C = CHUNK
    ii = lax.broadcasted_iota(jnp.int32, (C, C), 0)
    jj = lax.broadcasted_iota(jnp.int32, (C, C), 1)
    eye, causal, strict = ii == jj, ii >= jj, ii > jj
    q_v = [q[p // rep] for p in range(len(v))]
    k_v = [k[p // rep] for p in range(len(v))]
    g_col = _each(lambda g: _col(g, eye), g_row)
    gc_row = _each(lambda g: jnp.sum(jnp.where(ii <= jj, g, 0.0), axis=0, keepdims=True), g_col)
    gc_col = _each(lambda g: _col(g, eye), gc_row)
    gl = _each(lambda g: jnp.sum(jnp.where(jj[0:1, :] == C - 1, g, 0.0), axis=1, keepdims=True), gc_row)
    decay = _each(lambda gc, gr: jnp.where(causal, jnp.exp(jnp.where(causal, gc - gr, 0.0)), 0.0), gc_col, gc_row)
    b_col = _each(lambda b: _col(b, eye), b_row)
    kb = _each(_mul, k_v, b_col)
    vb = _each(_mul, v, b_col)
    m = _each(dot_nt, kb, k_v)
    a = _each(lambda m_, d_: jnp.where(strict, m_ * d_, 0.0), m, decay)
    if tinv is None:
        ident = jnp.where(eye, 1.0, 0.0)
        tinv = [ident - a_ for a_ in a]
        pw = _each(_hdot_nn, a, a)
        for step in range(5):
            tinv = _each(lambda t_, p_: t_ + _hdot_nn(t_, p_), tinv, pw)
            if step < 4:
                pw = _each(_hdot_nn, pw, pw)
    egc = _each(jnp.exp, gc_col)
    kbg = _each(_mul, kb, egc)
    u = _each(_hdot_nn, tinv, vb)
    w = _each(_hdot_nn, tinv, kbg)
    n_k = _each(dot_nt, q, k)
    n = [n_k[p // rep] for p in range(len(v))]
    attn = _each(lambda n_, d_: jnp.where(causal, n_ * d_, 0.0), n, decay)
    qg = _each(_mul, q_v, egc)
    ekl = _each(lambda l_, c_: jnp.exp(l_ - c_), gl, gc_col)
    ks = _each(_mul, k_v, ekl)
    dec = _each(jnp.exp, gl)
    return dict(eye=eye, causal=causal, strict=strict, ii=ii, jj=jj, q=q_v, k=k_v, gc_col=gc_col, gl=gl, decay=decay,
                b_col=b_col, kb=kb, vb=vb, m=m, tinv=tinv, egc=egc, kbg=kbg, u=u, w=w, n=n, attn=attn,
                qg=qg, ekl=ekl, ks=ks, dec=dec)


def delta_fwd(qkvn, g_rows, b_rows, key_dim, name):
    T = qkvn.shape[0]
    NK = key_dim // HEAD
    HV = g_rows.shape[0]
    rep = HV // NK
    NC = T // CHUNK

    P = DELTA_HEADS
    kw, vw = HEAD * P // rep, HEAD * P

    def body(q_ref, k_ref, v_ref, g_ref, b_ref, o_ref, st_ref, ti_ref, state):
        n = pl.program_id(1)

        @pl.when(n == 0)
        def _():
            state[...] = jnp.zeros_like(state)

        heads = range(P)
        q = [q_ref[:, pl.ds(kh * HEAD, HEAD)] for kh in range(P // rep)]
        k = [k_ref[:, pl.ds(kh * HEAD, HEAD)] for kh in range(P // rep)]
        v = [v_ref[:, pl.ds(p * HEAD, HEAD)] for p in heads]
        s0 = [state[p] for p in heads]
        c = _delta_chunk(q, k, v, [g_ref[p, 0] for p in heads], [b_ref[p, 0] for p in heads], rep)
        vn = _each(lambda u_, w_, s_: u_ - dot_nn(w_, s_), c["u"], c["w"], s0)
        o = _each(lambda qg_, s_, at_, vn_: dot_nn(qg_, s_) + dot_nn(at_, vn_), c["qg"], s0, c["attn"], vn)
        s1 = _each(lambda s_, d_, ks_, vn_: s_ * d_ + dot_tn(ks_, vn_), s0, c["dec"], c["ks"], vn)
        for p in heads:
            st_ref[p, 0] = s0[p]
            ti_ref[p, 0] = c["tinv"][p]
            o_ref[:, pl.ds(p * HEAD, HEAD)] = o[p]
            state[p] = s1[p]

    vrow = BS((P, 1, 1, CHUNK), lambda h, n: (h, n, 0, 0))
    return pl.pallas_call(
        body, grid=(HV // P, NC),
        in_specs=[BS((CHUNK, kw), lambda h, n: (n, h)),
                  BS((CHUNK, kw), lambda h, n: (n, key_dim // kw + h)),
                  BS((CHUNK, vw), lambda h, n: (n, 2 * key_dim // vw + h)), vrow, vrow],
        out_specs=[BS((CHUNK, vw), lambda h, n: (n, h)), BS((P, 1, HEAD, HEAD), lambda h, n: (h, n, 0, 0)),
                   BS((P, 1, CHUNK, CHUNK), lambda h, n: (h, n, 0, 0))],
        out_shape=[S((T, HV * HEAD), F32), S((HV, NC, HEAD, HEAD), F32), S((HV, NC, CHUNK, CHUNK), F32)],
        scratch_shapes=[pltpu.VMEM((P, HEAD, HEAD), F32)],
        compiler_params=_cp(("arbitrary", "arbitrary")), name=name)(qkvn, qkvn, qkvn, g_rows, b_rows)


def delta_bwd(qkvn, g_rows, b_rows, states, tinvs, do, key_dim, name):
    T = qkvn.shape[0]
    NK = key_dim // HEAD
    HV = g_rows.shape[0]
    rep = HV // NK
    NC = T // CHUNK
    P = DELTA_HEADS
    kw, vw = HEAD * P // rep, HEAD * P

    def body(q_ref, k_ref, v_ref, g_ref, b_ref, st_ref, ti_ref, do_ref, dq_ref, dk_ref, dv_ref, dg_ref, dbe_ref,
             dstate):
        step = pl.program_id(1)

        @pl.when(step == 0)
        def _():
            dstate[...] = jnp.zeros_like(dstate)

        heads = range(P)
        q_k = [q_ref[:, pl.ds(kh * HEAD, HEAD)] for kh in range(P // rep)]
        k_k = [k_ref[:, pl.ds(kh * HEAD, HEAD)] for kh in range(P // rep)]
        v = [v_ref[:, pl.ds(p * HEAD, HEAD)] for p in heads]
        s0 = [st_ref[p, 0] for p in heads]
        dsn = [dstate[p] for p in heads]
        dov = [do_ref[:, pl.ds(p * HEAD, HEAD)] for p in heads]
        c = _delta_chunk(q_k, k_k, v, [g_ref[p, 0] for p in heads], [b_ref[p, 0] for p in heads], rep,
                         tinv=[ti_ref[p, 0] for p in heads])
        eye, causal, strict = c["eye"], c["causal"], c["strict"]
        q, k, tinv, decay = c["q"], c["k"], c["tinv"], c["decay"]

        def rsum(a, b):
            return jnp.sum(a * b, axis=1, keepdims=True)

        vn = _each(lambda u_, w_, s_: u_ - dot_nn(w_, s_), c["u"], c["w"], s0)
        dvn = _each(lambda at_, do_, ks_, ds_: dot_tn(at_, do_) + dot_nn(ks_, ds_), c["attn"], dov, c["ks"], dsn)
        dattn = _each(lambda do_, vn_: jnp.where(causal, dot_nt(do_, vn_), 0.0), dov, vn)
        dqg = _each(dot_nt, dov, s0)
        dks = _each(dot_nt, vn, dsn)
        ddec = _each(lambda s_, ds_: jnp.sum(rsum(s_, ds_), axis=0, keepdims=True), s0, dsn)
        dw = _each(lambda dvn_, s_: -dot_nt(dvn_, s_), dvn, s0)
        ds0 = _each(lambda qg_, do_, ds_, d_, w_, dvn_: dot_tn(qg_, do_) + ds_ * d_ - dot_tn(w_, dvn_),
                    c["qg"], dov, dsn, c["dec"], c["w"], dvn)
        dvb = _each(lambda t_, x_: dot_tn(t_, x_, hi="split"), tinv, dvn)
        dkbg = _each(lambda t_, x_: dot_tn(t_, x_, hi="split"), tinv, dw)
        dt = _each(lambda dvn_, vb_, dw_, kbg_: dot_nt(dvn_, vb_, hi="split") + dot_nt(dw_, kbg_, hi="split"),
                   dvn, c["vb"], dw, c["kbg"])
        dtt = _each(lambda dt_, t_: dot_nt(dt_, t_, hi="split"), dt, tinv)
        da = _each(lambda t_, x_: jnp.where(strict, -dot_tn(t_, x_, hi="split"), 0.0), tinv, dtt)
        dm = _each(_mul, da, decay)
        dn = _each(_mul, dattn, decay)
        e = _each(lambda da_, m_, dat_, n_, d_: (da_ * m_ + dat_ * n_) * d_, da, c["m"], dattn, c["n"], decay)
        dkb = _each(lambda dm_, k_, dkbg_, egc_: dot_nn(dm_, k_) + dkbg_ * egc_, dm, k, dkbg, c["egc"])
        dk = _each(lambda dm_, kb_, dn_, q_, dks_, ekl_, dkb_, b_: dot_tn(dm_, kb_) + dot_tn(dn_, q_) + dks_ * ekl_
                   + dkb_ * b_, dm, c["kb"], dn, q, dks, c["ekl"], dkb, c["b_col"])
        dq = _each(lambda dn_, k_, dqg_, egc_: dot_nn(dn_, k_) + dqg_ * egc_, dn, k, dqg, c["egc"])
        dks_ks = _each(rsum, dks, c["ks"])
        dgc_col = _each(lambda e_, dkbg_, kbg_, dqg_, qg_, x_: jnp.sum(e_, axis=1, keepdims=True) + rsum(dkbg_, kbg_)
                        + rsum(dqg_, qg_) - x_ - _col(jnp.sum(e_, axis=0, keepdims=True), eye),
                        e, dkbg, c["kbg"], dqg, c["qg"], dks_ks)
        dgl = _each(lambda x_, dd_, d_: jnp.sum(x_, axis=0, keepdims=True) + dd_ * d_, dks_ks, ddec, c["dec"])
        last = c["ii"][:, 0:1] == CHUNK - 1
        dgc_col = _each(lambda g_, l_: g_ + jnp.where(last, l_, 0.0), dgc_col, dgl)
        dbe_col = _each(lambda dvb_, v_, dkb_, k_: rsum(dvb_, v_) + rsum(dkb_, k_), dvb, v, dkb, k)
        for p in heads:
            vc = pl.ds(p * HEAD, HEAD)
            dstate[p] = ds0[p]
            dq_ref[:, vc] = dq[p]
            dk_ref[:, vc] = dk[p]
            dv_ref[:, vc] = dvb[p] * c["b_col"][p]
            dbe_ref[p, 0] = _row(dbe_col[p], eye)
            dg_ref[p, 0] = jnp.sum(jnp.where(causal, dgc_col[p], 0.0), axis=0, keepdims=True)

    rev = lambda n: NC - 1 - n
    vrow = BS((P, 1, 1, CHUNK), lambda h, n: (h, rev(n), 0, 0))
    vblk = BS((CHUNK, vw), lambda h, n: (rev(n), h))
    return pl.pallas_call(
        body, grid=(HV // P, NC),
        in_specs=[BS((CHUNK, kw), lambda h, n: (rev(n), h)),
                  BS((CHUNK, kw), lambda h, n: (rev(n), key_dim // kw + h)),
                  BS((CHUNK, vw), lambda h, n: (rev(n), 2 * key_dim // vw + h)), vrow, vrow,
                  BS((P, 1, HEAD, HEAD), lambda h, n: (h, rev(n), 0, 0)),
                  BS((P, 1, CHUNK, CHUNK), lambda h, n: (h, rev(n), 0, 0)), vblk],
        out_specs=[vblk, vblk, vblk, vrow, vrow],
        out_shape=[S((T, HV * HEAD), F32)] * 3 + [S((HV, NC, 1, CHUNK), F32)] * 2,
        scratch_shapes=[pltpu.VMEM((P, HEAD, HEAD), F32)],
        compiler_params=_cp(("arbitrary", "arbitrary")), name=name)(
            qkvn, qkvn, qkvn, g_rows, b_rows, states, tinvs, do)


def gdn_post_fwd(o, proj, gate_col0, o_gain, name):
    T, V = o.shape
    cb = min(1024, V)
    j0 = gate_col0 // cb
    bt = _pick(T, (256, 128, 64))

    def body(o_ref, g_ref, gain_ref, y_ref):
        for hh in range(cb // HEAD):
            cols = pl.ds(hh * HEAD, HEAD)
            ov = o_ref[:, cols]
            gt = g_ref[:, cols]
            r = lax.rsqrt(jnp.mean(ov * ov, axis=-1, keepdims=True) + EPS)
            y_ref[:, cols] = (((ov * r) * gain_ref[...]) * (gt * _sigmoid(gt))).astype(y_ref.dtype)

    return pl.pallas_call(
        body, grid=(T // bt, V // cb),
        in_specs=[BS((bt, cb), lambda i, j: (i, j)), BS((bt, cb), lambda i, j: (i, j0 + j)),
                  BS((1, HEAD), lambda i, j: (0, 0))],
        out_specs=BS((bt, cb), lambda i, j: (i, j)), out_shape=S((T, V), MXU_DTYPE),
        compiler_params=_cp(("parallel", "parallel")), name=name)(o, proj, o_gain)


def gdn_post_bwd(o, proj, gate_col0, o_gain, dy, name):
    T, V = o.shape
    cb = min(1024, V)
    j0 = gate_col0 // cb
    bt = _pick(T, (256, 128, 64))

    def body(o_ref, g_ref, gain_ref, dy_ref, do_ref, dgt_ref, dgain_ref):
        i, j = pl.program_id(0), pl.program_id(1)

        @pl.when(jnp.logical_and(i == 0, j == 0))
        def _():
            dgain_ref[...] = jnp.zeros_like(dgain_ref)

        gain = gain_ref[...]
        for hh in range(cb // HEAD):
            cols = pl.ds(hh * HEAD, HEAD)
            ov = o_ref[:, cols]
            gt = g_ref[:, cols]
            dyv = dy_ref[:, cols]
            r = lax.rsqrt(jnp.mean(ov * ov, axis=-1, keepdims=True) + EPS)
            ohat = ov * r
            sg = _sigmoid(gt)
            dyn = dyv * (gt * sg)
            dgt_ref[:, cols] = (dyv * (ohat * gain) * (sg * (1.0 + gt * (1.0 - sg)))).astype(dgt_ref.dtype)
            dgain_ref[...] += jnp.sum(dyn * ohat, axis=0, keepdims=True)
            dh = dyn * gain
            do_ref[:, cols] = r * (dh - ohat * jnp.mean(dh * ohat, axis=-1, keepdims=True))

    blk = BS((bt, cb), lambda i, j: (i, j))
    vec = BS((1, HEAD), lambda i, j: (0, 0))
    return pl.pallas_call(
        body, grid=(T // bt, V // cb),
        in_specs=[blk, BS((bt, cb), lambda i, j: (i, j0 + j)), vec, blk],
        out_specs=[blk, blk, vec], out_shape=[S((T, V), F32), S((T, V), MXU_DTYPE), S((1, HEAD), F32)],
        compiler_params=_cp(("arbitrary", "arbitrary")), name=name)(o, proj, o_gain, dy)


def adamw(w, g, m, v, name):
    shape = w.shape
    n = 1
    for d in shape:
        n *= d
    cols = shape[-1] if len(shape) > 1 else n
    rows = n // cols
    br = rows
    for cand in (512, 256, 128, 64, 32, 16, 8):
        if rows % cand == 0 and cand * cols * 4 * 14 <= 36 * MIB:
            br = cand
            break
    c1 = 1.0 - ADAM_B1 ** ADAM_STEP
    c2 = 1.0 - ADAM_B2 ** ADAM_STEP

    def body(w_ref, g_ref, m_ref, v_ref, d_ref, nm_ref, nv_ref):
        gv = g_ref[...]
        nm = ADAM_B1 * m_ref[...] + (1.0 - ADAM_B1) * gv
        nv = ADAM_B2 * v_ref[...] + (1.0 - ADAM_B2) * (gv * gv)
        nm_ref[...] = nm
        nv_ref[...] = nv
        d_ref[...] = -ADAM_LR * ((nm / c1) / (jnp.sqrt(nv / c2) + ADAM_EPS) + ADAM_WD * w_ref[...])

    blk = BS((br, cols), lambda i: (i, 0))
    outs = pl.pallas_call(
        body, grid=(rows // br,), in_specs=[blk] * 4, out_specs=[blk] * 3,
        out_shape=[S((rows, cols), F32)] * 3, compiler_params=_cp(("parallel",), 48), name=name)(
            *[t.reshape(rows, cols) for t in (w, g, m, v)])
    return [t.reshape(shape) for t in outs]


def _with_job(job, *args, **kwargs):
    if job is None:
        return matmul(*args, **kwargs), []
    return matmul(*args, job=job, **kwargs)


def _ffn_fwd(x, gain, p, tag, jobs=(None, None)):
    (h2,) = norm_fwd(x, gain, [MXU_DTYPE], f"{tag}_norm")
    up, got_a = _with_job(jobs[0], h2, p["w_up_t"], "nt", out_dtype=MXU_DTYPE, name=f"{tag}_up")
    act = ffn_act_fwd(up, p["conv"], f"{tag}_act")
    out, got_b = _with_job(jobs[1], act, p["w_down"], "nn", res=x, name=f"{tag}_down")
    return out, (x, h2, up, act), got_a + got_b


def _ffn_bwd(dx, saved, gain, p, tag, first_job=None, then_jobs=None):
    x, h2, up, act = saved
    g_act, got = _with_job(first_job, dx[1], p["w_down"], "nt", out_dtype=MXU_DTYPE, name=f"{tag}_bdown")
    jobs = then_jobs(got) if then_jobs else (None, None)
    d_down = matmul(act, dx[1], "tn", name=f"{tag}_wdown")
    du, db, d_conv = ffn_act_bwd_a(up, p["conv"], g_act, f"{tag}_bact_a")
    dup = ffn_act_bwd_b(du, db, p["conv"], f"{tag}_bact_b")
    dh2, got_a = _with_job(jobs[0], dup, p["w_up_t"], "nn", name=f"{tag}_bup")
    d_up_t, got_b = _with_job(jobs[1], dup, h2, "tn", name=f"{tag}_wup")
    dx, d_gain = norm_bwd(x, gain, dh2, dx[0], f"{tag}_bnorm")
    return dx, d_gain, dict(w_up_t=d_up_t, conv=d_conv, w_down=d_down), got_a + got_b


def _joint_bias(bias):
    rows = [jnp.pad(bias, ((0, 0), (0, 0), (qc * CHUNK, (ATT_CHUNKS - 1 - qc) * CHUNK)), constant_values=MASK_VALUE)
            for qc in range(ATT_CHUNKS)]
    return jnp.concatenate(rows, axis=1)


def _joint_bias_grad(dbias):
    return sum(dbias[:, qc * CHUNK:(qc + 1) * CHUNK, qc * CHUNK:qc * CHUNK + BAND] for qc in range(ATT_CHUNKS))


def _att_fwd(x, gain, p, tag):
    (h,) = norm_fwd(x, gain, [MXU_DTYPE], f"{tag}_norm")
    qkv = matmul(h, p["w_qkv_t"], "nt", name=f"{tag}_qkv")
    bias = _joint_bias(rel_bias_expand(p["rel_bias"], f"{tag}_rel").transpose(1, 0, 2))
    o = attn_fwd(qkv, p["q_gain"], p["k_gain"], bias, f"{tag}_core")
    out = matmul(o, p["w_o"], "nn", res=x, name=f"{tag}_out")
    return out, (x, h, qkv, o, bias)


def _att_bwd(dx, saved, gain, p, tag):
    x, h, qkv, o, bias = saved
    do = matmul(dx[1], p["w_o"], "nt", name=f"{tag}_bout")
    d_wo = matmul(o, dx[1], "tn", name=f"{tag}_wout")
    dq, dk, dv, d_gq, d_gk, dbias = attn_bwd(qkv, do, p["q_gain"], p["k_gain"], bias, f"{tag}_bcore")
    dqkv = jnp.concatenate([dq, dk, dv], axis=1)
    dh = matmul(dqkv, p["w_qkv_t"], "nn", name=f"{tag}_bqkv")
    d_wqkv_t = matmul(dqkv, h, "tn", name=f"{tag}_wqkv")
    d_rb = rel_bias_reduce(_joint_bias_grad(dbias).transpose(1, 0, 2), p["rel_bias"].shape[1], f"{tag}_brel")
    dx, d_gain = norm_bwd(x, gain, dh, dx[0], f"{tag}_bnorm")
    return dx, d_gain, dict(w_qkv_t=d_wqkv_t, w_o=d_wo, q_gain=d_gq, k_gain=d_gk, rel_bias=d_rb)


def _pool_fwd(x, gain, p, tag):
    (hf,) = norm_fwd(x, gain, [F32], f"{tag}_norm")
    out, pooled = pool_fwd(hf, x, p["w"], p["scale"], f"{tag}_core")
    return out, (x, pooled)


def _pool_bwd(dx, saved, gain, p, tag):
    x, pooled = saved
    dpooled, d_w, d_scale = pool_bwd_a(dx[0], pooled, p["w"], p["scale"], f"{tag}_bcore_a")
    dh = pool_bwd_b(dpooled, f"{tag}_bcore_b")
    dx, d_gain = norm_bwd(x, gain, dh, dx[0], f"{tag}_bnorm")
    return dx, d_gain, dict(w=d_w, scale=d_scale)


def _rows_layout(t, hv):
    return t.T.reshape(hv, t.shape[0] // CHUNK, 1, CHUNK)


def _gdn_fwd(x, gain, p, tag):
    T = x.shape[0]
    hv = p["a_log"].shape[1]
    key_dim = p["key_dim"]
    C = p["conv"].shape[1]
    (h,) = norm_fwd(x, gain, [MXU_DTYPE], f"{tag}_norm")
    proj = matmul(h, p["w_main_t"], "nt", name=f"{tag}_in")
    ab = matmul(h, p["w_ab_t"], "nt", name=f"{tag}_in_ab")
    a, b = ab[:, :hv], ab[:, hv:2 * hv]
    qkvn = gdn_pre_fwd(proj, p["conv"], key_dim, f"{tag}_pre")
    g, beta = gdn_gate_fwd(a, b, p["a_log"], p["dt_bias"], f"{tag}_gate")
    g_rows, b_rows = _rows_layout(g, hv), _rows_layout(beta, hv)
    o, states, tinvs = delta_fwd(qkvn, g_rows, b_rows, key_dim, f"{tag}_delta")
    y = gdn_post_fwd(o, proj, C, p["o_gain"], f"{tag}_post")
    out = matmul(y, p["w_o"], "nn", res=x, name=f"{tag}_out")
    return out, (x, h, proj, a, b, qkvn, g_rows, b_rows, o, states, tinvs, y)


def _gdn_bwd(dx, saved, gain, p, tag):
    x, h, proj, a, b, qkvn, g_rows, b_rows, o, states, tinvs, y = saved
    T = x.shape[0]
    hv = p["a_log"].shape[1]
    key_dim = p["key_dim"]
    C = p["conv"].shape[1]
    dy = matmul(dx[1], p["w_o"], "nt", name=f"{tag}_bout")
    d_wo = matmul(y, dx[1], "tn", name=f"{tag}_wout")
    do, dgate, d_ogain = gdn_post_bwd(o, proj, C, p["o_gain"], dy, f"{tag}_bpost")
    dq_v, dk_v, dv, dg_rows, dbe_rows = delta_bwd(qkvn, g_rows, b_rows, states, tinvs, do, key_dim,
                                                  f"{tag}_bdelta")
    dg = dg_rows.reshape(hv, T).T
    dbeta = dbe_rows.reshape(hv, T).T
    da, db, d_alog, d_dtb = gdn_gate_bwd(a, b, p["a_log"], p["dt_bias"], dg, dbeta, f"{tag}_bgate")
    du, d_conv = gdn_pre_bwd_a(proj, p["conv"], dq_v, dk_v, dv, key_dim, f"{tag}_bpre_a")
    dproj = gdn_pre_bwd_b(du, dgate, p["conv"], key_dim, f"{tag}_bpre_b")
    dab = jnp.concatenate([da, db, jnp.zeros((T, LANES - 2 * hv), F32)], axis=1)
    dh = matmul(dproj, p["w_main_t"], "nn", name=f"{tag}_bin")
    dh = matmul(dab, p["w_ab_t"], "nn", res=dh, name=f"{tag}_bin_ab")
    d_main_t = matmul(dproj, h, "tn", name=f"{tag}_win")
    d_ab_t = matmul(dab, h, "tn", name=f"{tag}_win_ab")
    dx, d_gain = norm_bwd(x, gain, dh, dx[0], f"{tag}_bnorm")
    return dx, d_gain, dict(w_main_t=d_main_t, w_ab_t=d_ab_t, conv=d_conv, a_log=d_alog, dt_bias=d_dtb,
                            o_gain=d_ogain, w_o=d_wo)


_MIXERS = ((_att_fwd, _att_bwd), (_pool_fwd, _pool_bwd), (_gdn_fwd, _gdn_bwd))


def local_step(x, target, full, slots, place):
    depth = full["ffn_conv"].shape[0]
    groups = _layer_groups(depth)
    core = place[1:2]
    big = dict(zip(groups[0], gather_slots([slots[k] for k in groups[0]], "gather_l0")))
    saved = []
    for i in range(depth):
        kind, j = i % 3, i // 3
        mp, fp = layer_weights(i, big, full)
        x, s_mix = _MIXERS[kind][0](x, full["mix_norm"][i:i + 1], mp, f"l{i}_mix")
        jobs, keys = (None, None), ()
        if i + 1 < depth:
            keys = groups[i + 1][0::2] + groups[i + 1][1::2]
            jobs = (gather_job([slots[k] for k in groups[i + 1][0::2]]),
                    gather_job([slots[k] for k in groups[i + 1][1::2]]))
        x, s_ffn, got = _ffn_fwd(x, full["ffn_norm"][i:i + 1], fp, f"l{i}_ffn", jobs)
        saved.append((s_mix, s_ffn, mp, fp))
        if keys:
            big = dict(zip(keys, gather_pass(got, f"gather_pass_l{i + 1}")))
    loss, dx = loss_and_grad(x, target, "loss")
    n_mix = [len([i for i in range(depth) if i % 3 == kind]) for kind in range(3)]
    G = dict(mix_norm=[None] * depth, ffn_norm=[None] * depth, ffn=[None] * depth,
             att=[None] * n_mix[0], pool=[None] * n_mix[1], gdn=[None] * n_mix[2])
    summed = {}

    def finish(keys, chip_sums, got, tag):
        mine = [add_cols4(p, b, place, f"rs_add4_{n}_{j}") for (n, j), p, b in zip(keys, chip_sums, got)]
        summed.update(zip(keys, join_cols(mine, f"rs_join_{tag}")))

    def chip_sum(keys, gs, theirs):
        return [add_cols2(g, a, core, f"rs_add2_{n}_{j}") for (n, j), g, a in zip(keys, gs, theirs)]

    pending = None
    for i in reversed(range(depth)):
        kind, j = i % 3, i // 3
        s_mix, s_ffn, mp, fp = saved[i]
        if pending:
            keys, gs = pending
            order = keys[0::2] + keys[1::2]
            sums = []

            def then_jobs(theirs, keys=keys, gs=gs, sums=sums):
                sums.extend(chip_sum(keys, gs, theirs))
                return scatter_job(sums[0::2]), scatter_job(sums[1::2])

            dx, G["ffn_norm"][i], gf, got = _ffn_bwd(dx, s_ffn, full["ffn_norm"][i:i + 1], fp, f"l{i}_ffn",
                                                     swap_job(gs), then_jobs)
            finish(order, sums[0::2] + sums[1::2], got, f"l{i + 1}")
        else:
            dx, G["ffn_norm"][i], gf, _ = _ffn_bwd(dx, s_ffn, full["ffn_norm"][i:i + 1], fp, f"l{i}_ffn")
        dx, G["mix_norm"][i], gm = _MIXERS[kind][1](dx, s_mix, full["mix_norm"][i:i + 1], mp, f"l{i}_mix")
        G["ffn"][i] = gf
        G[("att", "pool", "gdn")[kind]][j] = gm
        gbig = layer_big_grads(i, gm, gf, full["gdn_a_log"].shape[1])
        pending = (groups[i], [gbig[k] for k in groups[i]])
    keys, gs = pending
    sums = chip_sum(keys, gs, swap_cols(gs, "rs_swap_l0"))
    finish(keys, sums, scatter_cols(sums, "rs_scatter_l0"), "l0")
    return loss, dx[0], summed, G


SHARD_AXIS = dict(att_w_qkv=2, att_w_o=1, pool_w=2, gdn_w_in=2, gdn_w_o=1, ffn_w_up=2, ffn_w_down=1,
                  att_rel_bias=2, gdn_conv=2, ffn_conv=2)
BIG = ("att_w_qkv", "att_w_o", "pool_w", "gdn_w_in", "gdn_w_o", "ffn_w_up", "ffn_w_down")
SMALL_SHARDED = ("att_rel_bias", "gdn_conv", "ffn_conv")
REPLICATED = ("mix_norm", "ffn_norm", "att_q_gain", "att_k_gain", "pool_scale", "gdn_a_log", "gdn_dt_bias",
              "gdn_o_gain")
WEIGHTS = ("mix_norm", "ffn_norm", "att_w_qkv", "att_q_gain", "att_k_gain", "att_rel_bias", "att_w_o", "pool_w",
           "pool_scale", "gdn_w_in", "gdn_conv", "gdn_a_log", "gdn_dt_bias", "gdn_o_gain", "gdn_w_o", "ffn_w_up",
           "ffn_conv", "ffn_w_down")


def _merge(stacked, axis):
    t = jnp.moveaxis(stacked, 0, axis)
    return t.reshape(t.shape[:axis] + (t.shape[axis] * t.shape[axis + 1],) + t.shape[axis + 2:])


def _pad_to(t, axis, size):
    pad = [(0, 0)] * t.ndim
    pad[axis] = (0, size - t.shape[axis])
    return jnp.pad(t, pad)


def _round_up(n, m):
    return (n + m - 1) // m * m


def to_comm(name, t):
    if name in ("att_w_qkv", "gdn_w_in"):
        return t.T
    if name == "ffn_w_up":
        d, n = t.shape
        return _pad_to(t.T.reshape(2, n // 2, d), 1, _round_up(n // 2, LANES)).reshape(-1, d)
    if name == "ffn_w_down":
        return _pad_to(t, 0, _round_up(t.shape[0], LANES))
    if name == "pool_w":
        return t.reshape(-1, t.shape[-1])
    return t


def from_comm(name, r, shape):
    if name in ("att_w_qkv", "gdn_w_in"):
        return r.T
    if name == "ffn_w_up":
        d, n = shape
        return r.reshape(2, -1, d)[:, :n // 2].reshape(n, d).T
    if name == "ffn_w_down":
        return r[:shape[0]]
    return r.reshape(shape)


def _rows(t):
    return t.reshape(-1, t.shape[-1])


def layer_weights(i, big, full):
    kind, j = i % 3, i // 3
    F4 = full["ffn_conv"].shape[2] // N_CHIPS
    conv = _pad_to(full["ffn_conv"][i].reshape(FFN_CONV, N_CHIPS, F4), 2, _round_up(F4, LANES)).reshape(FFN_CONV, -1)
    fp = dict(w_up_t=_rows(big["ffn_w_up", i]), conv=_pad_to(conv, 0, SUBLANES), w_down=_rows(big["ffn_w_down", i]))
    if kind == 0:
        mp = dict(w_qkv_t=_rows(big["att_w_qkv", j]), w_o=_rows(big["att_w_o", j]),
                  q_gain=full["att_q_gain"][j:j + 1], k_gain=full["att_k_gain"][j:j + 1],
                  rel_bias=full["att_rel_bias"][j])
    elif kind == 1:
        t = big["pool_w", j]
        G = len(POOL_WINDOWS)
        dg = t.shape[-1]
        w = jnp.moveaxis(t.reshape(N_CHIPS, G, dg // N_CHIPS, dg), 0, 1).reshape(G, dg, dg)
        mp = dict(w=w, scale=full["pool_scale"][j:j + 1])
    else:
        C = full["gdn_conv"].shape[2]
        wt = _rows(big["gdn_w_in", j])
        V = _rows(big["gdn_w_o", j]).shape[0]
        mp = dict(w_main_t=wt[:C + V], w_ab_t=_pad_to(wt[C + V:], 0, LANES),
                  conv=_pad_to(full["gdn_conv"][j], 0, SUBLANES), a_log=full["gdn_a_log"][j:j + 1],
                  dt_bias=full["gdn_dt_bias"][j:j + 1], o_gain=full["gdn_o_gain"][j:j + 1],
                  w_o=_rows(big["gdn_w_o", j]), key_dim=(C - V) // 2)
    return mp, fp


def layer_big_grads(i, gm, gf, hv):
    kind, j = i % 3, i // 3

    def slots(t):
        return t.reshape(N_CHIPS, t.shape[0] // N_CHIPS, t.shape[1])

    out = {("ffn_w_up", i): slots(gf["w_up_t"]), ("ffn_w_down", i): slots(gf["w_down"])}
    if kind == 0:
        out["att_w_qkv", j] = slots(gm["w_qkv_t"])
        out["att_w_o", j] = slots(gm["w_o"])
    elif kind == 1:
        n, dg, _ = gm["w"].shape
        out["pool_w", j] = jnp.moveaxis(gm["w"].reshape(n, N_CHIPS, dg // N_CHIPS, dg), 1, 0).reshape(N_CHIPS, -1, dg)
    else:
        out["gdn_w_in", j] = slots(jnp.concatenate([gm["w_main_t"], gm["w_ab_t"][:2 * hv]], axis=0))
        out["gdn_w_o", j] = slots(gm["w_o"])
    return out


def small_grads(G, full):
    F = full["ffn_conv"].shape[2]
    F4 = F // N_CHIPS

    def conv(g):
        return g["conv"][:FFN_CONV].reshape(FFN_CONV, N_CHIPS, -1)[:, :, :F4].reshape(FFN_CONV, F)

    return dict(
        mix_norm=jnp.concatenate(G["mix_norm"], axis=0), ffn_norm=jnp.concatenate(G["ffn_norm"], axis=0),
        ffn_conv=jnp.stack([conv(g) for g in G["ffn"]]),
        att_q_gain=jnp.concatenate([g["q_gain"] for g in G["att"]], axis=0),
        att_k_gain=jnp.concatenate([g["k_gain"] for g in G["att"]], axis=0),
        att_rel_bias=jnp.stack([g["rel_bias"] for g in G["att"]]),
        pool_scale=jnp.concatenate([g["scale"] for g in G["pool"]], axis=0),
        gdn_conv=jnp.stack([g["conv"][:GDN_CONV] for g in G["gdn"]]),
        gdn_a_log=jnp.concatenate([g["a_log"] for g in G["gdn"]], axis=0),
        gdn_dt_bias=jnp.concatenate([g["dt_bias"] for g in G["gdn"]], axis=0),
        gdn_o_gain=jnp.concatenate([g["o_gain"] for g in G["gdn"]], axis=0))


ANY = BS(memory_space=pl.ANY)
PACK_COLS = 1024
PACK_ROWS = 32


def _place():
    x, y, c = lax.axis_index("x"), lax.axis_index("y"), lax.axis_index("c")
    chips = [(1 - x, y), (x, 1 - y), (1 - x, 1 - y)]
    return x, y, c, chips


def _remote(src, dst, send_sem, recv_sem, to):
    return pltpu.make_async_remote_copy(src_ref=src, dst_ref=dst, send_sem=send_sem, recv_sem=recv_sem,
                                        device_id=to, device_id_type=MESH)


def gather_chips(shard, name):
    R, C = shard.shape
    half = R // 2

    def body(x_ref, o_ref, send_sems, recv_sems, local_sem):
        x, y, c, chips = _place()
        mine_rows = pl.ds(c * half, half)
        other_rows = pl.ds((1 - c) * half, half)
        own = pltpu.make_async_copy(x_ref, o_ref.at[2 * x + y], local_sem)
        own.start()
        first = [_remote(x_ref.at[mine_rows], o_ref.at[2 * x + y, mine_rows], send_sems.at[j], recv_sems.at[j],
                         (cx, cy, c)) for j, (cx, cy) in enumerate(chips)]
        for cp in first:
            cp.start()
        passed = []
        for j, (cx, cy) in enumerate(chips):
            landed = o_ref.at[2 * cx + cy, mine_rows]
            _remote(landed, landed, send_sems.at[j], recv_sems.at[j], (cx, cy, c)).wait_recv()
            cp = _remote(landed, landed, send_sems.at[3 + j], recv_sems.at[3 + j], (x, y, 1 - c))
            cp.start()
            passed.append(cp)
        for j, (cx, cy) in enumerate(chips):
            landed = o_ref.at[2 * cx + cy, other_rows]
            _remote(landed, landed, send_sems.at[3 + j], recv_sems.at[3 + j], (x, y, 1 - c)).wait_recv()
        for cp in first + passed:
            cp.wait_send()
        own.wait()

    return pl.pallas_call(
        body, out_shape=S((N_CHIPS, R, C), shard.dtype), in_specs=[ANY], out_specs=ANY,
        scratch_shapes=[pltpu.SemaphoreType.DMA((6,)), pltpu.SemaphoreType.DMA((6,)), pltpu.SemaphoreType.DMA],
        name=name)(shard)


def _tile(R, hc):
    if R % 256 == 0:
        return _pick(R, (512, 256)), hc
    return R, _pick(hc, (256, 128))


def _half(c, hc):
    return pl.ds(pl.multiple_of(c * hc, hc), hc)


def prep_slot(t, s_me, name):
    R, C = t.shape
    br, bc = _tile(R, C // 2)

    def body(s_ref, t_ref, o_ref):
        o_ref[0] = t_ref[...].astype(o_ref.dtype)

    return pl.pallas_call(
        body, grid_spec=pltpu.PrefetchScalarGridSpec(
            num_scalar_prefetch=1, grid=(R // br, C // bc),
            in_specs=[BS((br, bc), lambda i, j, s: (i, j))],
            out_specs=BS((1, br, bc), lambda i, j, s: (s[0], i, j))),
        out_shape=S((N_CHIPS, R, C), MXU_DTYPE), compiler_params=_cp(("parallel", "parallel")), name=name)(
            s_me.reshape(1), t)


def gather_slots(arrs, name):
    nt = len(arrs)

    def body(*refs):
        outs = refs[nt:2 * nt]
        send_sems, recv_sems = refs[2 * nt:]
        x, y, c, chips = _place()
        me = 2 * x + y
        first, passed = [], []
        for t, o in enumerate(outs):
            mine = _half(c, o.shape[2] // 2)
            for j, (cx, cy) in enumerate(chips):
                cp = _remote(o.at[me, :, mine], o.at[me, :, mine], send_sems.at[t, j], recv_sems.at[t, j], (cx, cy, c))
                cp.start()
                first.append(cp)
        for t, o in enumerate(outs):
            mine = _half(c, o.shape[2] // 2)
            for j, (cx, cy) in enumerate(chips):
                landed = o.at[2 * cx + cy, :, mine]
                _remote(landed, landed, send_sems.at[t, j], recv_sems.at[t, j], (cx, cy, c)).wait_recv()
                cp = _remote(landed, landed, send_sems.at[t, 3 + j], recv_sems.at[t, 3 + j], (x, y, 1 - c))
                cp.start()
                passed.append(cp)
        for t, o in enumerate(outs):
            other = _half(1 - c, o.shape[2] // 2)
            for j, (cx, cy) in enumerate(chips):
                landed = o.at[2 * cx + cy, :, other]
                _remote(landed, landed, send_sems.at[t, 3 + j], recv_sems.at[t, 3 + j], (x, y, 1 - c)).wait_recv()
        for cp in first + passed:
            cp.wait_send()

    return pl.pallas_call(
        body, out_shape=[S(a.shape, a.dtype) for a in arrs], in_specs=[ANY] * nt, out_specs=[ANY] * nt,
        input_output_aliases={t: t for t in range(nt)},
        scratch_shapes=[pltpu.SemaphoreType.DMA((nt, 6)), pltpu.SemaphoreType.DMA((nt, 6))], name=name)(*arrs)


def swap_cols(gs, name):
    nt = len(gs)

    def body(*refs):
        ins, outs = refs[:nt], refs[nt:2 * nt]
        send_sems, recv_sems = refs[2 * nt:]
        x, y, c, _ = _place()
        sent = []
        for t, (g, o) in enumerate(zip(ins, outs)):
            cp = _remote(g.at[:, :, _half(1 - c, o.shape[2])], o, send_sems.at[t], recv_sems.at[t], (x, y, 1 - c))
            cp.start()
            sent.append(cp)
        for cp in sent:
            cp.wait()

    return pl.pallas_call(
        body, out_shape=[S(g.shape[:2] + (g.shape[2] // 2,), g.dtype) for g in gs], in_specs=[ANY] * nt,
        out_specs=[ANY] * nt, scratch_shapes=[pltpu.SemaphoreType.DMA((nt,)), pltpu.SemaphoreType.DMA((nt,))],
        name=name)(*gs)


def add_cols2(g, other, c, name):
    n, R, C = g.shape
    hc = C // 2
    br, bc = _tile(R, hc)
    nj = hc // bc

    def body(c_ref, g_ref, o_ref, out_ref):
        out_ref[...] = (g_ref[...] + o_ref[...]).astype(out_ref.dtype)

    blk = BS((1, br, bc), lambda s, i, j, c_ref: (s, i, j))
    return pl.pallas_call(
        body, grid_spec=pltpu.PrefetchScalarGridSpec(
            num_scalar_prefetch=1, grid=(n, R // br, nj),
            in_specs=[BS((1, br, bc), lambda s, i, j, c_ref: (s, i, c_ref[0] * nj + j)), blk], out_specs=blk),
        out_shape=S((n, R, hc), BF16), compiler_params=_cp(("parallel", "parallel", "parallel")), name=name)(
            c, g, other)


def scatter_cols(ps, name):
    nt = len(ps)

    def body(*refs):
        ins, outs = refs[:nt], refs[nt:2 * nt]
        send_sems, recv_sems = refs[2 * nt:]
        x, y, c, chips = _place()
        sent = []
        for t, (p, o) in enumerate(zip(ins, outs)):
            for j, (cx, cy) in enumerate(chips):
                cp = _remote(p.at[2 * cx + cy], o.at[j], send_sems.at[t, j], recv_sems.at[t, j], (cx, cy, c))
                cp.start()
                sent.append(cp)
        for cp in sent:
            cp.wait()

    return pl.pallas_call(
        body, out_shape=[S((N_CHIPS - 1,) + p.shape[1:], p.dtype) for p in ps], in_specs=[ANY] * nt,
        out_specs=[ANY] * nt,
        scratch_shapes=[pltpu.SemaphoreType.DMA((nt, 3)), pltpu.SemaphoreType.DMA((nt, 3))], name=name)(*ps)


def add_cols4(p, got, place, name):
    n, R, hc = p.shape
    br, bc = _tile(R, hc)
    nj = hc // bc

    def body(pl_ref, p_ref, g_ref, out_ref):
        acc = p_ref[0].astype(F32)
        for j in range(n - 1):
            acc += g_ref[j].astype(F32)
        out_ref[...] = acc

    return pl.pallas_call(
        body, grid_spec=pltpu.PrefetchScalarGridSpec(
            num_scalar_prefetch=1, grid=(R // br, nj),
            in_specs=[BS((1, br, bc), lambda i, j, pl_ref: (pl_ref[0], i, j)),
                      BS((n - 1, br, bc), lambda i, j, pl_ref: (0, i, j))],
            out_specs=BS((br, bc), lambda i, j, pl_ref: (i, pl_ref[1] * nj + j))),
        out_shape=S((R, 2 * hc), F32), compiler_params=_cp(("parallel", "parallel")), name=name)(place, p, got)


def join_cols(rs, name):
    nt = len(rs)

    def body(*refs):
        outs = refs[nt:2 * nt]
        send_sems, recv_sems = refs[2 * nt:]
        x, y, c, _ = _place()
        sent = []
        for t, o in enumerate(outs):
            mine = o.at[:, _half(c, o.shape[1] // 2)]
            cp = _remote(mine, mine, send_sems.at[t], recv_sems.at[t], (x, y, 1 - c))
            cp.start()
            sent.append(cp)
        for t, o in enumerate(outs):
            theirs = o.at[:, _half(1 - c, o.shape[1] // 2)]
            _remote(theirs, theirs, send_sems.at[t], recv_sems.at[t], (x, y, 1 - c)).wait_recv()
        for cp in sent:
            cp.wait_send()

    return pl.pallas_call(
        body, out_shape=[S(r.shape, r.dtype) for r in rs], in_specs=[ANY] * nt, out_specs=[ANY] * nt,
        input_output_aliases={t: t for t in range(nt)},
        scratch_shapes=[pltpu.SemaphoreType.DMA((nt,)), pltpu.SemaphoreType.DMA((nt,))], name=name)(*rs)


def gather_job(slots):
    nt = len(slots)

    def copies(outs, send_sems, recv_sems, sent):
        x, y, c, chips = _place()
        me = 2 * x + y
        out = []
        for t, o in enumerate(outs):
            mine = _half(c, o.shape[2] // 2)
            for j, (cx, cy) in enumerate(chips):
                rows = o.at[me if sent else 2 * cx + cy, :, mine]
                out.append(_remote(rows, rows, send_sems.at[3 * t + j], recv_sems.at[3 * t + j], (cx, cy, c)))
        return out

    def start(ins, outs, send_sems, recv_sems):
        for cp in copies(outs, send_sems, recv_sems, True):
            cp.start()

    def finish(ins, outs, send_sems, recv_sems):
        for cp in copies(outs, send_sems, recv_sems, False):
            cp.wait_recv()
        for cp in copies(outs, send_sems, recv_sems, True):
            cp.wait_send()

    return dict(ins=list(slots), out_shapes=[S(a.shape, a.dtype) for a in slots], aliases=[(t, t) for t in range(nt)],
                nsem=3 * nt, start=start, finish=finish)


def gather_pass(arrs, name):
    nt = len(arrs)

    def body(*refs):
        outs = refs[nt:2 * nt]
        send_sems, recv_sems = refs[2 * nt:]
        x, y, c, chips = _place()
        sent = []
        for t, o in enumerate(outs):
            mine = _half(c, o.shape[2] // 2)
            for j, (cx, cy) in enumerate(chips):
                landed = o.at[2 * cx + cy, :, mine]
                cp = _remote(landed, landed, send_sems.at[t, j], recv_sems.at[t, j], (x, y, 1 - c))
                cp.start()
                sent.append(cp)
        for t, o in enumerate(outs):
            other = _half(1 - c, o.shape[2] // 2)
            for j, (cx, cy) in enumerate(chips):
                landed = o.at[2 * cx + cy, :, other]
                _remote(landed, landed, send_sems.at[t, j], recv_sems.at[t, j], (x, y, 1 - c)).wait_recv()
        for cp in sent:
            cp.wait_send()

    return pl.pallas_call(
        body, out_shape=[S(a.shape, a.dtype) for a in arrs], in_specs=[ANY] * nt, out_specs=[ANY] * nt,
        input_output_aliases={t: t for t in range(nt)},
        scratch_shapes=[pltpu.SemaphoreType.DMA((nt, 3)), pltpu.SemaphoreType.DMA((nt, 3))], name=name)(*arrs)


def swap_job(gs):
    def copies(ins, outs, send_sems, recv_sems):
        x, y, c, _ = _place()
        return [_remote(g.at[:, :, _half(1 - c, o.shape[2])], o, send_sems.at[t], recv_sems.at[t], (x, y, 1 - c))
                for t, (g, o) in enumerate(zip(ins, outs))]

    def start(ins, outs, send_sems, recv_sems):
        for cp in copies(ins, outs, send_sems, recv_sems):
            cp.start()

    def finish(ins, outs, send_sems, recv_sems):
        for cp in copies(ins, outs, send_sems, recv_sems):
            cp.wait()

    return dict(ins=list(gs), out_shapes=[S(g.shape[:2] + (g.shape[2] // 2,), g.dtype) for g in gs], aliases=[],
                nsem=len(gs), start=start, finish=finish)


def scatter_job(ps):
    nt = len(ps)

    def copies(ins, outs, send_sems, recv_sems):
        x, y, c, chips = _place()
        return [_remote(p.at[2 * cx + cy], o.at[j], send_sems.at[3 * t + j], recv_sems.at[3 * t + j], (cx, cy, c))
                for t, (p, o) in enumerate(zip(ins, outs)) for j, (cx, cy) in enumerate(chips)]

    def start(ins, outs, send_sems, recv_sems):
        for cp in copies(ins, outs, send_sems, recv_sems):
            cp.start()

    def finish(ins, outs, send_sems, recv_sems):
        for cp in copies(ins, outs, send_sems, recv_sems):
            cp.wait()

    return dict(ins=list(ps), out_shapes=[S((N_CHIPS - 1,) + p.shape[1:], p.dtype) for p in ps], aliases=[],
                nsem=3 * nt, start=start, finish=finish)


def sum_devices(v, name):
    R, C = v.shape

    def body(v_ref, o_ref, slots, send_sems, recv_sems):
        x, y, c, _ = _place()
        me = 4 * x + 2 * y + c
        slots[me] = v_ref[...]
        sent = []
        for r in range(1, 8):
            peer = (x ^ (r >> 2), y ^ ((r >> 1) & 1), c ^ (r & 1))
            cp = _remote(v_ref, slots.at[me], send_sems.at[r - 1], recv_sems.at[r - 1], peer)
            cp.start()
            sent.append(cp)
        for r in range(1, 8):
            peer = (x ^ (r >> 2), y ^ ((r >> 1) & 1), c ^ (r & 1))
            theirs = slots.at[4 * peer[0] + 2 * peer[1] + peer[2]]
            _remote(v_ref, theirs, send_sems.at[r - 1], recv_sems.at[r - 1], peer).wait_recv()
        for cp in sent:
            cp.wait_send()
        acc = slots[0]
        for k in range(1, 8):
            acc += slots[k]
        o_ref[...] = acc

    vm = BS(memory_space=pltpu.VMEM)
    return pl.pallas_call(
        body, out_shape=S((R, C), F32), in_specs=[vm], out_specs=vm,
        scratch_shapes=[pltpu.VMEM((8, R, C), F32), pltpu.SemaphoreType.DMA((7,)), pltpu.SemaphoreType.DMA((7,))],
        compiler_params=pltpu.CompilerParams(vmem_limit_bytes=32 * MIB), name=name)(v)


def _pack(arrays, dtype, cols, row_mult):
    flat = jnp.concatenate([a.astype(dtype).reshape(-1) for a in arrays])
    n = flat.shape[0]
    total = _round_up(n, cols * row_mult)
    return jnp.pad(flat, (0, total - n)).reshape(total // cols, cols)


def _unpack(flat, shapes):
    out, off = [], 0
    for shp in shapes:
        n = 1
        for d in shp:
            n *= d
        out.append(flat[..., off:off + n].reshape(flat.shape[:-1] + tuple(shp)))
        off += n
    return out


def _layer_groups(depth):
    groups = []
    for i in range(depth):
        kind, j = i % 3, i // 3
        mix = ((("att_w_qkv", j), ("att_w_o", j)), (("pool_w", j),), (("gdn_w_in", j), ("gdn_w_o", j)))[kind]
        groups.append(mix + (("ffn_w_up", i), ("ffn_w_down", i)))
    return groups


def kernel(x, mix_norm, ffn_norm, att_w_qkv, att_q_gain, att_k_gain, att_rel_bias, att_w_o, pool_w, pool_scale, gdn_w_in, gdn_conv, gdn_a_log, gdn_dt_bias, gdn_o_gain, gdn_w_o, ffn_w_up, ffn_conv, ffn_w_down, loss_target, m_mix_norm, m_ffn_norm, m_att_w_qkv, m_att_q_gain, m_att_k_gain, m_att_rel_bias, m_att_w_o, m_pool_w, m_pool_scale, m_gdn_w_in, m_gdn_conv, m_gdn_a_log, m_gdn_dt_bias, m_gdn_o_gain, m_gdn_w_o, m_ffn_w_up, m_ffn_conv, m_ffn_w_down, v_mix_norm, v_ffn_norm, v_att_w_qkv, v_att_q_gain, v_att_k_gain, v_att_rel_bias, v_att_w_o, v_pool_w, v_pool_scale, v_gdn_w_in, v_gdn_conv, v_gdn_a_log, v_gdn_dt_bias, v_gdn_o_gain, v_gdn_w_o, v_ffn_w_up, v_ffn_conv, v_ffn_w_down):
    w = dict(mix_norm=mix_norm, ffn_norm=ffn_norm, att_w_qkv=att_w_qkv, att_q_gain=att_q_gain, att_k_gain=att_k_gain, att_rel_bias=att_rel_bias, att_w_o=att_w_o, pool_w=pool_w, pool_scale=pool_scale, gdn_w_in=gdn_w_in, gdn_conv=gdn_conv, gdn_a_log=gdn_a_log, gdn_dt_bias=gdn_dt_bias, gdn_o_gain=gdn_o_gain, gdn_w_o=gdn_w_o, ffn_w_up=ffn_w_up, ffn_conv=ffn_conv, ffn_w_down=ffn_w_down)
    m = dict(mix_norm=m_mix_norm, ffn_norm=m_ffn_norm, att_w_qkv=m_att_w_qkv, att_q_gain=m_att_q_gain, att_k_gain=m_att_k_gain, att_rel_bias=m_att_rel_bias, att_w_o=m_att_w_o, pool_w=m_pool_w, pool_scale=m_pool_scale, gdn_w_in=m_gdn_w_in, gdn_conv=m_gdn_conv, gdn_a_log=m_gdn_a_log, gdn_dt_bias=m_gdn_dt_bias, gdn_o_gain=m_gdn_o_gain, gdn_w_o=m_gdn_w_o, ffn_w_up=m_ffn_w_up, ffn_conv=m_ffn_conv, ffn_w_down=m_ffn_w_down)
    v = dict(mix_norm=v_mix_norm, ffn_norm=v_ffn_norm, att_w_qkv=v_att_w_qkv, att_q_gain=v_att_q_gain, att_k_gain=v_att_k_gain, att_rel_bias=v_att_rel_bias, att_w_o=v_att_w_o, pool_w=v_pool_w, pool_scale=v_pool_scale, gdn_w_in=v_gdn_w_in, gdn_conv=v_gdn_conv, gdn_a_log=v_gdn_a_log, gdn_dt_bias=v_gdn_dt_bias, gdn_o_gain=v_gdn_o_gain, gdn_w_o=v_gdn_w_o, ffn_w_up=v_ffn_w_up, ffn_conv=v_ffn_conv, ffn_w_down=v_ffn_w_down)
    depth = ffn_w_up.shape[0]
    my_c = lax.axis_index("c").astype(jnp.int32)
    my_chip = (2 * lax.axis_index("x") + lax.axis_index("y")).astype(jnp.int32)
    groups = _layer_groups(depth)

    place = jnp.stack([my_chip, my_c])

    full = {n: w[n] for n in REPLICATED}
    slots = {(n, j): prep_slot(to_comm(n, w[n][j]), my_chip, f"prep_{n}_{j}") for group in groups for n, j in group}
    small = [w[n] for n in SMALL_SHARDED]
    got = gather_chips(_pack(small, F32, LANES, PACK_ROWS), "gather_small").reshape(N_CHIPS, -1)
    for n, t in zip(SMALL_SHARDED, _unpack(got, [s.shape for s in small])):
        full[n] = _merge(t, SHARD_AXIS[n])

    loss, grad_x, summed, G = local_step(x[0], loss_target[0], full, slots, place)
    loss = lax.psum(loss[0, 0], ("x", "y", "c"))
    gfull = small_grads(G, full)

    grads = {n: jnp.stack([from_comm(n, summed[n, j], w[n][j].shape) for j in range(w[n].shape[0])]) for n in BIG}

    small_names = REPLICATED + SMALL_SHARDED
    packed = _pack([gfull[n] for n in small_names], F32, LANES, SUBLANES)
    summed = sum_devices(packed, "sum_small").reshape(-1)
    for n, t in zip(small_names, _unpack(summed, [gfull[n].shape for n in small_names])):
        if n in SHARD_AXIS:
            size = w[n].shape[SHARD_AXIS[n]]
            t = lax.dynamic_slice_in_dim(t, my_chip * size, size, axis=SHARD_AXIS[n])
        grads[n] = t

    delta, new_m, new_v = {}, {}, {}
    for n in WEIGHTS:
        delta[n], new_m[n], new_v[n] = adamw(w[n], grads[n], m[n], v[n], f"adamw_{n}")
    return (loss, grad_x[None], *[grads[n] for n in WEIGHTS], *[delta[n] for n in WEIGHTS],
            *[new_m[n] for n in WEIGHTS], *[new_v[n] for n in WEIGHTS])
```

```python
import functools

import jax
import jax.numpy as jnp
from jax import lax
from jax.experimental import pallas as pl
from jax.experimental.pallas import tpu as pltpu

F32 = jnp.float32
BF16 = jnp.bfloat16
MXU_DTYPE = BF16
HI = lax.Precision.HIGHEST
S = jax.ShapeDtypeStruct
BS = pl.BlockSpec

EPS = 1e-6
MASK_VALUE = -1e30
CHUNK = 64
HEAD = 128
LEFT_CHUNKS = 8
BAND_LEFT = LEFT_CHUNKS * CHUNK
BAND = BAND_LEFT + CHUNK
MAX_REL = 256
ATT_CHUNKS = 4
ATT_QB = ATT_CHUNKS * CHUNK
ATT_BAND = BAND_LEFT + ATT_QB
POOL_WINDOWS = (2, 4, 8, 16)
POOL_HALO = 16
GDN_CONV = 4
DELTA_HEADS = 8
FFN_CONV = 3
SUBLANES = 8
LANES = 128
FF_ALIGN = 512
N_CHIPS = 4
ADAM_LR, ADAM_B1, ADAM_B2, ADAM_EPS, ADAM_WD, ADAM_STEP = 0.001, 0.9, 0.999, 1e-08, 0.01, 10
MIB = 1024 * 1024
MESH = pl.DeviceIdType.MESH


def _cp(sems, vmem_mib=40):
    return pltpu.CompilerParams(dimension_semantics=sems, vmem_limit_bytes=vmem_mib * MIB)


def _pick(n, cands):
    for c in cands:
        if n % c == 0:
            return c
    return n


def _mx(x):
    return x.astype(MXU_DTYPE)


def _hi_lo(x):
    hi = x.astype(MXU_DTYPE)
    return hi, (x - hi.astype(F32)).astype(MXU_DTYPE)


def _dot(a, b, dims, hi=False):
    dn = (dims, ((), ()))
    if hi is True or (hi == "split" and MXU_DTYPE == F32):
        return lax.dot_general(a.astype(F32), b.astype(F32), dn, precision=HI, preferred_element_type=F32)
    if hi == "split":
        ah, al = _hi_lo(a)
        bh, bl = _hi_lo(b)
        return (lax.dot_general(ah, bh, dn, preferred_element_type=F32)
                + (lax.dot_general(ah, bl, dn, preferred_element_type=F32)
                   + lax.dot_general(al, bh, dn, preferred_element_type=F32)))
    return lax.dot_general(_mx(a), _mx(b), dn, preferred_element_type=F32)


def dot_nn(a, b, hi=False):
    return _dot(a, b, ((1,), (0,)), hi)


def dot_nt(a, b, hi=False):
    return _dot(a, b, ((1,), (1,)), hi)


def dot_tn(a, b, hi=False):
    return _dot(a, b, ((0,), (0,)), hi)


def _sigmoid(x):
    return 0.5 * jnp.tanh(0.5 * x) + 0.5


def matmul(a, b, mode, *, out_dtype=F32, res=None, job=None, name):
    if mode == "nn":
        (M, K), N = a.shape, b.shape[1]
    elif mode == "nt":
        (M, K), N = a.shape, b.shape[0]
    else:
        (K, M), N = a.shape, b.shape[1]
    bm = _pick(M, (1024, 1408, 512, 256, 128))
    bn = _pick(N, (1024, 1408, 512, 256, 128))
    bk = _pick(K, (2816, 2048, 1408, 1024, 512, 256, 128))
    nk = K // bk
    if mode == "nn":
        a_spec = BS((bm, bk), lambda i, j, k: (i, k))
        b_spec = BS((bk, bn), lambda i, j, k: (k, j))
        dot = dot_nn
    elif mode == "nt":
        a_spec = BS((bm, bk), lambda i, j, k: (i, k))
        b_spec = BS((bn, bk), lambda i, j, k: (j, k))
        dot = dot_nt
    else:
        a_spec = BS((bk, bm), lambda i, j, k: (k, i))
        b_spec = BS((bk, bn), lambda i, j, k: (k, j))
        dot = dot_tn
    o_spec = BS((bm, bn), lambda i, j, k: (i, j))
    has_res = res is not None
    n_in = 3 if has_res else 2
    nji = len(job["ins"]) if job else 0
    njo = len(job["out_shapes"]) if job else 0
    grid = (M // bm, N // bn, nk)

    def body(*refs):
        a_ref, b_ref = refs[:2]
        r_ref = refs[2] if has_res else None
        o_ref = refs[n_in + nji]
        acc = refs[n_in + nji + 1 + njo]
        k = pl.program_id(2)
        if job:
            j_ins, j_outs = refs[n_in:n_in + nji], refs[n_in + nji + 1:n_in + nji + 1 + njo]
            sems = refs[n_in + nji + 2 + njo:]
            at_step = lambda s: jnp.logical_and(jnp.logical_and(pl.program_id(0) == s[0], pl.program_id(1) == s[1]),
                                                k == s[2])
            pl.when(at_step((0, 0, 0)))(lambda: job["start"](j_ins, j_outs, *sems))
        p = dot(a_ref[...], b_ref[...])

        def finish(total):
            if has_res:
                total = r_ref[...] + total
            o_ref[...] = total.astype(o_ref.dtype)

        if nk == 1:
            finish(p)
        else:
            @pl.when(k == 0)
            def _():
                acc[...] = p

            @pl.when(jnp.logical_and(k > 0, k < nk - 1))
            def _():
                acc[...] += p

            @pl.when(k == nk - 1)
            def _():
                finish(acc[...] + p)

        if job:
            pl.when(at_step((grid[0] - 1, grid[1] - 1, nk - 1)))(lambda: job["finish"](j_ins, j_outs, *sems))

    in_specs = [a_spec, b_spec] + ([o_spec] if has_res else [])
    args = (a, b) + ((res,) if has_res else ())
    if not job:
        return pl.pallas_call(
            body, grid=grid, in_specs=in_specs, out_specs=o_spec,
            out_shape=S((M, N), out_dtype), scratch_shapes=[pltpu.VMEM((bm, bn), F32)],
            compiler_params=_cp(("parallel", "parallel", "arbitrary"), 48), name=name)(*args)
    any_spec = BS(memory_space=pl.ANY)
    outs = pl.pallas_call(
        body, grid=grid, in_specs=in_specs + [any_spec] * nji, out_specs=[o_spec] + [any_spec] * njo,
        out_shape=[S((M, N), out_dtype)] + list(job["out_shapes"]),
        input_output_aliases={n_in + src: 1 + dst for src, dst in job["aliases"]},
        scratch_shapes=[pltpu.VMEM((bm, bn), F32), pltpu.SemaphoreType.DMA((job["nsem"],)),
                        pltpu.SemaphoreType.DMA((job["nsem"],))],
        compiler_params=_cp(("arbitrary", "arbitrary", "arbitrary"), 48), name=name)(*args, *job["ins"])
    return outs[0], list(outs[1:])


def norm_fwd(x, gain, out_dtypes, name):
    T, D = x.shape
    bt = _pick(T, (256, 128, 64))

    def body(x_ref, g_ref, *o_refs):
        xv = x_ref[...]
        r = lax.rsqrt(jnp.mean(xv * xv, axis=-1, keepdims=True) + EPS)
        y = (xv * r) * g_ref[...]
        for o in o_refs:
            o[...] = y.astype(o.dtype)

    row = BS((bt, D), lambda i: (i, 0))
    return pl.pallas_call(
        body, grid=(T // bt,), in_specs=[row, BS((1, D), lambda i: (0, 0))],
        out_specs=[row] * len(out_dtypes), out_shape=[S((T, D), dt) for dt in out_dtypes],
        compiler_params=_cp(("parallel",)), name=name)(x, gain)


def norm_bwd(x, gain, dy, dres, name):
    T, D = x.shape
    bt = _pick(T, (256, 128, 64))

    def body(x_ref, g_ref, dy_ref, dres_ref, dx_ref, dxm_ref, dg_ref):
        i = pl.program_id(0)
        xv = x_ref[...]
        dyv = dy_ref[...].astype(F32)
        r = lax.rsqrt(jnp.mean(xv * xv, axis=-1, keepdims=True) + EPS)
        xhat = xv * r
        dxhat = dyv * g_ref[...]
        dx = dres_ref[...] + r * (dxhat - xhat * jnp.mean(dxhat * xhat, axis=-1, keepdims=True))
        dx_ref[...] = dx
        dxm_ref[...] = dx.astype(dxm_ref.dtype)

        @pl.when(i == 0)
        def _():
            dg_ref[...] = jnp.zeros_like(dg_ref)

        dg_ref[...] += jnp.sum(dyv * xhat, axis=0, keepdims=True)

    row = BS((bt, D), lambda i: (i, 0))
    vec = BS((1, D), lambda i: (0, 0))
    dx, dxm, dg = pl.pallas_call(
        body, grid=(T // bt,), in_specs=[row, vec, row, row], out_specs=[row, row, vec],
        out_shape=[S((T, D), F32), S((T, D), MXU_DTYPE), S((1, D), F32)],
        compiler_params=_cp(("arbitrary",)), name=name)(x, gain, dy, dres)
    return (dx, dxm), dg


def loss_and_grad(y, target, name):
    T, D = y.shape
    bt = _pick(T, (256, 128, 64))
    nt = T // bt

    def body(y_ref, t_ref, l_ref, dy_ref, dym_ref, acc):
        i = pl.program_id(0)
        e = y_ref[...] - t_ref[...]
        dy_ref[...] = e * (1.0 / D)
        dym_ref[...] = (e * (1.0 / D)).astype(dym_ref.dtype)

        @pl.when(i == 0)
        def _():
            acc[...] = jnp.zeros_like(acc)

        acc[...] += jnp.sum(e * e, axis=0, keepdims=True)

        @pl.when(i == nt - 1)
        def _():
            l_ref[...] = jnp.sum(acc[...], axis=1, keepdims=True) * (0.5 / D)

    row = BS((bt, D), lambda i: (i, 0))
    loss, dy, dym = pl.pallas_call(
        body, grid=(nt,), in_specs=[row, row], out_specs=[BS((1, 1), lambda i: (0, 0)), row, row],
        out_shape=[S((1, 1), F32), S((T, D), F32), S((T, D), MXU_DTYPE)], scratch_shapes=[pltpu.VMEM((1, D), F32)],
        compiler_params=_cp(("arbitrary",)), name=name)(y, target)
    return loss, (dy, dym)


FFN_HALO = 16


def _prev_halo(bt):
    return lambda i: (jnp.maximum(i * (bt // FFN_HALO) - 1, 0), 0)


FFN_STRIP = 128


def _strips(width):
    return [pl.ds(c0, FFN_STRIP) for c0 in range(0, width, FFN_STRIP)]


def _ffn_u(i, a_ref, halo_ref, w_ref, ext, bt, cols):
    ext[pl.ds(0, FFN_HALO), cols] = jnp.where(i > 0, halo_ref[:, cols].astype(F32), 0.0)
    ext[pl.ds(FFN_HALO, bt), cols] = a_ref[:, cols].astype(F32)
    u = w_ref[2:3, cols] * ext[pl.ds(FFN_HALO, bt), cols]
    u += w_ref[1:2, cols] * ext[pl.ds(FFN_HALO - 1, bt), cols]
    u += w_ref[0:1, cols] * ext[pl.ds(FFN_HALO - 2, bt), cols]
    return u


def ffn_act_fwd(up, conv_w, name):
    T, F2 = up.shape
    Fp = F2 // 2
    bt = _pick(T, (128, 64))

    def body(a_ref, b_ref, halo_ref, w_ref, o_ref, ext):
        for cols in _strips(Fp):
            u = _ffn_u(pl.program_id(0), a_ref, halo_ref, w_ref, ext, bt, cols)
            o_ref[:, cols] = (u * _sigmoid(u) * b_ref[:, cols]).astype(o_ref.dtype)

    return pl.pallas_call(
        body, grid=(T // bt,),
        in_specs=[BS((bt, Fp), lambda i: (i, 0)), BS((bt, Fp), lambda i: (i, 1)),
                  BS((FFN_HALO, Fp), _prev_halo(bt)), BS((SUBLANES, Fp), lambda i: (0, 0))],
        out_specs=BS((bt, Fp), lambda i: (i, 0)), out_shape=S((T, Fp), MXU_DTYPE),
        scratch_shapes=[pltpu.VMEM((bt + FFN_HALO, Fp), F32)],
        compiler_params=_cp(("arbitrary",)), name=name)(up, up, up, conv_w)


def ffn_act_bwd_a(up, conv_w, g_act, name):
    T, F2 = up.shape
    Fp = F2 // 2
    bt = _pick(T, (128, 64))

    def body(a_ref, b_ref, halo_ref, w_ref, g_ref, du_ref, db_ref, dw_ref, ext):
        i = pl.program_id(0)

        @pl.when(i == 0)
        def _():
            dw_ref[...] = jnp.zeros_like(dw_ref)

        for cols in _strips(Fp):
            u = _ffn_u(i, a_ref, halo_ref, w_ref, ext, bt, cols)
            sg = _sigmoid(u)
            g = g_ref[:, cols].astype(F32)
            db_ref[:, cols] = (g * (u * sg)).astype(db_ref.dtype)
            du = g * b_ref[:, cols] * (sg * (1.0 + u * (1.0 - sg)))
            du_ref[:, cols] = du
            for j in range(FFN_CONV):
                shifted = ext[pl.ds(FFN_HALO - (FFN_CONV - 1) + j, bt), cols]
                dw_ref[j:j + 1, cols] += jnp.sum(du * shifted, axis=0, keepdims=True)

    blk = BS((bt, Fp), lambda i: (i, 0))
    full = BS((SUBLANES, Fp), lambda i: (0, 0))
    return pl.pallas_call(
        body, grid=(T // bt,),
        in_specs=[blk, BS((bt, Fp), lambda i: (i, 1)), BS((FFN_HALO, Fp), _prev_halo(bt)), full, blk],
        out_specs=[blk, blk, full],
        out_shape=[S((T, Fp), F32), S((T, Fp), MXU_DTYPE), S((SUBLANES, Fp), F32)],
        scratch_shapes=[pltpu.VMEM((bt + FFN_HALO, Fp), F32)],
        compiler_params=_cp(("arbitrary",)), name=name)(up, up, up, conv_w, g_act)


def ffn_act_bwd_b(du, db, conv_w, name):
    T, Fp = du.shape
    bt = _pick(T, (128, 64))
    nt = T // bt

    def body(du_ref, halo_ref, db_ref, w_ref, o_ref, ext):
        i = pl.program_id(0)
        for cols in _strips(Fp):
            ext[pl.ds(0, bt), cols] = du_ref[:, cols]
            ext[pl.ds(bt, SUBLANES), cols] = jnp.where(i < nt - 1, halo_ref[:, cols], 0.0)
            da = w_ref[2:3, cols] * ext[pl.ds(0, bt), cols]
            da += w_ref[1:2, cols] * ext[pl.ds(1, bt), cols]
            da += w_ref[0:1, cols] * ext[pl.ds(2, bt), cols]
            o_ref[:, cols] = da.astype(o_ref.dtype)
        o_ref[:, pl.ds(Fp, Fp)] = db_ref[...]

    blk = BS((bt, Fp), lambda i: (i, 0))
    nxt = BS((SUBLANES, Fp), lambda i: (jnp.minimum((i + 1) * (bt // SUBLANES), T // SUBLANES - 1), 0))
    return pl.pallas_call(
        body, grid=(nt,), in_specs=[blk, nxt, blk, BS((SUBLANES, Fp), lambda i: (0, 0))],
        out_specs=BS((bt, 2 * Fp), lambda i: (i, 0)), out_shape=S((T, 2 * Fp), MXU_DTYPE),
        scratch_shapes=[pltpu.VMEM((bt + SUBLANES, Fp), F32)],
        compiler_params=_cp(("arbitrary",)), name=name)(du, du, db, conv_w)


def _attn_fill(k_ref, v_ref, gk, kn_scr, vb_scr, T):
    kn_scr[pl.ds(0, BAND_LEFT), :] = jnp.zeros((BAND_LEFT, HEAD), kn_scr.dtype)
    vb_scr[pl.ds(0, BAND_LEFT), :] = jnp.zeros((BAND_LEFT, HEAD), vb_scr.dtype)
    rb = 512

    def fill(r, carry):
        rows = pl.ds(pl.multiple_of(r * rb, rb), rb)
        dst = pl.ds(pl.multiple_of(BAND_LEFT + r * rb, rb), rb)
        k = k_ref[rows, :]
        rk = lax.rsqrt(jnp.mean(k * k, axis=-1, keepdims=True) + EPS)
        kn_scr[dst, :] = ((k * rk) * gk).astype(kn_scr.dtype)
        vb_scr[dst, :] = v_ref[rows, :].astype(vb_scr.dtype)
        return carry

    lax.fori_loop(0, T // rb, fill, 0)


def _attn_probs(c, q_ref, gq, bias_ref, kn_scr):
    q = q_ref[...]
    rq = lax.rsqrt(jnp.mean(q * q, axis=-1, keepdims=True) + EPS)
    qn = (q * rq) * gq
    band = pl.ds(pl.multiple_of(c * ATT_QB, ATT_QB), ATT_BAND)
    kb = kn_scr[band, :]
    s = dot_nt(qn, kb) * (HEAD ** -0.5) + bias_ref[0]
    pos = c * ATT_QB - BAND_LEFT + lax.broadcasted_iota(jnp.int32, (ATT_QB, ATT_BAND), 1)
    s = jnp.where(pos >= 0, s, MASK_VALUE)
    m = jnp.max(s, axis=-1, keepdims=True)
    e = jnp.exp(s - m)
    p = e * (1.0 / jnp.sum(e, axis=-1, keepdims=True))
    return q, rq, qn, kb, p


def attn_fwd(qkv, gq, gk, bias, name):
    T, D3 = qkv.shape
    D = D3 // 3
    H = D // HEAD
    NC = T // ATT_QB

    def body(q_ref, k_ref, v_ref, gq_ref, gk_ref, bias_ref, o_ref, kn_scr, vb_scr):
        c = pl.program_id(1)

        @pl.when(c == 0)
        def _():
            _attn_fill(k_ref, v_ref, gk_ref[...], kn_scr, vb_scr, T)

        _, _, _, _, p = _attn_probs(c, q_ref, gq_ref[...], bias_ref, kn_scr)
        band = pl.ds(pl.multiple_of(c * ATT_QB, ATT_QB), ATT_BAND)
        o_ref[...] = dot_nn(p, vb_scr[band, :]).astype(o_ref.dtype)

    vec = BS((1, HEAD), lambda h, c: (0, 0))
    return pl.pallas_call(
        body, grid=(H, NC),
        in_specs=[BS((ATT_QB, HEAD), lambda h, c: (c, h)), BS((T, HEAD), lambda h, c: (0, H + h)),
                  BS((T, HEAD), lambda h, c: (0, 2 * H + h)), vec, vec,
                  BS((1, ATT_QB, ATT_BAND), lambda h, c: (h, 0, 0))],
        out_specs=BS((ATT_QB, HEAD), lambda h, c: (c, h)), out_shape=S((T, D), MXU_DTYPE),
        scratch_shapes=[pltpu.VMEM((T + BAND_LEFT, HEAD), MXU_DTYPE)] * 2,
        compiler_params=_cp(("arbitrary", "arbitrary"), 48), name=name)(qkv, qkv, qkv, gq, gk, bias)


def attn_bwd(qkv, do, gq, gk, bias, name):
    T, D3 = qkv.shape
    D = D3 // 3
    H = D // HEAD
    NC = T // ATT_QB
    scale = HEAD ** -0.5

    def body(q_ref, k_ref, v_ref, do_ref, gq_ref, gk_ref, bias_ref,
             dq_ref, dk_ref, dv_ref, dgq_ref, dgk_ref, dbias_ref, kn_scr, vb_scr, dkn_acc, dv_acc):
        h = pl.program_id(0)
        c = pl.program_id(1)
        gq = gq_ref[...]
        gk = gk_ref[...]

        @pl.when(c == 0)
        def _():
            _attn_fill(k_ref, v_ref, gk, kn_scr, vb_scr, T)
            dkn_acc[...] = jnp.zeros_like(dkn_acc)
            dv_acc[...] = jnp.zeros_like(dv_acc)
            dbias_ref[...] = jnp.zeros_like(dbias_ref)

        @pl.when(jnp.logical_and(c == 0, h == 0))
        def _():
            dgq_ref[...] = jnp.zeros_like(dgq_ref)
            dgk_ref[...] = jnp.zeros_like(dgk_ref)

        q, rq, qn, kb, p = _attn_probs(c, q_ref, gq, bias_ref, kn_scr)
        band = pl.ds(pl.multiple_of(c * ATT_QB, ATT_QB), ATT_BAND)
        dov = do_ref[...]
        dv_acc[band, :] += dot_tn(p, dov)
        dp = dot_nt(dov, vb_scr[band, :])
        ds = p * (dp - jnp.sum(dp * p, axis=-1, keepdims=True))
        dbias_ref[0] += ds
        dss = ds * scale
        dqn = dot_nn(dss, kb)
        dkn_acc[band, :] += dot_tn(dss, qn)
        xhat = q * rq
        dgq_ref[...] += jnp.sum(dqn * xhat, axis=0, keepdims=True)
        dxhat = dqn * gq
        dq = rq * (dxhat - xhat * jnp.mean(dxhat * xhat, axis=-1, keepdims=True))
        dq_ref[...] = dq.astype(dq_ref.dtype)

        @pl.when(c == NC - 1)
        def _():
            rb = 512

            def fin(r, carry):
                rows = pl.ds(pl.multiple_of(r * rb, rb), rb)
                src = pl.ds(pl.multiple_of(BAND_LEFT + r * rb, rb), rb)
                k = k_ref[rows, :]
                rk = lax.rsqrt(jnp.mean(k * k, axis=-1, keepdims=True) + EPS)
                khat = k * rk
                dkn = dkn_acc[src, :]
                dgk_ref[...] += jnp.sum(dkn * khat, axis=0, keepdims=True)
                dkh = dkn * gk
                dk = rk * (dkh - khat * jnp.mean(dkh * khat, axis=-1, keepdims=True))
                dk_ref[rows, :] = dk.astype(dk_ref.dtype)
                dv_ref[rows, :] = dv_acc[src, :].astype(dv_ref.dtype)
                return carry

            lax.fori_loop(0, T // rb, fin, 0)

    vec = BS((1, HEAD), lambda h, c: (0, 0))
    qblk = BS((ATT_QB, HEAD), lambda h, c: (c, h))
    col = BS((T, HEAD), lambda h, c: (0, h))
    bblk = BS((1, ATT_QB, ATT_BAND), lambda h, c: (h, 0, 0))
    return pl.pallas_call(
        body, grid=(H, NC),
        in_specs=[qblk, BS((T, HEAD), lambda h, c: (0, H + h)), BS((T, HEAD), lambda h, c: (0, 2 * H + h)),
                  qblk, vec, vec, bblk],
        out_specs=[qblk, col, col, vec, vec, bblk],
        out_shape=[S((T, D), MXU_DTYPE)] * 3 + [S((1, HEAD), F32)] * 2 + [S((H, ATT_QB, ATT_BAND), F32)],
        scratch_shapes=[pltpu.VMEM((T + BAND_LEFT, HEAD), MXU_DTYPE)] * 2
        + [pltpu.VMEM((T + BAND_LEFT, HEAD), F32)] * 2,
        compiler_params=_cp(("arbitrary", "arbitrary"), 56), name=name)(qkv, qkv, qkv, do, gq, gk, bias)


def _rel_onehot(qi, num_rel):
    kk = lax.broadcasted_iota(jnp.int32, (BAND, num_rel), 0)
    rr = lax.broadcasted_iota(jnp.int32, (BAND, num_rel), 1)
    idx = jnp.clip(BAND_LEFT + qi - kk, -(CHUNK - 1), MAX_REL) + (CHUNK - 1)
    return (idx == rr).astype(F32)


def rel_bias_expand(table, name):
    H, num_rel = table.shape

    def body(t_ref, o_ref):
        for qi in range(CHUNK):
            o_ref[qi] = dot_nt(t_ref[...], _rel_onehot(qi, num_rel), hi=True)

    return pl.pallas_call(body, out_shape=S((CHUNK, H, BAND), F32), name=name,
                          compiler_params=pltpu.CompilerParams(vmem_limit_bytes=40 * MIB))(table)


def rel_bias_reduce(dbias_t, num_rel, name):
    H = dbias_t.shape[1]

    def body(d_ref, o_ref):
        acc = jnp.zeros((H, num_rel), F32)
        for qi in range(CHUNK):
            acc += dot_nn(d_ref[qi], _rel_onehot(qi, num_rel), hi=True)
        o_ref[...] = acc

    return pl.pallas_call(body, out_shape=S((H, num_rel), F32), name=name,
                          compiler_params=pltpu.CompilerParams(vmem_limit_bytes=40 * MIB))(dbias_t)


def pool_fwd(h, x, w, scale, name):
    T, D = h.shape
    G = len(POOL_WINDOWS)
    Dg = D // G
    bt = _pick(T, (256, 128, 64))

    def body(h_ref, halo_ref, x_ref, w_ref, s_ref, o_ref, p_ref, ext):
        i = pl.program_id(0)
        ext[pl.ds(0, POOL_HALO), :] = jnp.where(i > 0, halo_ref[...], 0.0)
        ext[pl.ds(POOL_HALO, bt), :] = h_ref[...]
        t = i * bt + lax.broadcasted_iota(jnp.int32, (bt, 1), 0)
        for g, win in enumerate(POOL_WINDOWS):
            cols = pl.ds(g * Dg, Dg)
            acc = ext[pl.ds(POOL_HALO, bt), cols]
            for j in range(1, win):
                acc += ext[pl.ds(POOL_HALO - j, bt), cols]
            count = jnp.minimum(t + 1, win).astype(F32)
            pooled = acc / count - h_ref[:, cols]
            p_ref[:, cols] = pooled.astype(p_ref.dtype)
            y = dot_nn(pooled, w_ref[g]) * s_ref[:, cols]
            o_ref[:, cols] = x_ref[:, cols] + y

    row = BS((bt, D), lambda i: (i, 0))
    return pl.pallas_call(
        body, grid=(T // bt,),
        in_specs=[row, BS((POOL_HALO, D), lambda i: (jnp.maximum(i * (bt // POOL_HALO) - 1, 0), 0)), row,
                  BS((G, Dg, Dg), lambda i: (0, 0, 0)), BS((1, D), lambda i: (0, 0))],
        out_specs=[row, row], out_shape=[S((T, D), F32), S((T, D), MXU_DTYPE)],
        scratch_shapes=[pltpu.VMEM((bt + POOL_HALO, D), F32)],
        compiler_params=_cp(("arbitrary",)), name=name)(h, h, x, w, scale)


def pool_bwd_a(dy, pooled, w, scale, name):
    T, D = dy.shape
    G = len(POOL_WINDOWS)
    Dg = D // G
    bt = _pick(T, (256, 128, 64))

    def body(dy_ref, p_ref, w_ref, s_ref, dp_ref, dw_ref, ds_ref):
        i = pl.program_id(0)

        @pl.when(i == 0)
        def _():
            dw_ref[...] = jnp.zeros_like(dw_ref)
            ds_ref[...] = jnp.zeros_like(ds_ref)

        for g in range(G):
            cols = pl.ds(g * Dg, Dg)
            pg = p_ref[:, cols]
            dyg = dy_ref[:, cols]
            ypre = dot_nn(pg, w_ref[g])
            ds_ref[:, cols] += jnp.sum(dyg * ypre, axis=0, keepdims=True)
            dys = dyg * s_ref[:, cols]
            dp_ref[:, cols] = dot_nt(dys, w_ref[g])
            dw_ref[g] += dot_tn(pg, dys)

    row = BS((bt, D), lambda i: (i, 0))
    wspec = BS((G, Dg, Dg), lambda i: (0, 0, 0))
    vec = BS((1, D), lambda i: (0, 0))
    return pl.pallas_call(
        body, grid=(T // bt,), in_specs=[row, row, wspec, vec], out_specs=[row, wspec, vec],
        out_shape=[S((T, D), F32), S((G, Dg, Dg), F32), S((1, D), F32)],
        compiler_params=_cp(("arbitrary",)), name=name)(dy, pooled, w, scale)


def pool_bwd_b(dpooled, name):
    T, D = dpooled.shape
    G = len(POOL_WINDOWS)
    Dg = D // G
    bt = _pick(T, (256, 128, 64))
    nt = T // bt

    def body(d_ref, halo_ref, o_ref, ext):
        i = pl.program_id(0)
        t = i * bt + lax.broadcasted_iota(jnp.int32, (bt, 1), 0)
        for g, win in enumerate(POOL_WINDOWS):
            cols = pl.ds(g * Dg, Dg)
            count = jnp.minimum(t + 1, win).astype(F32)
            ext[pl.ds(0, bt), cols] = d_ref[:, cols] / count
            ext[pl.ds(bt, POOL_HALO), cols] = jnp.where(i < nt - 1, halo_ref[:, cols] * (1.0 / win), 0.0)
            acc = ext[pl.ds(0, bt), cols]
            for j in range(1, win):
                acc += ext[pl.ds(j, bt), cols]
            o_ref[:, cols] = acc - d_ref[:, cols]

    row = BS((bt, D), lambda i: (i, 0))
    nxt = BS((POOL_HALO, D), lambda i: (jnp.minimum((i + 1) * (bt // POOL_HALO), T // POOL_HALO - 1), 0))
    return pl.pallas_call(
        body, grid=(nt,), in_specs=[row, nxt], out_specs=row, out_shape=S((T, D), F32),
        scratch_shapes=[pltpu.VMEM((bt + POOL_HALO, D), F32)],
        compiler_params=_cp(("arbitrary",)), name=name)(dpooled, dpooled)


def _gdn_u(i, x_ref, halo_ref, w_ref, ext, bt, cols):
    ext[pl.ds(0, SUBLANES), cols] = jnp.where(i > 0, halo_ref[:, cols], 0.0)
    ext[pl.ds(SUBLANES, bt), cols] = x_ref[:, cols]
    u = w_ref[3:4, cols] * ext[pl.ds(SUBLANES, bt), cols]
    for j in range(GDN_CONV - 1):
        u += w_ref[j:j + 1, cols] * ext[pl.ds(SUBLANES - (GDN_CONV - 1) + j, bt), cols]
    return u


def gdn_pre_fwd(proj, conv_w, key_dim, name):
    T = proj.shape[0]
    C = conv_w.shape[1]
    cb = min(1024, key_dim)
    nq, nqk, J = key_dim // cb, 2 * key_dim // cb, C // cb
    bt = _pick(T, (256, 128, 64))

    def body(x_ref, halo_ref, w_ref, o_ref, ext):
        i, j = pl.program_id(0), pl.program_id(1)

        def silu_strip(hh):
            u = _gdn_u(i, x_ref, halo_ref, w_ref, ext, bt, pl.ds(hh * HEAD, HEAD))
            return u * _sigmoid(u)

        @pl.when(j < nqk)
        def _():
            sc = jnp.where(j < nq, HEAD ** -0.5, 1.0)
            for hh in range(cb // HEAD):
                blk = silu_strip(hh)
                r = lax.rsqrt(jnp.sum(blk * blk, axis=-1, keepdims=True) + EPS)
                o_ref[:, pl.ds(hh * HEAD, HEAD)] = (blk * r) * sc

        @pl.when(j >= nqk)
        def _():
            for hh in range(cb // HEAD):
                o_ref[:, pl.ds(hh * HEAD, HEAD)] = silu_strip(hh)

    return pl.pallas_call(
        body, grid=(T // bt, J),
        in_specs=[BS((bt, cb), lambda i, j: (i, j)),
                  BS((SUBLANES, cb), lambda i, j: (jnp.maximum(i * (bt // SUBLANES) - 1, 0), j)),
                  BS((SUBLANES, cb), lambda i, j: (0, j))],
        out_specs=BS((bt, cb), lambda i, j: (i, j)), out_shape=S((T, C), F32),
        scratch_shapes=[pltpu.VMEM((bt + SUBLANES, cb), F32)],
        compiler_params=_cp(("arbitrary", "arbitrary")), name=name)(proj, proj, conv_w)


def gdn_pre_bwd_a(proj, conv_w, dq_v, dk_v, dv, key_dim, name):
    T = proj.shape[0]
    C = conv_w.shape[1]
    cb = min(1024, key_dim)
    nq, nqk, J = key_dim // cb, 2 * key_dim // cb, C // cb
    nv = J - nqk
    bt = _pick(T, (256, 128, 64))

    def body(x_ref, halo_ref, w_ref, dq_ref, dk_ref, dv_ref, du_ref, dw_ref, ext, ds_scr):
        j, i = pl.program_id(0), pl.program_id(1)

        @pl.when(i == 0)
        def _():
            dw_ref[...] = jnp.zeros_like(dw_ref)

        def strip(hh, ds_of):
            cols = pl.ds(hh * HEAD, HEAD)
            u = _gdn_u(i, x_ref, halo_ref, w_ref, ext, bt, cols)
            sg = _sigmoid(u)
            du = ds_of(u * sg) * (sg * (1.0 + u * (1.0 - sg)))
            du_ref[:, cols] = du
            for k in range(GDN_CONV):
                shifted = ext[pl.ds(SUBLANES - (GDN_CONV - 1) + k, bt), cols]
                dw_ref[k:k + 1, cols] += jnp.sum(du * shifted, axis=0, keepdims=True)

        @pl.when(j < nqk)
        def _():
            sc = jnp.where(j < nq, HEAD ** -0.5, 1.0)
            for hh in range(cb // HEAD):
                lo = 2 * hh * HEAD
                dq2 = dq_ref[:, lo:lo + HEAD] + dq_ref[:, lo + HEAD:lo + 2 * HEAD]
                dk2 = dk_ref[:, lo:lo + HEAD] + dk_ref[:, lo + HEAD:lo + 2 * HEAD]
                dn = jnp.where(j < nq, dq2, dk2)

                def through_norm(blk, dn=dn):
                    r = lax.rsqrt(jnp.sum(blk * blk, axis=-1, keepdims=True) + EPS)
                    shat = blk * r
                    return (sc * r) * (dn - shat * jnp.sum(dn * shat, axis=-1, keepdims=True))

                strip(hh, through_norm)

        @pl.when(j >= nqk)
        def _():
            for hh in range(cb // HEAD):
                strip(hh, lambda s, hh=hh: dv_ref[:, pl.ds(hh * HEAD, HEAD)])

    blk = BS((bt, cb), lambda j, i: (i, j))
    return pl.pallas_call(
        body, grid=(J, T // bt),
        in_specs=[blk, BS((SUBLANES, cb), lambda j, i: (jnp.maximum(i * (bt // SUBLANES) - 1, 0), j)),
                  BS((SUBLANES, cb), lambda j, i: (0, j)),
                  BS((bt, 2 * cb), lambda j, i: (i, jnp.minimum(j, nq - 1))),
                  BS((bt, 2 * cb), lambda j, i: (i, jnp.clip(j - nq, 0, nq - 1))),
                  BS((bt, cb), lambda j, i: (i, jnp.clip(j - nqk, 0, nv - 1)))],
        out_specs=[blk, BS((SUBLANES, cb), lambda j, i: (0, j))],
        out_shape=[S((T, C), F32), S((SUBLANES, C), F32)],
        scratch_shapes=[pltpu.VMEM((bt + SUBLANES, cb), F32), pltpu.VMEM((bt, cb), F32)],
        compiler_params=_cp(("arbitrary", "arbitrary")), name=name)(proj, proj, conv_w, dq_v, dk_v, dv)


def gdn_pre_bwd_b(du, dgate, conv_w, key_dim, name):
    T, C = du.shape
    V = dgate.shape[1]
    cb = min(1024, key_dim)
    J = C // cb
    J2 = (C + V) // cb
    bt = _pick(T, (256, 128, 64))
    nt = T // bt

    def body(du_ref, halo_ref, w_ref, dg_ref, o_ref, ext):
        i, j = pl.program_id(0), pl.program_id(1)

        @pl.when(j < J)
        def _():
            ext[pl.ds(0, bt), :] = du_ref[...]
            ext[pl.ds(bt, SUBLANES), :] = jnp.where(i < nt - 1, halo_ref[...], 0.0)
            da = w_ref[3:4, :] * ext[pl.ds(0, bt), :]
            for k in range(GDN_CONV - 1):
                da += w_ref[k:k + 1, :] * ext[pl.ds(GDN_CONV - 1 - k, bt), :]
            o_ref[...] = da.astype(o_ref.dtype)

        @pl.when(j >= J)
        def _():
            o_ref[...] = dg_ref[...]

    jc = lambda j: jnp.minimum(j, J - 1)
    return pl.pallas_call(
        body, grid=(nt, J2),
        in_specs=[BS((bt, cb), lambda i, j: (i, jc(j))),
                  BS((SUBLANES, cb), lambda i, j: (jnp.minimum((i + 1) * (bt // SUBLANES), T // SUBLANES - 1), jc(j))),
                  BS((SUBLANES, cb), lambda i, j: (0, jc(j))),
                  BS((bt, cb), lambda i, j: (i, jnp.maximum(j - J, 0)))],
        out_specs=BS((bt, cb), lambda i, j: (i, j)), out_shape=S((T, C + V), MXU_DTYPE),
        scratch_shapes=[pltpu.VMEM((bt + SUBLANES, cb), F32)],
        compiler_params=_cp(("arbitrary", "arbitrary")), name=name)(du, du, conv_w, dgate)


def _softplus(x):
    return jnp.maximum(x, 0.0) + jnp.log1p(jnp.exp(-jnp.abs(x)))


def gdn_gate_fwd(a, b, a_log, dt_bias, name):
    T, HV = a.shape
    bt = _pick(T, (1024, 512, 256, 128, 64))

    def body(a_ref, b_ref, al_ref, dt_ref, g_ref, be_ref):
        g_ref[...] = -jnp.exp(al_ref[...]) * _softplus(a_ref[...] + dt_ref[...])
        be_ref[...] = _sigmoid(b_ref[...])

    row = BS((bt, HV), lambda i: (i, 0))
    vec = BS((1, HV), lambda i: (0, 0))
    return pl.pallas_call(body, grid=(T // bt,), in_specs=[row, row, vec, vec], out_specs=[row, row],
                          out_shape=[S((T, HV), F32)] * 2, compiler_params=_cp(("parallel",)), name=name)(
                              a, b, a_log, dt_bias)


def gdn_gate_bwd(a, b, a_log, dt_bias, dg, dbeta, name):
    T, HV = a.shape
    bt = _pick(T, (1024, 512, 256, 128, 64))

    def body(a_ref, b_ref, al_ref, dt_ref, dg_ref, dbe_ref, da_ref, db_ref, dal_ref, ddt_ref):
        i = pl.program_id(0)
        x = a_ref[...] + dt_ref[...]
        ea = jnp.exp(al_ref[...])
        dgv = dg_ref[...]
        da = dgv * (-ea * _sigmoid(x))
        da_ref[...] = da
        be = _sigmoid(b_ref[...])
        db_ref[...] = dbe_ref[...] * be * (1.0 - be)

        @pl.when(i == 0)
        def _():
            dal_ref[...] = jnp.zeros_like(dal_ref)
            ddt_ref[...] = jnp.zeros_like(ddt_ref)

        dal_ref[...] += jnp.sum(dgv * (-ea * _softplus(x)), axis=0, keepdims=True)
        ddt_ref[...] += jnp.sum(da, axis=0, keepdims=True)

    row = BS((bt, HV), lambda i: (i, 0))
    vec = BS((1, HV), lambda i: (0, 0))
    return pl.pallas_call(body, grid=(T // bt,), in_specs=[row, row, vec, vec, row, row],
                          out_specs=[row, row, vec, vec],
                          out_shape=[S((T, HV), F32)] * 2 + [S((1, HV), F32)] * 2,
                          compiler_params=_cp(("arbitrary",)), name=name)(a, b, a_log, dt_bias, dg, dbeta)


def _col(row_vec, eye):
    return jnp.sum(jnp.where(eye, row_vec, 0.0), axis=1, keepdims=True)


def _row(col_vec, eye):
    return jnp.sum(jnp.where(eye, col_vec, 0.0), axis=0, keepdims=True)


def _each(f, *lists):
    return [f(*args) for args in zip(*lists)]


def _mul(a, b):
    return a * b


def _hdot_nn(a, b):
    return dot_nn(a, b, hi="split")


def _delta_chunk(q, k, v, g_row, b_row, rep, tinv=None):
    C = CHUNK
    ii = lax.broadcasted_iota(jnp.int32, (C, C), 0)
    jj = lax.broadcasted_iota(jnp.int32, (C, C), 1)
    eye, causal, strict = ii == jj, ii >= jj, ii > jj
    q_v = [q[p // rep] for p in range(len(v))]
    k_v = [k[p // rep] for p in range(len(v))]
    g_col = _each(lambda g: _col(g, eye), g_row)
    gc_row = _each(lambda g: jnp.sum(jnp.where(ii <= jj, g, 0.0), axis=0, keepdims=True), g_col)
    gc_col = _each(lambda g: _col(g, eye), gc_row)
    gl = _each(lambda g: jnp.sum(jnp.where(jj[0:1, :] == C - 1, g, 0.0), axis=1, keepdims=True), gc_row)
    decay = _each(lambda gc, gr: jnp.where(causal, jnp.exp(jnp.where(causal, gc - gr, 0.0)), 0.0), gc_col, gc_row)
    b_col = _each(lambda b: _col(b, eye), b_row)
    kb = _each(_mul, k_v, b_col)
    vb = _each(_mul, v, b_col)
    m = _each(dot_nt, kb, k_v)
    a = _each(lambda m_, d_: jnp.where(strict, m_ * d_, 0.0), m, decay)
    if tinv is None:
        ident = jnp.where(eye, 1.0, 0.0)
        tinv = [ident - a_ for a_ in a]
        pw = _each(_hdot_nn, a, a)
        for step in range(5):
            tinv = _each(lambda t_, p_: t_ + _hdot_nn(t_, p_), tinv, pw)
            if step < 4:
                pw = _each(_hdot_nn, pw, pw)
    egc = _each(jnp.exp, gc_col)
    kbg = _each(_mul, kb, egc)
    u = _each(_hdot_nn, tinv, vb)
    w = _each(_hdot_nn, tinv, kbg)
    n_k = _each(dot_nt, q, k)
    n = [n_k[p // rep] for p in range(len(v))]
    attn = _each(lambda n_, d_: jnp.where(causal, n_ * d_, 0.0), n, decay)
    qg = _each(_mul, q_v, egc)
    ekl = _each(lambda l_, c_: jnp.exp(l_ - c_), gl, gc_col)
    ks = _each(_mul, k_v, ekl)
    dec = _each(jnp.exp, gl)
    return dict(eye=eye, causal=causal, strict=strict, ii=ii, jj=jj, q=q_v, k=k_v, gc_col=gc_col, gl=gl, decay=decay,
                b_col=b_col, kb=kb, vb=vb, m=m, tinv=tinv, egc=egc, kbg=kbg, u=u, w=w, n=n, attn=attn,
                qg=qg, ekl=ekl, ks=ks, dec=dec)


def delta_fwd(qkvn, g_rows, b_rows, key_dim, name):
    T = qkvn.shape[0]
    NK = key_dim // HEAD
    HV = g_rows.shape[0]
    rep = HV // NK
    NC = T // CHUNK

    P = DELTA_HEADS
    kw, vw = HEAD * P // rep, HEAD * P

    def body(q_ref, k_ref, v_ref, g_ref, b_ref, o_ref, st_ref, ti_ref, state):
        n = pl.program_id(1)

        @pl.when(n == 0)
        def _():
            state[...] = jnp.zeros_like(state)

        heads = range(P)
        q = [q_ref[:, pl.ds(kh * HEAD, HEAD)] for kh in range(P // rep)]
        k = [k_ref[:, pl.ds(kh * HEAD, HEAD)] for kh in range(P // rep)]
        v = [v_ref[:, pl.ds(p * HEAD, HEAD)] for p in heads]
        s0 = [state[p] for p in heads]
        c = _delta_chunk(q, k, v, [g_ref[p, 0] for p in heads], [b_ref[p, 0] for p in heads], rep)
        vn = _each(lambda u_, w_, s_: u_ - dot_nn(w_, s_), c["u"], c["w"], s0)
        o = _each(lambda qg_, s_, at_, vn_: dot_nn(qg_, s_) + dot_nn(at_, vn_), c["qg"], s0, c["attn"], vn)
        s1 = _each(lambda s_, d_, ks_, vn_: s_ * d_ + dot_tn(ks_, vn_), s0, c["dec"], c["ks"], vn)
        for p in heads:
            st_ref[p, 0] = s0[p]
            ti_ref[p, 0] = c["tinv"][p]
            o_ref[:, pl.ds(p * HEAD, HEAD)] = o[p]
            state[p] = s1[p]

    vrow = BS((P, 1, 1, CHUNK), lambda h, n: (h, n, 0, 0))
    return pl.pallas_call(
        body, grid=(HV // P, NC),
        in_specs=[BS((CHUNK, kw), lambda h, n: (n, h)),
                  BS((CHUNK, kw), lambda h, n: (n, key_dim // kw + h)),
                  BS((CHUNK, vw), lambda h, n: (n, 2 * key_dim // vw + h)), vrow, vrow],
        out_specs=[BS((CHUNK, vw), lambda h, n: (n, h)), BS((P, 1, HEAD, HEAD), lambda h, n: (h, n, 0, 0)),
                   BS((P, 1, CHUNK, CHUNK), lambda h, n: (h, n, 0, 0))],
        out_shape=[S((T, HV * HEAD), F32), S((HV, NC, HEAD, HEAD), F32), S((HV, NC, CHUNK, CHUNK), F32)],
        scratch_shapes=[pltpu.VMEM((P, HEAD, HEAD), F32)],
        compiler_params=_cp(("arbitrary", "arbitrary")), name=name)(qkvn, qkvn, qkvn, g_rows, b_rows)


def delta_bwd(qkvn, g_rows, b_rows, states, tinvs, do, key_dim, name):
    T = qkvn.shape[0]
    NK = key_dim // HEAD
    HV = g_rows.shape[0]
    rep = HV // NK
    NC = T // CHUNK
    P = DELTA_HEADS
    kw, vw = HEAD * P // rep, HEAD * P

    def body(q_ref, k_ref, v_ref, g_ref, b_ref, st_ref, ti_ref, do_ref, dq_ref, dk_ref, dv_ref, dg_ref, dbe_ref,
             dstate):
        step = pl.program_id(1)

        @pl.when(step == 0)
        def _():
            dstate[...] = jnp.zeros_like(dstate)

        heads = range(P)
        q_k = [q_ref[:, pl.ds(kh * HEAD, HEAD)] for kh in range(P // rep)]
        k_k = [k_ref[:, pl.ds(kh * HEAD, HEAD)] for kh in range(P // rep)]
        v = [v_ref[:, pl.ds(p * HEAD, HEAD)] for p in heads]
        s0 = [st_ref[p, 0] for p in heads]
        dsn = [dstate[p] for p in heads]
        dov = [do_ref[:, pl.ds(p * HEAD, HEAD)] for p in heads]
        c = _delta_chunk(q_k, k_k, v, [g_ref[p, 0] for p in heads], [b_ref[p, 0] for p in heads], rep,
                         tinv=[ti_ref[p, 0] for p in heads])
        eye, causal, strict = c["eye"], c["causal"], c["strict"]
        q, k, tinv, decay = c["q"], c["k"], c["tinv"], c["decay"]

        def rsum(a, b):
            return jnp.sum(a * b, axis=1, keepdims=True)

        vn = _each(lambda u_, w_, s_: u_ - dot_nn(w_, s_), c["u"], c["w"], s0)
        dvn = _each(lambda at_, do_, ks_, ds_: dot_tn(at_, do_) + dot_nn(ks_, ds_), c["attn"], dov, c["ks"], dsn)
        dattn = _each(lambda do_, vn_: jnp.where(causal, dot_nt(do_, vn_), 0.0), dov, vn)
        dqg = _each(dot_nt, dov, s0)
        dks = _each(dot_nt, vn, dsn)
        ddec = _each(lambda s_, ds_: jnp.sum(rsum(s_, ds_), axis=0, keepdims=True), s0, dsn)
        dw = _each(lambda dvn_, s_: -dot_nt(dvn_, s_), dvn, s0)
        ds0 = _each(lambda qg_, do_, ds_, d_, w_, dvn_: dot_tn(qg_, do_) + ds_ * d_ - dot_tn(w_, dvn_),
                    c["qg"], dov, dsn, c["dec"], c["w"], dvn)
        dvb = _each(lambda t_, x_: dot_tn(t_, x_, hi="split"), tinv, dvn)
        dkbg = _each(lambda t_, x_: dot_tn(t_, x_, hi="split"), tinv, dw)
        dt = _each(lambda dvn_, vb_, dw_, kbg_: dot_nt(dvn_, vb_, hi="split") + dot_nt(dw_, kbg_, hi="split"),
                   dvn, c["vb"], dw, c["kbg"])
        dtt = _each(lambda dt_, t_: dot_nt(dt_, t_, hi="split"), dt, tinv)
        da = _each(lambda t_, x_: jnp.where(strict, -dot_tn(t_, x_, hi="split"), 0.0), tinv, dtt)
        dm = _each(_mul, da, decay)
        dn = _each(_mul, dattn, decay)
        e = _each(lambda da_, m_, dat_, n_, d_: (da_ * m_ + dat_ * n_) * d_, da, c["m"], dattn, c["n"], decay)
        dkb = _each(lambda dm_, k_, dkbg_, egc_: dot_nn(dm_, k_) + dkbg_ * egc_, dm, k, dkbg, c["egc"])
        dk = _each(lambda dm_, kb_, dn_, q_, dks_, ekl_, dkb_, b_: dot_tn(dm_, kb_) + dot_tn(dn_, q_) + dks_ * ekl_
                   + dkb_ * b_, dm, c["kb"], dn, q, dks, c["ekl"], dkb, c["b_col"])
        dq = _each(lambda dn_, k_, dqg_, egc_: dot_nn(dn_, k_) + dqg_ * egc_, dn, k, dqg, c["egc"])
        dks_ks = _each(rsum, dks, c["ks"])
        dgc_col = _each(lambda e_, dkbg_, kbg_, dqg_, qg_, x_: jnp.sum(e_, axis=1, keepdims=True) + rsum(dkbg_, kbg_)
                        + rsum(dqg_, qg_) - x_ - _col(jnp.sum(e_, axis=0, keepdims=True), eye),
                        e, dkbg, c["kbg"], dqg, c["qg"], dks_ks)
        dgl = _each(lambda x_, dd_, d_: jnp.sum(x_, axis=0, keepdims=True) + dd_ * d_, dks_ks, ddec, c["dec"])
        last = c["ii"][:, 0:1] == CHUNK - 1
        dgc_col = _each(lambda g_, l_: g_ + jnp.where(last, l_, 0.0), dgc_col, dgl)
        dbe_col = _each(lambda dvb_, v_, dkb_, k_: rsum(dvb_, v_) + rsum(dkb_, k_), dvb, v, dkb, k)
        for p in heads:
            vc = pl.ds(p * HEAD, HEAD)
            dstate[p] = ds0[p]
            dq_ref[:, vc] = dq[p]
            dk_ref[:, vc] = dk[p]
            dv_ref[:, vc] = dvb[p] * c["b_col"][p]
            dbe_ref[p, 0] = _row(dbe_col[p], eye)
            dg_ref[p, 0] = jnp.sum(jnp.where(causal, dgc_col[p], 0.0), axis=0, keepdims=True)

    rev = lambda n: NC - 1 - n
    vrow = BS((P, 1, 1, CHUNK), lambda h, n: (h, rev(n), 0, 0))
    vblk = BS((CHUNK, vw), lambda h, n: (rev(n), h))
    return pl.pallas_call(
        body, grid=(HV // P, NC),
        in_specs=[BS((CHUNK, kw), lambda h, n: (rev(n), h)),
                  BS((CHUNK, kw), lambda h, n: (rev(n), key_dim // kw + h)),
                  BS((CHUNK, vw), lambda h, n: (rev(n), 2 * key_dim // vw + h)), vrow, vrow,
                  BS((P, 1, HEAD, HEAD), lambda h, n: (h, rev(n), 0, 0)),
                  BS((P, 1, CHUNK, CHUNK), lambda h, n: (h, rev(n), 0, 0)), vblk],
        out_specs=[vblk, vblk, vblk, vrow, vrow],
        out_shape=[S((T, HV * HEAD), F32)] * 3 + [S((HV, NC, 1, CHUNK), F32)] * 2,
        scratch_shapes=[pltpu.VMEM((P, HEAD, HEAD), F32)],
        compiler_params=_cp(("arbitrary", "arbitrary")), name=name)(
            qkvn, qkvn, qkvn, g_rows, b_rows, states, tinvs, do)


def gdn_post_fwd(o, proj, gate_col0, o_gain, name):
    T, V = o.shape
    cb = min(1024, V)
    j0 = gate_col0 // cb
    bt = _pick(T, (256, 128, 64))

    def body(o_ref, g_ref, gain_ref, y_ref):
        for hh in range(cb // HEAD):
            cols = pl.ds(hh * HEAD, HEAD)
            ov = o_ref[:, cols]
            gt = g_ref[:, cols]
            r = lax.rsqrt(jnp.mean(ov * ov, axis=-1, keepdims=True) + EPS)
            y_ref[:, cols] = (((ov * r) * gain_ref[...]) * (gt * _sigmoid(gt))).astype(y_ref.dtype)

    return pl.pallas_call(
        body, grid=(T // bt, V // cb),
        in_specs=[BS((bt, cb), lambda i, j: (i, j)), BS((bt, cb), lambda i, j: (i, j0 + j)),
                  BS((1, HEAD), lambda i, j: (0, 0))],
        out_specs=BS((bt, cb), lambda i, j: (i, j)), out_shape=S((T, V), MXU_DTYPE),
        compiler_params=_cp(("parallel", "parallel")), name=name)(o, proj, o_gain)


def gdn_post_bwd(o, proj, gate_col0, o_gain, dy, name):
    T, V = o.shape
    cb = min(1024, V)
    j0 = gate_col0 // cb
    bt = _pick(T, (256, 128, 64))

    def body(o_ref, g_ref, gain_ref, dy_ref, do_ref, dgt_ref, dgain_ref):
        i, j = pl.program_id(0), pl.program_id(1)

        @pl.when(jnp.logical_and(i == 0, j == 0))
        def _():
            dgain_ref[...] = jnp.zeros_like(dgain_ref)

        gain = gain_ref[...]
        for hh in range(cb // HEAD):
            cols = pl.ds(hh * HEAD, HEAD)
            ov = o_ref[:, cols]
            gt = g_ref[:, cols]
            dyv = dy_ref[:, cols]
            r = lax.rsqrt(jnp.mean(ov * ov, axis=-1, keepdims=True) + EPS)
            ohat = ov * r
            sg = _sigmoid(gt)
            dyn = dyv * (gt * sg)
            dgt_ref[:, cols] = (dyv * (ohat * gain) * (sg * (1.0 + gt * (1.0 - sg)))).astype(dgt_ref.dtype)
            dgain_ref[...] += jnp.sum(dyn * ohat, axis=0, keepdims=True)
            dh = dyn * gain
            do_ref[:, cols] = r * (dh - ohat * jnp.mean(dh * ohat, axis=-1, keepdims=True))

    blk = BS((bt, cb), lambda i, j: (i, j))
    vec = BS((1, HEAD), lambda i, j: (0, 0))
    return pl.pallas_call(
        body, grid=(T // bt, V // cb),
        in_specs=[blk, BS((bt, cb), lambda i, j: (i, j0 + j)), vec, blk],
        out_specs=[blk, blk, vec], out_shape=[S((T, V), F32), S((T, V), MXU_DTYPE), S((1, HEAD), F32)],
        compiler_params=_cp(("arbitrary", "arbitrary")), name=name)(o, proj, o_gain, dy)


def adamw(w, g, m, v, name):
    shape = w.shape
    n = 1
    for d in shape:
        n *= d
    cols = shape[-1] if len(shape) > 1 else n
    rows = n // cols
    br = rows
    for cand in (512, 256, 128, 64, 32, 16, 8):
        if rows % cand == 0 and cand * cols * 4 * 14 <= 36 * MIB:
            br = cand
            break
    c1 = 1.0 - ADAM_B1 ** ADAM_STEP
    c2 = 1.0 - ADAM_B2 ** ADAM_STEP

    def body(w_ref, g_ref, m_ref, v_ref, d_ref, nm_ref, nv_ref):
        gv = g_ref[...]
        nm = ADAM_B1 * m_ref[...] + (1.0 - ADAM_B1) * gv
        nv = ADAM_B2 * v_ref[...] + (1.0 - ADAM_B2) * (gv * gv)
        nm_ref[...] = nm
        nv_ref[...] = nv
        d_ref[...] = -ADAM_LR * ((nm / c1) / (jnp.sqrt(nv / c2) + ADAM_EPS) + ADAM_WD * w_ref[...])

    blk = BS((br, cols), lambda i: (i, 0))
    outs = pl.pallas_call(
        body, grid=(rows // br,), in_specs=[blk] * 4, out_specs=[blk] * 3,
        out_shape=[S((rows, cols), F32)] * 3, compiler_params=_cp(("parallel",), 48), name=name)(
            *[t.reshape(rows, cols) for t in (w, g, m, v)])
    return [t.reshape(shape) for t in outs]


def _with_job(job, *args, **kwargs):
    if job is None:
        return matmul(*args, **kwargs), []
    return matmul(*args, job=job, **kwargs)


def _ffn_fwd(x, gain, p, tag, jobs=(None, None)):
    (h2,) = norm_fwd(x, gain, [MXU_DTYPE], f"{tag}_norm")
    up, got_a = _with_job(jobs[0], h2, p["w_up_t"], "nt", out_dtype=MXU_DTYPE, name=f"{tag}_up")
    act = ffn_act_fwd(up, p["conv"], f"{tag}_act")
    out, got_b = _with_job(jobs[1], act, p["w_down"], "nn", res=x, name=f"{tag}_down")
    return out, (x, h2, up, act), got_a + got_b


def _ffn_bwd(dx, saved, gain, p, tag, first_job=None, then_jobs=None):
    x, h2, up, act = saved
    g_act, got = _with_job(first_job, dx[1], p["w_down"], "nt", out_dtype=MXU_DTYPE, name=f"{tag}_bdown")
    jobs = then_jobs(got) if then_jobs else (None, None)
    d_down = matmul(act, dx[1], "tn", name=f"{tag}_wdown")
    du, db, d_conv = ffn_act_bwd_a(up, p["conv"], g_act, f"{tag}_bact_a")
    dup = ffn_act_bwd_b(du, db, p["conv"], f"{tag}_bact_b")
    dh2, got_a = _with_job(jobs[0], dup, p["w_up_t"], "nn", name=f"{tag}_bup")
    d_up_t, got_b = _with_job(jobs[1], dup, h2, "tn", name=f"{tag}_wup")
    dx, d_gain = norm_bwd(x, gain, dh2, dx[0], f"{tag}_bnorm")
    return dx, d_gain, dict(w_up_t=d_up_t, conv=d_conv, w_down=d_down), got_a + got_b


def _joint_bias(bias):
    rows = [jnp.pad(bias, ((0, 0), (0, 0), (qc * CHUNK, (ATT_CHUNKS - 1 - qc) * CHUNK)), constant_values=MASK_VALUE)
            for qc in range(ATT_CHUNKS)]
    return jnp.concatenate(rows, axis=1)


def _joint_bias_grad(dbias):
    return sum(dbias[:, qc * CHUNK:(qc + 1) * CHUNK, qc * CHUNK:qc * CHUNK + BAND] for qc in range(ATT_CHUNKS))


def _att_fwd(x, gain, p, tag):
    (h,) = norm_fwd(x, gain, [MXU_DTYPE], f"{tag}_norm")
    qkv = matmul(h, p["w_qkv_t"], "nt", name=f"{tag}_qkv")
    bias = _joint_bias(rel_bias_expand(p["rel_bias"], f"{tag}_rel").transpose(1, 0, 2))
    o = attn_fwd(qkv, p["q_gain"], p["k_gain"], bias, f"{tag}_core")
    out = matmul(o, p["w_o"], "nn", res=x, name=f"{tag}_out")
    return out, (x, h, qkv, o, bias)


def _att_bwd(dx, saved, gain, p, tag):
    x, h, qkv, o, bias = saved
    do = matmul(dx[1], p["w_o"], "nt", name=f"{tag}_bout")
    d_wo = matmul(o, dx[1], "tn", name=f"{tag}_wout")
    dq, dk, dv, d_gq, d_gk, dbias = attn_bwd(qkv, do, p["q_gain"], p["k_gain"], bias, f"{tag}_bcore")
    dqkv = jnp.concatenate([dq, dk, dv], axis=1)
    dh = matmul(dqkv, p["w_qkv_t"], "nn", name=f"{tag}_bqkv")
    d_wqkv_t = matmul(dqkv, h, "tn", name=f"{tag}_wqkv")
    d_rb = rel_bias_reduce(_joint_bias_grad(dbias).transpose(1, 0, 2), p["rel_bias"].shape[1], f"{tag}_brel")
    dx, d_gain = norm_bwd(x, gain, dh, dx[0], f"{tag}_bnorm")
    return dx, d_gain, dict(w_qkv_t=d_wqkv_t, w_o=d_wo, q_gain=d_gq, k_gain=d_gk, rel_bias=d_rb)


def _pool_fwd(x, gain, p, tag):
    (hf,) = norm_fwd(x, gain, [F32], f"{tag}_norm")
    out, pooled = pool_fwd(hf, x, p["w"], p["scale"], f"{tag}_core")
    return out, (x, pooled)


def _pool_bwd(dx, saved, gain, p, tag):
    x, pooled = saved
    dpooled, d_w, d_scale = pool_bwd_a(dx[0], pooled, p["w"], p["scale"], f"{tag}_bcore_a")
    dh = pool_bwd_b(dpooled, f"{tag}_bcore_b")
    dx, d_gain = norm_bwd(x, gain, dh, dx[0], f"{tag}_bnorm")
    return dx, d_gain, dict(w=d_w, scale=d_scale)


def _rows_layout(t, hv):
    return t.T.reshape(hv, t.shape[0] // CHUNK, 1, CHUNK)


def _gdn_fwd(x, gain, p, tag):
    T = x.shape[0]
    hv = p["a_log"].shape[1]
    key_dim = p["key_dim"]
    C = p["conv"].shape[1]
    (h,) = norm_fwd(x, gain, [MXU_DTYPE], f"{tag}_norm")
    proj = matmul(h, p["w_main_t"], "nt", name=f"{tag}_in")
    ab = matmul(h, p["w_ab_t"], "nt", name=f"{tag}_in_ab")
    a, b = ab[:, :hv], ab[:, hv:2 * hv]
    qkvn = gdn_pre_fwd(proj, p["conv"], key_dim, f"{tag}_pre")
    g, beta = gdn_gate_fwd(a, b, p["a_log"], p["dt_bias"], f"{tag}_gate")
    g_rows, b_rows = _rows_layout(g, hv), _rows_layout(beta, hv)
    o, states, tinvs = delta_fwd(qkvn, g_rows, b_rows, key_dim, f"{tag}_delta")
    y = gdn_post_fwd(o, proj, C, p["o_gain"], f"{tag}_post")
    out = matmul(y, p["w_o"], "nn", res=x, name=f"{tag}_out")
    return out, (x, h, proj, a, b, qkvn, g_rows, b_rows, o, states, tinvs, y)


def _gdn_bwd(dx, saved, gain, p, tag):
    x, h, proj, a, b, qkvn, g_rows, b_rows, o, states, tinvs, y = saved
    T = x.shape[0]
    hv = p["a_log"].shape[1]
    key_dim = p["key_dim"]
    C = p["conv"].shape[1]
    dy = matmul(dx[1], p["w_o"], "nt", name=f"{tag}_bout")
    d_wo = matmul(y, dx[1], "tn", name=f"{tag}_wout")
    do, dgate, d_ogain = gdn_post_bwd(o, proj, C, p["o_gain"], dy, f"{tag}_bpost")
    dq_v, dk_v, dv, dg_rows, dbe_rows = delta_bwd(qkvn, g_rows, b_rows, states, tinvs, do, key_dim,
                                                  f"{tag}_bdelta")
    dg = dg_rows.reshape(hv, T).T
    dbeta = dbe_rows.reshape(hv, T).T
    da, db, d_alog, d_dtb = gdn_gate_bwd(a, b, p["a_log"], p["dt_bias"], dg, dbeta, f"{tag}_bgate")
    du, d_conv = gdn_pre_bwd_a(proj, p["conv"], dq_v, dk_v, dv, key_dim, f"{tag}_bpre_a")
    dproj = gdn_pre_bwd_b(du, dgate, p["conv"], key_dim, f"{tag}_bpre_b")
    dab = jnp.concatenate([da, db, jnp.zeros((T, LANES - 2 * hv), F32)], axis=1)
    dh = matmul(dproj, p["w_main_t"], "nn", name=f"{tag}_bin")
    dh = matmul(dab, p["w_ab_t"], "nn", res=dh, name=f"{tag}_bin_ab")
    d_main_t = matmul(dproj, h, "tn", name=f"{tag}_win")
    d_ab_t = matmul(dab, h, "tn", name=f"{tag}_win_ab")
    dx, d_gain = norm_bwd(x, gain, dh, dx[0], f"{tag}_bnorm")
    return dx, d_gain, dict(w_main_t=d_main_t, w_ab_t=d_ab_t, conv=d_conv, a_log=d_alog, dt_bias=d_dtb,
                            o_gain=d_ogain, w_o=d_wo)


_MIXERS = ((_att_fwd, _att_bwd), (_pool_fwd, _pool_bwd), (_gdn_fwd, _gdn_bwd))


def local_step(x, target, full, slots, place):
    depth = full["ffn_conv"].shape[0]
    groups = _layer_groups(depth)
    core = place[1:2]
    big = dict(zip(groups[0], gather_slots([slots[k] for k in groups[0]], "gather_l0")))
    saved = []
    for i in range(depth):
        kind, j = i % 3, i // 3
        mp, fp = layer_weights(i, big, full)
        x, s_mix = _MIXERS[kind][0](x, full["mix_norm"][i:i + 1], mp, f"l{i}_mix")
        jobs, keys = (None, None), ()
        if i + 1 < depth:
            keys = groups[i + 1][0::2] + groups[i + 1][1::2]
            jobs = (gather_job([slots[k] for k in groups[i + 1][0::2]]),
                    gather_job([slots[k] for k in groups[i + 1][1::2]]))
        x, s_ffn, got = _ffn_fwd(x, full["ffn_norm"][i:i + 1], fp, f"l{i}_ffn", jobs)
        saved.append((s_mix, s_ffn, mp, fp))
        if keys:
            big = dict(zip(keys, gather_pass(got, f"gather_pass_l{i + 1}")))
    loss, dx = loss_and_grad(x, target, "loss")
    n_mix = [len([i for i in range(depth) if i % 3 == kind]) for kind in range(3)]
    G = dict(mix_norm=[None] * depth, ffn_norm=[None] * depth, ffn=[None] * depth,
             att=[None] * n_mix[0], pool=[None] * n_mix[1], gdn=[None] * n_mix[2])
    summed = {}

    def finish(keys, chip_sums, got, tag):
        mine = [add_cols4(p, b, place, f"rs_add4_{n}_{j}") for (n, j), p, b in zip(keys, chip_sums, got)]
        summed.update(zip(keys, join_cols(mine, f"rs_join_{tag}")))

    def chip_sum(keys, gs, theirs):
        return [add_cols2(g, a, core, f"rs_add2_{n}_{j}") for (n, j), g, a in zip(keys, gs, theirs)]

    pending = None
    for i in reversed(range(depth)):
        kind, j = i % 3, i // 3
        s_mix, s_ffn, mp, fp = saved[i]
        if pending:
            keys, gs = pending
            order = keys[0::2] + keys[1::2]
            sums = []

            def then_jobs(theirs, keys=keys, gs=gs, sums=sums):
                sums.extend(chip_sum(keys, gs, theirs))
                return scatter_job(sums[0::2]), scatter_job(sums[1::2])

            dx, G["ffn_norm"][i], gf, got = _ffn_bwd(dx, s_ffn, full["ffn_norm"][i:i + 1], fp, f"l{i}_ffn",
                                                     swap_job(gs), then_jobs)
            finish(order, sums[0::2] + sums[1::2], got, f"l{i + 1}")
        else:
            dx, G["ffn_norm"][i], gf, _ = _ffn_bwd(dx, s_ffn, full["ffn_norm"][i:i + 1], fp, f"l{i}_ffn")
        dx, G["mix_norm"][i], gm = _MIXERS[kind][1](dx, s_mix, full["mix_norm"][i:i + 1], mp, f"l{i}_mix")
        G["ffn"][i] = gf
        G[("att", "pool", "gdn")[kind]][j] = gm
        gbig = layer_big_grads(i, gm, gf, full["gdn_a_log"].shape[1])
        pending = (groups[i], [gbig[k] for k in groups[i]])
    keys, gs = pending
    sums = chip_sum(keys, gs, swap_cols(gs, "rs_swap_l0"))
    finish(keys, sums, scatter_cols(sums, "rs_scatter_l0"), "l0")
    return loss, dx[0], summed, G


SHARD_AXIS = dict(att_w_qkv=2, att_w_o=1, pool_w=2, gdn_w_in=2, gdn_w_o=1, ffn_w_up=2, ffn_w_down=1,
                  att_rel_bias=2, gdn_conv=2, ffn_conv=2)
BIG = ("att_w_qkv", "att_w_o", "pool_w", "gdn_w_in", "gdn_w_o", "ffn_w_up", "ffn_w_down")
SMALL_SHARDED = ("att_rel_bias", "gdn_conv", "ffn_conv")
REPLICATED = ("mix_norm", "ffn_norm", "att_q_gain", "att_k_gain", "pool_scale", "gdn_a_log", "gdn_dt_bias",
              "gdn_o_gain")
WEIGHTS = ("mix_norm", "ffn_norm", "att_w_qkv", "att_q_gain", "att_k_gain", "att_rel_bias", "att_w_o", "pool_w",
           "pool_scale", "gdn_w_in", "gdn_conv", "gdn_a_log", "gdn_dt_bias", "gdn_o_gain", "gdn_w_o", "ffn_w_up",
           "ffn_conv", "ffn_w_down")


def _merge(stacked, axis):
    t = jnp.moveaxis(stacked, 0, axis)
    return t.reshape(t.shape[:axis] + (t.shape[axis] * t.shape[axis + 1],) + t.shape[axis + 2:])


def _pad_to(t, axis, size):
    pad = [(0, 0)] * t.ndim
    pad[axis] = (0, size - t.shape[axis])
    return jnp.pad(t, pad)


def _round_up(n, m):
    return (n + m - 1) // m * m


def to_comm(name, t):
    if name in ("att_w_qkv", "gdn_w_in"):
        return t.T
    if name == "ffn_w_up":
        d, n = t.shape
        return _pad_to(t.T.reshape(2, n // 2, d), 1, _round_up(n // 2, LANES)).reshape(-1, d)
    if name == "ffn_w_down":
        return _pad_to(t, 0, _round_up(t.shape[0], LANES))
    if name == "pool_w":
        return t.reshape(-1, t.shape[-1])
    return t


def from_comm(name, r, shape):
    if name in ("att_w_qkv", "gdn_w_in"):
        return r.T
    if name == "ffn_w_up":
        d, n = shape
        return r.reshape(2, -1, d)[:, :n // 2].reshape(n, d).T
    if name == "ffn_w_down":
        return r[:shape[0]]
    return r.reshape(shape)


def _rows(t):
    return t.reshape(-1, t.shape[-1])


def layer_weights(i, big, full):
    kind, j = i % 3, i // 3
    F4 = full["ffn_conv"].shape[2] // N_CHIPS
    conv = _pad_to(full["ffn_conv"][i].reshape(FFN_CONV, N_CHIPS, F4), 2, _round_up(F4, LANES)).reshape(FFN_CONV, -1)
    fp = dict(w_up_t=_rows(big["ffn_w_up", i]), conv=_pad_to(conv, 0, SUBLANES), w_down=_rows(big["ffn_w_down", i]))
    if kind == 0:
        mp = dict(w_qkv_t=_rows(big["att_w_qkv", j]), w_o=_rows(big["att_w_o", j]),
                  q_gain=full["att_q_gain"][j:j + 1], k_gain=full["att_k_gain"][j:j + 1],
                  rel_bias=full["att_rel_bias"][j])
    elif kind == 1:
        t = big["pool_w", j]
        G = len(POOL_WINDOWS)
        dg = t.shape[-1]
        w = jnp.moveaxis(t.reshape(N_CHIPS, G, dg // N_CHIPS, dg), 0, 1).reshape(G, dg, dg)
        mp = dict(w=w, scale=full["pool_scale"][j:j + 1])
    else:
        C = full["gdn_conv"].shape[2]
        wt = _rows(big["gdn_w_in", j])
        V = _rows(big["gdn_w_o", j]).shape[0]
        mp = dict(w_main_t=wt[:C + V], w_ab_t=_pad_to(wt[C + V:], 0, LANES),
                  conv=_pad_to(full["gdn_conv"][j], 0, SUBLANES), a_log=full["gdn_a_log"][j:j + 1],
                  dt_bias=full["gdn_dt_bias"][j:j + 1], o_gain=full["gdn_o_gain"][j:j + 1],
                  w_o=_rows(big["gdn_w_o", j]), key_dim=(C - V) // 2)
    return mp, fp


def layer_big_grads(i, gm, gf, hv):
    kind, j = i % 3, i // 3

    def slots(t):
        return t.reshape(N_CHIPS, t.shape[0] // N_CHIPS, t.shape[1])

    out = {("ffn_w_up", i): slots(gf["w_up_t"]), ("ffn_w_down", i): slots(gf["w_down"])}
    if kind == 0:
        out["att_w_qkv", j] = slots(gm["w_qkv_t"])
        out["att_w_o", j] = slots(gm["w_o"])
    elif kind == 1:
        n, dg, _ = gm["w"].shape
        out["pool_w", j] = jnp.moveaxis(gm["w"].reshape(n, N_CHIPS, dg // N_CHIPS, dg), 1, 0).reshape(N_CHIPS, -1, dg)
    else:
        out["gdn_w_in", j] = slots(jnp.concatenate([gm["w_main_t"], gm["w_ab_t"][:2 * hv]], axis=0))
        out["gdn_w_o", j] = slots(gm["w_o"])
    return out


def small_grads(G, full):
    F = full["ffn_conv"].shape[2]
    F4 = F // N_CHIPS

    def conv(g):
        return g["conv"][:FFN_CONV].reshape(FFN_CONV, N_CHIPS, -1)[:, :, :F4].reshape(FFN_CONV, F)

    return dict(
        mix_norm=jnp.concatenate(G["mix_norm"], axis=0), ffn_norm=jnp.concatenate(G["ffn_norm"], axis=0),
        ffn_conv=jnp.stack([conv(g) for g in G["ffn"]]),
        att_q_gain=jnp.concatenate([g["q_gain"] for g in G["att"]], axis=0),
        att_k_gain=jnp.concatenate([g["k_gain"] for g in G["att"]], axis=0),
        att_rel_bias=jnp.stack([g["rel_bias"] for g in G["att"]]),
        pool_scale=jnp.concatenate([g["scale"] for g in G["pool"]], axis=0),
        gdn_conv=jnp.stack([g["conv"][:GDN_CONV] for g in G["gdn"]]),
        gdn_a_log=jnp.concatenate([g["a_log"] for g in G["gdn"]], axis=0),
        gdn_dt_bias=jnp.concatenate([g["dt_bias"] for g in G["gdn"]], axis=0),
        gdn_o_gain=jnp.concatenate([g["o_gain"] for g in G["gdn"]], axis=0))


ANY = BS(memory_space=pl.ANY)
PACK_COLS = 1024
PACK_ROWS = 32


def _place():
    x, y, c = lax.axis_index("x"), lax.axis_index("y"), lax.axis_index("c")
    chips = [(1 - x, y), (x, 1 - y), (1 - x, 1 - y)]
    return x, y, c, chips


def _remote(src, dst, send_sem, recv_sem, to):
    return pltpu.make_async_remote_copy(src_ref=src, dst_ref=dst, send_sem=send_sem, recv_sem=recv_sem,
                                        device_id=to, device_id_type=MESH)


def gather_chips(shard, name):
    R, C = shard.shape
    half = R // 2

    def body(x_ref, o_ref, send_sems, recv_sems, local_sem):
        x, y, c, chips = _place()
        mine_rows = pl.ds(c * half, half)
        other_rows = pl.ds((1 - c) * half, half)
        own = pltpu.make_async_copy(x_ref, o_ref.at[2 * x + y], local_sem)
        own.start()
        first = [_remote(x_ref.at[mine_rows], o_ref.at[2 * x + y, mine_rows], send_sems.at[j], recv_sems.at[j],
                         (cx, cy, c)) for j, (cx, cy) in enumerate(chips)]
        for cp in first:
            cp.start()
        passed = []
        for j, (cx, cy) in enumerate(chips):
            landed = o_ref.at[2 * cx + cy, mine_rows]
            _remote(landed, landed, send_sems.at[j], recv_sems.at[j], (cx, cy, c)).wait_recv()
            cp = _remote(landed, landed, send_sems.at[3 + j], recv_sems.at[3 + j], (x, y, 1 - c))
            cp.start()
            passed.append(cp)
        for j, (cx, cy) in enumerate(chips):
            landed = o_ref.at[2 * cx + cy, other_rows]
            _remote(landed, landed, send_sems.at[3 + j], recv_sems.at[3 + j], (x, y, 1 - c)).wait_recv()
        for cp in first + passed:
            cp.wait_send()
        own.wait()

    return pl.pallas_call(
        body, out_shape=S((N_CHIPS, R, C), shard.dtype), in_specs=[ANY], out_specs=ANY,
        scratch_shapes=[pltpu.SemaphoreType.DMA((6,)), pltpu.SemaphoreType.DMA((6,)), pltpu.SemaphoreType.DMA],
        name=name)(shard)


def _tile(R, hc):
    if R % 256 == 0:
        return _pick(R, (512, 256)), hc
    return R, _pick(hc, (256, 128))


def _half(c, hc):
    return pl.ds(pl.multiple_of(c * hc, hc), hc)


def prep_slot(t, s_me, name):
    R, C = t.shape
    br, bc = _tile(R, C // 2)

    def body(s_ref, t_ref, o_ref):
        o_ref[0] = t_ref[...].astype(o_ref.dtype)

    return pl.pallas_call(
        body, grid_spec=pltpu.PrefetchScalarGridSpec(
            num_scalar_prefetch=1, grid=(R // br, C // bc),
            in_specs=[BS((br, bc), lambda i, j, s: (i, j))],
            out_specs=BS((1, br, bc), lambda i, j, s: (s[0], i, j))),
        out_shape=S((N_CHIPS, R, C), MXU_DTYPE), compiler_params=_cp(("parallel", "parallel")), name=name)(
            s_me.reshape(1), t)


def gather_slots(arrs, name):
    nt = len(arrs)

    def body(*refs):
        outs = refs[nt:2 * nt]
        send_sems, recv_sems = refs[2 * nt:]
        x, y, c, chips = _place()
        me = 2 * x + y
        first, passed = [], []
        for t, o in enumerate(outs):
            mine = _half(c, o.shape[2] // 2)
            for j, (cx, cy) in enumerate(chips):
                cp = _remote(o.at[me, :, mine], o.at[me, :, mine], send_sems.at[t, j], recv_sems.at[t, j], (cx, cy, c))
                cp.start()
                first.append(cp)
        for t, o in enumerate(outs):
            mine = _half(c, o.shape[2] // 2)
            for j, (cx, cy) in enumerate(chips):
                landed = o.at[2 * cx + cy, :, mine]
                _remote(landed, landed, send_sems.at[t, j], recv_sems.at[t, j], (cx, cy, c)).wait_recv()
                cp = _remote(landed, landed, send_sems.at[t, 3 + j], recv_sems.at[t, 3 + j], (x, y, 1 - c))
                cp.start()
                passed.append(cp)
        for t, o in enumerate(outs):
            other = _half(1 - c, o.shape[2] // 2)
            for j, (cx, cy) in enumerate(chips):
                landed = o.at[2 * cx + cy, :, other]
                _remote(landed, landed, send_sems.at[t, 3 + j], recv_sems.at[t, 3 + j], (x, y, 1 - c)).wait_recv()
        for cp in first + passed:
            cp.wait_send()

    return pl.pallas_call(
        body, out_shape=[S(a.shape, a.dtype) for a in arrs], in_specs=[ANY] * nt, out_specs=[ANY] * nt,
        input_output_aliases={t: t for t in range(nt)},
        scratch_shapes=[pltpu.SemaphoreType.DMA((nt, 6)), pltpu.SemaphoreType.DMA((nt, 6))], name=name)(*arrs)


def swap_cols(gs, name):
    nt = len(gs)

    def body(*refs):
        ins, outs = refs[:nt], refs[nt:2 * nt]
        send_sems, recv_sems = refs[2 * nt:]
        x, y, c, _ = _place()
        sent = []
        for t, (g, o) in enumerate(zip(ins, outs)):
            cp = _remote(g.at[:, :, _half(1 - c, o.shape[2])], o, send_sems.at[t], recv_sems.at[t], (x, y, 1 - c))
            cp.start()
            sent.append(cp)
        for cp in sent:
            cp.wait()

    return pl.pallas_call(
        body, out_shape=[S(g.shape[:2] + (g.shape[2] // 2,), g.dtype) for g in gs], in_specs=[ANY] * nt,
        out_specs=[ANY] * nt, scratch_shapes=[pltpu.SemaphoreType.DMA((nt,)), pltpu.SemaphoreType.DMA((nt,))],
        name=name)(*gs)


def add_cols2(g, other, c, name):
    n, R, C = g.shape
    hc = C // 2
    br, bc = _tile(R, hc)
    nj = hc // bc

    def body(c_ref, g_ref, o_ref, out_ref):
        out_ref[...] = (g_ref[...] + o_ref[...]).astype(out_ref.dtype)

    blk = BS((1, br, bc), lambda s, i, j, c_ref: (s, i, j))
    return pl.pallas_call(
        body, grid_spec=pltpu.PrefetchScalarGridSpec(
            num_scalar_prefetch=1, grid=(n, R // br, nj),
            in_specs=[BS((1, br, bc), lambda s, i, j, c_ref: (s, i, c_ref[0] * nj + j)), blk], out_specs=blk),
        out_shape=S((n, R, hc), BF16), compiler_params=_cp(("parallel", "parallel", "parallel")), name=name)(
            c, g, other)


def scatter_cols(ps, name):
    nt = len(ps)

    def body(*refs):
        ins, outs = refs[:nt], refs[nt:2 * nt]
        send_sems, recv_sems = refs[2 * nt:]
        x, y, c, chips = _place()
        sent = []
        for t, (p, o) in enumerate(zip(ins, outs)):
            for j, (cx, cy) in enumerate(chips):
                cp = _remote(p.at[2 * cx + cy], o.at[j], send_sems.at[t, j], recv_sems.at[t, j], (cx, cy, c))
                cp.start()
                sent.append(cp)
        for cp in sent:
            cp.wait()

    return pl.pallas_call(
        body, out_shape=[S((N_CHIPS - 1,) + p.shape[1:], p.dtype) for p in ps], in_specs=[ANY] * nt,
        out_specs=[ANY] * nt,
        scratch_shapes=[pltpu.SemaphoreType.DMA((nt, 3)), pltpu.SemaphoreType.DMA((nt, 3))], name=name)(*ps)


def add_cols4(p, got, place, name):
    n, R, hc = p.shape
    br, bc = _tile(R, hc)
    nj = hc // bc

    def body(pl_ref, p_ref, g_ref, out_ref):
        acc = p_ref[0].astype(F32)
        for j in range(n - 1):
            acc += g_ref[j].astype(F32)
        out_ref[...] = acc

    return pl.pallas_call(
        body, grid_spec=pltpu.PrefetchScalarGridSpec(
            num_scalar_prefetch=1, grid=(R // br, nj),
            in_specs=[BS((1, br, bc), lambda i, j, pl_ref: (pl_ref[0], i, j)),
                      BS((n - 1, br, bc), lambda i, j, pl_ref: (0, i, j))],
            out_specs=BS((br, bc), lambda i, j, pl_ref: (i, pl_ref[1] * nj + j))),
        out_shape=S((R, 2 * hc), F32), compiler_params=_cp(("parallel", "parallel")), name=name)(place, p, got)


def join_cols(rs, name):
    nt = len(rs)

    def body(*refs):
        outs = refs[nt:2 * nt]
        send_sems, recv_sems = refs[2 * nt:]
        x, y, c, _ = _place()
        sent = []
        for t, o in enumerate(outs):
            mine = o.at[:, _half(c, o.shape[1] // 2)]
            cp = _remote(mine, mine, send_sems.at[t], recv_sems.at[t], (x, y, 1 - c))
            cp.start()
            sent.append(cp)
        for t, o in enumerate(outs):
            theirs = o.at[:, _half(1 - c, o.shape[1] // 2)]
            _remote(theirs, theirs, send_sems.at[t], recv_sems.at[t], (x, y, 1 - c)).wait_recv()
        for cp in sent:
            cp.wait_send()

    return pl.pallas_call(
        body, out_shape=[S(r.shape, r.dtype) for r in rs], in_specs=[ANY] * nt, out_specs=[ANY] * nt,
        input_output_aliases={t: t for t in range(nt)},
        scratch_shapes=[pltpu.SemaphoreType.DMA((nt,)), pltpu.SemaphoreType.DMA((nt,))], name=name)(*rs)


def gather_job(slots):
    nt = len(slots)

    def copies(outs, send_sems, recv_sems, sent):
        x, y, c, chips = _place()
        me = 2 * x + y
        out = []
        for t, o in enumerate(outs):
            mine = _half(c, o.shape[2] // 2)
            for j, (cx, cy) in enumerate(chips):
                rows = o.at[me if sent else 2 * cx + cy, :, mine]
                out.append(_remote(rows, rows, send_sems.at[3 * t + j], recv_sems.at[3 * t + j], (cx, cy, c)))
        return out

    def start(ins, outs, send_sems, recv_sems):
        for cp in copies(outs, send_sems, recv_sems, True):
            cp.start()

    def finish(ins, outs, send_sems, recv_sems):
        for cp in copies(outs, send_sems, recv_sems, False):
            cp.wait_recv()
        for cp in copies(outs, send_sems, recv_sems, True):
            cp.wait_send()

    return dict(ins=list(slots), out_shapes=[S(a.shape, a.dtype) for a in slots], aliases=[(t, t) for t in range(nt)],
                nsem=3 * nt, start=start, finish=finish)


def gather_pass(arrs, name):
    nt = len(arrs)

    def body(*refs):
        outs = refs[nt:2 * nt]
        send_sems, recv_sems = refs[2 * nt:]
        x, y, c, chips = _place()
        sent = []
        for t, o in enumerate(outs):
            mine = _half(c, o.shape[2] // 2)
            for j, (cx, cy) in enumerate(chips):
                landed = o.at[2 * cx + cy, :, mine]
                cp = _remote(landed, landed, send_sems.at[t, j], recv_sems.at[t, j], (x, y, 1 - c))
                cp.start()
                sent.append(cp)
        for t, o in enumerate(outs):
            other = _half(1 - c, o.shape[2] // 2)
            for j, (cx, cy) in enumerate(chips):
                landed = o.at[2 * cx + cy, :, other]
                _remote(landed, landed, send_sems.at[t, j], recv_sems.at[t, j], (x, y, 1 - c)).wait_recv()
        for cp in sent:
            cp.wait_send()

    return pl.pallas_call(
        body, out_shape=[S(a.shape, a.dtype) for a in arrs], in_specs=[ANY] * nt, out_specs=[ANY] * nt,
        input_output_aliases={t: t for t in range(nt)},
        scratch_shapes=[pltpu.SemaphoreType.DMA((nt, 3)), pltpu.SemaphoreType.DMA((nt, 3))], name=name)(*arrs)


def swap_job(gs):
    def copies(ins, outs, send_sems, recv_sems):
        x, y, c, _ = _place()
        return [_remote(g.at[:, :, _half(1 - c, o.shape[2])], o, send_sems.at[t], recv_sems.at[t], (x, y, 1 - c))
                for t, (g, o) in enumerate(zip(ins, outs))]

    def start(ins, outs, send_sems, recv_sems):
        for cp in copies(ins, outs, send_sems, recv_sems):
            cp.start()

    def finish(ins, outs, send_sems, recv_sems):
        for cp in copies(ins, outs, send_sems, recv_sems):
            cp.wait()

    return dict(ins=list(gs), out_shapes=[S(g.shape[:2] + (g.shape[2] // 2,), g.dtype) for g in gs], aliases=[],
                nsem=len(gs), start=start, finish=finish)


def scatter_job(ps):
    nt = len(ps)

    def copies(ins, outs, send_sems, recv_sems):
        x, y, c, chips = _place()
        return [_remote(p.at[2 * cx + cy], o.at[j], send_sems.at[3 * t + j], recv_sems.at[3 * t + j], (cx, cy, c))
                for t, (p, o) in enumerate(zip(ins, outs)) for j, (cx, cy) in enumerate(chips)]

    def start(ins, outs, send_sems, recv_sems):
        for cp in copies(ins, outs, send_sems, recv_sems):
            cp.start()

    def finish(ins, outs, send_sems, recv_sems):
        for cp in copies(ins, outs, send_sems, recv_sems):
            cp.wait()

    return dict(ins=list(ps), out_shapes=[S((N_CHIPS - 1,) + p.shape[1:], p.dtype) for p in ps], aliases=[],
                nsem=3 * nt, start=start, finish=finish)


def sum_devices(v, name):
    R, C = v.shape

    def body(v_ref, o_ref, slots, send_sems, recv_sems):
        x, y, c, _ = _place()
        me = 4 * x + 2 * y + c
        slots[me] = v_ref[...]
        sent = []
        for r in range(1, 8):
            peer = (x ^ (r >> 2), y ^ ((r >> 1) & 1), c ^ (r & 1))
            cp = _remote(v_ref, slots.at[me], send_sems.at[r - 1], recv_sems.at[r - 1], peer)
            cp.start()
            sent.append(cp)
        for r in range(1, 8):
            peer = (x ^ (r >> 2), y ^ ((r >> 1) & 1), c ^ (r & 1))
            theirs = slots.at[4 * peer[0] + 2 * peer[1] + peer[2]]
            _remote(v_ref, theirs, send_sems.at[r - 1], recv_sems.at[r - 1], peer).wait_recv()
        for cp in sent:
            cp.wait_send()
        acc = slots[0]
        for k in range(1, 8):
            acc += slots[k]
        o_ref[...] = acc

    vm = BS(memory_space=pltpu.VMEM)
    return pl.pallas_call(
        body, out_shape=S((R, C), F32), in_specs=[vm], out_specs=vm,
        scratch_shapes=[pltpu.VMEM((8, R, C), F32), pltpu.SemaphoreType.DMA((7,)), pltpu.SemaphoreType.DMA((7,))],
        compiler_params=pltpu.CompilerParams(vmem_limit_bytes=32 * MIB), name=name)(v)


def _pack(arrays, dtype, cols, row_mult):
    flat = jnp.concatenate([a.astype(dtype).reshape(-1) for a in arrays])
    n = flat.shape[0]
    total = _round_up(n, cols * row_mult)
    return jnp.pad(flat, (0, total - n)).reshape(total // cols, cols)


def _unpack(flat, shapes):
    out, off = [], 0
    for shp in shapes:
        n = 1
        for d in shp:
            n *= d
        out.append(flat[..., off:off + n].reshape(flat.shape[:-1] + tuple(shp)))
        off += n
    return out


def _layer_groups(depth):
    groups = []
    for i in range(depth):
        kind, j = i % 3, i // 3
        mix = ((("att_w_qkv", j), ("att_w_o", j)), (("pool_w", j),), (("gdn_w_in", j), ("gdn_w_o", j)))[kind]
        groups.append(mix + (("ffn_w_up", i), ("ffn_w_down", i)))
    return groups


def kernel(x, mix_norm, ffn_norm, att_w_qkv, att_q_gain, att_k_gain, att_rel_bias, att_w_o, pool_w, pool_scale, gdn_w_in, gdn_conv, gdn_a_log, gdn_dt_bias, gdn_o_gain, gdn_w_o, ffn_w_up, ffn_conv, ffn_w_down, loss_target, m_mix_norm, m_ffn_norm, m_att_w_qkv, m_att_q_gain, m_att_k_gain, m_att_rel_bias, m_att_w_o, m_pool_w, m_pool_scale, m_gdn_w_in, m_gdn_conv, m_gdn_a_log, m_gdn_dt_bias, m_gdn_o_gain, m_gdn_w_o, m_ffn_w_up, m_ffn_conv, m_ffn_w_down, v_mix_norm, v_ffn_norm, v_att_w_qkv, v_att_q_gain, v_att_k_gain, v_att_rel_bias, v_att_w_o, v_pool_w, v_pool_scale, v_gdn_w_in, v_gdn_conv, v_gdn_a_log, v_gdn_dt_bias, v_gdn_o_gain, v_gdn_w_o, v_ffn_w_up, v_ffn_conv, v_ffn_w_down):
    w = dict(mix_norm=mix_norm, ffn_norm=ffn_norm, att_w_qkv=att_w_qkv, att_q_gain=att_q_gain, att_k_gain=att_k_gain, att_rel_bias=att_rel_bias, att_w_o=att_w_o, pool_w=pool_w, pool_scale=pool_scale, gdn_w_in=gdn_w_in, gdn_conv=gdn_conv, gdn_a_log=gdn_a_log, gdn_dt_bias=gdn_dt_bias, gdn_o_gain=gdn_o_gain, gdn_w_o=gdn_w_o, ffn_w_up=ffn_w_up, ffn_conv=ffn_conv, ffn_w_down=ffn_w_down)
    m = dict(mix_norm=m_mix_norm, ffn_norm=m_ffn_norm, att_w_qkv=m_att_w_qkv, att_q_gain=m_att_q_gain, att_k_gain=m_att_k_gain, att_rel_bias=m_att_rel_bias, att_w_o=m_att_w_o, pool_w=m_pool_w, pool_scale=m_pool_scale, gdn_w_in=m_gdn_w_in, gdn_conv=m_gdn_conv, gdn_a_log=m_gdn_a_log, gdn_dt_bias=m_gdn_dt_bias, gdn_o_gain=m_gdn_o_gain, gdn_w_o=m_gdn_w_o, ffn_w_up=m_ffn_w_up, ffn_conv=m_ffn_conv, ffn_w_down=m_ffn_w_down)
    v = dict(mix_norm=v_mix_norm, ffn_norm=v_ffn_norm, att_w_qkv=v_att_w_qkv, att_q_gain=v_att_q_gain, att_k_gain=v_att_k_gain, att_rel_bias=v_att_rel_bias, att_w_o=v_att_w_o, pool_w=v_pool_w, pool_scale=v_pool_scale, gdn_w_in=v_gdn_w_in, gdn_conv=v_gdn_conv, gdn_a_log=v_gdn_a_log, gdn_dt_bias=v_gdn_dt_bias, gdn_o_gain=v_gdn_o_gain, gdn_w_o=v_gdn_w_o, ffn_w_up=v_ffn_w_up, ffn_conv=v_ffn_conv, ffn_w_down=v_ffn_w_down)
    depth = ffn_w_up.shape[0]
    my_c = lax.axis_index("c").astype(jnp.int32)
    my_chip = (2 * lax.axis_index("x") + lax.axis_index("y")).astype(jnp.int32)
    groups = _layer_groups(depth)

    place = jnp.stack([my_chip, my_c])

    full = {n: w[n] for n in REPLICATED}
    slots = {(n, j): prep_slot(to_comm(n, w[n][j]), my_chip, f"prep_{n}_{j}") for group in groups for n, j in group}
    small = [w[n] for n in SMALL_SHARDED]
    got = gather_chips(_pack(small, F32, LANES, PACK_ROWS), "gather_small").reshape(N_CHIPS, -1)
    for n, t in zip(SMALL_SHARDED, _unpack(got, [s.shape for s in small])):
        full[n] = _merge(t, SHARD_AXIS[n])

    loss, grad_x, summed, G = local_step(x[0], loss_target[0], full, slots, place)
    loss = lax.psum(loss[0, 0], ("x", "y", "c"))
    gfull = small_grads(G, full)

    grads = {n: jnp.stack([from_comm(n, summed[n, j], w[n][j].shape) for j in range(w[n].shape[0])]) for n in BIG}

    small_names = REPLICATED + SMALL_SHARDED
    packed = _pack([gfull[n] for n in small_names], F32, LANES, SUBLANES)
    summed = sum_devices(packed, "sum_small").reshape(-1)
    for n, t in zip(small_names, _unpack(summed, [gfull[n].shape for n in small_names])):
        if n in SHARD_AXIS:
            size = w[n].shape[SHARD_AXIS[n]]
            t = lax.dynamic_slice_in_dim(t, my_chip * size, size, axis=SHARD_AXIS[n])
        grads[n] = t

    delta, new_m, new_v = {}, {}, {}
    for n in WEIGHTS:
        delta[n], new_m[n], new_v[n] = adamw(w[n], grads[n], m[n], v[n], f"adamw_{n}")
    return (loss, grad_x[None], *[grads[n] for n in WEIGHTS], *[delta[n] for n in WEIGHTS],
            *[new_m[n] for n in WEIGHTS], *[new_v[n] for n in WEIGHTS])
```
